```python
import math
import numpy as np
import jax
import jax.numpy as jnp
from jax import lax

D_MODEL = 1024
BATCH = 32
SEQ = 256
DEPTH = 2
DEC_BATCH = 4
DEC_SEQ = 2048
PAST_LEN = 512

GRID_W = 64
RET_H = 4
RET_DK = 64
RET_DV = 64
RET_W = RET_H * RET_DV
DIFF_H = 4
DIFF_DH = 64
DIFF_DV = 2 * DIFF_DH
DIFF_W = DIFF_H * DIFF_DV
HG_H = 4
HG_DK = 64
HG_DV = 64
HG_W = HG_H * HG_DV
MIX_W = RET_W + DIFF_W + HG_W
SPLIT_SIZES = (RET_H * RET_DK, RET_H * RET_DK, RET_W, RET_W,
               DIFF_H * 2 * DIFF_DH, DIFF_H * 2 * DIFF_DH, DIFF_W, DIFF_W,
               HG_H * HG_DK, HG_H * HG_DK, HG_H * HG_DK, HG_W, HG_W)
IN_W = sum(SPLIT_SIZES)
RET_CHUNK = 64
HG_CHUNK = 32
QBLOCK = 128
ROPE_BASE = 10000.0
EPS = 1e-6

kernel_name = 'hybrid_diffusion_parallel_heads_step'


def rms_f32(x):
    xf = x.astype(jnp.float32)
    return xf * lax.rsqrt(jnp.mean(xf * xf, axis=-1, keepdims=True) + EPS)


def rmsnorm(x, g):
    return (rms_f32(x) * g.astype(jnp.float32)).astype(x.dtype)


def head_rms(x, dtype):
    return rms_f32(x).astype(dtype)


def rope_1d(x, pos):
    half = x.shape[-1] // 2
    inv = ROPE_BASE ** (-jnp.arange(half, dtype=jnp.float32) / half)
    ang = pos.astype(jnp.float32)[:, None] * inv[None, :]
    cos = jnp.cos(ang)[:, None, :].astype(x.dtype)
    sin = jnp.sin(ang)[:, None, :].astype(x.dtype)
    x1, x2 = x[..., :half], x[..., half:]
    return jnp.concatenate([x1 * cos - x2 * sin, x1 * sin + x2 * cos], axis=-1)


def axial_rope(x):
    T = x.shape[1]
    rows = T // GRID_W
    row = jnp.repeat(jnp.arange(rows), GRID_W)
    col = jnp.tile(jnp.arange(GRID_W), rows)
    half = x.shape[-1] // 2
    return jnp.concatenate([rope_1d(x[..., :half], row), rope_1d(x[..., half:], col)], axis=-1)


def to_chunks(x, c):
    B, T, H, D = x.shape
    return x.reshape(B, T // c, c, H, D).transpose(1, 0, 3, 2, 4)


def from_chunks(y):
    nC, B, H, C, D = y.shape
    return y.transpose(1, 0, 3, 2, 4).reshape(B, nC * C, H, D)


def retention_scan(q, k, v, log_g, s0):
    f32 = jnp.float32
    idx = jnp.arange(RET_CHUNK, dtype=f32)
    b = (idx + 1.0)[None, :] * log_g[:, None]
    tril = jnp.tril(jnp.ones((RET_CHUNK, RET_CHUNK), dtype=bool))
    diff = jnp.where(tril, b[:, :, None] - b[:, None, :], 0.0)
    dmat = jnp.where(tril, jnp.exp(diff), 0.0)
    q_dec = jnp.exp(b)[:, :, None]
    k_dec = jnp.exp(b[:, -1:] - b)[:, :, None]
    s_dec = jnp.exp(b[:, -1])[:, None, None]

    def step(s, xs):
        qc, kc, vc = xs
        a = jnp.einsum('bhtd,bhsd->bhts', qc, kc) * dmat
        o = jnp.einsum('bhts,bhse->bhte', a, vc) + jnp.einsum('bhtd,bhde->bhte', qc * q_dec, s)
        s = s * s_dec + jnp.einsum('bhsd,bhse->bhde', kc * k_dec, vc)
        return s, o

    xs = tuple(to_chunks(a.astype(f32), RET_CHUNK) for a in (q, k, v))
    s, o = lax.scan(step, s0.astype(f32), xs)
    return from_chunks(o), s


def hgrn_scan(q, k, v, log_f, s0):
    f32 = jnp.float32
    tril = jnp.tril(jnp.ones((HG_CHUNK, HG_CHUNK), dtype=bool))[:, :, None]

    def step(s, xs):
        qc, kc, vc, lfc = xs
        b = jnp.cumsum(lfc, axis=2)
        diff = jnp.where(tril, b[:, :, :, None, :] - b[:, :, None, :, :], 0.0)
        dec = jnp.where(tril, jnp.exp(diff), 0.0)
        a = jnp.einsum('bhtd,bhsd,bhtsd->bhts', qc, kc, dec)
        b_last = b[:, :, -1:, :]
        o = jnp.einsum('bhts,bhse->bhte', a, vc) + jnp.einsum('bhtd,bhde->bhte', qc * jnp.exp(b), s)
        s = s * jnp.exp(b_last)[:, :, 0, :, None] + jnp.einsum('bhsd,bhse->bhde', kc * jnp.exp(b_last - b), vc)
        return s, o

    xs = tuple(to_chunks(a.astype(f32), HG_CHUNK) for a in (q, k, v, log_f))
    s, o = lax.scan(step, s0.astype(f32), xs)
    return from_chunks(o), s


def diff_attention(q, k, v, lam):
    B, Tq = q.shape[0], q.shape[1]
    nb = Tq // QBLOCK
    qb = q.reshape(B, nb, QBLOCK, DIFF_H, 2, DIFF_DH).swapaxes(0, 1)
    scale = DIFF_DH ** -0.5

    def block(qi):
        s = jnp.einsum('bqhcd,bkhcd->bhcqk', qi, k, preferred_element_type=jnp.float32) * scale
        p = jax.nn.softmax(s, axis=-1)
        w = p[:, :, 0] - lam * p[:, :, 1]
        return jnp.einsum('bhqk,bkhe->bqhe', w.astype(v.dtype), v)

    o = lax.map(block, qb)
    return o.swapaxes(0, 1).reshape(B, Tq, DIFF_H, DIFF_DV)


def trunk_layer(x, mod_vec, l, norm_g, w_ada, b_ada, w_in, w_out, ret_decay_logit,
                diff_qn_g, diff_kn_g, diff_lambda, hgrn_lb_logit, ctx):
    f32 = jnp.float32
    B, T, _ = x.shape
    dt = x.dtype
    latent = ctx is not None
    flip = lambda a: jnp.flip(a, axis=1)

    m = (jnp.dot(jax.nn.silu(mod_vec), w_ada[l]) + b_ada[l]).reshape(-1, 1, 3 * D_MODEL)
    shift, scale, gate = jnp.split(m, 3, axis=-1)
    h = rmsnorm(x, norm_g[l]) * (1 + scale) + shift
    z = jnp.dot(h, w_in[l])
    offsets = np.cumsum(SPLIT_SIZES)[:-1].tolist()
    rq, rk, rv, rg, dq, dk, dv, dg, hq, hff, hfb, hi, hg = jnp.split(z, offsets, axis=-1)

    if latent:
        k_ctx, v_ctx, s_ret_f0, s_ret_b0, s_hg_f0, s_hg_b0 = ctx
    else:
        s_ret_f0 = jnp.zeros((B, RET_H, RET_DK, RET_DV), f32)
        s_ret_b0 = jnp.zeros((B, RET_H, RET_DK, RET_DV), f32)
        s_hg_f0 = jnp.zeros((B, HG_H, HG_DK, HG_DV), f32)
        s_hg_b0 = jnp.zeros((B, HG_H, HG_DK, HG_DV), f32)

    rq = rq.reshape(B, T, RET_H, RET_DK)
    rk = rk.reshape(B, T, RET_H, RET_DK) * (RET_DK ** -0.5)
    rv = rv.reshape(B, T, RET_H, RET_DV)
    if latent:
        rq, rk = axial_rope(rq), axial_rope(rk)
    log_g = jax.nn.log_sigmoid(ret_decay_logit[l].astype(f32))
    o_f, s_ret_f = retention_scan(rq, rk, rv, log_g[0], s_ret_f0)
    o_b, s_ret_b = retention_scan(flip(rq), flip(rk), flip(rv), log_g[1], s_ret_b0)
    ret_out = head_rms(o_f + flip(o_b), dt).reshape(B, T, RET_W) * jax.nn.silu(rg)

    dq = rmsnorm(dq.reshape(B, T, DIFF_H, 2, DIFF_DH), diff_qn_g[l])
    dk = rmsnorm(dk.reshape(B, T, DIFF_H, 2, DIFF_DH), diff_kn_g[l])
    dv = dv.reshape(B, T, DIFF_H, DIFF_DV)
    if latent:
        dq_r = axial_rope(dq.reshape(B, T, 2 * DIFF_H, DIFF_DH)).reshape(B, T, DIFF_H, 2, DIFF_DH)
        dk_r = axial_rope(dk.reshape(B, T, 2 * DIFF_H, DIFF_DH)).reshape(B, T, DIFF_H, 2, DIFF_DH)
        keys = jnp.concatenate([dk_r, k_ctx.astype(dt)], axis=1)
        vals = jnp.concatenate([dv, v_ctx.astype(dt)], axis=1)
    else:
        dq_r, keys, vals = dq, dk, dv
    lam_init = 0.8 - 0.6 * math.exp(-0.3 * l)
    lp = diff_lambda[l].astype(f32)
    lam = jnp.exp(jnp.sum(lp[0] * lp[1])) - jnp.exp(jnp.sum(lp[2] * lp[3])) + lam_init
    d_o = diff_attention(dq_r, keys, vals, lam)
    diff_out = (head_rms(d_o, f32) * (1.0 - lam_init)).astype(dt).reshape(B, T, DIFF_W) * jax.nn.silu(dg)

    lb_all = jax.nn.softmax(hgrn_lb_logit.astype(f32), axis=0)
    lb_all = jnp.cumsum(lb_all, axis=0) - lb_all[0]
    lb = lb_all[l].reshape(HG_H, HG_DK)

    def forget(fz):
        fz = fz.reshape(B, T, HG_H, HG_DK).astype(f32)
        f = lb + (1.0 - lb) * jax.nn.sigmoid(fz)
        return jnp.log(f), 1.0 - f

    lf_f, kf = forget(hff)
    lf_b, kb = forget(hfb)
    hq = hq.reshape(B, T, HG_H, HG_DK) * (HG_DK ** -0.5)
    hi = hi.reshape(B, T, HG_H, HG_DV)
    o_f, s_hg_f = hgrn_scan(hq, kf, hi, lf_f, s_hg_f0)
    o_b, s_hg_b = hgrn_scan(flip(hq), flip(kb), flip(hi), flip(lf_b), s_hg_b0)
    hg_out = head_rms(o_f + flip(o_b), dt).reshape(B, T, HG_W) * jax.nn.silu(hg)

    mixed = jnp.concatenate([ret_out, diff_out, hg_out], axis=-1).astype(dt)
    x_new = x + gate * jnp.dot(mixed, w_out[l])
    if latent:
        return x_new, None
    return x_new, (dk, dv, s_ret_f, s_ret_b, s_hg_f, s_hg_b)


def setup_inputs(seed: int = 0) -> dict:
    key = jax.random.key(seed)
    ks = jax.random.split(key, 20)
    f32 = jnp.float32
    nrm = lambda k, shape: jax.random.normal(k, shape, f32)
    ret_base = jnp.log(2.0 ** (5.0 + jnp.arange(RET_H, dtype=f32)) - 1.0)
    return {
        'x_prompt': nrm(ks[0], (BATCH, SEQ, D_MODEL)),
        'x_sample': nrm(ks[1], (DEC_BATCH, DEC_SEQ, D_MODEL)),
        'c': nrm(ks[2], (DEC_BATCH, D_MODEL)),
        'c_ctx': nrm(ks[3], (D_MODEL,)),
        'cache_diff_k': nrm(ks[4], (DEC_BATCH, DEPTH, PAST_LEN, DIFF_H, 2, DIFF_DH)),
        'cache_diff_v': nrm(ks[5], (DEC_BATCH, DEPTH, PAST_LEN, DIFF_H, DIFF_DV)),
        'state_ret_fwd': 0.3 * nrm(ks[6], (DEC_BATCH, DEPTH, RET_H, RET_DK, RET_DV)),
        'state_ret_bwd': 0.3 * nrm(ks[7], (DEC_BATCH, DEPTH, RET_H, RET_DK, RET_DV)),
        'state_hgrn_fwd': 0.3 * nrm(ks[8], (DEC_BATCH, DEPTH, HG_H, HG_DK, HG_DV)),
        'state_hgrn_bwd': 0.3 * nrm(ks[9], (DEC_BATCH, DEPTH, HG_H, HG_DK, HG_DV)),
        'norm_g': 1.0 + 0.02 * nrm(ks[10], (DEPTH, D_MODEL)),
        'w_ada': 0.5 * D_MODEL ** -0.5 * nrm(ks[11], (DEPTH, D_MODEL, 3 * D_MODEL)),
        'b_ada': 0.02 * nrm(ks[12], (DEPTH, 3 * D_MODEL)),
        'w_in': D_MODEL ** -0.5 * nrm(ks[13], (DEPTH, D_MODEL, IN_W)),
        'w_out': MIX_W ** -0.5 * nrm(ks[14], (DEPTH, MIX_W, D_MODEL)),
        'ret_decay_logit': ret_base[None, None, :] + 0.1 * nrm(ks[15], (DEPTH, 2, RET_H)),
        'diff_qn_g': 1.0 + 0.02 * nrm(ks[16], (DEPTH, DIFF_DH)),
        'diff_kn_g': 1.0 + 0.02 * nrm(ks[17], (DEPTH, DIFF_DH)),
        'diff_lambda': 0.1 * nrm(ks[18], (DEPTH, 4, DIFF_DH)),
        'hgrn_lb_logit': 1.0 + 0.1 * nrm(ks[19], (DEPTH, HG_W)),
    }


def reference(x_prompt, x_sample, c, c_ctx, cache_diff_k, cache_diff_v, state_ret_fwd,
              state_ret_bwd, state_hgrn_fwd, state_hgrn_bwd, norm_g, w_ada, b_ada, w_in,
              w_out, ret_decay_logit, diff_qn_g, diff_kn_g, diff_lambda, hgrn_lb_logit):
    def run(x, mod_vec, l, ctx):
        return trunk_layer(x, mod_vec, l, norm_g, w_ada, b_ada, w_in, w_out, ret_decay_logit,
                           diff_qn_g, diff_kn_g, diff_lambda, hgrn_lb_logit, ctx)

    y_prompt = x_prompt
    per_layer = []
    for l in range(DEPTH):
        y_prompt, tensors = run(y_prompt, c_ctx, l, None)
        per_layer.append(tensors)
    new_cache_diff_k = jnp.stack([t[0] for t in per_layer], axis=1)
    new_cache_diff_v = jnp.stack([t[1] for t in per_layer], axis=1)
    new_state_ret_fwd = jnp.stack([t[2] for t in per_layer], axis=1)
    new_state_ret_bwd = jnp.stack([t[3] for t in per_layer], axis=1)
    new_state_hgrn_fwd = jnp.stack([t[4] for t in per_layer], axis=1)
    new_state_hgrn_bwd = jnp.stack([t[5] for t in per_layer], axis=1)

    y_sample = x_sample
    for l in range(DEPTH):
        ctx = (cache_diff_k[:, l], cache_diff_v[:, l], state_ret_fwd[:, l], state_ret_bwd[:, l],
               state_hgrn_fwd[:, l], state_hgrn_bwd[:, l])
        y_sample, _ = run(y_sample, c, l, ctx)

    return (y_prompt, y_sample, new_cache_diff_k, new_cache_diff_v, new_state_ret_fwd,
            new_state_ret_bwd, new_state_hgrn_fwd, new_state_hgrn_bwd)
```

```python
import functools
import math

import numpy as np
import jax
import jax.numpy as jnp
from jax import lax
from jax.experimental import pallas as pl
from jax.experimental.pallas import tpu as pltpu

F32 = jnp.float32
BF16 = jnp.bfloat16

D_MODEL = 1024
DEPTH = 2
GRID_W = 64
HEADS = 4
DH = 64
PACK_W = HEADS * DH
DIFF_W = 512
RET_COLS = 4 * PACK_W
DIFF_COLS = 4 * DIFF_W
HG_COLS = 5 * PACK_W
IN_W = RET_COLS + DIFF_COLS + HG_COLS
MIX_W = PACK_W + DIFF_W + PACK_W
ROPE_BASE = 10000.0
EPS = 1e-6

RET_C = 256
HG_C = 64
HG_SUB = 8
HG_LEVELS = (8, 16, 32)
TQ = 256
TM_IN = 256
TM_OUT = 512

VMEM_LIMIT = 56 * 1024 * 1024


def _params(n_axes):
    return pltpu.CompilerParams(
        dimension_semantics=("arbitrary",) * n_axes, vmem_limit_bytes=VMEM_LIMIT)


def _dot(a, b):
    return jnp.dot(a, b, preferred_element_type=F32)


def _dot_nt(a, b):
    return lax.dot_general(a, b, (((1,), (1,)), ((), ())), preferred_element_type=F32)


def _dot_tn(a, b):
    return lax.dot_general(a, b, (((0,), (0,)), ((), ())), preferred_element_type=F32)


def _silu(x):
    return x * jax.nn.sigmoid(x)


def _split2_dot(x, g):
    hi = x.astype(BF16)
    lo = (x - hi.astype(F32)).astype(BF16)
    return _dot(hi, g) + _dot(lo, g)


def _group_rms(x, g, width):
    ms = _split2_dot(x * x, g) * (1.0 / width)
    return x * lax.rsqrt(ms + EPS)


def _rope(x, cos, sin):
    w = x.shape[-1]
    lane = lax.broadcasted_iota(jnp.int32, (1, w), 1)
    first = (lane & 31) < 16
    swapped = jnp.where(first, pltpu.roll(x, w - 16, 1), pltpu.roll(x, 16, 1))
    return x * cos + swapped * sin


def _lane_head(w=PACK_W):
    return lax.broadcasted_iota(jnp.int32, (1, w), 1) // DH


def _ada_kernel(cv_ref, w_ref, b_ref, o_ref):
    cv = cv_ref[...]
    o_ref[0] = _dot(_silu(cv), w_ref[0]) + b_ref[0]


def _ada(cv, w_ada, b_ada):
    tn = 1024
    n = 3 * D_MODEL
    return pl.pallas_call(
        _ada_kernel,
        grid=(DEPTH, n // tn),
        in_specs=[
            pl.BlockSpec((8, D_MODEL), lambda l, j: (0, 0)),
            pl.BlockSpec((1, D_MODEL, tn), lambda l, j: (l, 0, j)),
            pl.BlockSpec((1, 1, tn), lambda l, j: (l, 0, j)),
        ],
        out_specs=pl.BlockSpec((1, 8, tn), lambda l, j: (l, 0, j)),
        out_shape=jax.ShapeDtypeStruct((DEPTH, 8, n), F32),
        compiler_params=_params(2),
        name="ada",
    )(cv, w_ada, b_ada.reshape(DEPTH, 1, n))


def _inproj_kernel(x_ref, g_ref, sc_ref, sh_ref, w_ref, zr_ref, zd_ref, zh_ref):
    x = x_ref[...]
    ms = jnp.mean(x * x, axis=-1, keepdims=True)
    h = x * lax.rsqrt(ms + EPS) * g_ref[...]
    h = h * (1.0 + sc_ref[0]) + sh_ref[0]
    z = _dot(h.astype(BF16), w_ref[...])
    zr_ref[...] = z[:, :RET_COLS]
    zd_ref[...] = z[:, RET_COLS:RET_COLS + DIFF_COLS]
    zh_ref[...] = z[:, RET_COLS + DIFF_COLS:]


def _inproj(x2d, g, scale, shift, w_bf, seq_len):
    m = x2d.shape[0]
    per_batch = scale.shape[0] > 1
    tiles_per_seq = seq_len // TM_IN
    mod_map = (lambda i: (i // tiles_per_seq, 0, 0)) if per_batch else (lambda i: (0, 0, 0))
    return pl.pallas_call(
        _inproj_kernel,
        grid=(m // TM_IN,),
        in_specs=[
            pl.BlockSpec((TM_IN, D_MODEL), lambda i: (i, 0)),
            pl.BlockSpec((1, D_MODEL), lambda i: (0, 0)),
            pl.BlockSpec((1, 1, D_MODEL), mod_map),
            pl.BlockSpec((1, 1, D_MODEL), mod_map),
            pl.BlockSpec((D_MODEL, IN_W), lambda i: (0, 0)),
        ],
        out_specs=[
            pl.BlockSpec((TM_IN, RET_COLS), lambda i: (i, 0)),
            pl.BlockSpec((TM_IN, DIFF_COLS), lambda i: (i, 0)),
            pl.BlockSpec((TM_IN, HG_COLS), lambda i: (i, 0)),
        ],
        out_shape=[
            jax.ShapeDtypeStruct((m, RET_COLS), F32),
            jax.ShapeDtypeStruct((m, DIFF_COLS), F32),
            jax.ShapeDtypeStruct((m, HG_COLS), F32),
        ],
        compiler_params=_params(1),
        name="inproj",
    )(x2d, g, scale, shift, w_bf)


def _outproj_kernel(x_ref, r_ref, d_ref, h_ref, gate_ref, w_ref, o_ref):
    acc = _dot(r_ref[...], w_ref[0:PACK_W, :])
    acc += _dot(d_ref[...], w_ref[PACK_W:PACK_W + DIFF_W, :])
    acc += _dot(h_ref[...], w_ref[PACK_W + DIFF_W:, :])
    o_ref[...] = x_ref[...] + gate_ref[0] * acc


def _outproj(x2d, r, d, h, gate, w_bf, seq_len):
    m = x2d.shape[0]
    per_batch = gate.shape[0] > 1
    tiles_per_seq = max(seq_len // TM_OUT, 1)
    tm = min(TM_OUT, seq_len)
    mod_map = (lambda i: (i // tiles_per_seq, 0, 0)) if per_batch else (lambda i: (0, 0, 0))
    return pl.pallas_call(
        _outproj_kernel,
        grid=(m // tm,),
        in_specs=[
            pl.BlockSpec((tm, D_MODEL), lambda i: (i, 0)),
            pl.BlockSpec((tm, PACK_W), lambda i: (i, 0)),
            pl.BlockSpec((tm, DIFF_W), lambda i: (i, 0)),
            pl.BlockSpec((tm, PACK_W), lambda i: (i, 0)),
            pl.BlockSpec((1, 1, D_MODEL), mod_map),
            pl.BlockSpec((MIX_W, D_MODEL), lambda i: (0, 0)),
        ],
        out_specs=pl.BlockSpec((tm, D_MODEL), lambda i: (i, 0)),
        out_shape=jax.ShapeDtypeStruct((m, D_MODEL), F32),
        compiler_params=_params(1),
        name="outproj",
    )(x2d, r, d, h, gate, w_bf)


def _ret_kernel(*refs, seq_len, latent):
    if latent:
        (lg_ref, z_ref, g_ref, cos_ref, sin_ref, s0f_ref, s0b_ref,
         out_ref, mask_scr, st_scr, o_scr) = refs
    else:
        lg_ref, z_ref, g_ref, out_ref, sf_ref, sb_ref, mask_scr, st_scr, o_scr = refs
    C = RET_C
    n_chunks = seq_len // C
    lh = _lane_head()

    def lane_vec(direction):
        v = jnp.zeros((1, PACK_W), F32)
        for h in range(HEADS):
            v = jnp.where(lh == h, lg_ref[direction, h], v)
        return v

    lgf, lgb = lane_vec(0), lane_vec(1)
    tau = lax.broadcasted_iota(jnp.int32, (C, 1), 0).astype(F32)
    qdec_f = jnp.exp((tau + 1.0) * lgf)
    qdec_b = jnp.exp((C - tau) * lgb)
    kdec_f = jnp.exp((C - 1.0 - tau) * lgf)
    kdec_b = jnp.exp(tau * lgb)
    sdec_f = jnp.exp(float(C) * lgf)
    sdec_b = jnp.exp(float(C) * lgb)
    gmask = g_ref[...].astype(F32)

    dd = (lax.broadcasted_iota(jnp.int32, (C, C), 0)
          - lax.broadcasted_iota(jnp.int32, (C, C), 1)).astype(F32)
    for h in range(HEADS):
        mf = jnp.where(dd >= 0, jnp.exp(jnp.maximum(dd, 0.0) * lg_ref[0, h]), 0.0)
        mb = jnp.where(dd <= 0, jnp.exp(jnp.maximum(-dd, 0.0) * lg_ref[1, h]), 0.0)
        mask_scr[h] = mf + mb

    def load(r0):
        q = z_ref[pl.ds(r0, C), 0:PACK_W]
        k = z_ref[pl.ds(r0, C), PACK_W:2 * PACK_W] * (DH ** -0.5)
        v = z_ref[pl.ds(r0, C), 2 * PACK_W:3 * PACK_W]
        if latent:
            cos = cos_ref[pl.ds(r0, C), :]
            sin = sin_ref[pl.ds(r0, C), :]
            q = _rope(q, cos, sin)
            k = _rope(k, cos, sin)
        return q, k, v

    def intra(q, k, v):
        kb = k.astype(BF16)
        vb = v.astype(BF16)
        o = jnp.zeros((C, PACK_W), F32)
        for h in range(HEADS):
            hm = lh == h
            qh = jnp.where(hm, q, 0.0).astype(BF16)
            a = _dot_nt(qh, kb) * mask_scr[h]
            o = o + jnp.where(hm, _dot(a.astype(BF16), vb), 0.0)
        return o

    def finish(o, r0):
        gate = z_ref[pl.ds(r0, C), 3 * PACK_W:4 * PACK_W]
        y = _group_rms(o, g_ref[...], DH) * _silu(gate)
        out_ref[pl.ds(r0, C), :] = y.astype(out_ref.dtype)

    def state_step(st, k, v, kdec, sdec):
        upd = _dot_tn(v.astype(BF16), (k * kdec).astype(BF16))
        return (st * sdec + upd) * gmask

    if not latent:
        q, k, v = load(0)
        finish(intra(q, k, v), 0)
        zero = jnp.zeros((PACK_W, PACK_W), F32)
        sf_ref[0] = state_step(zero, k, v, kdec_f, sdec_f)
        sb_ref[0] = state_step(zero, k, v, kdec_b, sdec_b)
        return

    st_scr[...] = s0b_ref[0]

    def bwd_body(i, carry):
        r0 = pl.multiple_of((n_chunks - 1 - i) * C, C)
        q, k, v = load(r0)
        st = st_scr[...]
        o_scr[pl.ds(r0, C), :] = _dot_nt((q * qdec_b).astype(BF16), st.astype(BF16))
        st_scr[...] = state_step(st, k, v, kdec_b, sdec_b)
        return carry

    lax.fori_loop(0, n_chunks, bwd_body, 0)

    st_scr[...] = s0f_ref[0]

    def fwd_body(i, carry):
        r0 = pl.multiple_of(i * C, C)
        q, k, v = load(r0)
        st = st_scr[...]
        o = intra(q, k, v) + o_scr[pl.ds(r0, C), :]
        o = o + _dot_nt((q * qdec_f).astype(BF16), st.astype(BF16))
        finish(o, r0)
        st_scr[...] = state_step(st, k, v, kdec_f, sdec_f)
        return carry

    lax.fori_loop(0, n_chunks, fwd_body, 0)


def _retention(z_ret, log_g, g256, batch, seq_len, latent, rope=None, states=None):
    m = batch * seq_len
    row = lambda b: (b, 0)
    const2 = lambda b: (0, 0)
    in_specs = [
        pl.BlockSpec(memory_space=pltpu.SMEM),
        pl.BlockSpec((seq_len, RET_COLS), row),
        pl.BlockSpec((PACK_W, PACK_W), const2),
    ]
    args = [log_g, z_ret, g256]
    out_specs = [pl.BlockSpec((seq_len, PACK_W), row)]
    out_shape = [jax.ShapeDtypeStruct((m, PACK_W), BF16)]
    state_spec = pl.BlockSpec((1, PACK_W, PACK_W), lambda b: (b, 0, 0))
    if latent:
        in_specs += [pl.BlockSpec((seq_len, PACK_W), const2)] * 2 + [state_spec] * 2
        args += [rope[0], rope[1], states[0], states[1]]
    else:
        out_specs += [state_spec] * 2
        out_shape += [jax.ShapeDtypeStruct((batch, PACK_W, PACK_W), F32)] * 2
    return pl.pallas_call(
        functools.partial(_ret_kernel, seq_len=seq_len, latent=latent),
        grid=(batch,),
        in_specs=in_specs,
        out_specs=out_specs,
        out_shape=out_shape,
        scratch_shapes=[
            pltpu.VMEM((HEADS, RET_C, RET_C), F32),
            pltpu.VMEM((PACK_W, PACK_W), F32),
            pltpu.VMEM((seq_len, PACK_W), F32),
        ],
        compiler_params=_params(1),
        name="ret_latent" if latent else "ret_ctx",
    )(*args)


def _attn_kernel(*refs, seq_len, past_len, latent, lam_init):
    if latent:
        (lam_ref, q_ref, k_ref, v_ref, gate_ref, gq_ref, gk_ref, g_ref,
         cosq_ref, sinq_ref, cosk_ref, sink_ref, kc_ref, vc_ref,
         out_ref, k_scr, v_scr) = refs
    else:
        (lam_ref, q_ref, k_ref, v_ref, gate_ref, gq_ref, gk_ref, g_ref,
         out_ref, kn_ref, k_scr, v_scr) = refs
    rows = 256

    @pl.when(pl.program_id(1) == 0)
    def _():
        for r in range(seq_len // rows):
            sl = pl.ds(r * rows, rows)
            kn = _group_rms(k_ref[sl, :], g_ref[...], DH) * gk_ref[...]
            if latent:
                kn = _rope(kn, cosk_ref[sl, :], sink_ref[sl, :])
            else:
                kn_ref[sl, :] = kn
            k_scr[sl, :] = kn.astype(BF16)
            v_scr[sl, :] = v_ref[sl, :].astype(BF16)
        if latent:
            k_scr[pl.ds(seq_len, past_len), :] = kc_ref[0].astype(BF16)
            v_scr[pl.ds(seq_len, past_len), :] = vc_ref[0].astype(BF16)

    lam = lam_ref[0]
    qn = _group_rms(q_ref[...], g_ref[...], DH) * gq_ref[...]
    if latent:
        qn = _rope(qn, cosq_ref[...], sinq_ref[...])
    qn = qn * (DH ** -0.5)
    lane = lax.broadcasted_iota(jnp.int32, (1, 2 * DH), 1)
    for h in range(HEADS):
        hs = slice(h * 2 * DH, (h + 1) * 2 * DH)
        qh = qn[:, hs]
        kh = k_scr[:, hs]
        s1 = _dot_nt(jnp.where(lane < DH, qh, 0.0).astype(BF16), kh)
        s2 = _dot_nt(jnp.where(lane >= DH, qh, 0.0).astype(BF16), kh)
        e1 = jnp.exp(s1 - jnp.max(s1, axis=-1, keepdims=True))
        e2 = jnp.exp(s2 - jnp.max(s2, axis=-1, keepdims=True))
        l1 = jnp.sum(e1, axis=-1, keepdims=True)
        l2 = jnp.sum(e2, axis=-1, keepdims=True)
        w = e1 - (lam * l1 / l2) * e2
        o = _dot(w.astype(BF16), v_scr[:, hs]) / l1
        ms = jnp.mean(o * o, axis=-1, keepdims=True)
        y = o * lax.rsqrt(ms + EPS) * (1.0 - lam_init) * _silu(gate_ref[:, hs])
        out_ref[:, hs] = y.astype(out_ref.dtype)


def _diff_attention(z_diff, lam, gq, gk, g512, batch, seq_len, latent, lam_init,
                    rope=None, ctx=None):
    m = batch * seq_len
    nq = seq_len // TQ
    past_len = ctx[0].shape[1] if latent else 0
    qmap = lambda col: (lambda b, i: (b * nq + i, col))
    kvmap = lambda col: (lambda b, i: (b, col))
    const2 = lambda b, i: (0, 0)
    in_specs = [
        pl.BlockSpec(memory_space=pltpu.SMEM),
        pl.BlockSpec((TQ, DIFF_W), qmap(0)),
        pl.BlockSpec((seq_len, DIFF_W), kvmap(1)),
        pl.BlockSpec((seq_len, DIFF_W), kvmap(2)),
        pl.BlockSpec((TQ, DIFF_W), qmap(3)),
        pl.BlockSpec((1, DIFF_W), const2),
        pl.BlockSpec((1, DIFF_W), const2),
        pl.BlockSpec((DIFF_W, DIFF_W), const2),
    ]
    args = [lam, z_diff, z_diff, z_diff, z_diff, gq, gk, g512]
    out_specs = [pl.BlockSpec((TQ, DIFF_W), qmap(0))]
    out_shape = [jax.ShapeDtypeStruct((m, DIFF_W), BF16)]
    if latent:
        in_specs += [
            pl.BlockSpec((TQ, DIFF_W), lambda b, i: (i, 0)),
            pl.BlockSpec((TQ, DIFF_W), lambda b, i: (i, 0)),
            pl.BlockSpec((seq_len, DIFF_W), const2),
            pl.BlockSpec((seq_len, DIFF_W), const2),
            pl.BlockSpec((1, past_len, DIFF_W), lambda b, i: (b, 0, 0)),
            pl.BlockSpec((1, past_len, DIFF_W), lambda b, i: (b, 0, 0)),
        ]
        args += [rope[0], rope[1], rope[0], rope[1], ctx[0], ctx[1]]
    else:
        out_specs.append(pl.BlockSpec((seq_len, DIFF_W), lambda b, i: (b, 0)))
        out_shape.append(jax.ShapeDtypeStruct((m, DIFF_W), F32))
    return pl.pallas_call(
        functools.partial(_attn_kernel, seq_len=seq_len, past_len=past_len,
                          latent=latent, lam_init=lam_init),
        grid=(batch, nq),
        in_specs=in_specs,
        out_specs=out_specs,
        out_shape=out_shape,
        scratch_shapes=[
            pltpu.VMEM((seq_len + past_len, DIFF_W), BF16),
            pltpu.VMEM((seq_len + past_len, DIFF_W), BF16),
        ],
        compiler_params=_params(2),
        name="attn_latent" if latent else "attn_ctx",
    )(*args)


def _hgrn_kernel(*refs, seq_len, latent):
    if latent:
        (z_ref, lb_ref, tri_ref, g_ref, lm_ref, s0f_ref, s0b_ref,
         out_ref, o_scr, st_scr, p_scr, r_scr) = refs
        s0_refs = (s0f_ref, s0b_ref)
        fin_refs = (None, None)
    else:
        (z_ref, lb_ref, tri_ref, g_ref, lm_ref,
         out_ref, sf_ref, sb_ref, o_scr, st_scr, p_scr, r_scr) = refs
        s0_refs = (None, None)
        fin_refs = (sf_ref, sb_ref)
    C, SUB = HG_C, HG_SUB
    n_chunks = seq_len // C
    n_sub = C // SUB
    lh = _lane_head()
    rowi = lax.broadcasted_iota(jnp.int32, (C, 1), 0)
    subi = lax.broadcasted_iota(jnp.int32, (SUB, 1), 0)

    def run_direction(bwd):
        d = 1 if bwd else 0
        fcol = (2 if bwd else 1) * PACK_W
        if latent:
            st_scr[...] = s0_refs[d][0]
        else:
            st_scr[...] = jnp.zeros((PACK_W, PACK_W), F32)

        def body(i, carry):
            r0 = pl.multiple_of(((n_chunks - 1 - i) if bwd else i) * C, C)
            rows = pl.ds(r0, C)
            lb = lb_ref[...]
            q = z_ref[rows, 0:PACK_W] * (DH ** -0.5)
            f = lb + (1.0 - lb) * jax.nn.sigmoid(z_ref[rows, fcol:fcol + PACK_W])
            lf = jnp.log(f)
            kk = 1.0 - f
            v = z_ref[rows, 3 * PACK_W:4 * PACK_W]

            hi = lf.astype(BF16)
            r1 = lf - hi.astype(F32)
            mid = r1.astype(BF16)
            lo = (r1 - mid.astype(F32)).astype(BF16)
            cs = _dot(tri_ref[d], jnp.concatenate([hi, mid, lo], axis=1))
            bc = cs[:, 0:PACK_W] + cs[:, PACK_W:2 * PACK_W] + cs[:, 2 * PACK_W:]
            last = bc[0:1] if bwd else bc[C - 1:C]

            st = st_scr[...]
            o = _dot_nt((q * jnp.exp(bc)).astype(BF16), st.astype(BF16))
            ks = (kk * jnp.exp(last - bc)).astype(BF16)
            st_scr[...] = (st * jnp.exp(last) + _dot_tn(v.astype(BF16), ks)) * g_ref[...].astype(F32)

            a = jnp.zeros((C, HEADS * C), F32)
            for li, m in enumerate(HG_LEVELS):
                pieces = []
                for blk in range(C // (2 * m)):
                    rr = 2 * m * blk + (m if bwd else m - 1)
                    pieces.append(jnp.broadcast_to(bc[rr:rr + 1], (2 * m, PACK_W)))
                anchor = jnp.concatenate(pieces, axis=0) if len(pieces) > 1 else pieces[0]
                e = jnp.exp(-jnp.abs(bc - anchor))
                upper = (rowi & (2 * m - 1)) >= m
                later = jnp.logical_not(upper) if bwd else upper
                qs = jnp.where(later, q * e, 0.0).astype(BF16)
                kf = jnp.where(later, 0.0, kk * e)
                ks_rep = jnp.concatenate(
                    [jnp.where(lh == h, kf, 0.0) for h in range(HEADS)], axis=0).astype(BF16)
                a = a + _dot_nt(qs, ks_rep) * lm_ref[li]
            v_rep = jnp.concatenate(
                [jnp.where(lh == h, v, 0.0) for h in range(HEADS)], axis=0).astype(BF16)
            o = o + _dot(a.astype(BF16), v_rep)

            for si in range(n_sub):
                t0 = si * SUB
                qt = q[t0:t0 + SUB]
                bt = bc[t0:t0 + SUB]
                for j in range(SUB):
                    s = t0 + j
                    kb = jnp.broadcast_to(kk[s:s + 1], (SUB, PACK_W))
                    bb = jnp.broadcast_to(bc[s:s + 1], (SUB, PACK_W))
                    p_scr[pl.ds(s * SUB, SUB), :] = qt * kb * jnp.exp(jnp.minimum(bt - bb, 0.0))
            r_scr[...] = _dot(p_scr[...].astype(BF16), g_ref[...])
            diag = []
            for si in range(n_sub):
                t0 = si * SUB
                acc = jnp.zeros((SUB, PACK_W), F32)
                for j in range(SUB):
                    s = t0 + j
                    vb = jnp.broadcast_to(v[s:s + 1], (SUB, PACK_W))
                    valid = (subi <= j) if bwd else (subi >= j)
                    acc = acc + jnp.where(valid, r_scr[pl.ds(s * SUB, SUB), :], 0.0) * vb
                diag.append(acc)
            o = o + jnp.concatenate(diag, axis=0)

            if not bwd:
                o_scr[rows, :] = o
            else:
                tot = o + o_scr[rows, :]
                gate = z_ref[rows, 4 * PACK_W:5 * PACK_W]
                y = _group_rms(tot, g_ref[...], DH) * _silu(gate)
                out_ref[rows, :] = y.astype(out_ref.dtype)
            return carry

        lax.fori_loop(0, n_chunks, body, 0)
        if not latent:
            fin_refs[d][0] = st_scr[...]

    run_direction(False)
    run_direction(True)


def _hgrn(z_hg, lb, consts, batch, seq_len, latent, states=None):
    m = batch * seq_len
    tri, g256, lm = consts
    row = lambda b: (b, 0)
    in_specs = [
        pl.BlockSpec((seq_len, HG_COLS), row),
        pl.BlockSpec((1, PACK_W), lambda b: (0, 0)),
        pl.BlockSpec((2, HG_C, HG_C), lambda b: (0, 0, 0)),
        pl.BlockSpec((PACK_W, PACK_W), lambda b: (0, 0)),
        pl.BlockSpec((len(HG_LEVELS), HG_C, HEADS * HG_C), lambda b: (0, 0, 0)),
    ]
    args = [z_hg, lb, tri, g256, lm]
    out_specs = [pl.BlockSpec((seq_len, PACK_W), row)]
    out_shape = [jax.ShapeDtypeStruct((m, PACK_W), BF16)]
    state_spec = pl.BlockSpec((1, PACK_W, PACK_W), lambda b: (b, 0, 0))
    if latent:
        in_specs += [state_spec] * 2
        args += [states[0], states[1]]
    else:
        out_specs += [state_spec] * 2
        out_shape += [jax.ShapeDtypeStruct((batch, PACK_W, PACK_W), F32)] * 2
    return pl.pallas_call(
        functools.partial(_hgrn_kernel, seq_len=seq_len, latent=latent),
        grid=(batch,),
        in_specs=in_specs,
        out_specs=out_specs,
        out_shape=out_shape,
        scratch_shapes=[
            pltpu.VMEM((seq_len, PACK_W), F32),
            pltpu.VMEM((PACK_W, PACK_W), F32),
            pltpu.VMEM((HG_C * HG_SUB, PACK_W), F32),
            pltpu.VMEM((HG_C * HG_SUB, PACK_W), F32),
        ],
        compiler_params=_params(1),
        name="hgrn_latent" if latent else "hgrn_ctx",
    )(*args)


def _block_ones(n):
    idx = np.arange(n) // DH
    return jnp.asarray((idx[:, None] == idx[None, :]).astype(np.float32), dtype=BF16)


def _hgrn_consts():
    C = HG_C
    t = np.arange(C)
    tri_f = (t[None, :] <= t[:, None]).astype(np.float32)
    tri_b = (t[None, :] >= t[:, None]).astype(np.float32)
    lms = []
    for m in HG_LEVELS:
        same = (t[:, None] // (2 * m)) == (t[None, :] // (2 * m))
        lms.append(np.tile(same.astype(np.float32), (1, HEADS)))
    return (jnp.asarray(np.stack([tri_f, tri_b]), dtype=BF16), _block_ones(PACK_W),
            jnp.asarray(np.stack(lms), dtype=F32))


def _rope_tables(seq_len, width):
    t = np.arange(seq_len)
    pos = np.stack([t // GRID_W, t % GRID_W], axis=1).astype(np.float32)
    j = np.arange(DH)
    axis = j // 32
    jj = j % 32
    inv = (ROPE_BASE ** (-(jj % 16).astype(np.float64) / 16.0)).astype(np.float32)
    ang = (pos[:, axis] * inv[None, :]).astype(np.float64)
    cos = np.cos(ang)
    sin = np.where(jj < 16, -np.sin(ang), np.sin(ang))
    reps = width // DH
    return (jnp.asarray(np.tile(cos, (1, reps)), dtype=F32),
            jnp.asarray(np.tile(sin, (1, reps)), dtype=F32))


def _to_blockdiag_t(s):
    b = s.shape[0]
    eye = jnp.eye(HEADS, dtype=s.dtype)
    return jnp.einsum('bhde,hg->bhegd', s, eye).reshape(b, PACK_W, PACK_W)


def _from_blockdiag_t(st):
    b = st.shape[0]
    x = st.reshape(b, HEADS, DH, HEADS, DH)
    blocks = jnp.stack([x[:, h, :, h, :] for h in range(HEADS)], axis=1)
    return blocks.swapaxes(-1, -2)


def _layer(x2d, batch, seq_len, mod, l, p, latent, ctx):
    shift, scale, gate = mod
    z_ret, z_diff, z_hg = _inproj(x2d, p['norm_g'][l], scale, shift, p['w_in'][l], seq_len)
    lam_init = 0.8 - 0.6 * math.exp(-0.3 * l)
    if latent:
        k_ctx, v_ctx, s_rf, s_rb, s_hf, s_hb = ctx
        r = _retention(z_ret, p['log_g'][l], p['g256'], batch, seq_len, True,
                       rope=p['rope256'], states=(s_rf, s_rb))[0]
        a = _diff_attention(z_diff, p['lam'][l], p['gq'][l], p['gk'][l], p['g512'], batch,
                            seq_len, True, lam_init, rope=p['rope512'], ctx=(k_ctx, v_ctx))[0]
        hgo = _hgrn(z_hg, p['lb'][l], p['hg_consts'], batch, seq_len, True,
                    states=(s_hf, s_hb))[0]
        extras = None
    else:
        r, s_rf, s_rb = _retention(z_ret, p['log_g'][l], p['g256'], batch, seq_len, False)
        a, kn = _diff_attention(z_diff, p['lam'][l], p['gq'][l], p['gk'][l], p['g512'], batch,
                                seq_len, False, lam_init)
        hgo, s_hf, s_hb = _hgrn(z_hg, p['lb'][l], p['hg_consts'], batch, seq_len, False)
        dv = z_diff[:, 2 * DIFF_W:3 * DIFF_W]
        extras = (kn, dv, s_rf, s_rb, s_hf, s_hb)
    y = _outproj(x2d, r, a, hgo, gate, p['w_out'][l], seq_len)
    return y, extras


def kernel(x_prompt, x_sample, c, c_ctx, cache_diff_k, cache_diff_v, state_ret_fwd,
           state_ret_bwd, state_hgrn_fwd, state_hgrn_bwd, norm_g, w_ada, b_ada, w_in,
           w_out, ret_decay_logit, diff_qn_g, diff_kn_g, diff_lambda, hgrn_lb_logit):
    batch, seq, _ = x_prompt.shape
    dec_batch, dec_seq, _ = x_sample.shape
    past_len = cache_diff_k.shape[2]

    cv = jnp.zeros((8, D_MODEL), F32).at[0].set(c_ctx).at[1:1 + dec_batch].set(c)
    mods = _ada(cv, w_ada, b_ada)

    lb_all = jax.nn.softmax(hgrn_lb_logit.astype(F32), axis=0)
    lb_all = jnp.cumsum(lb_all, axis=0) - lb_all[0]
    lp = diff_lambda.astype(F32)
    lam_inits = jnp.asarray([0.8 - 0.6 * math.exp(-0.3 * l) for l in range(DEPTH)], F32)
    lam = (jnp.exp(jnp.sum(lp[:, 0] * lp[:, 1], axis=-1))
           - jnp.exp(jnp.sum(lp[:, 2] * lp[:, 3], axis=-1)) + lam_inits)
    p = {
        'norm_g': norm_g.reshape(DEPTH, 1, D_MODEL),
        'w_in': w_in.astype(BF16),
        'w_out': w_out.astype(BF16),
        'log_g': jax.nn.log_sigmoid(ret_decay_logit.astype(F32)),
        'lam': lam.reshape(DEPTH, 1),
        'gq': jnp.tile(diff_qn_g, (1, DIFF_W // DH)).reshape(DEPTH, 1, DIFF_W),
        'gk': jnp.tile(diff_kn_g, (1, DIFF_W // DH)).reshape(DEPTH, 1, DIFF_W),
        'lb': lb_all.reshape(DEPTH, 1, PACK_W),
        'g256': _block_ones(PACK_W),
        'g512': _block_ones(DIFF_W),
        'hg_consts': _hgrn_consts(),
        'rope256': _rope_tables(dec_seq, PACK_W),
        'rope512': _rope_tables(dec_seq, DIFF_W),
    }

    def split_mod(rows):
        return tuple(rows[:, None, j * D_MODEL:(j + 1) * D_MODEL] for j in range(3))

    y = x_prompt.reshape(batch * seq, D_MODEL)
    per_layer = []
    for l in range(DEPTH):
        y, extras = _layer(y, batch, seq, split_mod(mods[l, 0:1]), l, p, False, None)
        per_layer.append(extras)
    y_prompt = y.reshape(batch, seq, D_MODEL)
    new_k = jnp.stack([t[0].reshape(batch, seq, HEADS, 2, DH) for t in per_layer], axis=1)
    new_v = jnp.stack([t[1].reshape(batch, seq, HEADS, 2 * DH) for t in per_layer], axis=1)
    new_states = [jnp.stack([_from_blockdiag_t(t[i]) for t in per_layer], axis=1)
                  for i in range(2, 6)]

    y = x_sample.reshape(dec_batch * dec_seq, D_MODEL)
    for l in range(DEPTH):
        ctx = (cache_diff_k[:, l].reshape(dec_batch, past_len, DIFF_W),
               cache_diff_v[:, l].reshape(dec_batch, past_len, DIFF_W),
               _to_blockdiag_t(state_ret_fwd[:, l]), _to_blockdiag_t(state_ret_bwd[:, l]),
               _to_blockdiag_t(state_hgrn_fwd[:, l]), _to_blockdiag_t(state_hgrn_bwd[:, l]))
        y, _ = _layer(y, dec_batch, dec_seq, split_mod(mods[l, 1:1 + dec_batch]), l, p, True, ctx)
    y_sample = y.reshape(dec_batch, dec_seq, D_MODEL)

    return (y_prompt, y_sample, new_k, new_v, *new_states)
```

```python
import functools
import math

import numpy as np
import jax
import jax.numpy as jnp
from jax import lax
from jax.experimental import pallas as pl
from jax.experimental.pallas import tpu as pltpu

F32 = jnp.float32
BF16 = jnp.bfloat16

D_MODEL = 1024
DEPTH = 2
GRID_W = 64
HEADS = 4
DH = 64
PACK_W = HEADS * DH
DIFF_W = 512
DIFF_DV = 128
RET_COLS = 4 * PACK_W
DIFF_COLS = 4 * DIFF_W
HG_COLS = 5 * PACK_W
IN_W = RET_COLS + DIFF_COLS + HG_COLS
MIX_W = PACK_W + DIFF_W + PACK_W
ROPE_BASE = 10000.0
ROPE_PAIR = 16
EPS = 1e-6
LOG2E = 1.4426950408889634
NEG_BIG = -1e30

RET_C = 256
HG_C = 64
HG_SUB = 8
HG_LEVELS = (8, 16, 32)
ROW_BLK = 256
TQ = 256
TM_IN = 256
TM_OUT = 512

VMEM_LIMIT = 56 * 1024 * 1024


def _params(n_axes):
    return pltpu.CompilerParams(
        dimension_semantics=("arbitrary",) * n_axes, vmem_limit_bytes=VMEM_LIMIT)


def _dot(a, b):
    return jnp.dot(a, b, preferred_element_type=F32)


def _dot_nt(a, b):
    return lax.dot_general(a, b, (((1,), (1,)), ((), ())), preferred_element_type=F32)


def _dot_tn(a, b):
    return lax.dot_general(a, b, (((0,), (0,)), ((), ())), preferred_element_type=F32)


def _silu(x):
    return x * jax.nn.sigmoid(x)


def _split2_dot(x, g):
    hi = x.astype(BF16)
    lo = (x - hi.astype(F32)).astype(BF16)
    return _dot(hi, g) + _dot(lo, g)


def _group_rms(x, g, width):
    ms = _split2_dot(x * x, g) * (1.0 / width)
    return x * lax.rsqrt(ms + EPS)


def _rope(x, cos, sin):
    w = x.shape[-1]
    lane = lax.broadcasted_iota(jnp.int32, (1, w), 1)
    first = (lane & (2 * ROPE_PAIR - 1)) < ROPE_PAIR
    swapped = jnp.where(first, pltpu.roll(x, w - ROPE_PAIR, 1), pltpu.roll(x, ROPE_PAIR, 1))
    return x * cos + swapped * sin


def _lane_head(w=PACK_W):
    return lax.broadcasted_iota(jnp.int32, (1, w), 1) // DH


def _state_blocks(st):
    return [st[h * DH:(h + 1) * DH, :].T[h * DH:(h + 1) * DH, :] for h in range(HEADS)]


def _row_blocks(n_rows, body):
    n_blk = n_rows // ROW_BLK
    if n_blk == 1:
        body(pl.ds(0, ROW_BLK))
    else:
        def step(i, carry):
            body(pl.ds(pl.multiple_of(i * ROW_BLK, ROW_BLK), ROW_BLK))
            return carry
        lax.fori_loop(0, n_blk, step, 0)


def _ada_kernel(cv_ref, w_ref, b_ref, o_ref):
    cv = cv_ref[...]
    o_ref[0] = _dot(_silu(cv), w_ref[0]) + b_ref[0]


def _ada(cv, w_ada, b_ada):
    tn = 1024
    n = 3 * D_MODEL
    return pl.pallas_call(
        _ada_kernel,
        grid=(DEPTH, n // tn),
        in_specs=[
            pl.BlockSpec((8, D_MODEL), lambda l, j: (0, 0)),
            pl.BlockSpec((1, D_MODEL, tn), lambda l, j: (l, 0, j)),
            pl.BlockSpec((1, 1, tn), lambda l, j: (l, 0, j)),
        ],
        out_specs=pl.BlockSpec((1, 8, tn), lambda l, j: (l, 0, j)),
        out_shape=jax.ShapeDtypeStruct((DEPTH, 8, n), F32),
        compiler_params=_params(2),
        name="ada",
    )(cv, w_ada, b_ada.reshape(DEPTH, 1, n))


def _inproj_kernel(x_ref, g_ref, sc_ref, sh_ref, w_ref, zr_ref, zd_ref, zh_ref):
    x = x_ref[...]
    ms = jnp.mean(x * x, axis=-1, keepdims=True)
    h = x * lax.rsqrt(ms + EPS) * g_ref[...]
    h = h * (1.0 + sc_ref[0]) + sh_ref[0]
    z = _dot(h.astype(BF16), w_ref[...])
    zr_ref[...] = z[:, :RET_COLS]
    zd_ref[...] = z[:, RET_COLS:RET_COLS + DIFF_COLS]
    zh_ref[...] = z[:, RET_COLS + DIFF_COLS:]


def _inproj(x2d, g, scale, shift, w_bf, seq_len):
    m = x2d.shape[0]
    per_batch = scale.shape[0] > 1
    tiles_per_seq = seq_len // TM_IN
    mod_map = (lambda i: (i // tiles_per_seq, 0, 0)) if per_batch else (lambda i: (0, 0, 0))
    return pl.pallas_call(
        _inproj_kernel,
        grid=(m // TM_IN,),
        in_specs=[
            pl.BlockSpec((TM_IN, D_MODEL), lambda i: (i, 0)),
            pl.BlockSpec((1, D_MODEL), lambda i: (0, 0)),
            pl.BlockSpec((1, 1, D_MODEL), mod_map),
            pl.BlockSpec((1, 1, D_MODEL), mod_map),
            pl.BlockSpec((D_MODEL, IN_W), lambda i: (0, 0)),
        ],
        out_specs=[
            pl.BlockSpec((TM_IN, RET_COLS), lambda i: (i, 0)),
            pl.BlockSpec((TM_IN, DIFF_COLS), lambda i: (i, 0)),
            pl.BlockSpec((TM_IN, HG_COLS), lambda i: (i, 0)),
        ],
        out_shape=[
            jax.ShapeDtypeStruct((m, RET_COLS), F32),
            jax.ShapeDtypeStruct((m, DIFF_COLS), F32),
            jax.ShapeDtypeStruct((m, HG_COLS), F32),
        ],
        compiler_params=_params(1),
        name="inproj",
    )(x2d, g, scale, shift, w_bf)


def _outproj_kernel(x_ref, r_ref, d_ref, h_ref, gate_ref, w_ref, o_ref):
    acc = _dot(r_ref[...], w_ref[0:PACK_W, :])
    acc += _dot(d_ref[...], w_ref[PACK_W:PACK_W + DIFF_W, :])
    acc += _dot(h_ref[...], w_ref[PACK_W + DIFF_W:, :])
    o_ref[...] = x_ref[...] + gate_ref[0] * acc


def _outproj(x2d, r, d, h, gate, w_bf, seq_len):
    m = x2d.shape[0]
    per_batch = gate.shape[0] > 1
    tiles_per_seq = max(seq_len // TM_OUT, 1)
    tm = min(TM_OUT, seq_len)
    mod_map = (lambda i: (i // tiles_per_seq, 0, 0)) if per_batch else (lambda i: (0, 0, 0))
    return pl.pallas_call(
        _outproj_kernel,
        grid=(m // tm,),
        in_specs=[
            pl.BlockSpec((tm, D_MODEL), lambda i: (i, 0)),
            pl.BlockSpec((tm, PACK_W), lambda i: (i, 0)),
            pl.BlockSpec((tm, DIFF_W), lambda i: (i, 0)),
            pl.BlockSpec((tm, PACK_W), lambda i: (i, 0)),
            pl.BlockSpec((1, 1, D_MODEL), mod_map),
            pl.BlockSpec((MIX_W, D_MODEL), lambda i: (0, 0)),
        ],
        out_specs=pl.BlockSpec((tm, D_MODEL), lambda i: (i, 0)),
        out_shape=jax.ShapeDtypeStruct((m, D_MODEL), F32),
        compiler_params=_params(1),
        name="outproj",
    )(x2d, r, d, h, gate, w_bf)


def _ret_kernel(*refs, seq_len, latent):
    if latent:
        (lg_ref, z_ref, g_ref, cos_ref, sin_ref, s0f_ref, s0b_ref,
         out_ref, mask_scr, st_scr, o_scr) = refs
    else:
        lg_ref, z_ref, g_ref, out_ref, sf_ref, sb_ref, mask_scr, st_scr, o_scr = refs
    C = RET_C
    n_chunks = seq_len // C
    lh = _lane_head()

    def lane_vec(direction):
        v = jnp.zeros((1, PACK_W), F32)
        for h in range(HEADS):
            v = jnp.where(lh == h, lg_ref[direction, h], v)
        return v

    lgf, lgb = lane_vec(0), lane_vec(1)
    tau = lax.broadcasted_iota(jnp.int32, (C, 1), 0).astype(F32)
    qdec_f = jnp.exp((tau + 1.0) * lgf)
    qdec_b = jnp.exp((C - tau) * lgb)
    kdec_f = jnp.exp((C - 1.0 - tau) * lgf)
    kdec_b = jnp.exp(tau * lgb)
    sdec_f = jnp.exp(float(C) * lgf)
    sdec_b = jnp.exp(float(C) * lgb)
    gmask = g_ref[...].astype(F32)

    dd = (lax.broadcasted_iota(jnp.int32, (C, C), 0)
          - lax.broadcasted_iota(jnp.int32, (C, C), 1)).astype(F32)
    for h in range(HEADS):
        mf = jnp.where(dd >= 0, jnp.exp(jnp.maximum(dd, 0.0) * lg_ref[0, h]), 0.0)
        mb = jnp.where(dd <= 0, jnp.exp(jnp.maximum(-dd, 0.0) * lg_ref[1, h]), 0.0)
        mask_scr[h] = mf + mb

    def load(r0):
        q = z_ref[pl.ds(r0, C), 0:PACK_W]
        k = z_ref[pl.ds(r0, C), PACK_W:2 * PACK_W] * (DH ** -0.5)
        v = z_ref[pl.ds(r0, C), 2 * PACK_W:3 * PACK_W]
        if latent:
            cos = cos_ref[pl.ds(r0, C), :]
            sin = sin_ref[pl.ds(r0, C), :]
            q = _rope(q, cos, sin)
            k = _rope(k, cos, sin)
        return q, k, v

    def intra(q, k, v):
        kb = k.astype(BF16)
        vb = v.astype(BF16)
        o = jnp.zeros((C, PACK_W), F32)
        for h in range(HEADS):
            hm = lh == h
            qh = jnp.where(hm, q, 0.0).astype(BF16)
            a = _dot_nt(qh, kb) * mask_scr[h]
            o = o + jnp.where(hm, _dot(a.astype(BF16), vb), 0.0)
        return o

    def finish(o, r0):
        gate = z_ref[pl.ds(r0, C), 3 * PACK_W:4 * PACK_W]
        y = _group_rms(o, g_ref[...], DH) * _silu(gate)
        out_ref[pl.ds(r0, C), :] = y.astype(out_ref.dtype)

    def state_step(st, k, v, kdec, sdec):
        upd = _dot_tn(v.astype(BF16), (k * kdec).astype(BF16))
        return (st * sdec + upd) * gmask

    if not latent:
        q, k, v = load(0)
        finish(intra(q, k, v), 0)
        zero = jnp.zeros((PACK_W, PACK_W), F32)
        for ref, kdec, sdec in ((sf_ref, kdec_f, sdec_f), (sb_ref, kdec_b, sdec_b)):
            for h, blk in enumerate(_state_blocks(state_step(zero, k, v, kdec, sdec))):
                ref[0, h] = blk
        return

    st_scr[...] = s0b_ref[0]

    def bwd_body(i, carry):
        r0 = pl.multiple_of((n_chunks - 1 - i) * C, C)
        q, k, v = load(r0)
        st = st_scr[...]
        o_scr[pl.ds(r0, C), :] = _dot_nt((q * qdec_b).astype(BF16), st.astype(BF16))
        st_scr[...] = state_step(st, k, v, kdec_b, sdec_b)
        return carry

    lax.fori_loop(0, n_chunks, bwd_body, 0)

    st_scr[...] = s0f_ref[0]

    def fwd_body(i, carry):
        r0 = pl.multiple_of(i * C, C)
        q, k, v = load(r0)
        st = st_scr[...]
        o = intra(q, k, v) + o_scr[pl.ds(r0, C), :]
        o = o + _dot_nt((q * qdec_f).astype(BF16), st.astype(BF16))
        finish(o, r0)
        st_scr[...] = state_step(st, k, v, kdec_f, sdec_f)
        return carry

    lax.fori_loop(0, n_chunks, fwd_body, 0)


def _retention(z_ret, log_g, g256, batch, seq_len, latent, rope=None, states=None):
    m = batch * seq_len
    row = lambda b: (b, 0)
    const2 = lambda b: (0, 0)
    in_specs = [
        pl.BlockSpec(memory_space=pltpu.SMEM),
        pl.BlockSpec((seq_len, RET_COLS), row),
        pl.BlockSpec((PACK_W, PACK_W), const2),
    ]
    args = [log_g, z_ret, g256]
    out_specs = [pl.BlockSpec((seq_len, PACK_W), row)]
    out_shape = [jax.ShapeDtypeStruct((m, PACK_W), BF16)]
    if latent:
        state_spec = pl.BlockSpec((1, PACK_W, PACK_W), lambda b: (b, 0, 0))
        in_specs += [pl.BlockSpec((seq_len, PACK_W), const2)] * 2 + [state_spec] * 2
        args += [rope[0], rope[1], states[0], states[1]]
    else:
        out_specs += [pl.BlockSpec((1, HEADS, DH, DH), lambda b: (b, 0, 0, 0))] * 2
        out_shape += [jax.ShapeDtypeStruct((batch, HEADS, DH, DH), F32)] * 2
    return pl.pallas_call(
        functools.partial(_ret_kernel, seq_len=seq_len, latent=latent),
        grid=(batch,),
        in_specs=in_specs,
        out_specs=out_specs,
        out_shape=out_shape,
        scratch_shapes=[
            pltpu.VMEM((HEADS, RET_C, RET_C), F32),
            pltpu.VMEM((PACK_W, PACK_W), F32),
            pltpu.VMEM((seq_len, PACK_W), F32),
        ],
        compiler_params=_params(1),
        name="ret_latent" if latent else "ret_ctx",
    )(*args)


def _attn_kernel(*refs, seq_len, past_len, latent, lam_init, layer):
    if latent:
        (lam_ref, q_ref, k_ref, v_ref, gate_ref, gq_ref, gk_ref, g_ref,
         cosq_ref, sinq_ref, cosk_ref, sink_ref, kc_ref, vc_ref,
         out_ref, k_scr, v_scr) = refs
    else:
        (lam_ref, q_ref, k_ref, v_ref, gate_ref, gq_ref, gk_ref, g_ref) = refs[:8]
        out_ref, kn_ref, vn_ref, k_scr, v_scr = refs[-5:]
    ones = jnp.ones((ROW_BLK, DIFF_DV), BF16)

    def put_values(sl, v):
        for h in range(HEADS):
            v_scr[sl, 2 * h * DIFF_DV:(2 * h + 1) * DIFF_DV] = (
                v[:, h * DIFF_DV:(h + 1) * DIFF_DV].astype(BF16))
            v_scr[sl, (2 * h + 1) * DIFF_DV:(2 * h + 2) * DIFF_DV] = ones

    @pl.when(pl.program_id(1) == 0)
    def _():
        for r in range(seq_len // ROW_BLK):
            sl = pl.ds(r * ROW_BLK, ROW_BLK)
            kn = _group_rms(k_ref[sl, :], g_ref[...], DH) * gk_ref[...]
            v = v_ref[sl, :]
            if latent:
                kn = _rope(kn, cosk_ref[sl, :], sink_ref[sl, :])
            else:
                kn_ref[0, 0, sl, :] = kn
                vn_ref[0, 0, sl, :] = v
            k_scr[sl, :] = kn.astype(BF16)
            put_values(sl, v)
        if latent:
            for r in range(past_len // ROW_BLK):
                src = pl.ds(r * ROW_BLK, ROW_BLK)
                dst = pl.ds(seq_len + r * ROW_BLK, ROW_BLK)
                k_scr[dst, :] = kc_ref[0, src, :].astype(BF16)
                put_values(dst, vc_ref[0, src, :])

    lam = lam_ref[0]
    qn = _group_rms(q_ref[...], g_ref[...], DH) * gq_ref[...]
    if latent:
        qn = _rope(qn, cosq_ref[...], sinq_ref[...])
    qn = qn * (DH ** -0.5 * LOG2E)
    lane = lax.broadcasted_iota(jnp.int32, (1, 2 * DH), 1)
    for h in range(HEADS):
        hs = slice(h * 2 * DH, (h + 1) * 2 * DH)
        vx = v_scr[:, 2 * h * DIFF_DV:(2 * h + 2) * DIFF_DV]
        qh = qn[:, hs]
        kh = k_scr[:, hs]
        parts = []
        for first_map in (True, False):
            qm = jnp.where((lane < DH) == first_map, qh, 0.0).astype(BF16)
            s = _dot_nt(qm, kh)
            e = jnp.exp2(s - jnp.max(s, axis=-1, keepdims=True)).astype(BF16)
            ox = _dot(e, vx)
            parts.append(ox[:, :DIFF_DV] / ox[:, DIFF_DV:DIFF_DV + 1])
        o = parts[0] - lam * parts[1]
        ms = jnp.mean(o * o, axis=-1, keepdims=True)
        y = o * lax.rsqrt(ms + EPS) * (1.0 - lam_init) * _silu(gate_ref[:, hs])
        out_ref[:, hs] = y.astype(out_ref.dtype)


def _diff_attention(z_diff, lam, gq, gk, g512, batch, seq_len, latent, lam_init, layer,
                    rope=None, ctx=None, caches=None):
    m = batch * seq_len
    nq = seq_len // TQ
    past_len = ctx[0].shape[1] if latent else 0
    qmap = lambda col: (lambda b, i: (b * nq + i, col))
    kvmap = lambda col: (lambda b, i: (b, col))
    const2 = lambda b, i: (0, 0)
    in_specs = [
        pl.BlockSpec(memory_space=pltpu.SMEM),
        pl.BlockSpec((TQ, DIFF_W), qmap(0)),
        pl.BlockSpec((seq_len, DIFF_W), kvmap(1)),
        pl.BlockSpec((seq_len, DIFF_W), kvmap(2)),
        pl.BlockSpec((TQ, DIFF_W), qmap(3)),
        pl.BlockSpec((1, DIFF_W), const2),
        pl.BlockSpec((1, DIFF_W), const2),
        pl.BlockSpec((DIFF_W, DIFF_W), const2),
    ]
    args = [lam, z_diff, z_diff, z_diff, z_diff, gq, gk, g512]
    out_specs = [pl.BlockSpec((TQ, DIFF_W), qmap(0))]
    out_shape = [jax.ShapeDtypeStruct((m, DIFF_W), BF16)]
    aliases = {}
    if latent:
        in_specs += [
            pl.BlockSpec((TQ, DIFF_W), lambda b, i: (i, 0)),
            pl.BlockSpec((TQ, DIFF_W), lambda b, i: (i, 0)),
            pl.BlockSpec((seq_len, DIFF_W), const2),
            pl.BlockSpec((seq_len, DIFF_W), const2),
            pl.BlockSpec((1, past_len, DIFF_W), lambda b, i: (b, 0, 0)),
            pl.BlockSpec((1, past_len, DIFF_W), lambda b, i: (b, 0, 0)),
        ]
        args += [rope[0], rope[1], rope[0], rope[1], ctx[0], ctx[1]]
    else:
        cache_spec = pl.BlockSpec((1, 1, seq_len, DIFF_W), lambda b, i: (b, layer, 0, 0))
        out_specs += [cache_spec] * 2
        out_shape += [jax.ShapeDtypeStruct((batch, DEPTH, seq_len, DIFF_W), F32)] * 2
        if caches is not None:
            aliases = {len(args): 1, len(args) + 1: 2}
            in_specs += [pl.BlockSpec(memory_space=pl.ANY)] * 2
            args += list(caches)
    return pl.pallas_call(
        functools.partial(_attn_kernel, seq_len=seq_len, past_len=past_len,
                          latent=latent, lam_init=lam_init, layer=layer),
        grid=(batch, nq),
        in_specs=in_specs,
        out_specs=out_specs,
        out_shape=out_shape,
        input_output_aliases=aliases,
        scratch_shapes=[
            pltpu.VMEM((seq_len + past_len, DIFF_W), BF16),
            pltpu.VMEM((seq_len + past_len, 2 * DIFF_W), BF16),
        ],
        compiler_params=_params(2),
        name="attn_latent" if latent else "attn_ctx",
    )(*args)


def _hgrn_kernel(*refs, seq_len, latent):
    if latent:
        (z_ref, lb_ref, tri_ref, g_ref, lm_ref, dm_ref, s0f_ref, s0b_ref,
         out_ref, kk_scr, bc_scr, o_scr, st_scr, p_scr) = refs
    else:
        (z_ref, lb_ref, tri_ref, g_ref, lm_ref, dm_ref,
         out_ref, sf_ref, sb_ref, kk_scr, bc_scr, o_scr, st_scr, p_scr) = refs
    C, SUB = HG_C, HG_SUB
    n_chunks = seq_len // C
    lh = _lane_head()
    tile_row = lax.broadcasted_iota(jnp.int32, (2 * SUB, 1), 0) & (SUB - 1)

    def gates(rows):
        lb = lb_ref[...]
        for d in range(2):
            f = lb + (1.0 - lb) * jax.nn.sigmoid(z_ref[rows, (1 + d) * PACK_W:(2 + d) * PACK_W])
            kk_scr[d, rows, :] = 1.0 - f
            l2 = jnp.log(f) * LOG2E
            hi = l2.astype(BF16)
            r1 = l2 - hi.astype(F32)
            mid = r1.astype(BF16)
            lo = (r1 - mid.astype(F32)).astype(BF16)
            cs = _dot(tri_ref[d], jnp.concatenate([hi, mid, lo], axis=1))
            bc_scr[d, rows, :] = cs[:, 0:PACK_W] + cs[:, PACK_W:2 * PACK_W] + cs[:, 2 * PACK_W:]

    _row_blocks(seq_len, gates)

    for d in range(2):
        if latent:
            st_scr[d] = (s0f_ref, s0b_ref)[d][0]
        else:
            st_scr[d] = jnp.zeros((PACK_W, PACK_W), F32)
    gmask = g_ref[...].astype(F32)

    def chunk(d, c):
        bwd = d == 1
        r0 = pl.multiple_of(c * C, C)
        rows = pl.ds(r0, C)
        q = z_ref[rows, 0:PACK_W] * (DH ** -0.5)
        v = z_ref[rows, 3 * PACK_W:4 * PACK_W]
        kk = kk_scr[d, rows, :]
        bc = bc_scr[d, rows, :]
        last = bc[0:1] if bwd else bc[C - 1:C]

        st = st_scr[d]
        o = _dot_nt((q * jnp.exp2(bc)).astype(BF16), st.astype(BF16))
        ks = (kk * jnp.exp2(last - bc)).astype(BF16)
        st_scr[d] = st * jnp.exp2(last) + _dot_tn(v.astype(BF16), ks) * gmask

        a = jnp.zeros((C, HEADS * C), F32)
        for li, m in enumerate(HG_LEVELS):
            pieces = []
            for blk in range(C // (2 * m)):
                rr = 2 * m * blk + (m if bwd else m - 1)
                pieces.append(jnp.broadcast_to(bc[rr:rr + 1], (2 * m, PACK_W)))
            anchor = jnp.concatenate(pieces, axis=0) if len(pieces) > 1 else pieces[0]
            e = jnp.exp2(-jnp.abs(bc - anchor))
            qs = (q * e).astype(BF16)
            kf = (kk * e).astype(BF16)
            ks_rep = jnp.concatenate(
                [jnp.where(lh == h, kf, jnp.zeros_like(kf)) for h in range(HEADS)], axis=0)
            a = a + _dot_nt(qs, ks_rep) * lm_ref[d, li]

        for t0 in range(0, C, 2 * SUB):
            qt = q[t0:t0 + 2 * SUB]
            bt = bc[t0:t0 + 2 * SUB]
            for j in range(SUB):
                kb = jnp.concatenate(
                    [jnp.broadcast_to(kk[t0 + u * SUB + j:t0 + u * SUB + j + 1], (SUB, PACK_W))
                     for u in range(2)], axis=0)
                bb = jnp.concatenate(
                    [jnp.broadcast_to(bc[t0 + u * SUB + j:t0 + u * SUB + j + 1], (SUB, PACK_W))
                     for u in range(2)], axis=0)
                valid = (tile_row <= j) if bwd else (tile_row >= j)
                arg = jnp.where(valid, bt - bb, NEG_BIG)
                p_scr[d, pl.ds(j * C + t0, 2 * SUB), :] = (qt * kb * jnp.exp2(arg)).astype(BF16)
        r = _dot(p_scr[d], g_ref[...])
        for j in range(SUB):
            a = a + r[j * C:(j + 1) * C] * dm_ref[j]

        v_rep = jnp.concatenate(
            [jnp.where(lh == h, v, 0.0) for h in range(HEADS)], axis=0).astype(BF16)
        o_scr[d, rows, :] = o + _dot(a.astype(BF16), v_rep)

    def body(i, carry):
        chunk(0, i)
        chunk(1, n_chunks - 1 - i)
        return carry

    lax.fori_loop(0, n_chunks, body, 0)

    def finish(rows):
        tot = o_scr[0, rows, :] + o_scr[1, rows, :]
        gate = z_ref[rows, 4 * PACK_W:5 * PACK_W]
        y = _group_rms(tot, g_ref[...], DH) * _silu(gate)
        out_ref[rows, :] = y.astype(out_ref.dtype)

    _row_blocks(seq_len, finish)

    if not latent:
        for d, ref in enumerate((sf_ref, sb_ref)):
            for h, blk in enumerate(_state_blocks(st_scr[d])):
                ref[0, h] = blk


def _hgrn(z_hg, lb, consts, batch, seq_len, latent, states=None):
    m = batch * seq_len
    tri, g256, lm, dm = consts
    row = lambda b: (b, 0)
    in_specs = [
        pl.BlockSpec((seq_len, HG_COLS), row),
        pl.BlockSpec((1, PACK_W), lambda b: (0, 0)),
        pl.BlockSpec(tri.shape, lambda b: (0, 0, 0)),
        pl.BlockSpec((PACK_W, PACK_W), lambda b: (0, 0)),
        pl.BlockSpec(lm.shape, lambda b: (0, 0, 0, 0)),
        pl.BlockSpec(dm.shape, lambda b: (0, 0, 0)),
    ]
    args = [z_hg, lb, tri, g256, lm, dm]
    out_specs = [pl.BlockSpec((seq_len, PACK_W), row)]
    out_shape = [jax.ShapeDtypeStruct((m, PACK_W), BF16)]
    if latent:
        in_specs += [pl.BlockSpec((1, PACK_W, PACK_W), lambda b: (b, 0, 0))] * 2
        args += [states[0], states[1]]
    else:
        out_specs += [pl.BlockSpec((1, HEADS, DH, DH), lambda b: (b, 0, 0, 0))] * 2
        out_shape += [jax.ShapeDtypeStruct((batch, HEADS, DH, DH), F32)] * 2
    return pl.pallas_call(
        functools.partial(_hgrn_kernel, seq_len=seq_len, latent=latent),
        grid=(batch,),
        in_specs=in_specs,
        out_specs=out_specs,
        out_shape=out_shape,
        scratch_shapes=[
            pltpu.VMEM((2, seq_len, PACK_W), F32),
            pltpu.VMEM((2, seq_len, PACK_W), F32),
            pltpu.VMEM((2, seq_len, PACK_W), F32),
            pltpu.VMEM((2, PACK_W, PACK_W), F32),
            pltpu.VMEM((2, HG_C * HG_SUB, PACK_W), BF16),
        ],
        compiler_params=_params(1),
        name="hgrn_latent" if latent else "hgrn_ctx",
    )(*args)


def _block_ones(n):
    idx = np.arange(n) // DH
    return jnp.asarray((idx[:, None] == idx[None, :]).astype(np.float32), dtype=BF16)


def _hgrn_consts():
    C = HG_C
    t = np.arange(ROW_BLK)
    same_chunk = (t[:, None] // C) == (t[None, :] // C)
    tri_f = same_chunk & (t[None, :] <= t[:, None])
    tri_b = same_chunk & (t[None, :] >= t[:, None])
    tc = np.arange(C)
    lms = []
    for fwd in (True, False):
        per_level = []
        for m in HG_LEVELS:
            same = (tc[:, None] // (2 * m)) == (tc[None, :] // (2 * m))
            upper = (tc % (2 * m)) >= m
            later, earlier = (upper, ~upper) if fwd else (~upper, upper)
            per_level.append(np.tile(same & later[:, None] & earlier[None, :], (1, HEADS)))
        lms.append(np.stack(per_level))
    dms = [np.tile(tc[None, :] == (tc[:, None] // HG_SUB) * HG_SUB + j, (1, HEADS))
           for j in range(HG_SUB)]
    return (jnp.asarray(np.stack([tri_f, tri_b]).astype(np.float32), dtype=BF16),
            _block_ones(PACK_W),
            jnp.asarray(np.stack(lms).astype(np.float32), dtype=F32),
            jnp.asarray(np.stack(dms).astype(np.float32), dtype=F32))


def _rope_tables(seq_len, width):
    t = np.arange(seq_len)
    pos = np.stack([t // GRID_W, t % GRID_W], axis=1).astype(np.float32)
    j = np.arange(DH)
    axis = j // (2 * ROPE_PAIR)
    jj = j % (2 * ROPE_PAIR)
    inv = (ROPE_BASE ** (-(jj % ROPE_PAIR).astype(np.float64) / ROPE_PAIR)).astype(np.float32)
    ang = (pos[:, axis] * inv[None, :]).astype(np.float64)
    cos = np.cos(ang)
    sin = np.where(jj < ROPE_PAIR, -np.sin(ang), np.sin(ang))
    reps = width // DH
    return (jnp.asarray(np.tile(cos, (1, reps)), dtype=F32),
            jnp.asarray(np.tile(sin, (1, reps)), dtype=F32))


def _to_blockdiag_t(s):
    b = s.shape[0]
    eye = jnp.eye(HEADS, dtype=s.dtype)
    return jnp.einsum('bhde,hg->bhegd', s, eye).reshape(b, PACK_W, PACK_W)


def _layer(x2d, batch, seq_len, mod, l, p, latent, ctx, caches):
    shift, scale, gate = mod
    z_ret, z_diff, z_hg = _inproj(x2d, p['norm_g'][l], scale, shift, p['w_in'][l], seq_len)
    lam_init = 0.8 - 0.6 * math.exp(-0.3 * l)
    if latent:
        k_ctx, v_ctx, s_rf, s_rb, s_hf, s_hb = ctx
        r = _retention(z_ret, p['log_g'][l], p['g256'], batch, seq_len, True,
                       rope=p['rope256'], states=(s_rf, s_rb))[0]
        a = _diff_attention(z_diff, p['lam'][l], p['gq'][l], p['gk'][l], p['g512'], batch,
                            seq_len, True, lam_init, l, rope=p['rope512'],
                            ctx=(k_ctx, v_ctx))[0]
        hgo = _hgrn(z_hg, p['lb'][l], p['hg_consts'], batch, seq_len, True,
                    states=(s_hf, s_hb))[0]
        extras = None
    else:
        r, s_rf, s_rb = _retention(z_ret, p['log_g'][l], p['g256'], batch, seq_len, False)
        a, kcache, vcache = _diff_attention(z_diff, p['lam'][l], p['gq'][l], p['gk'][l],
                                            p['g512'], batch, seq_len, False, lam_init, l,
                                            caches=caches)
        hgo, s_hf, s_hb = _hgrn(z_hg, p['lb'][l], p['hg_consts'], batch, seq_len, False)
        extras = ((kcache, vcache), (s_rf, s_rb, s_hf, s_hb))
    y = _outproj(x2d, r, a, hgo, gate, p['w_out'][l], seq_len)
    return y, extras


def kernel(x_prompt, x_sample, c, c_ctx, cache_diff_k, cache_diff_v, state_ret_fwd,
           state_ret_bwd, state_hgrn_fwd, state_hgrn_bwd, norm_g, w_ada, b_ada, w_in,
           w_out, ret_decay_logit, diff_qn_g, diff_kn_g, diff_lambda, hgrn_lb_logit):
    batch, seq, _ = x_prompt.shape
    dec_batch, dec_seq, _ = x_sample.shape
    past_len = cache_diff_k.shape[2]

    cv = jnp.zeros((8, D_MODEL), F32).at[0].set(c_ctx).at[1:1 + dec_batch].set(c)
    mods = _ada(cv, w_ada, b_ada)

    lb_all = jax.nn.softmax(hgrn_lb_logit.astype(F32), axis=0)
    lb_all = jnp.cumsum(lb_all, axis=0) - lb_all[0]
    lp = diff_lambda.astype(F32)
    lam_inits = jnp.asarray([0.8 - 0.6 * math.exp(-0.3 * l) for l in range(DEPTH)], F32)
    lam = (jnp.exp(jnp.sum(lp[:, 0] * lp[:, 1], axis=-1))
           - jnp.exp(jnp.sum(lp[:, 2] * lp[:, 3], axis=-1)) + lam_inits)
    p = {
        'norm_g': norm_g.reshape(DEPTH, 1, D_MODEL),
        'w_in': w_in.astype(BF16),
        'w_out': w_out.astype(BF16),
        'log_g': jax.nn.log_sigmoid(ret_decay_logit.astype(F32)),
        'lam': lam.reshape(DEPTH, 1),
        'gq': jnp.tile(diff_qn_g, (1, DIFF_W // DH)).reshape(DEPTH, 1, DIFF_W),
        'gk': jnp.tile(diff_kn_g, (1, DIFF_W // DH)).reshape(DEPTH, 1, DIFF_W),
        'lb': lb_all.reshape(DEPTH, 1, PACK_W),
        'g256': _block_ones(PACK_W),
        'g512': _block_ones(DIFF_W),
        'hg_consts': _hgrn_consts(),
        'rope256': _rope_tables(dec_seq, PACK_W),
        'rope512': _rope_tables(dec_seq, DIFF_W),
    }

    def split_mod(rows):
        return tuple(rows[:, None, j * D_MODEL:(j + 1) * D_MODEL] for j in range(3))

    y = x_prompt.reshape(batch * seq, D_MODEL)
    caches = None
    states = []
    for l in range(DEPTH):
        y, (caches, st) = _layer(y, batch, seq, split_mod(mods[l, 0:1]), l, p, False, None, caches)
        states.append(st)
    y_prompt = y.reshape(batch, seq, D_MODEL)
    new_k = caches[0].reshape(batch, DEPTH, seq, HEADS, 2, DH)
    new_v = caches[1].reshape(batch, DEPTH, seq, HEADS, DIFF_DV)
    new_states = [jnp.stack([st[i] for st in states], axis=1) for i in range(4)]

    y = x_sample.reshape(dec_batch * dec_seq, D_MODEL)
    for l in range(DEPTH):
        ctx = (cache_diff_k[:, l].reshape(dec_batch, past_len, DIFF_W),
               cache_diff_v[:, l].reshape(dec_batch, past_len, DIFF_W),
               _to_blockdiag_t(state_ret_fwd[:, l]), _to_blockdiag_t(state_ret_bwd[:, l]),
               _to_blockdiag_t(state_hgrn_fwd[:, l]), _to_blockdiag_t(state_hgrn_bwd[:, l]))
        y, _ = _layer(y, dec_batch, dec_seq, split_mod(mods[l, 1:1 + dec_batch]), l, p, True,
                      ctx, None)
    y_sample = y.reshape(dec_batch, dec_seq, D_MODEL)

    return (y_prompt, y_sample, new_k, new_v, *new_states)
```

```python
import functools
import math

import numpy as np
import jax
import jax.numpy as jnp
from jax import lax
from jax.experimental import pallas as pl
from jax.experimental.pallas import tpu as pltpu

F32 = jnp.float32
BF16 = jnp.bfloat16

D_MODEL = 1024
DEPTH = 2
GRID_W = 64
HEADS = 4
DH = 64
PACK_W = HEADS * DH
DIFF_W = 512
DIFF_DV = 128
RET_COLS = 4 * PACK_W
DIFF_COLS = 4 * DIFF_W
HG_COLS = 5 * PACK_W
IN_W = RET_COLS + DIFF_COLS + HG_COLS
MIX_W = PACK_W + DIFF_W + PACK_W
ROPE_BASE = 10000.0
ROPE_PAIR = 16
EPS = 1e-6
LOG2E = 1.4426950408889634
NEG_BIG = -1e30

RET_C = 256
HG_C = 64
HG_SUB = 4
HG_LEVELS = (4, 8, 16, 32)
SUBLANES = 8
ROW_BLK = 256
TQ = 256
TM_IN = 256
TM_OUT = 512

VMEM_LIMIT = 56 * 1024 * 1024


def _params(n_axes):
    return pltpu.CompilerParams(
        dimension_semantics=("arbitrary",) * n_axes, vmem_limit_bytes=VMEM_LIMIT)


def _dot(a, b):
    return jnp.dot(a, b, preferred_element_type=F32)


def _dot_nt(a, b):
    return lax.dot_general(a, b, (((1,), (1,)), ((), ())), preferred_element_type=F32)


def _dot_tn(a, b):
    return lax.dot_general(a, b, (((0,), (0,)), ((), ())), preferred_element_type=F32)


def _silu(x):
    return x * jax.nn.sigmoid(x)


def _split2_dot(x, g):
    hi = x.astype(BF16)
    lo = (x - hi.astype(F32)).astype(BF16)
    return _dot(hi, g) + _dot(lo, g)


def _group_rms(x, g, width):
    ms = _split2_dot(x * x, g) * (1.0 / width)
    return x * lax.rsqrt(ms + EPS)


def _rope(x, cos, sin):
    w = x.shape[-1]
    lane = lax.broadcasted_iota(jnp.int32, (1, w), 1)
    first = (lane & (2 * ROPE_PAIR - 1)) < ROPE_PAIR
    swapped = jnp.where(first, pltpu.roll(x, w - ROPE_PAIR, 1), pltpu.roll(x, ROPE_PAIR, 1))
    return x * cos + swapped * sin


def _lane_head(w=PACK_W):
    return lax.broadcasted_iota(jnp.int32, (1, w), 1) // DH


def _row_blocks(n_rows, body):
    n_blk = n_rows // ROW_BLK
    if n_blk == 1:
        body(0)
    else:
        def step(i, carry):
            body(pl.multiple_of(i * ROW_BLK, ROW_BLK))
            return carry
        lax.fori_loop(0, n_blk, step, 0)


def _ada_kernel(cv_ref, w_ref, b_ref, o_ref):
    cv = cv_ref[...]
    o_ref[0] = _dot(_silu(cv), w_ref[0]) + b_ref[0]


def _ada(cv, w_ada, b_ada):
    tn = 1024
    n = 3 * D_MODEL
    return pl.pallas_call(
        _ada_kernel,
        grid=(DEPTH, n // tn),
        in_specs=[
            pl.BlockSpec((8, D_MODEL), lambda l, j: (0, 0)),
            pl.BlockSpec((1, D_MODEL, tn), lambda l, j: (l, 0, j)),
            pl.BlockSpec((1, 1, tn), lambda l, j: (l, 0, j)),
        ],
        out_specs=pl.BlockSpec((1, 8, tn), lambda l, j: (l, 0, j)),
        out_shape=jax.ShapeDtypeStruct((DEPTH, 8, n), F32),
        compiler_params=_params(2),
        name="ada",
    )(cv, w_ada, b_ada.reshape(DEPTH, 1, n))


def _inproj_kernel(x_ref, g_ref, sc_ref, sh_ref, w_ref, zr_ref, zd_ref, zh_ref):
    x = x_ref[...]
    ms = jnp.mean(x * x, axis=-1, keepdims=True)
    h = x * lax.rsqrt(ms + EPS) * g_ref[...]
    h = h * (1.0 + sc_ref[0]) + sh_ref[0]
    z = _dot(h.astype(BF16), w_ref[...])
    zr_ref[...] = z[:, :RET_COLS]
    zd_ref[...] = z[:, RET_COLS:RET_COLS + DIFF_COLS]
    zh_ref[...] = z[:, RET_COLS + DIFF_COLS:]


def _inproj(x2d, g, scale, shift, w_bf, seq_len):
    m = x2d.shape[0]
    per_batch = scale.shape[0] > 1
    tiles_per_seq = seq_len // TM_IN
    mod_map = (lambda i: (i // tiles_per_seq, 0, 0)) if per_batch else (lambda i: (0, 0, 0))
    return pl.pallas_call(
        _inproj_kernel,
        grid=(m // TM_IN,),
        in_specs=[
            pl.BlockSpec((TM_IN, D_MODEL), lambda i: (i, 0)),
            pl.BlockSpec((1, D_MODEL), lambda i: (0, 0)),
            pl.BlockSpec((1, 1, D_MODEL), mod_map),
            pl.BlockSpec((1, 1, D_MODEL), mod_map),
            pl.BlockSpec((D_MODEL, IN_W), lambda i: (0, 0)),
        ],
        out_specs=[
            pl.BlockSpec((TM_IN, RET_COLS), lambda i: (i, 0)),
            pl.BlockSpec((TM_IN, DIFF_COLS), lambda i: (i, 0)),
            pl.BlockSpec((TM_IN, HG_COLS), lambda i: (i, 0)),
        ],
        out_shape=[
            jax.ShapeDtypeStruct((m, RET_COLS), F32),
            jax.ShapeDtypeStruct((m, DIFF_COLS), F32),
            jax.ShapeDtypeStruct((m, HG_COLS), F32),
        ],
        compiler_params=_params(1),
        name="inproj",
    )(x2d, g, scale, shift, w_bf)


def _outproj_kernel(x_ref, r_ref, d_ref, h_ref, gate_ref, w_ref, o_ref):
    acc = _dot(r_ref[...], w_ref[0:PACK_W, :])
    acc += _dot(d_ref[...], w_ref[PACK_W:PACK_W + DIFF_W, :])
    acc += _dot(h_ref[...], w_ref[PACK_W + DIFF_W:, :])
    o_ref[...] = x_ref[...] + gate_ref[0] * acc


def _outproj(x2d, r, d, h, gate, w_bf, seq_len):
    m = x2d.shape[0]
    per_batch = gate.shape[0] > 1
    tiles_per_seq = max(seq_len // TM_OUT, 1)
    tm = min(TM_OUT, seq_len)
    mod_map = (lambda i: (i // tiles_per_seq, 0, 0)) if per_batch else (lambda i: (0, 0, 0))
    return pl.pallas_call(
        _outproj_kernel,
        grid=(m // tm,),
        in_specs=[
            pl.BlockSpec((tm, D_MODEL), lambda i: (i, 0)),
            pl.BlockSpec((tm, PACK_W), lambda i: (i, 0)),
            pl.BlockSpec((tm, DIFF_W), lambda i: (i, 0)),
            pl.BlockSpec((tm, PACK_W), lambda i: (i, 0)),
            pl.BlockSpec((1, 1, D_MODEL), mod_map),
            pl.BlockSpec((MIX_W, D_MODEL), lambda i: (0, 0)),
        ],
        out_specs=pl.BlockSpec((tm, D_MODEL), lambda i: (i, 0)),
        out_shape=jax.ShapeDtypeStruct((m, D_MODEL), F32),
        compiler_params=_params(1),
        name="outproj",
    )(x2d, r, d, h, gate, w_bf)


def _ret_kernel(*refs, seq_len, latent):
    if latent:
        (lg_ref, z_ref, g_ref, cos_ref, sin_ref, s0f_ref, s0b_ref,
         out_ref, mask_scr, st_scr, o_scr) = refs
    else:
        lg_ref, z_ref, g_ref, out_ref, sf_ref, sb_ref, mask_scr, st_scr, o_scr = refs
    C = RET_C
    n_chunks = seq_len // C
    lh = _lane_head()

    def lane_vec(direction):
        v = jnp.zeros((1, PACK_W), F32)
        for h in range(HEADS):
            v = jnp.where(lh == h, lg_ref[direction, h], v)
        return v

    lgf, lgb = lane_vec(0), lane_vec(1)
    tau = lax.broadcasted_iota(jnp.int32, (C, 1), 0).astype(F32)
    qdec_f = jnp.exp((tau + 1.0) * lgf)
    qdec_b = jnp.exp((C - tau) * lgb)
    kdec_f = jnp.exp((C - 1.0 - tau) * lgf)
    kdec_b = jnp.exp(tau * lgb)
    sdec_f = jnp.exp(float(C) * lgf)
    sdec_b = jnp.exp(float(C) * lgb)
    gmask = g_ref[...].astype(F32)

    dd = (lax.broadcasted_iota(jnp.int32, (C, C), 0)
          - lax.broadcasted_iota(jnp.int32, (C, C), 1)).astype(F32)
    for h in range(HEADS):
        mf = jnp.where(dd >= 0, jnp.exp(jnp.maximum(dd, 0.0) * lg_ref[0, h]), 0.0)
        mb = jnp.where(dd <= 0, jnp.exp(jnp.maximum(-dd, 0.0) * lg_ref[1, h]), 0.0)
        mask_scr[h] = mf + mb

    def load(r0):
        q = z_ref[pl.ds(r0, C), 0:PACK_W]
        k = z_ref[pl.ds(r0, C), PACK_W:2 * PACK_W] * (DH ** -0.5)
        v = z_ref[pl.ds(r0, C), 2 * PACK_W:3 * PACK_W]
        if latent:
            cos = cos_ref[pl.ds(r0, C), :]
            sin = sin_ref[pl.ds(r0, C), :]
            q = _rope(q, cos, sin)
            k = _rope(k, cos, sin)
        return q, k, v

    def intra(q, k, v):
        kb = k.astype(BF16)
        vb = v.astype(BF16)
        o = jnp.zeros((C, PACK_W), F32)
        for h in range(HEADS):
            hm = lh == h
            qh = jnp.where(hm, q, 0.0).astype(BF16)
            a = _dot_nt(qh, kb) * mask_scr[h]
            o = o + jnp.where(hm, _dot(a.astype(BF16), vb), 0.0)
        return o

    def finish(o, r0):
        gate = z_ref[pl.ds(r0, C), 3 * PACK_W:4 * PACK_W]
        y = _group_rms(o, g_ref[...], DH) * _silu(gate)
        out_ref[pl.ds(r0, C), :] = y.astype(out_ref.dtype)

    def state_step(st, k, v, kdec, sdec):
        upd = _dot_tn(v.astype(BF16), (k * kdec).astype(BF16))
        return (st * sdec + upd) * gmask

    if not latent:
        q, k, v = load(0)
        finish(intra(q, k, v), 0)
        vb = v.astype(BF16)
        for ref, kdec in ((sf_ref, kdec_f), (sb_ref, kdec_b)):
            s = _dot_tn((k * kdec).astype(BF16), vb) * gmask
            for h in range(HEADS):
                ref[0, h] = s[h * DH:(h + 1) * DH, h * DH:(h + 1) * DH]
        return

    st_scr[...] = s0b_ref[0]

    def bwd_body(i, carry):
        r0 = pl.multiple_of((n_chunks - 1 - i) * C, C)
        q, k, v = load(r0)
        st = st_scr[...]
        o_scr[pl.ds(r0, C), :] = _dot_nt((q * qdec_b).astype(BF16), st.astype(BF16))
        st_scr[...] = state_step(st, k, v, kdec_b, sdec_b)
        return carry

    lax.fori_loop(0, n_chunks, bwd_body, 0)

    st_scr[...] = s0f_ref[0]

    def fwd_body(i, carry):
        r0 = pl.multiple_of(i * C, C)
        q, k, v = load(r0)
        st = st_scr[...]
        o = intra(q, k, v) + o_scr[pl.ds(r0, C), :]
        o = o + _dot_nt((q * qdec_f).astype(BF16), st.astype(BF16))
        finish(o, r0)
        st_scr[...] = state_step(st, k, v, kdec_f, sdec_f)
        return carry

    lax.fori_loop(0, n_chunks, fwd_body, 0)


def _retention(z_ret, log_g, g256, batch, seq_len, latent, rope=None, states=None):
    m = batch * seq_len
    row = lambda b: (b, 0)
    const2 = lambda b: (0, 0)
    in_specs = [
        pl.BlockSpec(memory_space=pltpu.SMEM),
        pl.BlockSpec((seq_len, RET_COLS), row),
        pl.BlockSpec((PACK_W, PACK_W), const2),
    ]
    args = [log_g, z_ret, g256]
    out_specs = [pl.BlockSpec((seq_len, PACK_W), row)]
    out_shape = [jax.ShapeDtypeStruct((m, PACK_W), BF16)]
    if latent:
        state_spec = pl.BlockSpec((1, PACK_W, PACK_W), lambda b: (b, 0, 0))
        in_specs += [pl.BlockSpec((seq_len, PACK_W), const2)] * 2 + [state_spec] * 2
        args += [rope[0], rope[1], states[0], states[1]]
    else:
        out_specs += [pl.BlockSpec((1, HEADS, DH, DH), lambda b: (b, 0, 0, 0))] * 2
        out_shape += [jax.ShapeDtypeStruct((batch, HEADS, DH, DH), F32)] * 2
    return pl.pallas_call(
        functools.partial(_ret_kernel, seq_len=seq_len, latent=latent),
        grid=(batch,),
        in_specs=in_specs,
        out_specs=out_specs,
        out_shape=out_shape,
        scratch_shapes=[
            pltpu.VMEM((HEADS, RET_C, RET_C), F32),
            pltpu.VMEM((PACK_W, PACK_W), F32),
            pltpu.VMEM((seq_len, PACK_W), F32),
        ],
        compiler_params=_params(1),
        name="ret_latent" if latent else "ret_ctx",
    )(*args)


def _attn_kernel(*refs, seq_len, past_len, latent, lam_init, layer):
    if latent:
        (lam_ref, q_ref, k_ref, v_ref, gate_ref, gq_ref, gk_ref, g_ref,
         cosq_ref, sinq_ref, cosk_ref, sink_ref, kc_ref, vc_ref,
         out_ref, k_scr, v_scr) = refs
    else:
        (lam_ref, q_ref, k_ref, v_ref, gate_ref, gq_ref, gk_ref, g_ref) = refs[:8]
        out_ref, kn_ref, vn_ref, k_scr, v_scr = refs[-5:]
    ones = jnp.ones((ROW_BLK, DIFF_DV), BF16)

    def put_values(sl, v):
        for h in range(HEADS):
            v_scr[sl, 2 * h * DIFF_DV:(2 * h + 1) * DIFF_DV] = (
                v[:, h * DIFF_DV:(h + 1) * DIFF_DV].astype(BF16))
            v_scr[sl, (2 * h + 1) * DIFF_DV:(2 * h + 2) * DIFF_DV] = ones

    @pl.when(pl.program_id(1) == 0)
    def _():
        for r in range(seq_len // ROW_BLK):
            sl = pl.ds(r * ROW_BLK, ROW_BLK)
            kn = _group_rms(k_ref[sl, :], g_ref[...], DH) * gk_ref[...]
            v = v_ref[sl, :]
            if latent:
                kn = _rope(kn, cosk_ref[sl, :], sink_ref[sl, :])
            else:
                kn_ref[0, 0, sl, :] = kn
                vn_ref[0, 0, sl, :] = v
            k_scr[sl, :] = kn.astype(BF16)
            put_values(sl, v)
        if latent:
            for r in range(past_len // ROW_BLK):
                src = pl.ds(r * ROW_BLK, ROW_BLK)
                dst = pl.ds(seq_len + r * ROW_BLK, ROW_BLK)
                k_scr[dst, :] = kc_ref[0, src, :].astype(BF16)
                put_values(dst, vc_ref[0, src, :])

    lam = lam_ref[0]
    qn = _group_rms(q_ref[...], g_ref[...], DH) * gq_ref[...]
    if latent:
        qn = _rope(qn, cosq_ref[...], sinq_ref[...])
    qn = qn * (DH ** -0.5 * LOG2E)
    lane = lax.broadcasted_iota(jnp.int32, (1, 2 * DH), 1)
    for h in range(HEADS):
        hs = slice(h * 2 * DH, (h + 1) * 2 * DH)
        vx = v_scr[:, 2 * h * DIFF_DV:(2 * h + 2) * DIFF_DV]
        qh = qn[:, hs]
        kh = k_scr[:, hs]
        parts = []
        for first_map in (True, False):
            qm = jnp.where((lane < DH) == first_map, qh, 0.0).astype(BF16)
            s = _dot_nt(qm, kh)
            e = jnp.exp2(s - jnp.max(s, axis=-1, keepdims=True)).astype(BF16)
            ox = _dot(e, vx)
            parts.append(ox[:, :DIFF_DV] / ox[:, DIFF_DV:DIFF_DV + 1])
        o = parts[0] - lam * parts[1]
        ms = jnp.mean(o * o, axis=-1, keepdims=True)
        y = o * lax.rsqrt(ms + EPS) * (1.0 - lam_init) * _silu(gate_ref[:, hs])
        out_ref[:, hs] = y.astype(out_ref.dtype)


def _diff_attention(z_diff, lam, gq, gk, g512, batch, seq_len, latent, lam_init, layer,
                    rope=None, ctx=None, caches=None):
    m = batch * seq_len
    nq = seq_len // TQ
    past_len = ctx[0].shape[1] if latent else 0
    qmap = lambda col: (lambda b, i: (b * nq + i, col))
    kvmap = lambda col: (lambda b, i: (b, col))
    const2 = lambda b, i: (0, 0)
    in_specs = [
        pl.BlockSpec(memory_space=pltpu.SMEM),
        pl.BlockSpec((TQ, DIFF_W), qmap(0)),
        pl.BlockSpec((seq_len, DIFF_W), kvmap(1)),
        pl.BlockSpec((seq_len, DIFF_W), kvmap(2)),
        pl.BlockSpec((TQ, DIFF_W), qmap(3)),
        pl.BlockSpec((1, DIFF_W), const2),
        pl.BlockSpec((1, DIFF_W), const2),
        pl.BlockSpec((DIFF_W, DIFF_W), const2),
    ]
    args = [lam, z_diff, z_diff, z_diff, z_diff, gq, gk, g512]
    out_specs = [pl.BlockSpec((TQ, DIFF_W), qmap(0))]
    out_shape = [jax.ShapeDtypeStruct((m, DIFF_W), BF16)]
    aliases = {}
    if latent:
        in_specs += [
            pl.BlockSpec((TQ, DIFF_W), lambda b, i: (i, 0)),
            pl.BlockSpec((TQ, DIFF_W), lambda b, i: (i, 0)),
            pl.BlockSpec((seq_len, DIFF_W), const2),
            pl.BlockSpec((seq_len, DIFF_W), const2),
            pl.BlockSpec((1, past_len, DIFF_W), lambda b, i: (b, 0, 0)),
            pl.BlockSpec((1, past_len, DIFF_W), lambda b, i: (b, 0, 0)),
        ]
        args += [rope[0], rope[1], rope[0], rope[1], ctx[0], ctx[1]]
    else:
        cache_spec = pl.BlockSpec((1, 1, seq_len, DIFF_W), lambda b, i: (b, layer, 0, 0))
        out_specs += [cache_spec] * 2
        out_shape += [jax.ShapeDtypeStruct((batch, DEPTH, seq_len, DIFF_W), F32)] * 2
        if caches is not None:
            aliases = {len(args): 1, len(args) + 1: 2}
            in_specs += [pl.BlockSpec(memory_space=pl.ANY)] * 2
            args += list(caches)
    return pl.pallas_call(
        functools.partial(_attn_kernel, seq_len=seq_len, past_len=past_len,
                          latent=latent, lam_init=lam_init, layer=layer),
        grid=(batch, nq),
        in_specs=in_specs,
        out_specs=out_specs,
        out_shape=out_shape,
        input_output_aliases=aliases,
        scratch_shapes=[
            pltpu.VMEM((seq_len + past_len, DIFF_W), BF16),
            pltpu.VMEM((seq_len + past_len, 2 * DIFF_W), BF16),
        ],
        compiler_params=_params(2),
        name="attn_latent" if latent else "attn_ctx",
    )(*args)


def _hgrn_kernel(*refs, seq_len, latent):
    if latent:
        (z_ref, lb_ref, tri_ref, g_ref, lm_ref, dm_ref, s0f_ref, s0b_ref,
         out_ref, qi_scr, ks_scr, dec_scr, o_scr, st_scr) = refs
    else:
        (z_ref, lb_ref, tri_ref, g_ref, lm_ref, dm_ref,
         out_ref, sf_ref, sb_ref, qi_scr, ks_scr, dec_scr, o_scr, st_scr) = refs
    C, SUB = HG_C, HG_SUB
    n_chunks = seq_len // C
    chunks_per_blk = ROW_BLK // C
    lh = _lane_head()
    sub_row = lax.broadcasted_iota(jnp.int32, (ROW_BLK, 1), 0) & (SUB - 1)

    def tile_roll(x, shift):
        return jnp.concatenate(
            [pltpu.roll(x[i:i + SUBLANES], shift, 0) for i in range(0, ROW_BLK, SUBLANES)], axis=0)

    def head_rep(x):
        return jnp.concatenate([jnp.where(lh == h, x, jnp.zeros_like(x)) for h in range(HEADS)],
                               axis=0)

    def intra(blk):
        start = blk * ROW_BLK if isinstance(blk, int) else pl.multiple_of(blk * ROW_BLK, ROW_BLK)
        rows = pl.ds(start, ROW_BLK)
        lb = lb_ref[...]
        q = z_ref[rows, 0:PACK_W] * (DH ** -0.5)
        v = z_ref[rows, 3 * PACK_W:4 * PACK_W]
        v_reps = [head_rep(v[c * C:(c + 1) * C].astype(BF16)) for c in range(chunks_per_blk)]
        kks, bcs, a_nears = [], [], []
        for d in range(2):
            f = lb + (1.0 - lb) * jax.nn.sigmoid(z_ref[rows, (1 + d) * PACK_W:(2 + d) * PACK_W])
            kks.append(1.0 - f)
            l2 = jnp.log(f) * LOG2E
            hi = l2.astype(BF16)
            r1 = l2 - hi.astype(F32)
            mid = r1.astype(BF16)
            lo = (r1 - mid.astype(F32)).astype(BF16)
            cs = _dot(tri_ref[d], jnp.concatenate([hi, mid, lo], axis=1))
            bcs.append(cs[:, 0:PACK_W] + cs[:, PACK_W:2 * PACK_W] + cs[:, 2 * PACK_W:])

        for d in range(2):
            bwd = d == 1
            kk, bc = kks[d], bcs[d]
            ps = [(q * kk).astype(BF16)]
            for dist in range(1, SUB):
                shift = SUBLANES - dist if bwd else dist
                valid = (sub_row < SUB - dist) if bwd else (sub_row >= dist)
                arg = jnp.where(valid, bc - tile_roll(bc, shift), NEG_BIG)
                ps.append((q * tile_roll(kk, shift) * jnp.exp2(arg)).astype(BF16))
            r = _dot(jnp.concatenate(ps, axis=0), g_ref[...])
            a_near = r[0:ROW_BLK] * dm_ref[d, 0]
            for dist in range(1, SUB):
                a_near = a_near + r[dist * ROW_BLK:(dist + 1) * ROW_BLK] * dm_ref[d, dist]
            a_nears.append(a_near)

        for c in range(chunks_per_blk):
            for d in range(2):
                bwd = d == 1
                kk, bc, a_near = kks[d], bcs[d], a_nears[d]
                cs_ = slice(c * C, (c + 1) * C)
                qc, kc, bcc = q[cs_], kk[cs_], bc[cs_]
                last = bcc[0:1] if bwd else bcc[C - 1:C]
                a = a_near[cs_]
                for li, m in enumerate(HG_LEVELS):
                    pieces = []
                    for b2 in range(C // (2 * m)):
                        rr = 2 * m * b2 + (m if bwd else m - 1)
                        pieces.append(jnp.broadcast_to(bcc[rr:rr + 1], (2 * m, PACK_W)))
                    anchor = jnp.concatenate(pieces, axis=0) if len(pieces) > 1 else pieces[0]
                    e = jnp.exp2(-jnp.abs(bcc - anchor))
                    a = a + _dot_nt((qc * e).astype(BF16),
                                    head_rep((kc * e).astype(BF16))) * lm_ref[d, li]
                crow = pl.ds(start + c * C, C)
                o_scr[d, crow, :] = _dot(a.astype(BF16), v_reps[c])
                qi_scr[d, crow, :] = (qc * jnp.exp2(bcc)).astype(BF16)
                ks_scr[d, crow, :] = (kc * jnp.exp2(last - bcc)).astype(BF16)
                slot = (blk * chunks_per_blk + c) * SUBLANES
                if not isinstance(slot, int):
                    slot = pl.multiple_of(slot, SUBLANES)
                dec_scr[d, pl.ds(slot, SUBLANES), :] = jnp.broadcast_to(
                    jnp.exp2(last), (SUBLANES, PACK_W))

        if not latent:
            vb = v.astype(BF16)
            for d, ref in enumerate((sf_ref, sb_ref)):
                bwd = d == 1
                sweep = range(chunks_per_blk - 1, -1, -1) if bwd else range(chunks_per_blk)
                tail = jnp.zeros((1, PACK_W), F32)
                decayed = [None] * chunks_per_blk
                for c in reversed(sweep):
                    cs_ = slice(c * C, (c + 1) * C)
                    bcc = bcs[d][cs_]
                    total = tail + (bcc[0:1] if bwd else bcc[C - 1:C])
                    decayed[c] = (kks[d][cs_] * jnp.exp2(total - bcc)).astype(BF16)
                    tail = total
                s = _dot_tn(jnp.concatenate(decayed, axis=0), vb) * g_ref[...].astype(F32)
                for h in range(HEADS):
                    ref[0, h] = s[h * DH:(h + 1) * DH, h * DH:(h + 1) * DH]

    n_blk = seq_len // ROW_BLK
    assert latent or n_blk == 1
    if n_blk == 1:
        intra(0)
    else:
        def intra_step(i, carry):
            intra(i)
            return carry
        lax.fori_loop(0, n_blk, intra_step, 0)

    for d in range(2):
        if latent:
            st_scr[d] = (s0f_ref, s0b_ref)[d][0]
        else:
            st_scr[d] = jnp.zeros((PACK_W, PACK_W), F32)
    gmask = g_ref[...].astype(F32)

    def body(i, carry):
        for d in range(2):
            c = (n_chunks - 1 - i) if d == 1 else i
            rows = pl.ds(pl.multiple_of(c * C, C), C)
            st = st_scr[d]
            o_scr[d, rows, :] += _dot_nt(qi_scr[d, rows, :], st.astype(BF16))
            dec = dec_scr[d, pl.ds(pl.multiple_of(c * SUBLANES, SUBLANES), 1), :]
            vb = z_ref[rows, 3 * PACK_W:4 * PACK_W].astype(BF16)
            st_scr[d] = st * dec + _dot_tn(vb, ks_scr[d, rows, :]) * gmask
        return carry

    lax.fori_loop(0, n_chunks, body, 0, unroll=4)

    def finish(start):
        rows = pl.ds(start, ROW_BLK)
        tot = o_scr[0, rows, :] + o_scr[1, rows, :]
        gate = z_ref[rows, 4 * PACK_W:5 * PACK_W]
        y = _group_rms(tot, g_ref[...], DH) * _silu(gate)
        out_ref[rows, :] = y.astype(out_ref.dtype)

    _row_blocks(seq_len, finish)


def _hgrn(z_hg, lb, consts, batch, seq_len, latent, states=None):
    m = batch * seq_len
    tri, g256, lm, dm = consts
    row = lambda b: (b, 0)
    in_specs = [
        pl.BlockSpec((seq_len, HG_COLS), row),
        pl.BlockSpec((1, PACK_W), lambda b: (0, 0)),
        pl.BlockSpec(tri.shape, lambda b: (0, 0, 0)),
        pl.BlockSpec((PACK_W, PACK_W), lambda b: (0, 0)),
        pl.BlockSpec(lm.shape, lambda b: (0, 0, 0, 0)),
        pl.BlockSpec(dm.shape, lambda b: (0, 0, 0, 0)),
    ]
    args = [z_hg, lb, tri, g256, lm, dm]
    out_specs = [pl.BlockSpec((seq_len, PACK_W), row)]
    out_shape = [jax.ShapeDtypeStruct((m, PACK_W), BF16)]
    if latent:
        in_specs += [pl.BlockSpec((1, PACK_W, PACK_W), lambda b: (b, 0, 0))] * 2
        args += [states[0], states[1]]
    else:
        out_specs += [pl.BlockSpec((1, HEADS, DH, DH), lambda b: (b, 0, 0, 0))] * 2
        out_shape += [jax.ShapeDtypeStruct((batch, HEADS, DH, DH), F32)] * 2
    return pl.pallas_call(
        functools.partial(_hgrn_kernel, seq_len=seq_len, latent=latent),
        grid=(batch,),
        in_specs=in_specs,
        out_specs=out_specs,
        out_shape=out_shape,
        scratch_shapes=[
            pltpu.VMEM((2, seq_len, PACK_W), BF16),
            pltpu.VMEM((2, seq_len, PACK_W), BF16),
            pltpu.VMEM((2, seq_len // HG_C * SUBLANES, PACK_W), F32),
            pltpu.VMEM((2, seq_len, PACK_W), F32),
            pltpu.VMEM((2, PACK_W, PACK_W), F32),
        ],
        compiler_params=_params(1),
        name="hgrn_latent" if latent else "hgrn_ctx",
    )(*args)


def _block_ones(n):
    idx = np.arange(n) // DH
    return jnp.asarray((idx[:, None] == idx[None, :]).astype(np.float32), dtype=BF16)


def _hgrn_consts():
    C = HG_C
    t = np.arange(ROW_BLK)
    same_chunk = (t[:, None] // C) == (t[None, :] // C)
    tri_f = same_chunk & (t[None, :] <= t[:, None])
    tri_b = same_chunk & (t[None, :] >= t[:, None])
    tc = np.arange(C)
    lms = []
    for fwd in (True, False):
        per_level = []
        for m in HG_LEVELS:
            same = (tc[:, None] // (2 * m)) == (tc[None, :] // (2 * m))
            upper = (tc % (2 * m)) >= m
            later, earlier = (upper, ~upper) if fwd else (~upper, upper)
            per_level.append(np.tile(same & later[:, None] & earlier[None, :], (1, HEADS)))
        lms.append(np.stack(per_level))
    dms = [[np.tile(tc[None, :] == tc[:, None] + sign * dist, (ROW_BLK // C, HEADS))
            for dist in range(HG_SUB)] for sign in (-1, 1)]
    return (jnp.asarray(np.stack([tri_f, tri_b]).astype(np.float32), dtype=BF16),
            _block_ones(PACK_W),
            jnp.asarray(np.stack(lms).astype(np.float32), dtype=F32),
            jnp.asarray(np.stack(dms).astype(np.float32), dtype=F32))


def _rope_tables(seq_len, width):
    t = np.arange(seq_len)
    pos = np.stack([t // GRID_W, t % GRID_W], axis=1).astype(np.float32)
    j = np.arange(DH)
    axis = j // (2 * ROPE_PAIR)
    jj = j % (2 * ROPE_PAIR)
    inv = (ROPE_BASE ** (-(jj % ROPE_PAIR).astype(np.float64) / ROPE_PAIR)).astype(np.float32)
    ang = (pos[:, axis] * inv[None, :]).astype(np.float64)
    cos = np.cos(ang)
    sin = np.where(jj < ROPE_PAIR, -np.sin(ang), np.sin(ang))
    reps = width // DH
    return (jnp.asarray(np.tile(cos, (1, reps)), dtype=F32),
            jnp.asarray(np.tile(sin, (1, reps)), dtype=F32))


def _to_blockdiag_t(s):
    b = s.shape[0]
    eye = jnp.eye(HEADS, dtype=s.dtype)
    return jnp.einsum('bhde,hg->bhegd', s, eye).reshape(b, PACK_W, PACK_W)


def _layer(x2d, batch, seq_len, mod, l, p, latent, ctx, caches):
    shift, scale, gate = mod
    z_ret, z_diff, z_hg = _inproj(x2d, p['norm_g'][l], scale, shift, p['w_in'][l], seq_len)
    lam_init = 0.8 - 0.6 * math.exp(-0.3 * l)
    if latent:
        k_ctx, v_ctx, s_rf, s_rb, s_hf, s_hb = ctx
        r = _retention(z_ret, p['log_g'][l], p['g256'], batch, seq_len, True,
                       rope=p['rope256'], states=(s_rf, s_rb))[0]
        a = _diff_attention(z_diff, p['lam'][l], p['gq'][l], p['gk'][l], p['g512'], batch,
                            seq_len, True, lam_init, l, rope=p['rope512'],
                            ctx=(k_ctx, v_ctx))[0]
        hgo = _hgrn(z_hg, p['lb'][l], p['hg_consts'], batch, seq_len, True,
                    states=(s_hf, s_hb))[0]
        extras = None
    else:
        r, s_rf, s_rb = _retention(z_ret, p['log_g'][l], p['g256'], batch, seq_len, False)
        a, kcache, vcache = _diff_attention(z_diff, p['lam'][l], p['gq'][l], p['gk'][l],
                                            p['g512'], batch, seq_len, False, lam_init, l,
                                            caches=caches)
        hgo, s_hf, s_hb = _hgrn(z_hg, p['lb'][l], p['hg_consts'], batch, seq_len, False)
        extras = ((kcache, vcache), (s_rf, s_rb, s_hf, s_hb))
    y = _outproj(x2d, r, a, hgo, gate, p['w_out'][l], seq_len)
    return y, extras


def kernel(x_prompt, x_sample, c, c_ctx, cache_diff_k, cache_diff_v, state_ret_fwd,
           state_ret_bwd, state_hgrn_fwd, state_hgrn_bwd, norm_g, w_ada, b_ada, w_in,
           w_out, ret_decay_logit, diff_qn_g, diff_kn_g, diff_lambda, hgrn_lb_logit):
    batch, seq, _ = x_prompt.shape
    dec_batch, dec_seq, _ = x_sample.shape
    past_len = cache_diff_k.shape[2]

    cv = jnp.zeros((8, D_MODEL), F32).at[0].set(c_ctx).at[1:1 + dec_batch].set(c)
    mods = _ada(cv, w_ada, b_ada)

    lb_all = jax.nn.softmax(hgrn_lb_logit.astype(F32), axis=0)
    lb_all = jnp.cumsum(lb_all, axis=0) - lb_all[0]
    lp = diff_lambda.astype(F32)
    lam_inits = jnp.asarray([0.8 - 0.6 * math.exp(-0.3 * l) for l in range(DEPTH)], F32)
    lam = (jnp.exp(jnp.sum(lp[:, 0] * lp[:, 1], axis=-1))
           - jnp.exp(jnp.sum(lp[:, 2] * lp[:, 3], axis=-1)) + lam_inits)
    p = {
        'norm_g': norm_g.reshape(DEPTH, 1, D_MODEL),
        'w_in': w_in.astype(BF16),
        'w_out': w_out.astype(BF16),
        'log_g': jax.nn.log_sigmoid(ret_decay_logit.astype(F32)),
        'lam': lam.reshape(DEPTH, 1),
        'gq': jnp.tile(diff_qn_g, (1, DIFF_W // DH)).reshape(DEPTH, 1, DIFF_W),
        'gk': jnp.tile(diff_kn_g, (1, DIFF_W // DH)).reshape(DEPTH, 1, DIFF_W),
        'lb': lb_all.reshape(DEPTH, 1, PACK_W),
        'g256': _block_ones(PACK_W),
        'g512': _block_ones(DIFF_W),
        'hg_consts': _hgrn_consts(),
        'rope256': _rope_tables(dec_seq, PACK_W),
        'rope512': _rope_tables(dec_seq, DIFF_W),
    }

    def split_mod(rows):
        return tuple(rows[:, None, j * D_MODEL:(j + 1) * D_MODEL] for j in range(3))

    y = x_prompt.reshape(batch * seq, D_MODEL)
    caches = None
    states = []
    for l in range(DEPTH):
        y, (caches, st) = _layer(y, batch, seq, split_mod(mods[l, 0:1]), l, p, False, None, caches)
        states.append(st)
    y_prompt = y.reshape(batch, seq, D_MODEL)
    new_k = caches[0].reshape(batch, DEPTH, seq, HEADS, 2, DH)
    new_v = caches[1].reshape(batch, DEPTH, seq, HEADS, DIFF_DV)
    new_states = [jnp.stack([st[i] for st in states], axis=1) for i in range(4)]

    y = x_sample.reshape(dec_batch * dec_seq, D_MODEL)
    for l in range(DEPTH):
        ctx = (cache_diff_k[:, l].reshape(dec_batch, past_len, DIFF_W),
               cache_diff_v[:, l].reshape(dec_batch, past_len, DIFF_W),
               _to_blockdiag_t(state_ret_fwd[:, l]), _to_blockdiag_t(state_ret_bwd[:, l]),
               _to_blockdiag_t(state_hgrn_fwd[:, l]), _to_blockdiag_t(state_hgrn_bwd[:, l]))
        y, _ = _layer(y, dec_batch, dec_seq, split_mod(mods[l, 1:1 + dec_batch]), l, p, True,
                      ctx, None)
    y_sample = y.reshape(dec_batch, dec_seq, D_MODEL)

    return (y_prompt, y_sample, new_k, new_v, *new_states)
```

```python
import functools
import math

import numpy as np
import jax
import jax.numpy as jnp
from jax import lax
from jax.experimental import pallas as pl
from jax.experimental.pallas import tpu as pltpu

F32 = jnp.float32
BF16 = jnp.bfloat16

D_MODEL = 1024
DEPTH = 2
GRID_W = 64
HEADS = 4
DH = 64
PACK_W = HEADS * DH
DIFF_W = 512
DIFF_DV = 128
RET_COLS = 4 * PACK_W
DIFF_COLS = 4 * DIFF_W
HG_COLS = 5 * PACK_W
IN_W = RET_COLS + DIFF_COLS + HG_COLS
MIX_W = PACK_W + DIFF_W + PACK_W
ROPE_BASE = 10000.0
ROPE_PAIR = 16
EPS = 1e-6
LOG2E = 1.4426950408889634
NEG_BIG = -1e30

RET_C = 256
HG_C = 64
HG_SUB = 4
HG_LEVELS = (4, 8, 16, 32)
SUBLANES = 8
LANES = 128
ROW_BLK = 256
TQ = 256
TM_IN = 256
TM_OUT = 512

VMEM_LIMIT = 56 * 1024 * 1024


def _params(n_axes):
    return pltpu.CompilerParams(
        dimension_semantics=("arbitrary",) * n_axes, vmem_limit_bytes=VMEM_LIMIT)


def _dot(a, b):
    return jnp.dot(a, b, preferred_element_type=F32)


def _dot_nt(a, b):
    return lax.dot_general(a, b, (((1,), (1,)), ((), ())), preferred_element_type=F32)


def _dot_tn(a, b):
    return lax.dot_general(a, b, (((0,), (0,)), ((), ())), preferred_element_type=F32)


def _silu(x):
    return x * jax.nn.sigmoid(x)


def _group_rms(x, g, width):
    n, w = x.shape
    x2 = x * x
    cols = range(0, w, LANES)
    stacked = jnp.concatenate([x2[:, j:j + LANES] for j in cols], axis=0)
    hi = stacked.astype(BF16)
    lo = (stacked - hi.astype(F32)).astype(BF16)
    sums = _dot(hi, g) + _dot(lo, g)
    ms = jnp.concatenate([sums[i * n:(i + 1) * n] for i in range(len(cols))], axis=1)
    return x * lax.rsqrt(ms * (1.0 / width) + EPS)


def _rope(x, cos, sin):
    w = x.shape[-1]
    lane = lax.broadcasted_iota(jnp.int32, (1, w), 1)
    first = (lane & (2 * ROPE_PAIR - 1)) < ROPE_PAIR
    swapped = jnp.where(first, pltpu.roll(x, w - ROPE_PAIR, 1), pltpu.roll(x, ROPE_PAIR, 1))
    return x * cos + swapped * sin


def _lane_head(w=PACK_W):
    return lax.broadcasted_iota(jnp.int32, (1, w), 1) // DH


def _row_blocks(n_rows, body):
    n_blk = n_rows // ROW_BLK
    if n_blk == 1:
        body(0)
    else:
        def step(i, carry):
            body(pl.multiple_of(i * ROW_BLK, ROW_BLK))
            return carry
        lax.fori_loop(0, n_blk, step, 0)


def _ada_kernel(cv_ref, w_ref, b_ref, o_ref):
    cv = cv_ref[...]
    o_ref[0] = _dot(_silu(cv), w_ref[0]) + b_ref[0]


def _ada(cv, w_ada, b_ada):
    tn = 1024
    n = 3 * D_MODEL
    return pl.pallas_call(
        _ada_kernel,
        grid=(DEPTH, n // tn),
        in_specs=[
            pl.BlockSpec((8, D_MODEL), lambda l, j: (0, 0)),
            pl.BlockSpec((1, D_MODEL, tn), lambda l, j: (l, 0, j)),
            pl.BlockSpec((1, 1, tn), lambda l, j: (l, 0, j)),
        ],
        out_specs=pl.BlockSpec((1, 8, tn), lambda l, j: (l, 0, j)),
        out_shape=jax.ShapeDtypeStruct((DEPTH, 8, n), F32),
        compiler_params=_params(2),
        name="ada",
    )(cv, w_ada, b_ada.reshape(DEPTH, 1, n))


def _inproj_kernel(x_ref, g_ref, sc_ref, sh_ref, w_ref, zr_ref, zd_ref, zh_ref):
    x = x_ref[...]
    ms = jnp.mean(x * x, axis=-1, keepdims=True)
    h = x * lax.rsqrt(ms + EPS) * g_ref[...]
    h = h * (1.0 + sc_ref[0]) + sh_ref[0]
    z = _dot(h.astype(BF16), w_ref[...])
    zr_ref[...] = z[:, :RET_COLS]
    zd_ref[...] = z[:, RET_COLS:RET_COLS + DIFF_COLS]
    zh_ref[...] = z[:, RET_COLS + DIFF_COLS:]


def _inproj(x2d, g, scale, shift, w_bf, seq_len):
    m = x2d.shape[0]
    per_batch = scale.shape[0] > 1
    tiles_per_seq = seq_len // TM_IN
    mod_map = (lambda i: (i // tiles_per_seq, 0, 0)) if per_batch else (lambda i: (0, 0, 0))
    return pl.pallas_call(
        _inproj_kernel,
        grid=(m // TM_IN,),
        in_specs=[
            pl.BlockSpec((TM_IN, D_MODEL), lambda i: (i, 0)),
            pl.BlockSpec((1, D_MODEL), lambda i: (0, 0)),
            pl.BlockSpec((1, 1, D_MODEL), mod_map),
            pl.BlockSpec((1, 1, D_MODEL), mod_map),
            pl.BlockSpec((D_MODEL, IN_W), lambda i: (0, 0)),
        ],
        out_specs=[
            pl.BlockSpec((TM_IN, RET_COLS), lambda i: (i, 0)),
            pl.BlockSpec((TM_IN, DIFF_COLS), lambda i: (i, 0)),
            pl.BlockSpec((TM_IN, HG_COLS), lambda i: (i, 0)),
        ],
        out_shape=[
            jax.ShapeDtypeStruct((m, RET_COLS), F32),
            jax.ShapeDtypeStruct((m, DIFF_COLS), F32),
            jax.ShapeDtypeStruct((m, HG_COLS), F32),
        ],
        compiler_params=_params(1),
        name="inproj",
    )(x2d, g, scale, shift, w_bf)


def _outproj_kernel(x_ref, r_ref, d_ref, h_ref, gate_ref, w_ref, o_ref):
    acc = _dot(r_ref[...], w_ref[0:PACK_W, :])
    acc += _dot(d_ref[...], w_ref[PACK_W:PACK_W + DIFF_W, :])
    acc += _dot(h_ref[...], w_ref[PACK_W + DIFF_W:, :])
    o_ref[...] = x_ref[...] + gate_ref[0] * acc


def _outproj(x2d, r, d, h, gate, w_bf, seq_len):
    m = x2d.shape[0]
    per_batch = gate.shape[0] > 1
    tiles_per_seq = max(seq_len // TM_OUT, 1)
    tm = min(TM_OUT, seq_len)
    mod_map = (lambda i: (i // tiles_per_seq, 0, 0)) if per_batch else (lambda i: (0, 0, 0))
    return pl.pallas_call(
        _outproj_kernel,
        grid=(m // tm,),
        in_specs=[
            pl.BlockSpec((tm, D_MODEL), lambda i: (i, 0)),
            pl.BlockSpec((tm, PACK_W), lambda i: (i, 0)),
            pl.BlockSpec((tm, DIFF_W), lambda i: (i, 0)),
            pl.BlockSpec((tm, PACK_W), lambda i: (i, 0)),
            pl.BlockSpec((1, 1, D_MODEL), mod_map),
            pl.BlockSpec((MIX_W, D_MODEL), lambda i: (0, 0)),
        ],
        out_specs=pl.BlockSpec((tm, D_MODEL), lambda i: (i, 0)),
        out_shape=jax.ShapeDtypeStruct((m, D_MODEL), F32),
        compiler_params=_params(1),
        name="outproj",
    )(x2d, r, d, h, gate, w_bf)


def _ret_kernel(*refs, seq_len, latent):
    if latent:
        (lg_ref, z_ref, g_ref, cos_ref, sin_ref, s0f_ref, s0b_ref,
         out_ref, mask_scr, st_scr, o_scr) = refs
    else:
        lg_ref, z_ref, g_ref, out_ref, sf_ref, sb_ref, mask_scr, st_scr, o_scr = refs
    C = RET_C
    n_chunks = seq_len // C
    lh = _lane_head()

    def lane_vec(direction):
        v = jnp.zeros((1, PACK_W), F32)
        for h in range(HEADS):
            v = jnp.where(lh == h, lg_ref[direction, h], v)
        return v

    lgf, lgb = lane_vec(0), lane_vec(1)
    tau = lax.broadcasted_iota(jnp.int32, (C, 1), 0).astype(F32)
    qdec_f = jnp.exp((tau + 1.0) * lgf)
    qdec_b = jnp.exp((C - tau) * lgb)
    kdec_f = jnp.exp((C - 1.0 - tau) * lgf)
    kdec_b = jnp.exp(tau * lgb)
    sdec_f = jnp.exp(float(C) * lgf)
    sdec_b = jnp.exp(float(C) * lgb)
    gmask = g_ref[...].astype(F32)

    dd = (lax.broadcasted_iota(jnp.int32, (C, C), 0)
          - lax.broadcasted_iota(jnp.int32, (C, C), 1)).astype(F32)
    for h in range(HEADS):
        mf = jnp.where(dd >= 0, jnp.exp(jnp.maximum(dd, 0.0) * lg_ref[0, h]), 0.0)
        mb = jnp.where(dd <= 0, jnp.exp(jnp.maximum(-dd, 0.0) * lg_ref[1, h]), 0.0)
        mask_scr[h] = mf + mb

    def load(r0):
        q = z_ref[pl.ds(r0, C), 0:PACK_W]
        k = z_ref[pl.ds(r0, C), PACK_W:2 * PACK_W] * (DH ** -0.5)
        v = z_ref[pl.ds(r0, C), 2 * PACK_W:3 * PACK_W]
        if latent:
            cos = cos_ref[pl.ds(r0, C), :]
            sin = sin_ref[pl.ds(r0, C), :]
            q = _rope(q, cos, sin)
            k = _rope(k, cos, sin)
        return q, k, v

    def intra(q, k, v):
        kb = k.astype(BF16)
        vb = v.astype(BF16)
        o = jnp.zeros((C, PACK_W), F32)
        for h in range(HEADS):
            hm = lh == h
            qh = jnp.where(hm, q, 0.0).astype(BF16)
            a = _dot_nt(qh, kb) * mask_scr[h]
            o = o + jnp.where(hm, _dot(a.astype(BF16), vb), 0.0)
        return o

    def finish(o, r0):
        gate = z_ref[pl.ds(r0, C), 3 * PACK_W:4 * PACK_W]
        y = _group_rms(o, g_ref[0:LANES, 0:LANES], DH) * _silu(gate)
        out_ref[pl.ds(r0, C), :] = y.astype(out_ref.dtype)

    def state_step(st, k, v, kdec, sdec):
        upd = _dot_tn(v.astype(BF16), (k * kdec).astype(BF16))
        return (st * sdec + upd) * gmask

    if not latent:
        q, k, v = load(0)
        finish(intra(q, k, v), 0)
        vb = v.astype(BF16)
        for ref, kdec in ((sf_ref, kdec_f), (sb_ref, kdec_b)):
            s = _dot_tn((k * kdec).astype(BF16), vb) * gmask
            for h in range(HEADS):
                ref[0, h] = s[h * DH:(h + 1) * DH, h * DH:(h + 1) * DH]
        return

    st_scr[...] = s0b_ref[0]

    def bwd_body(i, carry):
        r0 = pl.multiple_of((n_chunks - 1 - i) * C, C)
        q, k, v = load(r0)
        st = st_scr[...]
        o_scr[pl.ds(r0, C), :] = _dot_nt((q * qdec_b).astype(BF16), st.astype(BF16))
        st_scr[...] = state_step(st, k, v, kdec_b, sdec_b)
        return carry

    lax.fori_loop(0, n_chunks, bwd_body, 0)

    st_scr[...] = s0f_ref[0]

    def fwd_body(i, carry):
        r0 = pl.multiple_of(i * C, C)
        q, k, v = load(r0)
        st = st_scr[...]
        o = intra(q, k, v) + o_scr[pl.ds(r0, C), :]
        o = o + _dot_nt((q * qdec_f).astype(BF16), st.astype(BF16))
        finish(o, r0)
        st_scr[...] = state_step(st, k, v, kdec_f, sdec_f)
        return carry

    lax.fori_loop(0, n_chunks, fwd_body, 0)


def _retention(z_ret, log_g, g256, batch, seq_len, latent, rope=None, states=None):
    m = batch * seq_len
    row = lambda b: (b, 0)
    const2 = lambda b: (0, 0)
    in_specs = [
        pl.BlockSpec(memory_space=pltpu.SMEM),
        pl.BlockSpec((seq_len, RET_COLS), row),
        pl.BlockSpec((PACK_W, PACK_W), const2),
    ]
    args = [log_g, z_ret, g256]
    out_specs = [pl.BlockSpec((seq_len, PACK_W), row)]
    out_shape = [jax.ShapeDtypeStruct((m, PACK_W), BF16)]
    if latent:
        state_spec = pl.BlockSpec((1, PACK_W, PACK_W), lambda b: (b, 0, 0))
        in_specs += [pl.BlockSpec((seq_len, PACK_W), const2)] * 2 + [state_spec] * 2
        args += [rope[0], rope[1], states[0], states[1]]
    else:
        out_specs += [pl.BlockSpec((1, HEADS, DH, DH), lambda b: (b, 0, 0, 0))] * 2
        out_shape += [jax.ShapeDtypeStruct((batch, HEADS, DH, DH), F32)] * 2
    return pl.pallas_call(
        functools.partial(_ret_kernel, seq_len=seq_len, latent=latent),
        grid=(batch,),
        in_specs=in_specs,
        out_specs=out_specs,
        out_shape=out_shape,
        scratch_shapes=[
            pltpu.VMEM((HEADS, RET_C, RET_C), F32),
            pltpu.VMEM((PACK_W, PACK_W), F32),
            pltpu.VMEM((seq_len, PACK_W), F32),
        ],
        compiler_params=_params(1),
        name="ret_latent" if latent else "ret_ctx",
    )(*args)


def _attn_kernel(*refs, seq_len, past_len, latent, lam_init, layer):
    if latent:
        (lam_ref, q_ref, k_ref, v_ref, gate_ref, gq_ref, gk_ref, g_ref,
         cosq_ref, sinq_ref, cosk_ref, sink_ref, kc_ref, vc_ref,
         out_ref, k_scr, v_scr) = refs
        prev_refs = ()
    else:
        (lam_ref, q_ref, k_ref, v_ref, gate_ref, gq_ref, gk_ref, g_ref) = refs[:8]
        prev_refs = refs[8:-5]
        out_ref, kn_ref, vn_ref, k_scr, v_scr = refs[-5:]
    ones = jnp.ones((ROW_BLK, DIFF_DV), BF16)

    def put_values(sl, v):
        for h in range(HEADS):
            v_scr[sl, 2 * h * DIFF_DV:(2 * h + 1) * DIFF_DV] = (
                v[:, h * DIFF_DV:(h + 1) * DIFF_DV].astype(BF16))
            v_scr[sl, (2 * h + 1) * DIFF_DV:(2 * h + 2) * DIFF_DV] = ones

    @pl.when(pl.program_id(1) == 0)
    def _():
        for r in range(seq_len // ROW_BLK):
            sl = pl.ds(r * ROW_BLK, ROW_BLK)
            kn = _group_rms(k_ref[sl, :], g_ref[...], DH) * gk_ref[...]
            v = v_ref[sl, :]
            if latent:
                kn = _rope(kn, cosk_ref[sl, :], sink_ref[sl, :])
            else:
                kn_ref[0, layer, sl, :] = kn
                vn_ref[0, layer, sl, :] = v
            k_scr[sl, :] = kn.astype(BF16)
            put_values(sl, v)
        if prev_refs:
            kn_ref[0, 0:layer] = prev_refs[0][0]
            vn_ref[0, 0:layer] = prev_refs[1][0]
        if latent:
            for r in range(past_len // ROW_BLK):
                src = pl.ds(r * ROW_BLK, ROW_BLK)
                dst = pl.ds(seq_len + r * ROW_BLK, ROW_BLK)
                k_scr[dst, :] = kc_ref[0, src, :].astype(BF16)
                put_values(dst, vc_ref[0, src, :])

    lam = lam_ref[0]
    qn = _group_rms(q_ref[...], g_ref[...], DH) * gq_ref[...]
    if latent:
        qn = _rope(qn, cosq_ref[...], sinq_ref[...])
    qn = qn * (DH ** -0.5 * LOG2E)
    lane = lax.broadcasted_iota(jnp.int32, (1, 2 * DH), 1)
    for h in range(HEADS):
        hs = slice(h * 2 * DH, (h + 1) * 2 * DH)
        vx = v_scr[:, 2 * h * DIFF_DV:(2 * h + 2) * DIFF_DV]
        qh = qn[:, hs]
        kh = k_scr[:, hs]
        parts = []
        for first_map in (True, False):
            qm = jnp.where((lane < DH) == first_map, qh, 0.0).astype(BF16)
            s = _dot_nt(qm, kh)
            e = jnp.exp2(s - jnp.max(s, axis=-1, keepdims=True)).astype(BF16)
            ox = _dot(e, vx)
            parts.append(ox[:, :DIFF_DV] / ox[:, DIFF_DV:])
        o = parts[0] - lam * parts[1]
        ms = jnp.mean(o * o, axis=-1, keepdims=True)
        y = o * lax.rsqrt(ms + EPS) * (1.0 - lam_init) * _silu(gate_ref[:, hs])
        out_ref[:, hs] = y.astype(out_ref.dtype)


def _diff_attention(z_diff, lam, gq, gk, g128, batch, seq_len, latent, lam_init, layer,
                    rope=None, ctx=None, caches=None):
    m = batch * seq_len
    nq = seq_len // TQ
    past_len = ctx[0].shape[1] if latent else 0
    qmap = lambda col: (lambda b, i: (b * nq + i, col))
    kvmap = lambda col: (lambda b, i: (b, col))
    const2 = lambda b, i: (0, 0)
    in_specs = [
        pl.BlockSpec(memory_space=pltpu.SMEM),
        pl.BlockSpec((TQ, DIFF_W), qmap(0)),
        pl.BlockSpec((seq_len, DIFF_W), kvmap(1)),
        pl.BlockSpec((seq_len, DIFF_W), kvmap(2)),
        pl.BlockSpec((TQ, DIFF_W), qmap(3)),
        pl.BlockSpec((1, DIFF_W), const2),
        pl.BlockSpec((1, DIFF_W), const2),
        pl.BlockSpec((LANES, LANES), const2),
    ]
    args = [lam, z_diff, z_diff, z_diff, z_diff, gq, gk, g128]
    out_specs = [pl.BlockSpec((TQ, DIFF_W), qmap(0))]
    out_shape = [jax.ShapeDtypeStruct((m, DIFF_W), BF16)]
    if latent:
        in_specs += [
            pl.BlockSpec((TQ, DIFF_W), lambda b, i: (i, 0)),
            pl.BlockSpec((TQ, DIFF_W), lambda b, i: (i, 0)),
            pl.BlockSpec((seq_len, DIFF_W), const2),
            pl.BlockSpec((seq_len, DIFF_W), const2),
            pl.BlockSpec((1, past_len, DIFF_W), lambda b, i: (b, 0, 0)),
            pl.BlockSpec((1, past_len, DIFF_W), lambda b, i: (b, 0, 0)),
        ]
        args += [rope[0], rope[1], rope[0], rope[1], ctx[0], ctx[1]]
    else:
        cache_spec = lambda n: pl.BlockSpec((1, n, seq_len, DIFF_W), lambda b, i: (b, 0, 0, 0))
        out_specs += [cache_spec(layer + 1)] * 2
        out_shape += [jax.ShapeDtypeStruct((batch, layer + 1, seq_len, DIFF_W), F32)] * 2
        if layer > 0:
            in_specs += [cache_spec(layer)] * 2
            args += list(caches)
    return pl.pallas_call(
        functools.partial(_attn_kernel, seq_len=seq_len, past_len=past_len,
                          latent=latent, lam_init=lam_init, layer=layer),
        grid=(batch, nq),
        in_specs=in_specs,
        out_specs=out_specs,
        out_shape=out_shape,
        scratch_shapes=[
            pltpu.VMEM((seq_len + past_len, DIFF_W), BF16),
            pltpu.VMEM((seq_len + past_len, 2 * DIFF_W), BF16),
        ],
        compiler_params=_params(2),
        name="attn_latent" if latent else "attn_ctx",
    )(*args)


def _hgrn_kernel(*refs, seq_len, latent):
    if latent:
        (z_ref, lb_ref, tri_ref, g_ref, lm_ref, dm_ref, s0f_ref, s0b_ref,
         out_ref, qi_scr, ks_scr, dec_scr, o_scr, st_scr) = refs
    else:
        (z_ref, lb_ref, tri_ref, g_ref, lm_ref, dm_ref,
         out_ref, sf_ref, sb_ref, qi_scr, ks_scr, dec_scr, o_scr, st_scr) = refs
    C, SUB = HG_C, HG_SUB
    n_chunks = seq_len // C
    chunks_per_blk = ROW_BLK // C
    lh = _lane_head()
    sub_row = lax.broadcasted_iota(jnp.int32, (ROW_BLK, 1), 0) & (SUB - 1)

    def tile_roll(x, shift):
        return jnp.concatenate(
            [pltpu.roll(x[i:i + SUBLANES], shift, 0) for i in range(0, ROW_BLK, SUBLANES)], axis=0)

    def head_rep(x):
        return jnp.concatenate([jnp.where(lh == h, x, jnp.zeros_like(x)) for h in range(HEADS)],
                               axis=0)

    def intra(blk):
        start = blk * ROW_BLK if isinstance(blk, int) else pl.multiple_of(blk * ROW_BLK, ROW_BLK)
        rows = pl.ds(start, ROW_BLK)
        lb = lb_ref[...]
        q = z_ref[rows, 0:PACK_W] * (DH ** -0.5)
        v = z_ref[rows, 3 * PACK_W:4 * PACK_W]
        v_reps = [head_rep(v[c * C:(c + 1) * C].astype(BF16)) for c in range(chunks_per_blk)]
        kks, bcs, a_nears = [], [], []
        for d in range(2):
            f = lb + (1.0 - lb) * jax.nn.sigmoid(z_ref[rows, (1 + d) * PACK_W:(2 + d) * PACK_W])
            kks.append(1.0 - f)
            l2 = jnp.log(f) * LOG2E
            hi = l2.astype(BF16)
            r1 = l2 - hi.astype(F32)
            mid = r1.astype(BF16)
            lo = (r1 - mid.astype(F32)).astype(BF16)
            cs = _dot(tri_ref[d], jnp.concatenate([hi, mid, lo], axis=1))
            bcs.append(cs[:, 0:PACK_W] + cs[:, PACK_W:2 * PACK_W] + cs[:, 2 * PACK_W:])

        for d in range(2):
            bwd = d == 1
            kk, bc = kks[d], bcs[d]
            ps = [(q * kk).astype(BF16)]
            for dist in range(1, SUB):
                shift = SUBLANES - dist if bwd else dist
                valid = (sub_row < SUB - dist) if bwd else (sub_row >= dist)
                arg = jnp.where(valid, bc - tile_roll(bc, shift), NEG_BIG)
                ps.append((q * tile_roll(kk, shift) * jnp.exp2(arg)).astype(BF16))
            r = _dot(jnp.concatenate(ps, axis=0), g_ref[...])
            a_near = r[0:ROW_BLK] * dm_ref[d, 0]
            for dist in range(1, SUB):
                a_near = a_near + r[dist * ROW_BLK:(dist + 1) * ROW_BLK] * dm_ref[d, dist]
            a_nears.append(a_near)

        for c in range(chunks_per_blk):
            for d in range(2):
                bwd = d == 1
                kk, bc, a_near = kks[d], bcs[d], a_nears[d]
                cs_ = slice(c * C, (c + 1) * C)
                qc, kc, bcc = q[cs_], kk[cs_], bc[cs_]
                last = bcc[0:1] if bwd else bcc[C - 1:C]
                a = a_near[cs_]
                for li, m in enumerate(HG_LEVELS):
                    pieces = []
                    for b2 in range(C // (2 * m)):
                        rr = 2 * m * b2 + (m if bwd else m - 1)
                        pieces.append(jnp.broadcast_to(bcc[rr:rr + 1], (2 * m, PACK_W)))
                    anchor = jnp.concatenate(pieces, axis=0) if len(pieces) > 1 else pieces[0]
                    e = jnp.exp2(-jnp.abs(bcc - anchor))
                    a = a + _dot_nt((qc * e).astype(BF16),
                                    head_rep((kc * e).astype(BF16))) * lm_ref[d, li]
                crow = pl.ds(start + c * C, C)
                o_scr[d, crow, :] = _dot(a.astype(BF16), v_reps[c])
                qi_scr[d, crow, :] = (qc * jnp.exp2(bcc)).astype(BF16)
                ks_scr[d, crow, :] = (kc * jnp.exp2(last - bcc)).astype(BF16)
                slot = (blk * chunks_per_blk + c) * SUBLANES
                if not isinstance(slot, int):
                    slot = pl.multiple_of(slot, SUBLANES)
                dec_scr[d, pl.ds(slot, SUBLANES), :] = jnp.broadcast_to(
                    jnp.exp2(last), (SUBLANES, PACK_W))

        if not latent:
            vb = v.astype(BF16)
            for d, ref in enumerate((sf_ref, sb_ref)):
                bwd = d == 1
                sweep = range(chunks_per_blk - 1, -1, -1) if bwd else range(chunks_per_blk)
                tail = jnp.zeros((1, PACK_W), F32)
                decayed = [None] * chunks_per_blk
                for c in reversed(sweep):
                    cs_ = slice(c * C, (c + 1) * C)
                    bcc = bcs[d][cs_]
                    total = tail + (bcc[0:1] if bwd else bcc[C - 1:C])
                    decayed[c] = (kks[d][cs_] * jnp.exp2(total - bcc)).astype(BF16)
                    tail = total
                s = _dot_tn(jnp.concatenate(decayed, axis=0), vb) * g_ref[...].astype(F32)
                for h in range(HEADS):
                    ref[0, h] = s[h * DH:(h + 1) * DH, h * DH:(h + 1) * DH]

    n_blk = seq_len // ROW_BLK
    assert latent or n_blk == 1
    if n_blk == 1:
        intra(0)
    else:
        def intra_step(i, carry):
            intra(i)
            return carry
        lax.fori_loop(0, n_blk, intra_step, 0)

    for d in range(2):
        if latent:
            st_scr[d] = (s0f_ref, s0b_ref)[d][0]
        else:
            st_scr[d] = jnp.zeros((PACK_W, PACK_W), F32)
    gmask = g_ref[...].astype(F32)

    def body(i, carry):
        for d in range(2):
            c = (n_chunks - 1 - i) if d == 1 else i
            rows = pl.ds(pl.multiple_of(c * C, C), C)
            st = st_scr[d]
            o_scr[d, rows, :] += _dot_nt(qi_scr[d, rows, :], st.astype(BF16))
            dec = dec_scr[d, pl.ds(pl.multiple_of(c * SUBLANES, SUBLANES), 1), :]
            vb = z_ref[rows, 3 * PACK_W:4 * PACK_W].astype(BF16)
            st_scr[d] = st * dec + _dot_tn(vb, ks_scr[d, rows, :]) * gmask
        return carry

    lax.fori_loop(0, n_chunks, body, 0, unroll=4)

    def finish(start):
        rows = pl.ds(start, ROW_BLK)
        tot = o_scr[0, rows, :] + o_scr[1, rows, :]
        gate = z_ref[rows, 4 * PACK_W:5 * PACK_W]
        y = _group_rms(tot, g_ref[0:LANES, 0:LANES], DH) * _silu(gate)
        out_ref[rows, :] = y.astype(out_ref.dtype)

    _row_blocks(seq_len, finish)


def _hgrn(z_hg, lb, consts, batch, seq_len, latent, states=None):
    m = batch * seq_len
    tri, g256, lm, dm = consts
    row = lambda b: (b, 0)
    in_specs = [
        pl.BlockSpec((seq_len, HG_COLS), row),
        pl.BlockSpec((1, PACK_W), lambda b: (0, 0)),
        pl.BlockSpec(tri.shape, lambda b: (0, 0, 0)),
        pl.BlockSpec((PACK_W, PACK_W), lambda b: (0, 0)),
        pl.BlockSpec(lm.shape, lambda b: (0, 0, 0, 0)),
        pl.BlockSpec(dm.shape, lambda b: (0, 0, 0, 0)),
    ]
    args = [z_hg, lb, tri, g256, lm, dm]
    out_specs = [pl.BlockSpec((seq_len, PACK_W), row)]
    out_shape = [jax.ShapeDtypeStruct((m, PACK_W), BF16)]
    if latent:
        in_specs += [pl.BlockSpec((1, PACK_W, PACK_W), lambda b: (b, 0, 0))] * 2
        args += [states[0], states[1]]
    else:
        out_specs += [pl.BlockSpec((1, HEADS, DH, DH), lambda b: (b, 0, 0, 0))] * 2
        out_shape += [jax.ShapeDtypeStruct((batch, HEADS, DH, DH), F32)] * 2
    return pl.pallas_call(
        functools.partial(_hgrn_kernel, seq_len=seq_len, latent=latent),
        grid=(batch,),
        in_specs=in_specs,
        out_specs=out_specs,
        out_shape=out_shape,
        scratch_shapes=[
            pltpu.VMEM((2, seq_len, PACK_W), BF16),
            pltpu.VMEM((2, seq_len, PACK_W), BF16),
            pltpu.VMEM((2, seq_len // HG_C * SUBLANES, PACK_W), F32),
            pltpu.VMEM((2, seq_len, PACK_W), F32),
            pltpu.VMEM((2, PACK_W, PACK_W), F32),
        ],
        compiler_params=_params(1),
        name="hgrn_latent" if latent else "hgrn_ctx",
    )(*args)


def _block_ones(n):
    idx = np.arange(n) // DH
    return jnp.asarray((idx[:, None] == idx[None, :]).astype(np.float32), dtype=BF16)


def _hgrn_consts():
    C = HG_C
    t = np.arange(ROW_BLK)
    same_chunk = (t[:, None] // C) == (t[None, :] // C)
    tri_f = same_chunk & (t[None, :] <= t[:, None])
    tri_b = same_chunk & (t[None, :] >= t[:, None])
    tc = np.arange(C)
    lms = []
    for fwd in (True, False):
        per_level = []
        for m in HG_LEVELS:
            same = (tc[:, None] // (2 * m)) == (tc[None, :] // (2 * m))
            upper = (tc % (2 * m)) >= m
            later, earlier = (upper, ~upper) if fwd else (~upper, upper)
            per_level.append(np.tile(same & later[:, None] & earlier[None, :], (1, HEADS)))
        lms.append(np.stack(per_level))
    dms = [[np.tile(tc[None, :] == tc[:, None] + sign * dist, (ROW_BLK // C, HEADS))
            for dist in range(HG_SUB)] for sign in (-1, 1)]
    return (jnp.asarray(np.stack([tri_f, tri_b]).astype(np.float32), dtype=BF16),
            _block_ones(PACK_W),
            jnp.asarray(np.stack(lms).astype(np.float32), dtype=F32),
            jnp.asarray(np.stack(dms).astype(np.float32), dtype=F32))


def _rope_tables(seq_len, width):
    t = np.arange(seq_len)
    pos = np.stack([t // GRID_W, t % GRID_W], axis=1).astype(np.float32)
    j = np.arange(DH)
    axis = j // (2 * ROPE_PAIR)
    jj = j % (2 * ROPE_PAIR)
    inv = (ROPE_BASE ** (-(jj % ROPE_PAIR).astype(np.float64) / ROPE_PAIR)).astype(np.float32)
    ang = (pos[:, axis] * inv[None, :]).astype(np.float64)
    cos = np.cos(ang)
    sin = np.where(jj < ROPE_PAIR, -np.sin(ang), np.sin(ang))
    reps = width // DH
    return (jnp.asarray(np.tile(cos, (1, reps)), dtype=F32),
            jnp.asarray(np.tile(sin, (1, reps)), dtype=F32))


def _to_blockdiag_t(s):
    b = s.shape[0]
    eye = jnp.eye(HEADS, dtype=s.dtype)
    return jnp.einsum('bhde,hg->bhegd', s, eye).reshape(b, PACK_W, PACK_W)


def _layer(x2d, batch, seq_len, mod, l, p, latent, ctx, caches):
    shift, scale, gate = mod
    z_ret, z_diff, z_hg = _inproj(x2d, p['norm_g'][l], scale, shift, p['w_in'][l], seq_len)
    lam_init = 0.8 - 0.6 * math.exp(-0.3 * l)
    if latent:
        k_ctx, v_ctx, s_rf, s_rb, s_hf, s_hb = ctx
        r = _retention(z_ret, p['log_g'][l], p['g256'], batch, seq_len, True,
                       rope=p['rope256'], states=(s_rf, s_rb))[0]
        a = _diff_attention(z_diff, p['lam'][l], p['gq'][l], p['gk'][l], p['g128'], batch,
                            seq_len, True, lam_init, l, rope=p['rope512'],
                            ctx=(k_ctx, v_ctx))[0]
        hgo = _hgrn(z_hg, p['lb'][l], p['hg_consts'], batch, seq_len, True,
                    states=(s_hf, s_hb))[0]
        extras = None
    else:
        r, s_rf, s_rb = _retention(z_ret, p['log_g'][l], p['g256'], batch, seq_len, False)
        a, kcache, vcache = _diff_attention(z_diff, p['lam'][l], p['gq'][l], p['gk'][l],
                                            p['g128'], batch, seq_len, False, lam_init, l,
                                            caches=caches)
        hgo, s_hf, s_hb = _hgrn(z_hg, p['lb'][l], p['hg_consts'], batch, seq_len, False)
        extras = ((kcache, vcache), (s_rf, s_rb, s_hf, s_hb))
    y = _outproj(x2d, r, a, hgo, gate, p['w_out'][l], seq_len)
    return y, extras


def kernel(x_prompt, x_sample, c, c_ctx, cache_diff_k, cache_diff_v, state_ret_fwd,
           state_ret_bwd, state_hgrn_fwd, state_hgrn_bwd, norm_g, w_ada, b_ada, w_in,
           w_out, ret_decay_logit, diff_qn_g, diff_kn_g, diff_lambda, hgrn_lb_logit):
    batch, seq, _ = x_prompt.shape
    dec_batch, dec_seq, _ = x_sample.shape
    past_len = cache_diff_k.shape[2]

    cv = jnp.zeros((8, D_MODEL), F32).at[0].set(c_ctx).at[1:1 + dec_batch].set(c)
    mods = _ada(cv, w_ada, b_ada)

    lb_all = jax.nn.softmax(hgrn_lb_logit.astype(F32), axis=0)
    lb_all = jnp.cumsum(lb_all, axis=0) - lb_all[0]
    lp = diff_lambda.astype(F32)
    lam_inits = jnp.asarray([0.8 - 0.6 * math.exp(-0.3 * l) for l in range(DEPTH)], F32)
    lam = (jnp.exp(jnp.sum(lp[:, 0] * lp[:, 1], axis=-1))
           - jnp.exp(jnp.sum(lp[:, 2] * lp[:, 3], axis=-1)) + lam_inits)
    p = {
        'norm_g': norm_g.reshape(DEPTH, 1, D_MODEL),
        'w_in': w_in.astype(BF16),
        'w_out': w_out.astype(BF16),
        'log_g': jax.nn.log_sigmoid(ret_decay_logit.astype(F32)),
        'lam': lam.reshape(DEPTH, 1),
        'gq': jnp.tile(diff_qn_g, (1, DIFF_W // DH)).reshape(DEPTH, 1, DIFF_W),
        'gk': jnp.tile(diff_kn_g, (1, DIFF_W // DH)).reshape(DEPTH, 1, DIFF_W),
        'lb': lb_all.reshape(DEPTH, 1, PACK_W),
        'g256': _block_ones(PACK_W),
        'g128': _block_ones(LANES),
        'hg_consts': _hgrn_consts(),
        'rope256': _rope_tables(dec_seq, PACK_W),
        'rope512': _rope_tables(dec_seq, DIFF_W),
    }

    def split_mod(rows):
        return tuple(rows[:, None, j * D_MODEL:(j + 1) * D_MODEL] for j in range(3))

    y = x_prompt.reshape(batch * seq, D_MODEL)
    caches = None
    states = []
    for l in range(DEPTH):
        y, (caches, st) = _layer(y, batch, seq, split_mod(mods[l, 0:1]), l, p, False, None, caches)
        states.append(st)
    y_prompt = y.reshape(batch, seq, D_MODEL)
    new_k = caches[0].reshape(batch, DEPTH, seq, HEADS, 2, DH)
    new_v = caches[1].reshape(batch, DEPTH, seq, HEADS, DIFF_DV)
    new_states = [jnp.stack([st[i] for st in states], axis=1) for i in range(4)]

    y = x_sample.reshape(dec_batch * dec_seq, D_MODEL)
    for l in range(DEPTH):
        ctx = (cache_diff_k[:, l].reshape(dec_batch, past_len, DIFF_W),
               cache_diff_v[:, l].reshape(dec_batch, past_len, DIFF_W),
               _to_blockdiag_t(state_ret_fwd[:, l]), _to_blockdiag_t(state_ret_bwd[:, l]),
               _to_blockdiag_t(state_hgrn_fwd[:, l]), _to_blockdiag_t(state_hgrn_bwd[:, l]))
        y, _ = _layer(y, dec_batch, dec_seq, split_mod(mods[l, 1:1 + dec_batch]), l, p, True,
                      ctx, None)
    y_sample = y.reshape(dec_batch, dec_seq, D_MODEL)

    return (y_prompt, y_sample, new_k, new_v, *new_states)
```

```python
import functools
import math

import numpy as np
import jax
import jax.numpy as jnp
from jax import lax
from jax.experimental import pallas as pl
from jax.experimental.pallas import tpu as pltpu

F32 = jnp.float32
BF16 = jnp.bfloat16

D_MODEL = 1024
DEPTH = 2
GRID_W = 64
HEADS = 4
DH = 64
PACK_W = HEADS * DH
DIFF_W = 512
DIFF_DV = 128
RET_COLS = 4 * PACK_W
DIFF_COLS = 4 * DIFF_W
HG_COLS = 5 * PACK_W
IN_W = RET_COLS + DIFF_COLS + HG_COLS
MIX_W = PACK_W + DIFF_W + PACK_W
ROPE_BASE = 10000.0
ROPE_PAIR = 16
EPS = 1e-6
LOG2E = 1.4426950408889634
NEG_BIG = -1e30

RET_C = 256
HG_C = 64
HG_SUB = 4
HG_LEVELS = (4, 8, 16, 32)
SUBLANES = 8
LANES = 128
ROW_BLK = 256
TQ = 256
TM_IN = 512
TM_OUT = 512

VMEM_LIMIT = 56 * 1024 * 1024


def _params(n_axes):
    return pltpu.CompilerParams(
        dimension_semantics=("arbitrary",) * n_axes, vmem_limit_bytes=VMEM_LIMIT)


def _dot(a, b):
    return jnp.dot(a, b, preferred_element_type=F32)


def _dot_nt(a, b):
    return lax.dot_general(a, b, (((1,), (1,)), ((), ())), preferred_element_type=F32)


def _dot_tn(a, b):
    return lax.dot_general(a, b, (((0,), (0,)), ((), ())), preferred_element_type=F32)


def _silu(x):
    return x * jax.nn.sigmoid(x)


def _group_rms(x, g, width):
    n, w = x.shape
    x2 = x * x
    cols = range(0, w, LANES)
    stacked = jnp.concatenate([x2[:, j:j + LANES] for j in cols], axis=0)
    hi = stacked.astype(BF16)
    lo = (stacked - hi.astype(F32)).astype(BF16)
    sums = _dot(hi, g) + _dot(lo, g)
    ms = jnp.concatenate([sums[i * n:(i + 1) * n] for i in range(len(cols))], axis=1)
    return x * lax.rsqrt(ms * (1.0 / width) + EPS)


def _rope(x, cos, sin):
    w = x.shape[-1]
    lane = lax.broadcasted_iota(jnp.int32, (1, w), 1)
    first = (lane & (2 * ROPE_PAIR - 1)) < ROPE_PAIR
    swapped = jnp.where(first, pltpu.roll(x, w - ROPE_PAIR, 1), pltpu.roll(x, ROPE_PAIR, 1))
    return x * cos + swapped * sin


def _lane_head(w=PACK_W):
    return lax.broadcasted_iota(jnp.int32, (1, w), 1) // DH


def _row_blocks(n_rows, body):
    n_blk = n_rows // ROW_BLK
    if n_blk == 1:
        body(0)
    else:
        def step(i, carry):
            body(pl.multiple_of(i * ROW_BLK, ROW_BLK))
            return carry
        lax.fori_loop(0, n_blk, step, 0)


def _ada_kernel(cv_ref, w_ref, b_ref, o_ref):
    cv = cv_ref[...]
    o_ref[0] = _dot(_silu(cv), w_ref[0]) + b_ref[0]


def _ada(cv, w_ada, b_ada):
    tn = 1024
    n = 3 * D_MODEL
    return pl.pallas_call(
        _ada_kernel,
        grid=(DEPTH, n // tn),
        in_specs=[
            pl.BlockSpec((8, D_MODEL), lambda l, j: (0, 0)),
            pl.BlockSpec((1, D_MODEL, tn), lambda l, j: (l, 0, j)),
            pl.BlockSpec((1, 1, tn), lambda l, j: (l, 0, j)),
        ],
        out_specs=pl.BlockSpec((1, 8, tn), lambda l, j: (l, 0, j)),
        out_shape=jax.ShapeDtypeStruct((DEPTH, 8, n), F32),
        compiler_params=_params(2),
        name="ada",
    )(cv, w_ada, b_ada.reshape(DEPTH, 1, n))


def _inproj_kernel(x_ref, g_ref, sc_ref, sh_ref, w_ref, zr_ref, zd_ref, zh_ref):
    x = x_ref[...]
    ms = jnp.mean(x * x, axis=-1, keepdims=True)
    h = x * lax.rsqrt(ms + EPS) * g_ref[...]
    h = h * (1.0 + sc_ref[0]) + sh_ref[0]
    z = _dot(h.astype(BF16), w_ref[...])
    zr_ref[...] = z[:, :RET_COLS]
    zd_ref[...] = z[:, RET_COLS:RET_COLS + DIFF_COLS]
    zh_ref[...] = z[:, RET_COLS + DIFF_COLS:]


def _inproj(x2d, g, scale, shift, w_bf, seq_len):
    m = x2d.shape[0]
    per_batch = scale.shape[0] > 1
    tiles_per_seq = seq_len // TM_IN
    mod_map = (lambda i: (i // tiles_per_seq, 0, 0)) if per_batch else (lambda i: (0, 0, 0))
    return pl.pallas_call(
        _inproj_kernel,
        grid=(m // TM_IN,),
        in_specs=[
            pl.BlockSpec((TM_IN, D_MODEL), lambda i: (i, 0)),
            pl.BlockSpec((1, D_MODEL), lambda i: (0, 0)),
            pl.BlockSpec((1, 1, D_MODEL), mod_map),
            pl.BlockSpec((1, 1, D_MODEL), mod_map),
            pl.BlockSpec((D_MODEL, IN_W), lambda i: (0, 0)),
        ],
        out_specs=[
            pl.BlockSpec((TM_IN, RET_COLS), lambda i: (i, 0)),
            pl.BlockSpec((TM_IN, DIFF_COLS), lambda i: (i, 0)),
            pl.BlockSpec((TM_IN, HG_COLS), lambda i: (i, 0)),
        ],
        out_shape=[
            jax.ShapeDtypeStruct((m, RET_COLS), F32),
            jax.ShapeDtypeStruct((m, DIFF_COLS), F32),
            jax.ShapeDtypeStruct((m, HG_COLS), F32),
        ],
        compiler_params=_params(1),
        name="inproj",
    )(x2d, g, scale, shift, w_bf)


def _outproj_kernel(x_ref, r_ref, d_ref, h_ref, gate_ref, w_ref, o_ref):
    acc = _dot(r_ref[...], w_ref[0:PACK_W, :])
    acc += _dot(d_ref[...], w_ref[PACK_W:PACK_W + DIFF_W, :])
    acc += _dot(h_ref[...], w_ref[PACK_W + DIFF_W:, :])
    o_ref[...] = x_ref[...] + gate_ref[0] * acc


def _outproj(x2d, r, d, h, gate, w_bf, seq_len):
    m = x2d.shape[0]
    per_batch = gate.shape[0] > 1
    tiles_per_seq = max(seq_len // TM_OUT, 1)
    tm = min(TM_OUT, seq_len)
    mod_map = (lambda i: (i // tiles_per_seq, 0, 0)) if per_batch else (lambda i: (0, 0, 0))
    return pl.pallas_call(
        _outproj_kernel,
        grid=(m // tm,),
        in_specs=[
            pl.BlockSpec((tm, D_MODEL), lambda i: (i, 0)),
            pl.BlockSpec((tm, PACK_W), lambda i: (i, 0)),
            pl.BlockSpec((tm, DIFF_W), lambda i: (i, 0)),
            pl.BlockSpec((tm, PACK_W), lambda i: (i, 0)),
            pl.BlockSpec((1, 1, D_MODEL), mod_map),
            pl.BlockSpec((MIX_W, D_MODEL), lambda i: (0, 0)),
        ],
        out_specs=pl.BlockSpec((tm, D_MODEL), lambda i: (i, 0)),
        out_shape=jax.ShapeDtypeStruct((m, D_MODEL), F32),
        compiler_params=_params(1),
        name="outproj",
    )(x2d, r, d, h, gate, w_bf)


def _ret_kernel(*refs, seq_len, latent):
    if latent:
        (lg_ref, z_ref, g_ref, cos_ref, sin_ref, s0f_ref, s0b_ref,
         out_ref, mask_scr, o_scr) = refs
    else:
        lg_ref, z_ref, g_ref, out_ref, sf_ref, sb_ref, mask_scr, o_scr = refs
    C = RET_C
    n_chunks = seq_len // C
    lh = _lane_head()

    def lane_vec(direction):
        v = jnp.zeros((1, PACK_W), F32)
        for h in range(HEADS):
            v = jnp.where(lh == h, lg_ref[direction, h], v)
        return v

    lgf, lgb = lane_vec(0), lane_vec(1)
    tau = lax.broadcasted_iota(jnp.int32, (C, 1), 0).astype(F32)
    qdec_f = jnp.exp((tau + 1.0) * lgf)
    qdec_b = jnp.exp((C - tau) * lgb)
    kdec_f = jnp.exp((C - 1.0 - tau) * lgf)
    kdec_b = jnp.exp(tau * lgb)
    sdec_f = jnp.exp(float(C) * lgf)
    sdec_b = jnp.exp(float(C) * lgb)
    gmask = g_ref[...].astype(F32)

    @pl.when(pl.program_id(0) == 0)
    def _():
        dd = (lax.broadcasted_iota(jnp.int32, (C, C), 0)
              - lax.broadcasted_iota(jnp.int32, (C, C), 1)).astype(F32)
        for h in range(HEADS):
            mf = jnp.where(dd >= 0, jnp.exp(jnp.maximum(dd, 0.0) * lg_ref[0, h]), 0.0)
            mb = jnp.where(dd <= 0, jnp.exp(jnp.maximum(-dd, 0.0) * lg_ref[1, h]), 0.0)
            mask_scr[h] = mf + mb

    def load(r0):
        q = z_ref[pl.ds(r0, C), 0:PACK_W]
        k = z_ref[pl.ds(r0, C), PACK_W:2 * PACK_W] * (DH ** -0.5)
        v = z_ref[pl.ds(r0, C), 2 * PACK_W:3 * PACK_W]
        if latent:
            cos = cos_ref[pl.ds(r0, C), :]
            sin = sin_ref[pl.ds(r0, C), :]
            q = _rope(q, cos, sin)
            k = _rope(k, cos, sin)
        return q, k, v

    def intra(q, k, v):
        kb = k.astype(BF16)
        vb = v.astype(BF16)
        o = jnp.zeros((C, PACK_W), F32)
        for h in range(HEADS):
            hm = lh == h
            qh = jnp.where(hm, q, 0.0).astype(BF16)
            a = _dot_nt(qh, kb) * mask_scr[h]
            o = o + jnp.where(hm, _dot(a.astype(BF16), vb), 0.0)
        return o

    def finish(o, r0):
        gate = z_ref[pl.ds(r0, C), 3 * PACK_W:4 * PACK_W]
        y = _group_rms(o, g_ref[0:LANES, 0:LANES], DH) * _silu(gate)
        out_ref[pl.ds(r0, C), :] = y.astype(out_ref.dtype)

    def state_step(st, k, v, kdec, sdec):
        upd = _dot_tn(v.astype(BF16), (k * kdec).astype(BF16))
        return (st * sdec + upd) * gmask

    if not latent:
        q, k, v = load(0)
        finish(intra(q, k, v), 0)
        vb = v.astype(BF16)
        for ref, kdec in ((sf_ref, kdec_f), (sb_ref, kdec_b)):
            s = _dot_tn((k * kdec).astype(BF16), vb) * gmask
            for h in range(HEADS):
                ref[0, h] = s[h * DH:(h + 1) * DH, h * DH:(h + 1) * DH]
        return

    def bwd_body(i, st):
        r0 = pl.multiple_of((n_chunks - 1 - i) * C, C)
        q, k, v = load(r0)
        o_scr[pl.ds(r0, C), :] = _dot_nt((q * qdec_b).astype(BF16), st.astype(BF16))
        return state_step(st, k, v, kdec_b, sdec_b)

    lax.fori_loop(0, n_chunks, bwd_body, s0b_ref[0], unroll=2)

    def fwd_body(i, st):
        r0 = pl.multiple_of(i * C, C)
        q, k, v = load(r0)
        o = intra(q, k, v) + o_scr[pl.ds(r0, C), :]
        o = o + _dot_nt((q * qdec_f).astype(BF16), st.astype(BF16))
        finish(o, r0)
        return state_step(st, k, v, kdec_f, sdec_f)

    lax.fori_loop(0, n_chunks, fwd_body, s0f_ref[0], unroll=2)


def _retention(z_ret, log_g, g256, batch, seq_len, latent, rope=None, states=None):
    m = batch * seq_len
    row = lambda b: (b, 0)
    const2 = lambda b: (0, 0)
    in_specs = [
        pl.BlockSpec(memory_space=pltpu.SMEM),
        pl.BlockSpec((seq_len, RET_COLS), row),
        pl.BlockSpec((PACK_W, PACK_W), const2),
    ]
    args = [log_g, z_ret, g256]
    out_specs = [pl.BlockSpec((seq_len, PACK_W), row)]
    out_shape = [jax.ShapeDtypeStruct((m, PACK_W), BF16)]
    if latent:
        state_spec = pl.BlockSpec((1, PACK_W, PACK_W), lambda b: (b, 0, 0))
        in_specs += [pl.BlockSpec((seq_len, PACK_W), const2)] * 2 + [state_spec] * 2
        args += [rope[0], rope[1], states[0], states[1]]
    else:
        out_specs += [pl.BlockSpec((1, HEADS, DH, DH), lambda b: (b, 0, 0, 0))] * 2
        out_shape += [jax.ShapeDtypeStruct((batch, HEADS, DH, DH), F32)] * 2
    return pl.pallas_call(
        functools.partial(_ret_kernel, seq_len=seq_len, latent=latent),
        grid=(batch,),
        in_specs=in_specs,
        out_specs=out_specs,
        out_shape=out_shape,
        scratch_shapes=[
            pltpu.VMEM((HEADS, RET_C, RET_C), F32),
            pltpu.VMEM((seq_len, PACK_W), F32),
        ],
        compiler_params=_params(1),
        name="ret_latent" if latent else "ret_ctx",
    )(*args)


def _attn_kernel(*refs, seq_len, past_len, latent, lam_init, layer):
    if latent:
        (lam_ref, q_ref, k_ref, v_ref, gate_ref, gq_ref, gk_ref, g_ref,
         cosq_ref, sinq_ref, cosk_ref, sink_ref, kc_ref, vc_ref,
         out_ref, k_scr, v_scr) = refs
        prev_refs = ()
    else:
        (lam_ref, q_ref, k_ref, v_ref, gate_ref, gq_ref, gk_ref, g_ref) = refs[:8]
        prev_refs = refs[8:-5]
        out_ref, kn_ref, vn_ref, k_scr, v_scr = refs[-5:]
    ones = jnp.ones((ROW_BLK, DIFF_DV), BF16)

    def put_values(sl, v):
        for h in range(HEADS):
            v_scr[sl, 2 * h * DIFF_DV:(2 * h + 1) * DIFF_DV] = (
                v[:, h * DIFF_DV:(h + 1) * DIFF_DV].astype(BF16))
            v_scr[sl, (2 * h + 1) * DIFF_DV:(2 * h + 2) * DIFF_DV] = ones

    @pl.when(pl.program_id(1) == 0)
    def _():
        for r in range(seq_len // ROW_BLK):
            sl = pl.ds(r * ROW_BLK, ROW_BLK)
            kn = _group_rms(k_ref[sl, :], g_ref[...], DH) * gk_ref[...]
            v = v_ref[sl, :]
            if latent:
                kn = _rope(kn, cosk_ref[sl, :], sink_ref[sl, :])
            else:
                kn_ref[0, layer, sl, :] = kn
                vn_ref[0, layer, sl, :] = v
            k_scr[sl, :] = kn.astype(BF16)
            put_values(sl, v)
        if prev_refs:
            kn_ref[0, 0:layer] = prev_refs[0][0]
            vn_ref[0, 0:layer] = prev_refs[1][0]
        if latent:
            for r in range(past_len // ROW_BLK):
                src = pl.ds(r * ROW_BLK, ROW_BLK)
                dst = pl.ds(seq_len + r * ROW_BLK, ROW_BLK)
                k_scr[dst, :] = kc_ref[0, src, :].astype(BF16)
                put_values(dst, vc_ref[0, src, :])

    lam = lam_ref[0]
    qn = _group_rms(q_ref[...], g_ref[...], DH) * gq_ref[...]
    if latent:
        qn = _rope(qn, cosq_ref[...], sinq_ref[...])
    qn = qn * (DH ** -0.5 * LOG2E)
    lane = lax.broadcasted_iota(jnp.int32, (1, 2 * DH), 1)
    for h in range(HEADS):
        hs = slice(h * 2 * DH, (h + 1) * 2 * DH)
        vx = v_scr[:, 2 * h * DIFF_DV:(2 * h + 2) * DIFF_DV]
        qh = qn[:, hs]
        kh = k_scr[:, hs]
        parts = []
        for first_map in (True, False):
            qm = jnp.where((lane < DH) == first_map, qh, 0.0).astype(BF16)
            s = _dot_nt(qm, kh)
            e = jnp.exp2(s - jnp.max(s, axis=-1, keepdims=True)).astype(BF16)
            ox = _dot(e, vx)
            parts.append(ox[:, :DIFF_DV] / ox[:, DIFF_DV:])
        o = parts[0] - lam * parts[1]
        ms = jnp.mean(o * o, axis=-1, keepdims=True)
        y = o * lax.rsqrt(ms + EPS) * (1.0 - lam_init) * _silu(gate_ref[:, hs])
        out_ref[:, hs] = y.astype(out_ref.dtype)


def _diff_attention(z_diff, lam, gq, gk, g128, batch, seq_len, latent, lam_init, layer,
                    rope=None, ctx=None, caches=None):
    m = batch * seq_len
    nq = seq_len // TQ
    past_len = ctx[0].shape[1] if latent else 0
    qmap = lambda col: (lambda b, i: (b * nq + i, col))
    kvmap = lambda col: (lambda b, i: (b, col))
    const2 = lambda b, i: (0, 0)
    in_specs = [
        pl.BlockSpec(memory_space=pltpu.SMEM),
        pl.BlockSpec((TQ, DIFF_W), qmap(0)),
        pl.BlockSpec((seq_len, DIFF_W), kvmap(1)),
        pl.BlockSpec((seq_len, DIFF_W), kvmap(2)),
        pl.BlockSpec((TQ, DIFF_W), qmap(3)),
        pl.BlockSpec((1, DIFF_W), const2),
        pl.BlockSpec((1, DIFF_W), const2),
        pl.BlockSpec((LANES, LANES), const2),
    ]
    args = [lam, z_diff, z_diff, z_diff, z_diff, gq, gk, g128]
    out_specs = [pl.BlockSpec((TQ, DIFF_W), qmap(0))]
    out_shape = [jax.ShapeDtypeStruct((m, DIFF_W), BF16)]
    if latent:
        in_specs += [
            pl.BlockSpec((TQ, DIFF_W), lambda b, i: (i, 0)),
            pl.BlockSpec((TQ, DIFF_W), lambda b, i: (i, 0)),
            pl.BlockSpec((seq_len, DIFF_W), const2),
            pl.BlockSpec((seq_len, DIFF_W), const2),
            pl.BlockSpec((1, past_len, DIFF_W), lambda b, i: (b, 0, 0)),
            pl.BlockSpec((1, past_len, DIFF_W), lambda b, i: (b, 0, 0)),
        ]
        args += [rope[0], rope[1], rope[0], rope[1], ctx[0], ctx[1]]
    else:
        cache_spec = lambda n: pl.BlockSpec((1, n, seq_len, DIFF_W), lambda b, i: (b, 0, 0, 0))
        out_specs += [cache_spec(layer + 1)] * 2
        out_shape += [jax.ShapeDtypeStruct((batch, layer + 1, seq_len, DIFF_W), F32)] * 2
        if layer > 0:
            in_specs += [cache_spec(layer)] * 2
            args += list(caches)
    return pl.pallas_call(
        functools.partial(_attn_kernel, seq_len=seq_len, past_len=past_len,
                          latent=latent, lam_init=lam_init, layer=layer),
        grid=(batch, nq),
        in_specs=in_specs,
        out_specs=out_specs,
        out_shape=out_shape,
        scratch_shapes=[
            pltpu.VMEM((seq_len + past_len, DIFF_W), BF16),
            pltpu.VMEM((seq_len + past_len, 2 * DIFF_W), BF16),
        ],
        compiler_params=_params(2),
        name="attn_latent" if latent else "attn_ctx",
    )(*args)


def _hgrn_kernel(*refs, seq_len, latent):
    if latent:
        (z_ref, lb_ref, tri_ref, g_ref, lm_ref, dm_ref, s0f_ref, s0b_ref,
         out_ref, qi_scr, ks_scr, dec_scr, o_scr, oi_scr) = refs
    else:
        (z_ref, lb_ref, tri_ref, g_ref, lm_ref, dm_ref,
         out_ref, sf_ref, sb_ref, qi_scr, ks_scr, dec_scr, o_scr, oi_scr) = refs
    C, SUB = HG_C, HG_SUB
    n_chunks = seq_len // C
    chunks_per_blk = ROW_BLK // C
    lh = _lane_head()
    sub_row = lax.broadcasted_iota(jnp.int32, (ROW_BLK, 1), 0) & (SUB - 1)
    chunk_row = lax.broadcasted_iota(jnp.int32, (C, 1), 0)

    def tile_roll(x, shift):
        return jnp.concatenate(
            [pltpu.roll(x[i:i + SUBLANES], shift, 0) for i in range(0, ROW_BLK, SUBLANES)], axis=0)

    def head_rep(x):
        return jnp.concatenate([jnp.where(lh == h, x, jnp.zeros_like(x)) for h in range(HEADS)],
                               axis=0)

    def intra(blk):
        start = blk * ROW_BLK if isinstance(blk, int) else pl.multiple_of(blk * ROW_BLK, ROW_BLK)
        rows = pl.ds(start, ROW_BLK)
        lb = lb_ref[...]
        q = z_ref[rows, 0:PACK_W] * (DH ** -0.5)
        v = z_ref[rows, 3 * PACK_W:4 * PACK_W]
        v_reps = [head_rep(v[c * C:(c + 1) * C].astype(BF16)) for c in range(chunks_per_blk)]
        kks, bcs, a_nears = [], [], []
        for d in range(2):
            f = lb + (1.0 - lb) * jax.nn.sigmoid(z_ref[rows, (1 + d) * PACK_W:(2 + d) * PACK_W])
            kks.append(1.0 - f)
            l2 = jnp.log(f) * LOG2E
            hi = l2.astype(BF16)
            r1 = l2 - hi.astype(F32)
            mid = r1.astype(BF16)
            lo = (r1 - mid.astype(F32)).astype(BF16)
            cs = _dot(tri_ref[d], jnp.concatenate([hi, mid, lo], axis=1))
            bcs.append(cs[:, 0:PACK_W] + cs[:, PACK_W:2 * PACK_W] + cs[:, 2 * PACK_W:])

        for d in range(2):
            bwd = d == 1
            kk, bc = kks[d], bcs[d]
            ps = [(q * kk).astype(BF16)]
            for dist in range(1, SUB):
                shift = SUBLANES - dist if bwd else dist
                valid = (sub_row < SUB - dist) if bwd else (sub_row >= dist)
                arg = jnp.where(valid, bc - tile_roll(bc, shift), NEG_BIG)
                ps.append((q * tile_roll(kk, shift) * jnp.exp2(arg)).astype(BF16))
            r = _dot(jnp.concatenate(ps, axis=0), g_ref[...]).astype(BF16)
            a_near = r[0:ROW_BLK] * dm_ref[d, 0]
            for dist in range(1, SUB):
                a_near = a_near + r[dist * ROW_BLK:(dist + 1) * ROW_BLK] * dm_ref[d, dist]
            a_nears.append(a_near)

        for c in range(chunks_per_blk):
            for d in range(2):
                bwd = d == 1
                kk, bc, a_near = kks[d], bcs[d], a_nears[d]
                cs_ = slice(c * C, (c + 1) * C)
                qc, kc, bcc = q[cs_], kk[cs_], bc[cs_]
                last = bcc[0:1] if bwd else bcc[C - 1:C]
                a = a_near[cs_]
                for li, m in enumerate(HG_LEVELS):
                    pieces = []
                    for b2 in range(C // (2 * m)):
                        rr = 2 * m * b2 + (m if bwd else m - 1)
                        pieces.append(jnp.broadcast_to(bcc[rr:rr + 1], (2 * m, PACK_W)))
                    anchor = jnp.concatenate(pieces, axis=0) if len(pieces) > 1 else pieces[0]
                    upper = (chunk_row & (2 * m - 1)) >= m
                    later = jnp.logical_not(upper) if bwd else upper
                    e = jnp.exp2((bcc - anchor) * jnp.where(later, 1.0, -1.0))
                    a = a + _dot_nt((qc * e).astype(BF16),
                                    head_rep((kc * e).astype(BF16))).astype(BF16) * lm_ref[d, li]
                crow = pl.ds(start + c * C, C)
                o_scr[d, crow, :] = _dot(a, v_reps[c])
                qi_scr[d, crow, :] = (qc * jnp.exp2(bcc)).astype(BF16)
                ks_scr[d, crow, :] = (kc * jnp.exp2(last - bcc)).astype(BF16)
                slot = (blk * chunks_per_blk + c) * SUBLANES
                if not isinstance(slot, int):
                    slot = pl.multiple_of(slot, SUBLANES)
                dec_scr[d, pl.ds(slot, SUBLANES), :] = jnp.broadcast_to(
                    jnp.exp2(last), (SUBLANES, PACK_W))

        if not latent:
            vb = v.astype(BF16)
            for d, ref in enumerate((sf_ref, sb_ref)):
                bwd = d == 1
                sweep = range(chunks_per_blk - 1, -1, -1) if bwd else range(chunks_per_blk)
                tail = jnp.zeros((1, PACK_W), F32)
                decayed = [None] * chunks_per_blk
                for c in reversed(sweep):
                    cs_ = slice(c * C, (c + 1) * C)
                    bcc = bcs[d][cs_]
                    total = tail + (bcc[0:1] if bwd else bcc[C - 1:C])
                    decayed[c] = (kks[d][cs_] * jnp.exp2(total - bcc)).astype(BF16)
                    tail = total
                s = _dot_tn(jnp.concatenate(decayed, axis=0), vb) * g_ref[...].astype(F32)
                for h in range(HEADS):
                    ref[0, h] = s[h * DH:(h + 1) * DH, h * DH:(h + 1) * DH]

    n_blk = seq_len // ROW_BLK
    assert latent or n_blk == 1
    if n_blk == 1:
        intra(0)
    else:
        def intra_step(i, carry):
            intra(i)
            return carry
        lax.fori_loop(0, n_blk, intra_step, 0)

    if latent:
        states = (s0f_ref[0], s0b_ref[0])
    else:
        states = (jnp.zeros((PACK_W, PACK_W), F32),) * 2
    gmask = g_ref[...].astype(F32)

    def body(i, states):
        new_states = []
        for d in range(2):
            c = (n_chunks - 1 - i) if d == 1 else i
            rows = pl.ds(pl.multiple_of(c * C, C), C)
            st = states[d]
            oi_scr[d, rows, :] = _dot_nt(qi_scr[d, rows, :], st.astype(BF16))
            dec = dec_scr[d, pl.ds(pl.multiple_of(c * SUBLANES, SUBLANES), 1), :]
            vb = z_ref[rows, 3 * PACK_W:4 * PACK_W].astype(BF16)
            new_states.append(st * dec + _dot_tn(vb, ks_scr[d, rows, :]) * gmask)
        return tuple(new_states)

    lax.fori_loop(0, n_chunks, body, states, unroll=4)

    def finish(start):
        rows = pl.ds(start, ROW_BLK)
        tot = (o_scr[0, rows, :] + oi_scr[0, rows, :]) + (o_scr[1, rows, :] + oi_scr[1, rows, :])
        gate = z_ref[rows, 4 * PACK_W:5 * PACK_W]
        y = _group_rms(tot, g_ref[0:LANES, 0:LANES], DH) * _silu(gate)
        out_ref[rows, :] = y.astype(out_ref.dtype)

    _row_blocks(seq_len, finish)


def _hgrn(z_hg, lb, consts, batch, seq_len, latent, states=None):
    m = batch * seq_len
    tri, g256, lm, dm = consts
    row = lambda b: (b, 0)
    in_specs = [
        pl.BlockSpec((seq_len, HG_COLS), row),
        pl.BlockSpec((1, PACK_W), lambda b: (0, 0)),
        pl.BlockSpec(tri.shape, lambda b: (0, 0, 0)),
        pl.BlockSpec((PACK_W, PACK_W), lambda b: (0, 0)),
        pl.BlockSpec(lm.shape, lambda b: (0, 0, 0, 0)),
        pl.BlockSpec(dm.shape, lambda b: (0, 0, 0, 0)),
    ]
    args = [z_hg, lb, tri, g256, lm, dm]
    out_specs = [pl.BlockSpec((seq_len, PACK_W), row)]
    out_shape = [jax.ShapeDtypeStruct((m, PACK_W), BF16)]
    if latent:
        in_specs += [pl.BlockSpec((1, PACK_W, PACK_W), lambda b: (b, 0, 0))] * 2
        args += [states[0], states[1]]
    else:
        out_specs += [pl.BlockSpec((1, HEADS, DH, DH), lambda b: (b, 0, 0, 0))] * 2
        out_shape += [jax.ShapeDtypeStruct((batch, HEADS, DH, DH), F32)] * 2
    return pl.pallas_call(
        functools.partial(_hgrn_kernel, seq_len=seq_len, latent=latent),
        grid=(batch,),
        in_specs=in_specs,
        out_specs=out_specs,
        out_shape=out_shape,
        scratch_shapes=[
            pltpu.VMEM((2, seq_len, PACK_W), BF16),
            pltpu.VMEM((2, seq_len, PACK_W), BF16),
            pltpu.VMEM((2, seq_len // HG_C * SUBLANES, PACK_W), F32),
            pltpu.VMEM((2, seq_len, PACK_W), F32),
            pltpu.VMEM((2, seq_len, PACK_W), F32),
        ],
        compiler_params=_params(1),
        name="hgrn_latent" if latent else "hgrn_ctx",
    )(*args)


def _block_ones(n):
    idx = np.arange(n) // DH
    return jnp.asarray((idx[:, None] == idx[None, :]).astype(np.float32), dtype=BF16)


def _hgrn_consts():
    C = HG_C
    t = np.arange(ROW_BLK)
    same_chunk = (t[:, None] // C) == (t[None, :] // C)
    tri_f = same_chunk & (t[None, :] <= t[:, None])
    tri_b = same_chunk & (t[None, :] >= t[:, None])
    tc = np.arange(C)
    lms = []
    for fwd in (True, False):
        per_level = []
        for m in HG_LEVELS:
            same = (tc[:, None] // (2 * m)) == (tc[None, :] // (2 * m))
            upper = (tc % (2 * m)) >= m
            later, earlier = (upper, ~upper) if fwd else (~upper, upper)
            per_level.append(np.tile(same & later[:, None] & earlier[None, :], (1, HEADS)))
        lms.append(np.stack(per_level))
    dms = [[np.tile(tc[None, :] == tc[:, None] + sign * dist, (ROW_BLK // C, HEADS))
            for dist in range(HG_SUB)] for sign in (-1, 1)]
    return (jnp.asarray(np.stack([tri_f, tri_b]).astype(np.float32), dtype=BF16),
            _block_ones(PACK_W),
            jnp.asarray(np.stack(lms).astype(np.float32), dtype=BF16),
            jnp.asarray(np.stack(dms).astype(np.float32), dtype=BF16))


def _rope_tables(seq_len, width):
    t = np.arange(seq_len)
    pos = np.stack([t // GRID_W, t % GRID_W], axis=1).astype(np.float32)
    j = np.arange(DH)
    axis = j // (2 * ROPE_PAIR)
    jj = j % (2 * ROPE_PAIR)
    inv = (ROPE_BASE ** (-(jj % ROPE_PAIR).astype(np.float64) / ROPE_PAIR)).astype(np.float32)
    ang = (pos[:, axis] * inv[None, :]).astype(np.float64)
    cos = np.cos(ang)
    sin = np.where(jj < ROPE_PAIR, -np.sin(ang), np.sin(ang))
    reps = width // DH
    return (jnp.asarray(np.tile(cos, (1, reps)), dtype=F32),
            jnp.asarray(np.tile(sin, (1, reps)), dtype=F32))


def _to_blockdiag_t(s):
    b = s.shape[0]
    eye = jnp.eye(HEADS, dtype=s.dtype)
    return jnp.einsum('bhde,hg->bhegd', s, eye).reshape(b, PACK_W, PACK_W)


def _layer(x2d, batch, seq_len, mod, l, p, latent, ctx, caches):
    shift, scale, gate = mod
    z_ret, z_diff, z_hg = _inproj(x2d, p['norm_g'][l], scale, shift, p['w_in'][l], seq_len)
    lam_init = 0.8 - 0.6 * math.exp(-0.3 * l)
    if latent:
        k_ctx, v_ctx, s_rf, s_rb, s_hf, s_hb = ctx
        r = _retention(z_ret, p['log_g'][l], p['g256'], batch, seq_len, True,
                       rope=p['rope256'], states=(s_rf, s_rb))[0]
        a = _diff_attention(z_diff, p['lam'][l], p['gq'][l], p['gk'][l], p['g128'], batch,
                            seq_len, True, lam_init, l, rope=p['rope512'],
                            ctx=(k_ctx, v_ctx))[0]
        hgo = _hgrn(z_hg, p['lb'][l], p['hg_consts'], batch, seq_len, True,
                    states=(s_hf, s_hb))[0]
        extras = None
    else:
        r, s_rf, s_rb = _retention(z_ret, p['log_g'][l], p['g256'], batch, seq_len, False)
        a, kcache, vcache = _diff_attention(z_diff, p['lam'][l], p['gq'][l], p['gk'][l],
                                            p['g128'], batch, seq_len, False, lam_init, l,
                                            caches=caches)
        hgo, s_hf, s_hb = _hgrn(z_hg, p['lb'][l], p['hg_consts'], batch, seq_len, False)
        extras = ((kcache, vcache), (s_rf, s_rb, s_hf, s_hb))
    y = _outproj(x2d, r, a, hgo, gate, p['w_out'][l], seq_len)
    return y, extras


def kernel(x_prompt, x_sample, c, c_ctx, cache_diff_k, cache_diff_v, state_ret_fwd,
           state_ret_bwd, state_hgrn_fwd, state_hgrn_bwd, norm_g, w_ada, b_ada, w_in,
           w_out, ret_decay_logit, diff_qn_g, diff_kn_g, diff_lambda, hgrn_lb_logit):
    batch, seq, _ = x_prompt.shape
    dec_batch, dec_seq, _ = x_sample.shape
    past_len = cache_diff_k.shape[2]

    cv = jnp.zeros((8, D_MODEL), F32).at[0].set(c_ctx).at[1:1 + dec_batch].set(c)
    mods = _ada(cv, w_ada, b_ada)

    lb_all = jax.nn.softmax(hgrn_lb_logit.astype(F32), axis=0)
    lb_all = jnp.cumsum(lb_all, axis=0) - lb_all[0]
    lp = diff_lambda.astype(F32)
    lam_inits = jnp.asarray([0.8 - 0.6 * math.exp(-0.3 * l) for l in range(DEPTH)], F32)
    lam = (jnp.exp(jnp.sum(lp[:, 0] * lp[:, 1], axis=-1))
           - jnp.exp(jnp.sum(lp[:, 2] * lp[:, 3], axis=-1)) + lam_inits)
    p = {
        'norm_g': norm_g.reshape(DEPTH, 1, D_MODEL),
        'w_in': w_in.astype(BF16),
        'w_out': w_out.astype(BF16),
        'log_g': jax.nn.log_sigmoid(ret_decay_logit.astype(F32)),
        'lam': lam.reshape(DEPTH, 1),
        'gq': jnp.tile(diff_qn_g, (1, DIFF_W // DH)).reshape(DEPTH, 1, DIFF_W),
        'gk': jnp.tile(diff_kn_g, (1, DIFF_W // DH)).reshape(DEPTH, 1, DIFF_W),
        'lb': lb_all.reshape(DEPTH, 1, PACK_W),
        'g256': _block_ones(PACK_W),
        'g128': _block_ones(LANES),
        'hg_consts': _hgrn_consts(),
        'rope256': _rope_tables(dec_seq, PACK_W),
        'rope512': _rope_tables(dec_seq, DIFF_W),
    }

    def split_mod(rows):
        return tuple(rows[:, None, j * D_MODEL:(j + 1) * D_MODEL] for j in range(3))

    y = x_prompt.reshape(batch * seq, D_MODEL)
    caches = None
    states = []
    for l in range(DEPTH):
        y, (caches, st) = _layer(y, batch, seq, split_mod(mods[l, 0:1]), l, p, False, None, caches)
        states.append(st)
    y_prompt = y.reshape(batch, seq, D_MODEL)
    new_k = caches[0].reshape(batch, DEPTH, seq, HEADS, 2, DH)
    new_v = caches[1].reshape(batch, DEPTH, seq, HEADS, DIFF_DV)
    new_states = [jnp.stack([st[i] for st in states], axis=1) for i in range(4)]

    y = x_sample.reshape(dec_batch * dec_seq, D_MODEL)
    for l in range(DEPTH):
        ctx = (cache_diff_k[:, l].reshape(dec_batch, past_len, DIFF_W),
               cache_diff_v[:, l].reshape(dec_batch, past_len, DIFF_W),
               _to_blockdiag_t(state_ret_fwd[:, l]), _to_blockdiag_t(state_ret_bwd[:, l]),
               _to_blockdiag_t(state_hgrn_fwd[:, l]), _to_blockdiag_t(state_hgrn_bwd[:, l]))
        y, _ = _layer(y, dec_batch, dec_seq, split_mod(mods[l, 1:1 + dec_batch]), l, p, True,
                      ctx, None)
    y_sample = y.reshape(dec_batch, dec_seq, D_MODEL)

    return (y_prompt, y_sample, new_k, new_v, *new_states)
```

```python
import functools
import math

import numpy as np
import jax
import jax.numpy as jnp
from jax import lax
from jax.experimental import pallas as pl
from jax.experimental.pallas import tpu as pltpu

F32 = jnp.float32
BF16 = jnp.bfloat16

D_MODEL = 1024
DEPTH = 2
GRID_W = 64
HEADS = 4
DH = 64
PACK_W = HEADS * DH
DIFF_W = 512
DIFF_DV = 128
RET_COLS = 4 * PACK_W
DIFF_COLS = 4 * DIFF_W
HG_COLS = 5 * PACK_W
IN_W = RET_COLS + DIFF_COLS + HG_COLS
MIX_W = PACK_W + DIFF_W + PACK_W
ROPE_BASE = 10000.0
ROPE_PAIR = 16
EPS = 1e-6
LOG2E = 1.4426950408889634
NEG_BIG = -1e30

RET_C = 256
HG_C = 64
HG_SUB = 4
HG_LEVELS = (4, 8, 16, 32)
SUBLANES = 8
LANES = 128
ROW_BLK = 256
TQ = 512
TM_IN = 512
TM_OUT = 512

VMEM_LIMIT = 56 * 1024 * 1024


def _params(n_axes):
    return pltpu.CompilerParams(
        dimension_semantics=("arbitrary",) * n_axes, vmem_limit_bytes=VMEM_LIMIT)


def _dot(a, b):
    return jnp.dot(a, b, preferred_element_type=F32)


def _dot_nt(a, b):
    return lax.dot_general(a, b, (((1,), (1,)), ((), ())), preferred_element_type=F32)


def _dot_tn(a, b):
    return lax.dot_general(a, b, (((0,), (0,)), ((), ())), preferred_element_type=F32)


def _silu(x):
    return x * jax.nn.sigmoid(x)


def _group_rms(x, g, width):
    n, w = x.shape
    x2 = x * x
    cols = range(0, w, LANES)
    stacked = jnp.concatenate([x2[:, j:j + LANES] for j in cols], axis=0)
    hi = stacked.astype(BF16)
    lo = (stacked - hi.astype(F32)).astype(BF16)
    sums = _dot(hi, g) + _dot(lo, g)
    ms = jnp.concatenate([sums[i * n:(i + 1) * n] for i in range(len(cols))], axis=1)
    return x * lax.rsqrt(ms * (1.0 / width) + EPS)


def _rope(x, cos, sin):
    w = x.shape[-1]
    lane = lax.broadcasted_iota(jnp.int32, (1, w), 1)
    first = (lane & (2 * ROPE_PAIR - 1)) < ROPE_PAIR
    swapped = jnp.where(first, pltpu.roll(x, w - ROPE_PAIR, 1), pltpu.roll(x, ROPE_PAIR, 1))
    return x * cos + swapped * sin


def _lane_head(w=PACK_W):
    return lax.broadcasted_iota(jnp.int32, (1, w), 1) // DH


def _row_blocks(n_rows, body):
    n_blk = n_rows // ROW_BLK
    if n_blk == 1:
        body(0)
    else:
        def step(i, carry):
            body(pl.multiple_of(i * ROW_BLK, ROW_BLK))
            return carry
        lax.fori_loop(0, n_blk, step, 0)


def _ada_kernel(cv_ref, w_ref, b_ref, o_ref):
    cv = cv_ref[...]
    o_ref[0] = _dot(_silu(cv), w_ref[0]) + b_ref[0]


def _ada(cv, w_ada, b_ada):
    tn = 1024
    n = 3 * D_MODEL
    return pl.pallas_call(
        _ada_kernel,
        grid=(DEPTH, n // tn),
        in_specs=[
            pl.BlockSpec((8, D_MODEL), lambda l, j: (0, 0)),
            pl.BlockSpec((1, D_MODEL, tn), lambda l, j: (l, 0, j)),
            pl.BlockSpec((1, 1, tn), lambda l, j: (l, 0, j)),
        ],
        out_specs=pl.BlockSpec((1, 8, tn), lambda l, j: (l, 0, j)),
        out_shape=jax.ShapeDtypeStruct((DEPTH, 8, n), F32),
        compiler_params=_params(2),
        name="ada",
    )(cv, w_ada, b_ada.reshape(DEPTH, 1, n))


def _inproj_kernel(x_ref, g_ref, sc_ref, sh_ref, w_ref, zr_ref, zd_ref, zh_ref):
    x = x_ref[...]
    ms = jnp.mean(x * x, axis=-1, keepdims=True)
    h = x * lax.rsqrt(ms + EPS) * g_ref[...]
    h = h * (1.0 + sc_ref[0]) + sh_ref[0]
    z = _dot(h.astype(BF16), w_ref[...])
    zr_ref[...] = z[:, :RET_COLS].astype(zr_ref.dtype)
    zd_ref[...] = z[:, RET_COLS:RET_COLS + DIFF_COLS].astype(zd_ref.dtype)
    zh_ref[...] = z[:, RET_COLS + DIFF_COLS:]


def _inproj(x2d, g, scale, shift, w_bf, seq_len):
    m = x2d.shape[0]
    per_batch = scale.shape[0] > 1
    tiles_per_seq = seq_len // TM_IN
    mod_map = (lambda i: (i // tiles_per_seq, 0, 0)) if per_batch else (lambda i: (0, 0, 0))
    return pl.pallas_call(
        _inproj_kernel,
        grid=(m // TM_IN,),
        in_specs=[
            pl.BlockSpec((TM_IN, D_MODEL), lambda i: (i, 0)),
            pl.BlockSpec((1, D_MODEL), lambda i: (0, 0)),
            pl.BlockSpec((1, 1, D_MODEL), mod_map),
            pl.BlockSpec((1, 1, D_MODEL), mod_map),
            pl.BlockSpec((D_MODEL, IN_W), lambda i: (0, 0)),
        ],
        out_specs=[
            pl.BlockSpec((TM_IN, RET_COLS), lambda i: (i, 0)),
            pl.BlockSpec((TM_IN, DIFF_COLS), lambda i: (i, 0)),
            pl.BlockSpec((TM_IN, HG_COLS), lambda i: (i, 0)),
        ],
        out_shape=[
            jax.ShapeDtypeStruct((m, RET_COLS), BF16),
            jax.ShapeDtypeStruct((m, DIFF_COLS), BF16),
            jax.ShapeDtypeStruct((m, HG_COLS), F32),
        ],
        compiler_params=_params(1),
        name="inproj",
    )(x2d, g, scale, shift, w_bf)


def _outproj_kernel(x_ref, r_ref, d_ref, h_ref, gate_ref, w_ref, o_ref):
    acc = _dot(r_ref[...], w_ref[0:PACK_W, :])
    acc += _dot(d_ref[...], w_ref[PACK_W:PACK_W + DIFF_W, :])
    acc += _dot(h_ref[...], w_ref[PACK_W + DIFF_W:, :])
    o_ref[...] = x_ref[...] + gate_ref[0] * acc


def _outproj(x2d, r, d, h, gate, w_bf, seq_len):
    m = x2d.shape[0]
    per_batch = gate.shape[0] > 1
    tiles_per_seq = max(seq_len // TM_OUT, 1)
    tm = min(TM_OUT, seq_len)
    mod_map = (lambda i: (i // tiles_per_seq, 0, 0)) if per_batch else (lambda i: (0, 0, 0))
    return pl.pallas_call(
        _outproj_kernel,
        grid=(m // tm,),
        in_specs=[
            pl.BlockSpec((tm, D_MODEL), lambda i: (i, 0)),
            pl.BlockSpec((tm, PACK_W), lambda i: (i, 0)),
            pl.BlockSpec((tm, DIFF_W), lambda i: (i, 0)),
            pl.BlockSpec((tm, PACK_W), lambda i: (i, 0)),
            pl.BlockSpec((1, 1, D_MODEL), mod_map),
            pl.BlockSpec((MIX_W, D_MODEL), lambda i: (0, 0)),
        ],
        out_specs=pl.BlockSpec((tm, D_MODEL), lambda i: (i, 0)),
        out_shape=jax.ShapeDtypeStruct((m, D_MODEL), F32),
        compiler_params=_params(1),
        name="outproj",
    )(x2d, r, d, h, gate, w_bf)


def _ret_kernel(*refs, seq_len, latent):
    if latent:
        (lg_ref, z_ref, g_ref, cos_ref, sin_ref, s0f_ref, s0b_ref,
         out_ref, mask_scr, o_scr) = refs
    else:
        lg_ref, z_ref, g_ref, out_ref, sf_ref, sb_ref, mask_scr, o_scr = refs
    C = RET_C
    n_chunks = seq_len // C
    lh = _lane_head()

    def lane_vec(direction):
        v = jnp.zeros((1, PACK_W), F32)
        for h in range(HEADS):
            v = jnp.where(lh == h, lg_ref[direction, h], v)
        return v

    lgf, lgb = lane_vec(0), lane_vec(1)
    tau = lax.broadcasted_iota(jnp.int32, (C, 1), 0).astype(F32)
    qdec_f = jnp.exp((tau + 1.0) * lgf)
    qdec_b = jnp.exp((C - tau) * lgb)
    kdec_f = jnp.exp((C - 1.0 - tau) * lgf)
    kdec_b = jnp.exp(tau * lgb)
    sdec_f = jnp.exp(float(C) * lgf)
    sdec_b = jnp.exp(float(C) * lgb)
    gmask = g_ref[...].astype(F32)

    @pl.when(pl.program_id(0) == 0)
    def _():
        dd = (lax.broadcasted_iota(jnp.int32, (C, C), 0)
              - lax.broadcasted_iota(jnp.int32, (C, C), 1)).astype(F32)
        for h in range(HEADS):
            mf = jnp.where(dd >= 0, jnp.exp(jnp.maximum(dd, 0.0) * lg_ref[0, h]), 0.0)
            mb = jnp.where(dd <= 0, jnp.exp(jnp.maximum(-dd, 0.0) * lg_ref[1, h]), 0.0)
            mask_scr[h] = mf + mb

    def load(r0):
        q = z_ref[pl.ds(r0, C), 0:PACK_W].astype(F32)
        k = z_ref[pl.ds(r0, C), PACK_W:2 * PACK_W].astype(F32) * (DH ** -0.5)
        v = z_ref[pl.ds(r0, C), 2 * PACK_W:3 * PACK_W].astype(F32)
        if latent:
            cos = cos_ref[pl.ds(r0, C), :]
            sin = sin_ref[pl.ds(r0, C), :]
            q = _rope(q, cos, sin)
            k = _rope(k, cos, sin)
        return q, k, v

    def intra(q, k, v):
        kb = k.astype(BF16)
        vb = v.astype(BF16)
        o = jnp.zeros((C, PACK_W), F32)
        for h in range(HEADS):
            hm = lh == h
            qh = jnp.where(hm, q, 0.0).astype(BF16)
            a = _dot_nt(qh, kb) * mask_scr[h]
            o = o + jnp.where(hm, _dot(a.astype(BF16), vb), 0.0)
        return o

    def finish(o, r0):
        gate = z_ref[pl.ds(r0, C), 3 * PACK_W:4 * PACK_W].astype(F32)
        y = _group_rms(o, g_ref[0:LANES, 0:LANES], DH) * _silu(gate)
        out_ref[pl.ds(r0, C), :] = y.astype(out_ref.dtype)

    def state_step(st, k, v, kdec, sdec):
        upd = _dot_tn(v.astype(BF16), (k * kdec).astype(BF16))
        return (st * sdec + upd) * gmask

    if not latent:
        q, k, v = load(0)
        finish(intra(q, k, v), 0)
        vb = v.astype(BF16)
        for ref, kdec in ((sf_ref, kdec_f), (sb_ref, kdec_b)):
            s = _dot_tn((k * kdec).astype(BF16), vb) * gmask
            for h in range(HEADS):
                ref[0, h] = s[h * DH:(h + 1) * DH, h * DH:(h + 1) * DH]
        return

    def bwd_body(i, st):
        r0 = pl.multiple_of((n_chunks - 1 - i) * C, C)
        q, k, v = load(r0)
        o_scr[pl.ds(r0, C), :] = _dot_nt((q * qdec_b).astype(BF16), st.astype(BF16))
        return state_step(st, k, v, kdec_b, sdec_b)

    lax.fori_loop(0, n_chunks, bwd_body, s0b_ref[0], unroll=2)

    def fwd_body(i, st):
        r0 = pl.multiple_of(i * C, C)
        q, k, v = load(r0)
        o = intra(q, k, v) + o_scr[pl.ds(r0, C), :]
        o = o + _dot_nt((q * qdec_f).astype(BF16), st.astype(BF16))
        finish(o, r0)
        return state_step(st, k, v, kdec_f, sdec_f)

    lax.fori_loop(0, n_chunks, fwd_body, s0f_ref[0], unroll=2)


def _retention(z_ret, log_g, g256, batch, seq_len, latent, rope=None, states=None):
    m = batch * seq_len
    row = lambda b: (b, 0)
    const2 = lambda b: (0, 0)
    in_specs = [
        pl.BlockSpec(memory_space=pltpu.SMEM),
        pl.BlockSpec((seq_len, RET_COLS), row),
        pl.BlockSpec((PACK_W, PACK_W), const2),
    ]
    args = [log_g, z_ret, g256]
    out_specs = [pl.BlockSpec((seq_len, PACK_W), row)]
    out_shape = [jax.ShapeDtypeStruct((m, PACK_W), BF16)]
    if latent:
        state_spec = pl.BlockSpec((1, PACK_W, PACK_W), lambda b: (b, 0, 0))
        in_specs += [pl.BlockSpec((seq_len, PACK_W), const2)] * 2 + [state_spec] * 2
        args += [rope[0], rope[1], states[0], states[1]]
    else:
        out_specs += [pl.BlockSpec((1, HEADS, DH, DH), lambda b: (b, 0, 0, 0))] * 2
        out_shape += [jax.ShapeDtypeStruct((batch, HEADS, DH, DH), F32)] * 2
    return pl.pallas_call(
        functools.partial(_ret_kernel, seq_len=seq_len, latent=latent),
        grid=(batch,),
        in_specs=in_specs,
        out_specs=out_specs,
        out_shape=out_shape,
        scratch_shapes=[
            pltpu.VMEM((HEADS, RET_C, RET_C), F32),
            pltpu.VMEM((seq_len, PACK_W), F32),
        ],
        compiler_params=_params(1),
        name="ret_latent" if latent else "ret_ctx",
    )(*args)


def _attn_kernel(*refs, seq_len, past_len, latent, lam_init, layer):
    if latent:
        (lam_ref, q_ref, k_ref, v_ref, gate_ref, gq_ref, gk_ref, g_ref,
         cosq_ref, sinq_ref, cosk_ref, sink_ref, kc_ref, vc_ref,
         out_ref, k_scr, v_scr) = refs
        prev_refs = ()
    else:
        (lam_ref, q_ref, k_ref, v_ref, gate_ref, gq_ref, gk_ref, g_ref) = refs[:8]
        prev_refs = refs[8:-5]
        out_ref, kn_ref, vn_ref, k_scr, v_scr = refs[-5:]
    ones = jnp.ones((ROW_BLK, DIFF_DV), BF16)

    def put_values(sl, v):
        for h in range(HEADS):
            v_scr[sl, 2 * h * DIFF_DV:(2 * h + 1) * DIFF_DV] = (
                v[:, h * DIFF_DV:(h + 1) * DIFF_DV].astype(BF16))
            v_scr[sl, (2 * h + 1) * DIFF_DV:(2 * h + 2) * DIFF_DV] = ones

    @pl.when(pl.program_id(1) == 0)
    def _():
        for r in range(seq_len // ROW_BLK):
            sl = pl.ds(r * ROW_BLK, ROW_BLK)
            kn = _group_rms(k_ref[sl, :].astype(F32), g_ref[...], DH) * gk_ref[...]
            v = v_ref[sl, :].astype(F32)
            if latent:
                kn = _rope(kn, cosk_ref[sl, :], sink_ref[sl, :])
            else:
                kn_ref[0, layer, sl, :] = kn
                vn_ref[0, layer, sl, :] = v
            k_scr[sl, :] = kn.astype(BF16)
            put_values(sl, v)
        if prev_refs:
            kn_ref[0, 0:layer] = prev_refs[0][0]
            vn_ref[0, 0:layer] = prev_refs[1][0]
        if latent:
            for r in range(past_len // ROW_BLK):
                src = pl.ds(r * ROW_BLK, ROW_BLK)
                dst = pl.ds(seq_len + r * ROW_BLK, ROW_BLK)
                k_scr[dst, :] = kc_ref[0, src, :].astype(BF16)
                put_values(dst, vc_ref[0, src, :])

    lam = lam_ref[0]
    qn = _group_rms(q_ref[...].astype(F32), g_ref[...], DH) * gq_ref[...]
    if latent:
        qn = _rope(qn, cosq_ref[...], sinq_ref[...])
    qn = qn * (DH ** -0.5 * LOG2E)
    lane = lax.broadcasted_iota(jnp.int32, (1, 2 * DH), 1)
    for h in range(HEADS):
        hs = slice(h * 2 * DH, (h + 1) * 2 * DH)
        vx = v_scr[:, 2 * h * DIFF_DV:(2 * h + 2) * DIFF_DV]
        qh = qn[:, hs]
        kh = k_scr[:, hs]
        parts = []
        for first_map in (True, False):
            qm = jnp.where((lane < DH) == first_map, qh, 0.0).astype(BF16)
            s = _dot_nt(qm, kh)
            e = jnp.exp2(s - jnp.max(s, axis=-1, keepdims=True)).astype(BF16)
            ox = _dot(e, vx)
            parts.append(ox[:, :DIFF_DV] / ox[:, DIFF_DV:])
        o = parts[0] - lam * parts[1]
        ms = jnp.mean(o * o, axis=-1, keepdims=True)
        y = o * lax.rsqrt(ms + EPS) * (1.0 - lam_init) * _silu(gate_ref[:, hs].astype(F32))
        out_ref[:, hs] = y.astype(out_ref.dtype)


def _diff_attention(z_diff, lam, gq, gk, g128, batch, seq_len, latent, lam_init, layer,
                    rope=None, ctx=None, caches=None):
    m = batch * seq_len
    tq = min(TQ, seq_len)
    nq = seq_len // tq
    past_len = ctx[0].shape[1] if latent else 0
    qmap = lambda col: (lambda b, i: (b * nq + i, col))
    kvmap = lambda col: (lambda b, i: (b, col))
    const2 = lambda b, i: (0, 0)
    in_specs = [
        pl.BlockSpec(memory_space=pltpu.SMEM),
        pl.BlockSpec((tq, DIFF_W), qmap(0)),
        pl.BlockSpec((seq_len, DIFF_W), kvmap(1)),
        pl.BlockSpec((seq_len, DIFF_W), kvmap(2)),
        pl.BlockSpec((tq, DIFF_W), qmap(3)),
        pl.BlockSpec((1, DIFF_W), const2),
        pl.BlockSpec((1, DIFF_W), const2),
        pl.BlockSpec((LANES, LANES), const2),
    ]
    args = [lam, z_diff, z_diff, z_diff, z_diff, gq, gk, g128]
    out_specs = [pl.BlockSpec((tq, DIFF_W), qmap(0))]
    out_shape = [jax.ShapeDtypeStruct((m, DIFF_W), BF16)]
    if latent:
        in_specs += [
            pl.BlockSpec((tq, DIFF_W), lambda b, i: (i, 0)),
            pl.BlockSpec((tq, DIFF_W), lambda b, i: (i, 0)),
            pl.BlockSpec((seq_len, DIFF_W), const2),
            pl.BlockSpec((seq_len, DIFF_W), const2),
            pl.BlockSpec((1, past_len, DIFF_W), lambda b, i: (b, 0, 0)),
            pl.BlockSpec((1, past_len, DIFF_W), lambda b, i: (b, 0, 0)),
        ]
        args += [rope[0], rope[1], rope[0], rope[1], ctx[0], ctx[1]]
    else:
        cache_spec = lambda n: pl.BlockSpec((1, n, seq_len, DIFF_W), lambda b, i: (b, 0, 0, 0))
        out_specs += [cache_spec(layer + 1)] * 2
        out_shape += [jax.ShapeDtypeStruct((batch, layer + 1, seq_len, DIFF_W), F32)] * 2
        if layer > 0:
            in_specs += [cache_spec(layer)] * 2
            args += list(caches)
    return pl.pallas_call(
        functools.partial(_attn_kernel, seq_len=seq_len, past_len=past_len,
                          latent=latent, lam_init=lam_init, layer=layer),
        grid=(batch, nq),
        in_specs=in_specs,
        out_specs=out_specs,
        out_shape=out_shape,
        scratch_shapes=[
            pltpu.VMEM((seq_len + past_len, DIFF_W), BF16),
            pltpu.VMEM((seq_len + past_len, 2 * DIFF_W), BF16),
        ],
        compiler_params=_params(2),
        name="attn_latent" if latent else "attn_ctx",
    )(*args)


def _hgrn_kernel(*refs, seq_len, latent):
    if latent:
        (z_ref, lb_ref, tri_ref, g_ref, lm_ref, dm_ref, s0f_ref, s0b_ref,
         out_ref, qi_scr, ks_scr, dec_scr, o_scr, oi_scr) = refs
    else:
        (z_ref, lb_ref, tri_ref, g_ref, lm_ref, dm_ref,
         out_ref, sf_ref, sb_ref, qi_scr, ks_scr, dec_scr, o_scr, oi_scr) = refs
    C, SUB = HG_C, HG_SUB
    n_chunks = seq_len // C
    chunks_per_blk = ROW_BLK // C
    lh = _lane_head()
    sub_row = lax.broadcasted_iota(jnp.int32, (ROW_BLK, 1), 0) & (SUB - 1)
    chunk_row = lax.broadcasted_iota(jnp.int32, (C, 1), 0)

    def tile_roll(x, shift):
        return jnp.concatenate(
            [pltpu.roll(x[i:i + SUBLANES], shift, 0) for i in range(0, ROW_BLK, SUBLANES)], axis=0)

    def head_rep(x):
        return jnp.concatenate([jnp.where(lh == h, x, jnp.zeros_like(x)) for h in range(HEADS)],
                               axis=0)

    def intra(blk):
        start = blk * ROW_BLK if isinstance(blk, int) else pl.multiple_of(blk * ROW_BLK, ROW_BLK)
        rows = pl.ds(start, ROW_BLK)
        lb = lb_ref[...]
        q = z_ref[rows, 0:PACK_W] * (DH ** -0.5)
        v = z_ref[rows, 3 * PACK_W:4 * PACK_W]
        v_reps = [head_rep(v[c * C:(c + 1) * C].astype(BF16)) for c in range(chunks_per_blk)]
        kks, bcs, a_nears = [], [], []
        for d in range(2):
            f = lb + (1.0 - lb) * jax.nn.sigmoid(z_ref[rows, (1 + d) * PACK_W:(2 + d) * PACK_W])
            kks.append(1.0 - f)
            l2 = jnp.log(f) * LOG2E
            hi = l2.astype(BF16)
            r1 = l2 - hi.astype(F32)
            mid = r1.astype(BF16)
            lo = (r1 - mid.astype(F32)).astype(BF16)
            cs = _dot(tri_ref[d], jnp.concatenate([hi, mid, lo], axis=1))
            bcs.append(cs[:, 0:PACK_W] + cs[:, PACK_W:2 * PACK_W] + cs[:, 2 * PACK_W:])

        for d in range(2):
            bwd = d == 1
            kk, bc = kks[d], bcs[d]
            ps = [(q * kk).astype(BF16)]
            for dist in range(1, SUB):
                shift = SUBLANES - dist if bwd else dist
                valid = (sub_row < SUB - dist) if bwd else (sub_row >= dist)
                arg = jnp.where(valid, bc - tile_roll(bc, shift), NEG_BIG)
                ps.append((q * tile_roll(kk, shift) * jnp.exp2(arg)).astype(BF16))
            r = _dot(jnp.concatenate(ps, axis=0), g_ref[...]).astype(BF16)
            a_near = r[0:ROW_BLK] * dm_ref[d, 0]
            for dist in range(1, SUB):
                a_near = a_near + r[dist * ROW_BLK:(dist + 1) * ROW_BLK] * dm_ref[d, dist]
            a_nears.append(a_near)

        for c in range(chunks_per_blk):
            for d in range(2):
                bwd = d == 1
                kk, bc, a_near = kks[d], bcs[d], a_nears[d]
                cs_ = slice(c * C, (c + 1) * C)
                qc, kc, bcc = q[cs_], kk[cs_], bc[cs_]
                last = bcc[0:1] if bwd else bcc[C - 1:C]
                a = a_near[cs_]
                for li, m in enumerate(HG_LEVELS):
                    pieces = []
                    for b2 in range(C // (2 * m)):
                        rr = 2 * m * b2 + (m if bwd else m - 1)
                        pieces.append(jnp.broadcast_to(bcc[rr:rr + 1], (2 * m, PACK_W)))
                    anchor = jnp.concatenate(pieces, axis=0) if len(pieces) > 1 else pieces[0]
                    upper = (chunk_row & (2 * m - 1)) >= m
                    later = jnp.logical_not(upper) if bwd else upper
                    e = jnp.exp2((bcc - anchor) * jnp.where(later, 1.0, -1.0))
                    a = a + _dot_nt((qc * e).astype(BF16),
                                    head_rep((kc * e).astype(BF16))).astype(BF16) * lm_ref[d, li]
                crow = pl.ds(start + c * C, C)
                o_scr[d, crow, :] = _dot(a, v_reps[c])
                qi_scr[d, crow, :] = (qc * jnp.exp2(bcc)).astype(BF16)
                ks_scr[d, crow, :] = (kc * jnp.exp2(last - bcc)).astype(BF16)
                slot = (blk * chunks_per_blk + c) * SUBLANES
                if not isinstance(slot, int):
                    slot = pl.multiple_of(slot, SUBLANES)
                dec_scr[d, pl.ds(slot, SUBLANES), :] = jnp.broadcast_to(
                    jnp.exp2(last), (SUBLANES, PACK_W))

        if not latent:
            vb = v.astype(BF16)
            for d, ref in enumerate((sf_ref, sb_ref)):
                bwd = d == 1
                sweep = range(chunks_per_blk - 1, -1, -1) if bwd else range(chunks_per_blk)
                tail = jnp.zeros((1, PACK_W), F32)
                decayed = [None] * chunks_per_blk
                for c in reversed(sweep):
                    cs_ = slice(c * C, (c + 1) * C)
                    bcc = bcs[d][cs_]
                    total = tail + (bcc[0:1] if bwd else bcc[C - 1:C])
                    decayed[c] = (kks[d][cs_] * jnp.exp2(total - bcc)).astype(BF16)
                    tail = total
                s = _dot_tn(jnp.concatenate(decayed, axis=0), vb) * g_ref[...].astype(F32)
                for h in range(HEADS):
                    ref[0, h] = s[h * DH:(h + 1) * DH, h * DH:(h + 1) * DH]

    n_blk = seq_len // ROW_BLK
    assert latent or n_blk == 1
    if n_blk == 1:
        intra(0)
    else:
        def intra_step(i, carry):
            intra(i)
            return carry
        lax.fori_loop(0, n_blk, intra_step, 0)

    if latent:
        states = (s0f_ref[0], s0b_ref[0])
    else:
        states = (jnp.zeros((PACK_W, PACK_W), F32),) * 2
    gmask = g_ref[...].astype(F32)

    def body(i, states):
        new_states = []
        for d in range(2):
            c = (n_chunks - 1 - i) if d == 1 else i
            rows = pl.ds(pl.multiple_of(c * C, C), C)
            st = states[d]
            oi_scr[d, rows, :] = _dot_nt(qi_scr[d, rows, :], st.astype(BF16))
            dec = dec_scr[d, pl.ds(pl.multiple_of(c * SUBLANES, SUBLANES), 1), :]
            vb = z_ref[rows, 3 * PACK_W:4 * PACK_W].astype(BF16)
            new_states.append(st * dec + _dot_tn(vb, ks_scr[d, rows, :]) * gmask)
        return tuple(new_states)

    lax.fori_loop(0, n_chunks, body, states, unroll=4)

    def finish(start):
        rows = pl.ds(start, ROW_BLK)
        tot = (o_scr[0, rows, :] + oi_scr[0, rows, :]) + (o_scr[1, rows, :] + oi_scr[1, rows, :])
        gate = z_ref[rows, 4 * PACK_W:5 * PACK_W]
        y = _group_rms(tot, g_ref[0:LANES, 0:LANES], DH) * _silu(gate)
        out_ref[rows, :] = y.astype(out_ref.dtype)

    _row_blocks(seq_len, finish)


def _hgrn(z_hg, lb, consts, batch, seq_len, latent, states=None):
    m = batch * seq_len
    tri, g256, lm, dm = consts
    row = lambda b: (b, 0)
    in_specs = [
        pl.BlockSpec((seq_len, HG_COLS), row),
        pl.BlockSpec((1, PACK_W), lambda b: (0, 0)),
        pl.BlockSpec(tri.shape, lambda b: (0, 0, 0)),
        pl.BlockSpec((PACK_W, PACK_W), lambda b: (0, 0)),
        pl.BlockSpec(lm.shape, lambda b: (0, 0, 0, 0)),
        pl.BlockSpec(dm.shape, lambda b: (0, 0, 0, 0)),
    ]
    args = [z_hg, lb, tri, g256, lm, dm]
    out_specs = [pl.BlockSpec((seq_len, PACK_W), row)]
    out_shape = [jax.ShapeDtypeStruct((m, PACK_W), BF16)]
    if latent:
        in_specs += [pl.BlockSpec((1, PACK_W, PACK_W), lambda b: (b, 0, 0))] * 2
        args += [states[0], states[1]]
    else:
        out_specs += [pl.BlockSpec((1, HEADS, DH, DH), lambda b: (b, 0, 0, 0))] * 2
        out_shape += [jax.ShapeDtypeStruct((batch, HEADS, DH, DH), F32)] * 2
    return pl.pallas_call(
        functools.partial(_hgrn_kernel, seq_len=seq_len, latent=latent),
        grid=(batch,),
        in_specs=in_specs,
        out_specs=out_specs,
        out_shape=out_shape,
        scratch_shapes=[
            pltpu.VMEM((2, seq_len, PACK_W), BF16),
            pltpu.VMEM((2, seq_len, PACK_W), BF16),
            pltpu.VMEM((2, seq_len // HG_C * SUBLANES, PACK_W), F32),
            pltpu.VMEM((2, seq_len, PACK_W), F32),
            pltpu.VMEM((2, seq_len, PACK_W), F32),
        ],
        compiler_params=_params(1),
        name="hgrn_latent" if latent else "hgrn_ctx",
    )(*args)


def _block_ones(n):
    idx = np.arange(n) // DH
    return jnp.asarray((idx[:, None] == idx[None, :]).astype(np.float32), dtype=BF16)


def _hgrn_consts():
    C = HG_C
    t = np.arange(ROW_BLK)
    same_chunk = (t[:, None] // C) == (t[None, :] // C)
    tri_f = same_chunk & (t[None, :] <= t[:, None])
    tri_b = same_chunk & (t[None, :] >= t[:, None])
    tc = np.arange(C)
    lms = []
    for fwd in (True, False):
        per_level = []
        for m in HG_LEVELS:
            same = (tc[:, None] // (2 * m)) == (tc[None, :] // (2 * m))
            upper = (tc % (2 * m)) >= m
            later, earlier = (upper, ~upper) if fwd else (~upper, upper)
            per_level.append(np.tile(same & later[:, None] & earlier[None, :], (1, HEADS)))
        lms.append(np.stack(per_level))
    dms = [[np.tile(tc[None, :] == tc[:, None] + sign * dist, (ROW_BLK // C, HEADS))
            for dist in range(HG_SUB)] for sign in (-1, 1)]
    return (jnp.asarray(np.stack([tri_f, tri_b]).astype(np.float32), dtype=BF16),
            _block_ones(PACK_W),
            jnp.asarray(np.stack(lms).astype(np.float32), dtype=BF16),
            jnp.asarray(np.stack(dms).astype(np.float32), dtype=BF16))


def _rope_tables(seq_len, width):
    t = np.arange(seq_len)
    pos = np.stack([t // GRID_W, t % GRID_W], axis=1).astype(np.float32)
    j = np.arange(DH)
    axis = j // (2 * ROPE_PAIR)
    jj = j % (2 * ROPE_PAIR)
    inv = (ROPE_BASE ** (-(jj % ROPE_PAIR).astype(np.float64) / ROPE_PAIR)).astype(np.float32)
    ang = (pos[:, axis] * inv[None, :]).astype(np.float64)
    cos = np.cos(ang)
    sin = np.where(jj < ROPE_PAIR, -np.sin(ang), np.sin(ang))
    reps = width // DH
    return (jnp.asarray(np.tile(cos, (1, reps)), dtype=F32),
            jnp.asarray(np.tile(sin, (1, reps)), dtype=F32))


def _to_blockdiag_t(s):
    b = s.shape[0]
    eye = jnp.eye(HEADS, dtype=s.dtype)
    return jnp.einsum('bhde,hg->bhegd', s, eye).reshape(b, PACK_W, PACK_W)


def _layer(x2d, batch, seq_len, mod, l, p, latent, ctx, caches):
    shift, scale, gate = mod
    z_ret, z_diff, z_hg = _inproj(x2d, p['norm_g'][l], scale, shift, p['w_in'][l], seq_len)
    lam_init = 0.8 - 0.6 * math.exp(-0.3 * l)
    if latent:
        k_ctx, v_ctx, s_rf, s_rb, s_hf, s_hb = ctx
        r = _retention(z_ret, p['log_g'][l], p['g256'], batch, seq_len, True,
                       rope=p['rope256'], states=(s_rf, s_rb))[0]
        a = _diff_attention(z_diff, p['lam'][l], p['gq'][l], p['gk'][l], p['g128'], batch,
                            seq_len, True, lam_init, l, rope=p['rope512'],
                            ctx=(k_ctx, v_ctx))[0]
        hgo = _hgrn(z_hg, p['lb'][l], p['hg_consts'], batch, seq_len, True,
                    states=(s_hf, s_hb))[0]
        extras = None
    else:
        r, s_rf, s_rb = _retention(z_ret, p['log_g'][l], p['g256'], batch, seq_len, False)
        a, kcache, vcache = _diff_attention(z_diff, p['lam'][l], p['gq'][l], p['gk'][l],
                                            p['g128'], batch, seq_len, False, lam_init, l,
                                            caches=caches)
        hgo, s_hf, s_hb = _hgrn(z_hg, p['lb'][l], p['hg_consts'], batch, seq_len, False)
        extras = ((kcache, vcache), (s_rf, s_rb, s_hf, s_hb))
    y = _outproj(x2d, r, a, hgo, gate, p['w_out'][l], seq_len)
    return y, extras


def kernel(x_prompt, x_sample, c, c_ctx, cache_diff_k, cache_diff_v, state_ret_fwd,
           state_ret_bwd, state_hgrn_fwd, state_hgrn_bwd, norm_g, w_ada, b_ada, w_in,
           w_out, ret_decay_logit, diff_qn_g, diff_kn_g, diff_lambda, hgrn_lb_logit):
    batch, seq, _ = x_prompt.shape
    dec_batch, dec_seq, _ = x_sample.shape
    past_len = cache_diff_k.shape[2]

    cv = jnp.zeros((8, D_MODEL), F32).at[0].set(c_ctx).at[1:1 + dec_batch].set(c)
    mods = _ada(cv, w_ada, b_ada)

    lb_all = jax.nn.softmax(hgrn_lb_logit.astype(F32), axis=0)
    lb_all = jnp.cumsum(lb_all, axis=0) - lb_all[0]
    lp = diff_lambda.astype(F32)
    lam_inits = jnp.asarray([0.8 - 0.6 * math.exp(-0.3 * l) for l in range(DEPTH)], F32)
    lam = (jnp.exp(jnp.sum(lp[:, 0] * lp[:, 1], axis=-1))
           - jnp.exp(jnp.sum(lp[:, 2] * lp[:, 3], axis=-1)) + lam_inits)
    p = {
        'norm_g': norm_g.reshape(DEPTH, 1, D_MODEL),
        'w_in': w_in.astype(BF16),
        'w_out': w_out.astype(BF16),
        'log_g': jax.nn.log_sigmoid(ret_decay_logit.astype(F32)),
        'lam': lam.reshape(DEPTH, 1),
        'gq': jnp.tile(diff_qn_g, (1, DIFF_W // DH)).reshape(DEPTH, 1, DIFF_W),
        'gk': jnp.tile(diff_kn_g, (1, DIFF_W // DH)).reshape(DEPTH, 1, DIFF_W),
        'lb': lb_all.reshape(DEPTH, 1, PACK_W),
        'g256': _block_ones(PACK_W),
        'g128': _block_ones(LANES),
        'hg_consts': _hgrn_consts(),
        'rope256': _rope_tables(dec_seq, PACK_W),
        'rope512': _rope_tables(dec_seq, DIFF_W),
    }

    def split_mod(rows):
        return tuple(rows[:, None, j * D_MODEL:(j + 1) * D_MODEL] for j in range(3))

    y = x_prompt.reshape(batch * seq, D_MODEL)
    caches = None
    states = []
    for l in range(DEPTH):
        y, (caches, st) = _layer(y, batch, seq, split_mod(mods[l, 0:1]), l, p, False, None, caches)
        states.append(st)
    y_prompt = y.reshape(batch, seq, D_MODEL)
    new_k = caches[0].reshape(batch, DEPTH, seq, HEADS, 2, DH)
    new_v = caches[1].reshape(batch, DEPTH, seq, HEADS, DIFF_DV)
    new_states = [jnp.stack([st[i] for st in states], axis=1) for i in range(4)]

    y = x_sample.reshape(dec_batch * dec_seq, D_MODEL)
    for l in range(DEPTH):
        ctx = (cache_diff_k[:, l].reshape(dec_batch, past_len, DIFF_W),
               cache_diff_v[:, l].reshape(dec_batch, past_len, DIFF_W),
               _to_blockdiag_t(state_ret_fwd[:, l]), _to_blockdiag_t(state_ret_bwd[:, l]),
               _to_blockdiag_t(state_hgrn_fwd[:, l]), _to_blockdiag_t(state_hgrn_bwd[:, l]))
        y, _ = _layer(y, dec_batch, dec_seq, split_mod(mods[l, 1:1 + dec_batch]), l, p, True,
                      ctx, None)
    y_sample = y.reshape(dec_batch, dec_seq, D_MODEL)

    return (y_prompt, y_sample, new_k, new_v, *new_states)
```

```python
import functools
import math

import numpy as np
import jax
import jax.numpy as jnp
from jax import lax
from jax.experimental import pallas as pl
from jax.experimental.pallas import tpu as pltpu

F32 = jnp.float32
BF16 = jnp.bfloat16

D_MODEL = 1024
DEPTH = 2
GRID_W = 64
HEADS = 4
DH = 64
PACK_W = HEADS * DH
DIFF_W = 512
DIFF_DV = 128
RET_COLS = 4 * PACK_W
DIFF_COLS = 4 * DIFF_W
HG_COLS = 5 * PACK_W
IN_W = RET_COLS + DIFF_COLS + HG_COLS
MIX_W = PACK_W + DIFF_W + PACK_W
ROPE_BASE = 10000.0
ROPE_PAIR = 16
EPS = 1e-6
LOG2E = 1.4426950408889634
NEG_BIG = -1e30

RET_C = 256
HG_C = 64
HG_SUB = 4
HG_LEVELS = (4, 8, 16, 32)
SUBLANES = 8
LANES = 128
ROW_BLK = 256
TQ = 512
TM_IN = 1024
TM_OUT = 512

VMEM_LIMIT = 56 * 1024 * 1024


def _params(n_axes):
    return pltpu.CompilerParams(
        dimension_semantics=("arbitrary",) * n_axes, vmem_limit_bytes=VMEM_LIMIT)


def _dot(a, b):
    return jnp.dot(a, b, preferred_element_type=F32)


def _dot_nt(a, b):
    return lax.dot_general(a, b, (((1,), (1,)), ((), ())), preferred_element_type=F32)


def _dot_tn(a, b):
    return lax.dot_general(a, b, (((0,), (0,)), ((), ())), preferred_element_type=F32)


def _silu(x):
    return x * jax.nn.sigmoid(x)


def _group_rms(x, g, width):
    n, w = x.shape
    x2 = x * x
    cols = range(0, w, LANES)
    stacked = jnp.concatenate([x2[:, j:j + LANES] for j in cols], axis=0)
    hi = stacked.astype(BF16)
    lo = (stacked - hi.astype(F32)).astype(BF16)
    sums = _dot(hi, g) + _dot(lo, g)
    ms = jnp.concatenate([sums[i * n:(i + 1) * n] for i in range(len(cols))], axis=1)
    return x * lax.rsqrt(ms * (1.0 / width) + EPS)


def _rope(x, cos, sin):
    w = x.shape[-1]
    lane = lax.broadcasted_iota(jnp.int32, (1, w), 1)
    first = (lane & (2 * ROPE_PAIR - 1)) < ROPE_PAIR
    swapped = jnp.where(first, pltpu.roll(x, w - ROPE_PAIR, 1), pltpu.roll(x, ROPE_PAIR, 1))
    return x * cos + swapped * sin


def _lane_head(w=PACK_W):
    return lax.broadcasted_iota(jnp.int32, (1, w), 1) // DH


def _row_blocks(n_rows, body):
    n_blk = n_rows // ROW_BLK
    if n_blk == 1:
        body(0)
    else:
        def step(i, carry):
            body(pl.multiple_of(i * ROW_BLK, ROW_BLK))
            return carry
        lax.fori_loop(0, n_blk, step, 0)


def _ada_kernel(cv_ref, w_ref, b_ref, o_ref):
    cv = cv_ref[...]
    o_ref[0] = _dot(_silu(cv), w_ref[0]) + b_ref[0]


def _ada(cv, w_ada, b_ada):
    tn = 1024
    n = 3 * D_MODEL
    return pl.pallas_call(
        _ada_kernel,
        grid=(DEPTH, n // tn),
        in_specs=[
            pl.BlockSpec((8, D_MODEL), lambda l, j: (0, 0)),
            pl.BlockSpec((1, D_MODEL, tn), lambda l, j: (l, 0, j)),
            pl.BlockSpec((1, 1, tn), lambda l, j: (l, 0, j)),
        ],
        out_specs=pl.BlockSpec((1, 8, tn), lambda l, j: (l, 0, j)),
        out_shape=jax.ShapeDtypeStruct((DEPTH, 8, n), F32),
        compiler_params=_params(2),
        name="ada",
    )(cv, w_ada, b_ada.reshape(DEPTH, 1, n))


def _inproj_kernel(x_ref, g_ref, sc_ref, sh_ref, w_ref, zr_ref, zd_ref, zh_ref):
    x = x_ref[...]
    ms = jnp.mean(x * x, axis=-1, keepdims=True)
    h = x * lax.rsqrt(ms + EPS) * g_ref[...]
    h = h * (1.0 + sc_ref[0]) + sh_ref[0]
    z = _dot(h.astype(BF16), w_ref[...])
    zr_ref[...] = z[:, :RET_COLS].astype(zr_ref.dtype)
    zd_ref[...] = z[:, RET_COLS:RET_COLS + DIFF_COLS].astype(zd_ref.dtype)
    zh_ref[...] = z[:, RET_COLS + DIFF_COLS:]


def _inproj(x2d, g, scale, shift, w_bf, seq_len):
    m = x2d.shape[0]
    per_batch = scale.shape[0] > 1
    tiles_per_seq = seq_len // TM_IN
    mod_map = (lambda i: (i // tiles_per_seq, 0, 0)) if per_batch else (lambda i: (0, 0, 0))
    return pl.pallas_call(
        _inproj_kernel,
        grid=(m // TM_IN,),
        in_specs=[
            pl.BlockSpec((TM_IN, D_MODEL), lambda i: (i, 0)),
            pl.BlockSpec((1, D_MODEL), lambda i: (0, 0)),
            pl.BlockSpec((1, 1, D_MODEL), mod_map),
            pl.BlockSpec((1, 1, D_MODEL), mod_map),
            pl.BlockSpec((D_MODEL, IN_W), lambda i: (0, 0)),
        ],
        out_specs=[
            pl.BlockSpec((TM_IN, RET_COLS), lambda i: (i, 0)),
            pl.BlockSpec((TM_IN, DIFF_COLS), lambda i: (i, 0)),
            pl.BlockSpec((TM_IN, HG_COLS), lambda i: (i, 0)),
        ],
        out_shape=[
            jax.ShapeDtypeStruct((m, RET_COLS), BF16),
            jax.ShapeDtypeStruct((m, DIFF_COLS), BF16),
            jax.ShapeDtypeStruct((m, HG_COLS), F32),
        ],
        compiler_params=_params(1),
        name="inproj",
    )(x2d, g, scale, shift, w_bf)


def _outproj_kernel(x_ref, r_ref, d_ref, h_ref, gate_ref, w_ref, o_ref):
    mixed = jnp.concatenate([r_ref[...], d_ref[...], h_ref[...]], axis=1)
    o_ref[...] = x_ref[...] + gate_ref[0] * _dot(mixed, w_ref[...])


def _outproj(x2d, r, d, h, gate, w_bf, seq_len):
    m = x2d.shape[0]
    per_batch = gate.shape[0] > 1
    tiles_per_seq = max(seq_len // TM_OUT, 1)
    tm = TM_OUT if (not per_batch or seq_len % TM_OUT == 0) else seq_len
    mod_map = (lambda i: (i // tiles_per_seq, 0, 0)) if per_batch else (lambda i: (0, 0, 0))
    return pl.pallas_call(
        _outproj_kernel,
        grid=(m // tm,),
        in_specs=[
            pl.BlockSpec((tm, D_MODEL), lambda i: (i, 0)),
            pl.BlockSpec((tm, PACK_W), lambda i: (i, 0)),
            pl.BlockSpec((tm, DIFF_W), lambda i: (i, 0)),
            pl.BlockSpec((tm, PACK_W), lambda i: (i, 0)),
            pl.BlockSpec((1, 1, D_MODEL), mod_map),
            pl.BlockSpec((MIX_W, D_MODEL), lambda i: (0, 0)),
        ],
        out_specs=pl.BlockSpec((tm, D_MODEL), lambda i: (i, 0)),
        out_shape=jax.ShapeDtypeStruct((m, D_MODEL), F32),
        compiler_params=_params(1),
        name="outproj",
    )(x2d, r, d, h, gate, w_bf)


def _ret_kernel(*refs, seq_len, latent):
    if latent:
        (lg_ref, z_ref, g_ref, cos_ref, sin_ref, s0f_ref, s0b_ref,
         out_ref, mask_scr, o_scr) = refs
    else:
        lg_ref, z_ref, g_ref, out_ref, sf_ref, sb_ref, mask_scr, o_scr = refs
    C = RET_C
    n_chunks = seq_len // C
    lh = _lane_head()

    def lane_vec(direction):
        v = jnp.zeros((1, PACK_W), F32)
        for h in range(HEADS):
            v = jnp.where(lh == h, lg_ref[direction, h], v)
        return v

    lgf, lgb = lane_vec(0), lane_vec(1)
    tau = lax.broadcasted_iota(jnp.int32, (C, 1), 0).astype(F32)
    qdec_f = jnp.exp((tau + 1.0) * lgf)
    qdec_b = jnp.exp((C - tau) * lgb)
    kdec_f = jnp.exp((C - 1.0 - tau) * lgf)
    kdec_b = jnp.exp(tau * lgb)
    sdec_f = jnp.exp(float(C) * lgf)
    sdec_b = jnp.exp(float(C) * lgb)
    gmask = g_ref[...].astype(F32)

    @pl.when(pl.program_id(0) == 0)
    def _():
        dd = (lax.broadcasted_iota(jnp.int32, (C, C), 0)
              - lax.broadcasted_iota(jnp.int32, (C, C), 1)).astype(F32)
        for h in range(HEADS):
            mf = jnp.where(dd >= 0, jnp.exp(jnp.maximum(dd, 0.0) * lg_ref[0, h]), 0.0)
            mb = jnp.where(dd <= 0, jnp.exp(jnp.maximum(-dd, 0.0) * lg_ref[1, h]), 0.0)
            mask_scr[h * C:(h + 1) * C, :] = mf + mb

    def load(r0):
        q = z_ref[pl.ds(r0, C), 0:PACK_W].astype(F32)
        k = z_ref[pl.ds(r0, C), PACK_W:2 * PACK_W].astype(F32) * (DH ** -0.5)
        v = z_ref[pl.ds(r0, C), 2 * PACK_W:3 * PACK_W].astype(F32)
        if latent:
            cos = cos_ref[pl.ds(r0, C), :]
            sin = sin_ref[pl.ds(r0, C), :]
            q = _rope(q, cos, sin)
            k = _rope(k, cos, sin)
        return q, k, v

    def intra(q, k, v):
        qb = q.astype(BF16)
        q4 = jnp.concatenate([jnp.where(lh == h, qb, jnp.zeros_like(qb)) for h in range(HEADS)],
                             axis=0)
        a4 = _dot_nt(q4, k.astype(BF16)) * mask_scr[...]
        o4 = _dot(a4.astype(BF16), v.astype(BF16))
        o = jnp.zeros((C, PACK_W), F32)
        for h in range(HEADS):
            o = o + jnp.where(lh == h, o4[h * C:(h + 1) * C], 0.0)
        return o

    def finish(o, r0):
        gate = z_ref[pl.ds(r0, C), 3 * PACK_W:4 * PACK_W].astype(F32)
        y = _group_rms(o, g_ref[0:LANES, 0:LANES], DH) * _silu(gate)
        out_ref[pl.ds(r0, C), :] = y.astype(out_ref.dtype)

    def state_step(st, k, v, kdec, sdec):
        upd = _dot_tn(v.astype(BF16), (k * kdec).astype(BF16))
        return (st * sdec + upd) * gmask

    if not latent:
        q, k, v = load(0)
        finish(intra(q, k, v), 0)
        vb = v.astype(BF16)
        for ref, kdec in ((sf_ref, kdec_f), (sb_ref, kdec_b)):
            s = _dot_tn((k * kdec).astype(BF16), vb) * gmask
            for h in range(HEADS):
                ref[0, h] = s[h * DH:(h + 1) * DH, h * DH:(h + 1) * DH]
        return

    def bwd_body(i, st):
        r0 = pl.multiple_of((n_chunks - 1 - i) * C, C)
        q, k, v = load(r0)
        o_scr[pl.ds(r0, C), :] = _dot_nt((q * qdec_b).astype(BF16), st.astype(BF16))
        return state_step(st, k, v, kdec_b, sdec_b)

    lax.fori_loop(0, n_chunks, bwd_body, s0b_ref[0], unroll=2)

    def fwd_body(i, st):
        r0 = pl.multiple_of(i * C, C)
        q, k, v = load(r0)
        o = intra(q, k, v) + o_scr[pl.ds(r0, C), :]
        o = o + _dot_nt((q * qdec_f).astype(BF16), st.astype(BF16))
        finish(o, r0)
        return state_step(st, k, v, kdec_f, sdec_f)

    lax.fori_loop(0, n_chunks, fwd_body, s0f_ref[0], unroll=2)


def _retention(z_ret, log_g, g256, batch, seq_len, latent, rope=None, states=None):
    m = batch * seq_len
    row = lambda b: (b, 0)
    const2 = lambda b: (0, 0)
    in_specs = [
        pl.BlockSpec(memory_space=pltpu.SMEM),
        pl.BlockSpec((seq_len, RET_COLS), row),
        pl.BlockSpec((PACK_W, PACK_W), const2),
    ]
    args = [log_g, z_ret, g256]
    out_specs = [pl.BlockSpec((seq_len, PACK_W), row)]
    out_shape = [jax.ShapeDtypeStruct((m, PACK_W), BF16)]
    if latent:
        state_spec = pl.BlockSpec((1, PACK_W, PACK_W), lambda b: (b, 0, 0))
        in_specs += [pl.BlockSpec((seq_len, PACK_W), const2)] * 2 + [state_spec] * 2
        args += [rope[0], rope[1], states[0], states[1]]
    else:
        out_specs += [pl.BlockSpec((1, HEADS, DH, DH), lambda b: (b, 0, 0, 0))] * 2
        out_shape += [jax.ShapeDtypeStruct((batch, HEADS, DH, DH), F32)] * 2
    return pl.pallas_call(
        functools.partial(_ret_kernel, seq_len=seq_len, latent=latent),
        grid=(batch,),
        in_specs=in_specs,
        out_specs=out_specs,
        out_shape=out_shape,
        scratch_shapes=[
            pltpu.VMEM((HEADS * RET_C, RET_C), F32),
            pltpu.VMEM((seq_len, PACK_W), F32),
        ],
        compiler_params=_params(1),
        name="ret_latent" if latent else "ret_ctx",
    )(*args)


def _attn_kernel(*refs, seq_len, past_len, latent, lam_init, layer):
    if latent:
        (lam_ref, q_ref, k_ref, v_ref, gate_ref, gq_ref, gk_ref, g_ref,
         cosq_ref, sinq_ref, cosk_ref, sink_ref, kc_ref, vc_ref,
         out_ref, k_scr, v_scr) = refs
        prev_refs = ()
    else:
        (lam_ref, q_ref, k_ref, v_ref, gate_ref, gq_ref, gk_ref, g_ref) = refs[:8]
        prev_refs = refs[8:-5]
        out_ref, kn_ref, vn_ref, k_scr, v_scr = refs[-5:]
    ones = jnp.ones((ROW_BLK, DIFF_DV), BF16)

    def put_values(sl, v):
        for h in range(HEADS):
            v_scr[sl, 2 * h * DIFF_DV:(2 * h + 1) * DIFF_DV] = (
                v[:, h * DIFF_DV:(h + 1) * DIFF_DV].astype(BF16))
            v_scr[sl, (2 * h + 1) * DIFF_DV:(2 * h + 2) * DIFF_DV] = ones

    @pl.when(pl.program_id(1) == 0)
    def _():
        for r in range(seq_len // ROW_BLK):
            sl = pl.ds(r * ROW_BLK, ROW_BLK)
            kn = _group_rms(k_ref[sl, :].astype(F32), g_ref[...], DH) * gk_ref[...]
            v = v_ref[sl, :].astype(F32)
            if latent:
                kn = _rope(kn, cosk_ref[sl, :], sink_ref[sl, :])
            else:
                kn_ref[0, layer, sl, :] = kn
                vn_ref[0, layer, sl, :] = v
            k_scr[sl, :] = kn.astype(BF16)
            put_values(sl, v)
        if prev_refs:
            kn_ref[0, 0:layer] = prev_refs[0][0]
            vn_ref[0, 0:layer] = prev_refs[1][0]
        if latent:
            for r in range(past_len // ROW_BLK):
                src = pl.ds(r * ROW_BLK, ROW_BLK)
                dst = pl.ds(seq_len + r * ROW_BLK, ROW_BLK)
                k_scr[dst, :] = kc_ref[0, src, :].astype(BF16)
                put_values(dst, vc_ref[0, src, :])

    lam = lam_ref[0]
    qn = _group_rms(q_ref[...].astype(F32), g_ref[...], DH) * gq_ref[...]
    if latent:
        qn = _rope(qn, cosq_ref[...], sinq_ref[...])
    qn = qn * (DH ** -0.5 * LOG2E)
    lane = lax.broadcasted_iota(jnp.int32, (1, 2 * DH), 1)
    for h in range(HEADS):
        hs = slice(h * 2 * DH, (h + 1) * 2 * DH)
        vx = v_scr[:, 2 * h * DIFF_DV:(2 * h + 2) * DIFF_DV]
        qh = qn[:, hs]
        kh = k_scr[:, hs]
        parts = []
        for first_map in (True, False):
            qm = jnp.where((lane < DH) == first_map, qh, 0.0).astype(BF16)
            s = _dot_nt(qm, kh)
            e = jnp.exp2(s - jnp.max(s, axis=-1, keepdims=True)).astype(BF16)
            ox = _dot(e, vx)
            parts.append(ox[:, :DIFF_DV] / ox[:, DIFF_DV:])
        o = parts[0] - lam * parts[1]
        ms = jnp.mean(o * o, axis=-1, keepdims=True)
        y = o * lax.rsqrt(ms + EPS) * (1.0 - lam_init) * _silu(gate_ref[:, hs].astype(F32))
        out_ref[:, hs] = y.astype(out_ref.dtype)


def _diff_attention(z_diff, lam, gq, gk, g128, batch, seq_len, latent, lam_init, layer,
                    rope=None, ctx=None, caches=None):
    m = batch * seq_len
    tq = min(TQ, seq_len)
    nq = seq_len // tq
    past_len = ctx[0].shape[1] if latent else 0
    qmap = lambda col: (lambda b, i: (b * nq + i, col))
    kvmap = lambda col: (lambda b, i: (b, col))
    const2 = lambda b, i: (0, 0)
    in_specs = [
        pl.BlockSpec(memory_space=pltpu.SMEM),
        pl.BlockSpec((tq, DIFF_W), qmap(0)),
        pl.BlockSpec((seq_len, DIFF_W), kvmap(1)),
        pl.BlockSpec((seq_len, DIFF_W), kvmap(2)),
        pl.BlockSpec((tq, DIFF_W), qmap(3)),
        pl.BlockSpec((1, DIFF_W), const2),
        pl.BlockSpec((1, DIFF_W), const2),
        pl.BlockSpec((LANES, LANES), const2),
    ]
    args = [lam, z_diff, z_diff, z_diff, z_diff, gq, gk, g128]
    out_specs = [pl.BlockSpec((tq, DIFF_W), qmap(0))]
    out_shape = [jax.ShapeDtypeStruct((m, DIFF_W), BF16)]
    if latent:
        in_specs += [
            pl.BlockSpec((tq, DIFF_W), lambda b, i: (i, 0)),
            pl.BlockSpec((tq, DIFF_W), lambda b, i: (i, 0)),
            pl.BlockSpec((seq_len, DIFF_W), const2),
            pl.BlockSpec((seq_len, DIFF_W), const2),
            pl.BlockSpec((1, past_len, DIFF_W), lambda b, i: (b, 0, 0)),
            pl.BlockSpec((1, past_len, DIFF_W), lambda b, i: (b, 0, 0)),
        ]
        args += [rope[0], rope[1], rope[0], rope[1], ctx[0], ctx[1]]
    else:
        cache_spec = lambda n: pl.BlockSpec((1, n, seq_len, DIFF_W), lambda b, i: (b, 0, 0, 0))
        out_specs += [cache_spec(layer + 1)] * 2
        out_shape += [jax.ShapeDtypeStruct((batch, layer + 1, seq_len, DIFF_W), F32)] * 2
        if layer > 0:
            in_specs += [cache_spec(layer)] * 2
            args += list(caches)
    return pl.pallas_call(
        functools.partial(_attn_kernel, seq_len=seq_len, past_len=past_len,
                          latent=latent, lam_init=lam_init, layer=layer),
        grid=(batch, nq),
        in_specs=in_specs,
        out_specs=out_specs,
        out_shape=out_shape,
        scratch_shapes=[
            pltpu.VMEM((seq_len + past_len, DIFF_W), BF16),
            pltpu.VMEM((seq_len + past_len, 2 * DIFF_W), BF16),
        ],
        compiler_params=_params(2),
        name="attn_latent" if latent else "attn_ctx",
    )(*args)


def _hgrn_kernel(*refs, seq_len, latent):
    if latent:
        (z_ref, lb_ref, tri_ref, g_ref, lm_ref, dm_ref, s0f_ref, s0b_ref,
         out_ref, qi_scr, ks_scr, dec_scr, o_scr, oi_scr) = refs
    else:
        (z_ref, lb_ref, tri_ref, g_ref, lm_ref, dm_ref,
         out_ref, sf_ref, sb_ref, qi_scr, ks_scr, dec_scr, o_scr, oi_scr) = refs
    C, SUB = HG_C, HG_SUB
    n_chunks = seq_len // C
    chunks_per_blk = ROW_BLK // C
    lh = _lane_head()
    sub_row = lax.broadcasted_iota(jnp.int32, (ROW_BLK, 1), 0) & (SUB - 1)
    chunk_row = lax.broadcasted_iota(jnp.int32, (C, 1), 0)

    def tile_roll(x, shift):
        return jnp.concatenate(
            [pltpu.roll(x[i:i + SUBLANES], shift, 0) for i in range(0, ROW_BLK, SUBLANES)], axis=0)

    def head_rep(x):
        return jnp.concatenate([jnp.where(lh == h, x, jnp.zeros_like(x)) for h in range(HEADS)],
                               axis=0)

    def intra(blk):
        start = blk * ROW_BLK if isinstance(blk, int) else pl.multiple_of(blk * ROW_BLK, ROW_BLK)
        rows = pl.ds(start, ROW_BLK)
        lb = lb_ref[...]
        q = z_ref[rows, 0:PACK_W] * (DH ** -0.5)
        v = z_ref[rows, 3 * PACK_W:4 * PACK_W]
        v_reps = [head_rep(v[c * C:(c + 1) * C].astype(BF16)) for c in range(chunks_per_blk)]
        kks, bcs, a_nears = [], [], []
        for d in range(2):
            f = lb + (1.0 - lb) * jax.nn.sigmoid(z_ref[rows, (1 + d) * PACK_W:(2 + d) * PACK_W])
            kks.append(1.0 - f)
            l2 = jnp.log(f) * LOG2E
            hi = l2.astype(BF16)
            r1 = l2 - hi.astype(F32)
            mid = r1.astype(BF16)
            lo = (r1 - mid.astype(F32)).astype(BF16)
            cs = _dot(tri_ref[d], jnp.concatenate([hi, mid, lo], axis=1))
            bcs.append(cs[:, 0:PACK_W] + cs[:, PACK_W:2 * PACK_W] + cs[:, 2 * PACK_W:])

        for d in range(2):
            bwd = d == 1
            kk, bc = kks[d], bcs[d]
            ps = [(q * kk).astype(BF16)]
            for dist in range(1, SUB):
                shift = SUBLANES - dist if bwd else dist
                valid = (sub_row < SUB - dist) if bwd else (sub_row >= dist)
                arg = jnp.where(valid, bc - tile_roll(bc, shift), NEG_BIG)
                ps.append((q * tile_roll(kk, shift) * jnp.exp2(arg)).astype(BF16))
            r = _dot(jnp.concatenate(ps, axis=0), g_ref[...]).astype(BF16)
            a_near = r[0:ROW_BLK] * dm_ref[d, 0]
            for dist in range(1, SUB):
                a_near = a_near + r[dist * ROW_BLK:(dist + 1) * ROW_BLK] * dm_ref[d, dist]
            a_nears.append(a_near)

        for c in range(chunks_per_blk):
            for d in range(2):
                bwd = d == 1
                kk, bc, a_near = kks[d], bcs[d], a_nears[d]
                cs_ = slice(c * C, (c + 1) * C)
                qc, kc, bcc = q[cs_], kk[cs_], bc[cs_]
                last = bcc[0:1] if bwd else bcc[C - 1:C]
                a = a_near[cs_]
                for li, m in enumerate(HG_LEVELS):
                    pieces = []
                    for b2 in range(C // (2 * m)):
                        rr = 2 * m * b2 + (m if bwd else m - 1)
                        pieces.append(jnp.broadcast_to(bcc[rr:rr + 1], (2 * m, PACK_W)))
                    anchor = jnp.concatenate(pieces, axis=0) if len(pieces) > 1 else pieces[0]
                    upper = (chunk_row & (2 * m - 1)) >= m
                    later = jnp.logical_not(upper) if bwd else upper
                    e = jnp.exp2((bcc - anchor) * jnp.where(later, 1.0, -1.0))
                    a = a + _dot_nt((qc * e).astype(BF16),
                                    head_rep((kc * e).astype(BF16))).astype(BF16) * lm_ref[d, li]
                crow = pl.ds(start + c * C, C)
                o_scr[d, crow, :] = _dot(a, v_reps[c])
                qi_scr[d, crow, :] = (qc * jnp.exp2(bcc)).astype(BF16)
                ks_scr[d, crow, :] = (kc * jnp.exp2(last - bcc)).astype(BF16)
                slot = (blk * chunks_per_blk + c) * SUBLANES
                if not isinstance(slot, int):
                    slot = pl.multiple_of(slot, SUBLANES)
                dec_scr[d, pl.ds(slot, SUBLANES), :] = jnp.broadcast_to(
                    jnp.exp2(last), (SUBLANES, PACK_W))

        if not latent:
            vb = v.astype(BF16)
            for d, ref in enumerate((sf_ref, sb_ref)):
                bwd = d == 1
                sweep = range(chunks_per_blk - 1, -1, -1) if bwd else range(chunks_per_blk)
                tail = jnp.zeros((1, PACK_W), F32)
                decayed = [None] * chunks_per_blk
                for c in reversed(sweep):
                    cs_ = slice(c * C, (c + 1) * C)
                    bcc = bcs[d][cs_]
                    total = tail + (bcc[0:1] if bwd else bcc[C - 1:C])
                    decayed[c] = (kks[d][cs_] * jnp.exp2(total - bcc)).astype(BF16)
                    tail = total
                s = _dot_tn(jnp.concatenate(decayed, axis=0), vb) * g_ref[...].astype(F32)
                for h in range(HEADS):
                    ref[0, h] = s[h * DH:(h + 1) * DH, h * DH:(h + 1) * DH]

    n_blk = seq_len // ROW_BLK
    assert latent or n_blk == 1
    if n_blk == 1:
        intra(0)
    else:
        def intra_step(i, carry):
            intra(i)
            return carry
        lax.fori_loop(0, n_blk, intra_step, 0)

    if latent:
        states = (s0f_ref[0], s0b_ref[0])
    else:
        states = (jnp.zeros((PACK_W, PACK_W), F32),) * 2
    gmask = g_ref[...].astype(F32)

    def body(i, states):
        new_states = []
        for d in range(2):
            c = (n_chunks - 1 - i) if d == 1 else i
            rows = pl.ds(pl.multiple_of(c * C, C), C)
            st = states[d]
            oi_scr[d, rows, :] = _dot_nt(qi_scr[d, rows, :], st.astype(BF16))
            dec = dec_scr[d, pl.ds(pl.multiple_of(c * SUBLANES, SUBLANES), 1), :]
            vb = z_ref[rows, 3 * PACK_W:4 * PACK_W].astype(BF16)
            new_states.append(st * dec + _dot_tn(vb, ks_scr[d, rows, :]) * gmask)
        return tuple(new_states)

    lax.fori_loop(0, n_chunks, body, states, unroll=4)

    def finish(start):
        rows = pl.ds(start, ROW_BLK)
        tot = (o_scr[0, rows, :] + oi_scr[0, rows, :]) + (o_scr[1, rows, :] + oi_scr[1, rows, :])
        gate = z_ref[rows, 4 * PACK_W:5 * PACK_W]
        y = _group_rms(tot, g_ref[0:LANES, 0:LANES], DH) * _silu(gate)
        out_ref[rows, :] = y.astype(out_ref.dtype)

    _row_blocks(seq_len, finish)


def _hgrn(z_hg, lb, consts, batch, seq_len, latent, states=None):
    m = batch * seq_len
    tri, g256, lm, dm = consts
    row = lambda b: (b, 0)
    in_specs = [
        pl.BlockSpec((seq_len, HG_COLS), row),
        pl.BlockSpec((1, PACK_W), lambda b: (0, 0)),
        pl.BlockSpec(tri.shape, lambda b: (0, 0, 0)),
        pl.BlockSpec((PACK_W, PACK_W), lambda b: (0, 0)),
        pl.BlockSpec(lm.shape, lambda b: (0, 0, 0, 0)),
        pl.BlockSpec(dm.shape, lambda b: (0, 0, 0, 0)),
    ]
    args = [z_hg, lb, tri, g256, lm, dm]
    out_specs = [pl.BlockSpec((seq_len, PACK_W), row)]
    out_shape = [jax.ShapeDtypeStruct((m, PACK_W), BF16)]
    if latent:
        in_specs += [pl.BlockSpec((1, PACK_W, PACK_W), lambda b: (b, 0, 0))] * 2
        args += [states[0], states[1]]
    else:
        out_specs += [pl.BlockSpec((1, HEADS, DH, DH), lambda b: (b, 0, 0, 0))] * 2
        out_shape += [jax.ShapeDtypeStruct((batch, HEADS, DH, DH), F32)] * 2
    return pl.pallas_call(
        functools.partial(_hgrn_kernel, seq_len=seq_len, latent=latent),
        grid=(batch,),
        in_specs=in_specs,
        out_specs=out_specs,
        out_shape=out_shape,
        scratch_shapes=[
            pltpu.VMEM((2, seq_len, PACK_W), BF16),
            pltpu.VMEM((2, seq_len, PACK_W), BF16),
            pltpu.VMEM((2, seq_len // HG_C * SUBLANES, PACK_W), F32),
            pltpu.VMEM((2, seq_len, PACK_W), F32),
            pltpu.VMEM((2, seq_len, PACK_W), F32),
        ],
        compiler_params=_params(1),
        name="hgrn_latent" if latent else "hgrn_ctx",
    )(*args)


def _block_ones(n):
    idx = np.arange(n) // DH
    return jnp.asarray((idx[:, None] == idx[None, :]).astype(np.float32), dtype=BF16)


def _hgrn_consts():
    C = HG_C
    t = np.arange(ROW_BLK)
    same_chunk = (t[:, None] // C) == (t[None, :] // C)
    tri_f = same_chunk & (t[None, :] <= t[:, None])
    tri_b = same_chunk & (t[None, :] >= t[:, None])
    tc = np.arange(C)
    lms = []
    for fwd in (True, False):
        per_level = []
        for m in HG_LEVELS:
            same = (tc[:, None] // (2 * m)) == (tc[None, :] // (2 * m))
            upper = (tc % (2 * m)) >= m
            later, earlier = (upper, ~upper) if fwd else (~upper, upper)
            per_level.append(np.tile(same & later[:, None] & earlier[None, :], (1, HEADS)))
        lms.append(np.stack(per_level))
    dms = [[np.tile(tc[None, :] == tc[:, None] + sign * dist, (ROW_BLK // C, HEADS))
            for dist in range(HG_SUB)] for sign in (-1, 1)]
    return (jnp.asarray(np.stack([tri_f, tri_b]).astype(np.float32), dtype=BF16),
            _block_ones(PACK_W),
            jnp.asarray(np.stack(lms).astype(np.float32), dtype=BF16),
            jnp.asarray(np.stack(dms).astype(np.float32), dtype=BF16))


def _rope_tables(seq_len, width):
    t = np.arange(seq_len)
    pos = np.stack([t // GRID_W, t % GRID_W], axis=1).astype(np.float32)
    j = np.arange(DH)
    axis = j // (2 * ROPE_PAIR)
    jj = j % (2 * ROPE_PAIR)
    inv = (ROPE_BASE ** (-(jj % ROPE_PAIR).astype(np.float64) / ROPE_PAIR)).astype(np.float32)
    ang = (pos[:, axis] * inv[None, :]).astype(np.float64)
    cos = np.cos(ang)
    sin = np.where(jj < ROPE_PAIR, -np.sin(ang), np.sin(ang))
    reps = width // DH
    return (jnp.asarray(np.tile(cos, (1, reps)), dtype=F32),
            jnp.asarray(np.tile(sin, (1, reps)), dtype=F32))


def _to_blockdiag_t(s):
    b = s.shape[0]
    eye = jnp.eye(HEADS, dtype=s.dtype)
    return jnp.einsum('bhde,hg->bhegd', s, eye).reshape(b, PACK_W, PACK_W)


def _layer(x2d, batch, seq_len, mod, l, p, latent, ctx, caches):
    shift, scale, gate = mod
    z_ret, z_diff, z_hg = _inproj(x2d, p['norm_g'][l], scale, shift, p['w_in'][l], seq_len)
    lam_init = 0.8 - 0.6 * math.exp(-0.3 * l)
    if latent:
        k_ctx, v_ctx, s_rf, s_rb, s_hf, s_hb = ctx
        r = _retention(z_ret, p['log_g'][l], p['g256'], batch, seq_len, True,
                       rope=p['rope256'], states=(s_rf, s_rb))[0]
        a = _diff_attention(z_diff, p['lam'][l], p['gq'][l], p['gk'][l], p['g128'], batch,
                            seq_len, True, lam_init, l, rope=p['rope512'],
                            ctx=(k_ctx, v_ctx))[0]
        hgo = _hgrn(z_hg, p['lb'][l], p['hg_consts'], batch, seq_len, True,
                    states=(s_hf, s_hb))[0]
        extras = None
    else:
        r, s_rf, s_rb = _retention(z_ret, p['log_g'][l], p['g256'], batch, seq_len, False)
        a, kcache, vcache = _diff_attention(z_diff, p['lam'][l], p['gq'][l], p['gk'][l],
                                            p['g128'], batch, seq_len, False, lam_init, l,
                                            caches=caches)
        hgo, s_hf, s_hb = _hgrn(z_hg, p['lb'][l], p['hg_consts'], batch, seq_len, False)
        extras = ((kcache, vcache), (s_rf, s_rb, s_hf, s_hb))
    y = _outproj(x2d, r, a, hgo, gate, p['w_out'][l], seq_len)
    return y, extras


def kernel(x_prompt, x_sample, c, c_ctx, cache_diff_k, cache_diff_v, state_ret_fwd,
           state_ret_bwd, state_hgrn_fwd, state_hgrn_bwd, norm_g, w_ada, b_ada, w_in,
           w_out, ret_decay_logit, diff_qn_g, diff_kn_g, diff_lambda, hgrn_lb_logit):
    batch, seq, _ = x_prompt.shape
    dec_batch, dec_seq, _ = x_sample.shape
    past_len = cache_diff_k.shape[2]

    cv = jnp.zeros((8, D_MODEL), F32).at[0].set(c_ctx).at[1:1 + dec_batch].set(c)
    mods = _ada(cv, w_ada, b_ada)

    lb_all = jax.nn.softmax(hgrn_lb_logit.astype(F32), axis=0)
    lb_all = jnp.cumsum(lb_all, axis=0) - lb_all[0]
    lp = diff_lambda.astype(F32)
    lam_inits = jnp.asarray([0.8 - 0.6 * math.exp(-0.3 * l) for l in range(DEPTH)], F32)
    lam = (jnp.exp(jnp.sum(lp[:, 0] * lp[:, 1], axis=-1))
           - jnp.exp(jnp.sum(lp[:, 2] * lp[:, 3], axis=-1)) + lam_inits)
    p = {
        'norm_g': norm_g.reshape(DEPTH, 1, D_MODEL),
        'w_in': w_in.astype(BF16),
        'w_out': w_out.astype(BF16),
        'log_g': jax.nn.log_sigmoid(ret_decay_logit.astype(F32)),
        'lam': lam.reshape(DEPTH, 1),
        'gq': jnp.tile(diff_qn_g, (1, DIFF_W // DH)).reshape(DEPTH, 1, DIFF_W),
        'gk': jnp.tile(diff_kn_g, (1, DIFF_W // DH)).reshape(DEPTH, 1, DIFF_W),
        'lb': lb_all.reshape(DEPTH, 1, PACK_W),
        'g256': _block_ones(PACK_W),
        'g128': _block_ones(LANES),
        'hg_consts': _hgrn_consts(),
        'rope256': _rope_tables(dec_seq, PACK_W),
        'rope512': _rope_tables(dec_seq, DIFF_W),
    }

    def split_mod(rows):
        return tuple(rows[:, None, j * D_MODEL:(j + 1) * D_MODEL] for j in range(3))

    y = x_prompt.reshape(batch * seq, D_MODEL)
    caches = None
    states = []
    for l in range(DEPTH):
        y, (caches, st) = _layer(y, batch, seq, split_mod(mods[l, 0:1]), l, p, False, None, caches)
        states.append(st)
    y_prompt = y.reshape(batch, seq, D_MODEL)
    new_k = caches[0].reshape(batch, DEPTH, seq, HEADS, 2, DH)
    new_v = caches[1].reshape(batch, DEPTH, seq, HEADS, DIFF_DV)
    new_states = [jnp.stack([st[i] for st in states], axis=1) for i in range(4)]

    y = x_sample.reshape(dec_batch * dec_seq, D_MODEL)
    for l in range(DEPTH):
        ctx = (cache_diff_k[:, l].reshape(dec_batch, past_len, DIFF_W),
               cache_diff_v[:, l].reshape(dec_batch, past_len, DIFF_W),
               _to_blockdiag_t(state_ret_fwd[:, l]), _to_blockdiag_t(state_ret_bwd[:, l]),
               _to_blockdiag_t(state_hgrn_fwd[:, l]), _to_blockdiag_t(state_hgrn_bwd[:, l]))
        y, _ = _layer(y, dec_batch, dec_seq, split_mod(mods[l, 1:1 + dec_batch]), l, p, True,
                      ctx, None)
    y_sample = y.reshape(dec_batch, dec_seq, D_MODEL)

    return (y_prompt, y_sample, new_k, new_v, *new_states)
```

```python
import functools
import math

import numpy as np
import jax
import jax.numpy as jnp
from jax import lax
from jax.experimental import pallas as pl
from jax.experimental.pallas import tpu as pltpu

F32 = jnp.float32
BF16 = jnp.bfloat16

D_MODEL = 1024
DEPTH = 2
GRID_W = 64
HEADS = 4
DH = 64
PACK_W = HEADS * DH
DIFF_W = 512
DIFF_DV = 128
RET_COLS = 4 * PACK_W
DIFF_COLS = 4 * DIFF_W
HG_COLS = 5 * PACK_W
IN_W = RET_COLS + DIFF_COLS + HG_COLS
MIX_W = PACK_W + DIFF_W + PACK_W
ROPE_BASE = 10000.0
ROPE_PAIR = 16
EPS = 1e-6
LOG2E = 1.4426950408889634
NEG_BIG = -1e30

RET_C = 256
HG_C = 64
HG_SUB = 4
HG_LEVELS = (4, 8, 16, 32)
SUBLANES = 8
LANES = 128
ROW_BLK = 256
TQ = 512
TM_IN = 1024
TM_OUT = 512

VMEM_LIMIT = 56 * 1024 * 1024


def _params(n_axes):
    return pltpu.CompilerParams(
        dimension_semantics=("arbitrary",) * n_axes, vmem_limit_bytes=VMEM_LIMIT)


def _dot(a, b):
    return jnp.dot(a, b, preferred_element_type=F32)


def _dot_nt(a, b):
    return lax.dot_general(a, b, (((1,), (1,)), ((), ())), preferred_element_type=F32)


def _dot_tn(a, b):
    return lax.dot_general(a, b, (((0,), (0,)), ((), ())), preferred_element_type=F32)


def _silu(x):
    return x * jax.nn.sigmoid(x)


def _group_rms(x, g, width):
    n, w = x.shape
    x2 = x * x
    cols = range(0, w, LANES)
    stacked = jnp.concatenate([x2[:, j:j + LANES] for j in cols], axis=0)
    hi = stacked.astype(BF16)
    lo = (stacked - hi.astype(F32)).astype(BF16)
    sums = _dot(hi, g) + _dot(lo, g)
    ms = jnp.concatenate([sums[i * n:(i + 1) * n] for i in range(len(cols))], axis=1)
    return x * lax.rsqrt(ms * (1.0 / width) + EPS)


def _rope(x, cos, sin):
    w = x.shape[-1]
    lane = lax.broadcasted_iota(jnp.int32, (1, w), 1)
    first = (lane & (2 * ROPE_PAIR - 1)) < ROPE_PAIR
    swapped = jnp.where(first, pltpu.roll(x, w - ROPE_PAIR, 1), pltpu.roll(x, ROPE_PAIR, 1))
    return x * cos + swapped * sin


def _lane_head(w=PACK_W):
    return lax.broadcasted_iota(jnp.int32, (1, w), 1) // DH


def _row_blocks(n_rows, body):
    n_blk = n_rows // ROW_BLK
    if n_blk == 1:
        body(0)
    else:
        def step(i, carry):
            body(pl.multiple_of(i * ROW_BLK, ROW_BLK))
            return carry
        lax.fori_loop(0, n_blk, step, 0)


def _ada_kernel(cv_ref, w_ref, b_ref, o_ref):
    cv = cv_ref[...]
    o_ref[0, 0] = _dot(_silu(cv), w_ref[0]) + b_ref[0]


def _ada(cv, w_ada, b_ada):
    tn = 1024
    n = 3 * D_MODEL
    return pl.pallas_call(
        _ada_kernel,
        grid=(DEPTH, n // tn),
        in_specs=[
            pl.BlockSpec((8, D_MODEL), lambda l, j: (0, 0)),
            pl.BlockSpec((1, D_MODEL, tn), lambda l, j: (l, 0, j)),
            pl.BlockSpec((1, 1, tn), lambda l, j: (l, 0, j)),
        ],
        out_specs=pl.BlockSpec((1, 1, 8, tn), lambda l, j: (l, j, 0, 0)),
        out_shape=jax.ShapeDtypeStruct((DEPTH, n // tn, 8, tn), F32),
        compiler_params=_params(2),
        name="ada",
    )(cv, w_ada, b_ada.reshape(DEPTH, 1, n))


def _inproj_kernel(x_ref, g_ref, sc_ref, sh_ref, w_ref, zr_ref, zd_ref, zh_ref):
    x = x_ref[...]
    ms = jnp.mean(x * x, axis=-1, keepdims=True)
    h = x * lax.rsqrt(ms + EPS) * g_ref[...]
    h = h * (1.0 + sc_ref[0]) + sh_ref[0]
    z = _dot(h.astype(BF16), w_ref[...])
    zr_ref[...] = z[:, :RET_COLS].astype(zr_ref.dtype)
    zd_ref[...] = z[:, RET_COLS:RET_COLS + DIFF_COLS].astype(zd_ref.dtype)
    zh_ref[...] = z[:, RET_COLS + DIFF_COLS:]


MOD_SHIFT, MOD_SCALE, MOD_GATE = 0, 1, 2
MOD_ROWS = 8


def _mod_spec(layer, which, per_batch, tiles_per_seq):
    base = (layer * 3 + which) * MOD_ROWS
    if per_batch:
        return pl.BlockSpec((1, 1, D_MODEL), lambda i: (base + 1 + i // tiles_per_seq, 0, 0))
    return pl.BlockSpec((1, 1, D_MODEL), lambda i: (base, 0, 0))


def _inproj(x2d, g, mods, layer, per_batch, w_bf, seq_len):
    m = x2d.shape[0]
    tiles_per_seq = seq_len // TM_IN
    return pl.pallas_call(
        _inproj_kernel,
        grid=(m // TM_IN,),
        in_specs=[
            pl.BlockSpec((TM_IN, D_MODEL), lambda i: (i, 0)),
            pl.BlockSpec((1, D_MODEL), lambda i: (0, 0)),
            _mod_spec(layer, MOD_SCALE, per_batch, tiles_per_seq),
            _mod_spec(layer, MOD_SHIFT, per_batch, tiles_per_seq),
            pl.BlockSpec((D_MODEL, IN_W), lambda i: (0, 0)),
        ],
        out_specs=[
            pl.BlockSpec((TM_IN, RET_COLS), lambda i: (i, 0)),
            pl.BlockSpec((TM_IN, DIFF_COLS), lambda i: (i, 0)),
            pl.BlockSpec((TM_IN, HG_COLS), lambda i: (i, 0)),
        ],
        out_shape=[
            jax.ShapeDtypeStruct((m, RET_COLS), BF16),
            jax.ShapeDtypeStruct((m, DIFF_COLS), BF16),
            jax.ShapeDtypeStruct((m, HG_COLS), F32),
        ],
        compiler_params=_params(1),
        name="inproj",
    )(x2d, g, mods, mods, w_bf)


def _outproj_kernel(x_ref, r_ref, d_ref, h_ref, gate_ref, w_ref, o_ref):
    mixed = jnp.concatenate([r_ref[...], d_ref[...], h_ref[...]], axis=1)
    o_ref[...] = x_ref[...] + gate_ref[0] * _dot(mixed, w_ref[...])


def _outproj(x2d, r, d, h, mods, layer, per_batch, w_bf, seq_len):
    m = x2d.shape[0]
    tiles_per_seq = max(seq_len // TM_OUT, 1)
    tm = TM_OUT if (not per_batch or seq_len % TM_OUT == 0) else seq_len
    return pl.pallas_call(
        _outproj_kernel,
        grid=(m // tm,),
        in_specs=[
            pl.BlockSpec((tm, D_MODEL), lambda i: (i, 0)),
            pl.BlockSpec((tm, PACK_W), lambda i: (i, 0)),
            pl.BlockSpec((tm, DIFF_W), lambda i: (i, 0)),
            pl.BlockSpec((tm, PACK_W), lambda i: (i, 0)),
            _mod_spec(layer, MOD_GATE, per_batch, tiles_per_seq),
            pl.BlockSpec((MIX_W, D_MODEL), lambda i: (0, 0)),
        ],
        out_specs=pl.BlockSpec((tm, D_MODEL), lambda i: (i, 0)),
        out_shape=jax.ShapeDtypeStruct((m, D_MODEL), F32),
        compiler_params=_params(1),
        name="outproj",
    )(x2d, r, d, h, mods, w_bf)


def _ret_kernel(*refs, seq_len, latent, layer):
    if latent:
        (lg_ref, z_ref, g_ref, cos_ref, sin_ref, s0f_ref, s0b_ref,
         out_ref, mask_scr, o_scr) = refs
    else:
        lg_ref, z_ref, g_ref = refs[:3]
        prev_refs = refs[3:-5]
        out_ref, sf_ref, sb_ref, mask_scr, o_scr = refs[-5:]
    C = RET_C
    n_chunks = seq_len // C
    lh = _lane_head()

    def lane_vec(direction):
        v = jnp.zeros((1, PACK_W), F32)
        for h in range(HEADS):
            v = jnp.where(lh == h, lg_ref[direction, h], v)
        return v

    lgf, lgb = lane_vec(0), lane_vec(1)
    tau = lax.broadcasted_iota(jnp.int32, (C, 1), 0).astype(F32)
    qdec_f = jnp.exp((tau + 1.0) * lgf)
    qdec_b = jnp.exp((C - tau) * lgb)
    kdec_f = jnp.exp((C - 1.0 - tau) * lgf)
    kdec_b = jnp.exp(tau * lgb)
    sdec_f = jnp.exp(float(C) * lgf)
    sdec_b = jnp.exp(float(C) * lgb)
    gmask = g_ref[...].astype(F32)

    @pl.when(pl.program_id(0) == 0)
    def _():
        dd = (lax.broadcasted_iota(jnp.int32, (C, C), 0)
              - lax.broadcasted_iota(jnp.int32, (C, C), 1)).astype(F32)
        for h in range(HEADS):
            mf = jnp.where(dd >= 0, jnp.exp(jnp.maximum(dd, 0.0) * lg_ref[0, h]), 0.0)
            mb = jnp.where(dd <= 0, jnp.exp(jnp.maximum(-dd, 0.0) * lg_ref[1, h]), 0.0)
            mask_scr[h * C:(h + 1) * C, :] = mf + mb

    def load(r0):
        q = z_ref[pl.ds(r0, C), 0:PACK_W].astype(F32)
        k = z_ref[pl.ds(r0, C), PACK_W:2 * PACK_W].astype(F32) * (DH ** -0.5)
        v = z_ref[pl.ds(r0, C), 2 * PACK_W:3 * PACK_W].astype(F32)
        if latent:
            cos = cos_ref[pl.ds(r0, C), :]
            sin = sin_ref[pl.ds(r0, C), :]
            q = _rope(q, cos, sin)
            k = _rope(k, cos, sin)
        return q, k, v

    def intra(q, k, v):
        qb = q.astype(BF16)
        q4 = jnp.concatenate([jnp.where(lh == h, qb, jnp.zeros_like(qb)) for h in range(HEADS)],
                             axis=0)
        a4 = _dot_nt(q4, k.astype(BF16)) * mask_scr[...]
        o4 = _dot(a4.astype(BF16), v.astype(BF16))
        o = jnp.zeros((C, PACK_W), F32)
        for h in range(HEADS):
            o = o + jnp.where(lh == h, o4[h * C:(h + 1) * C], 0.0)
        return o

    def finish(o, r0):
        gate = z_ref[pl.ds(r0, C), 3 * PACK_W:4 * PACK_W].astype(F32)
        y = _group_rms(o, g_ref[0:LANES, 0:LANES], DH) * _silu(gate)
        out_ref[pl.ds(r0, C), :] = y.astype(out_ref.dtype)

    def state_step(st, k, v, kdec, sdec):
        upd = _dot_tn(v.astype(BF16), (k * kdec).astype(BF16))
        return (st * sdec + upd) * gmask

    if not latent:
        q, k, v = load(0)
        finish(intra(q, k, v), 0)
        vb = v.astype(BF16)
        for ref, kdec in ((sf_ref, kdec_f), (sb_ref, kdec_b)):
            s = _dot_tn((k * kdec).astype(BF16), vb) * gmask
            for h in range(HEADS):
                ref[0, layer, h] = s[h * DH:(h + 1) * DH, h * DH:(h + 1) * DH]
        for ref, prev in zip((sf_ref, sb_ref), prev_refs):
            ref[0, 0:layer] = prev[0]
        return

    def bwd_body(i, st):
        r0 = pl.multiple_of((n_chunks - 1 - i) * C, C)
        q, k, v = load(r0)
        o_scr[pl.ds(r0, C), :] = _dot_nt((q * qdec_b).astype(BF16), st.astype(BF16))
        return state_step(st, k, v, kdec_b, sdec_b)

    lax.fori_loop(0, n_chunks, bwd_body, s0b_ref[0], unroll=2)

    def fwd_body(i, st):
        r0 = pl.multiple_of(i * C, C)
        q, k, v = load(r0)
        o = intra(q, k, v) + o_scr[pl.ds(r0, C), :]
        o = o + _dot_nt((q * qdec_f).astype(BF16), st.astype(BF16))
        finish(o, r0)
        return state_step(st, k, v, kdec_f, sdec_f)

    lax.fori_loop(0, n_chunks, fwd_body, s0f_ref[0], unroll=2)


def _state_specs(batch, layer, prev):
    spec = lambda n: pl.BlockSpec((1, n, HEADS, DH, DH), lambda b: (b, 0, 0, 0, 0))
    out_specs = [spec(layer + 1)] * 2
    out_shape = [jax.ShapeDtypeStruct((batch, layer + 1, HEADS, DH, DH), F32)] * 2
    in_specs = [spec(layer)] * 2 if layer > 0 else []
    return in_specs, (list(prev) if layer > 0 else []), out_specs, out_shape


def _retention(z_ret, log_g, g256, batch, seq_len, latent, layer, rope=None, states=None):
    m = batch * seq_len
    row = lambda b: (b, 0)
    const2 = lambda b: (0, 0)
    in_specs = [
        pl.BlockSpec(memory_space=pltpu.SMEM),
        pl.BlockSpec((seq_len, RET_COLS), row),
        pl.BlockSpec((PACK_W, PACK_W), const2),
    ]
    args = [log_g, z_ret, g256]
    out_specs = [pl.BlockSpec((seq_len, PACK_W), row)]
    out_shape = [jax.ShapeDtypeStruct((m, PACK_W), BF16)]
    if latent:
        state_spec = pl.BlockSpec((1, PACK_W, PACK_W), lambda b: (b, 0, 0))
        in_specs += [pl.BlockSpec((seq_len, PACK_W), const2)] * 2 + [state_spec] * 2
        args += [rope[0], rope[1], states[0], states[1]]
    else:
        s_in, s_args, s_out, s_shape = _state_specs(batch, layer, states)
        in_specs += s_in
        args += s_args
        out_specs += s_out
        out_shape += s_shape
    return pl.pallas_call(
        functools.partial(_ret_kernel, seq_len=seq_len, latent=latent, layer=layer),
        grid=(batch,),
        in_specs=in_specs,
        out_specs=out_specs,
        out_shape=out_shape,
        scratch_shapes=[
            pltpu.VMEM((HEADS * RET_C, RET_C), F32),
            pltpu.VMEM((seq_len, PACK_W), F32),
        ],
        compiler_params=_params(1),
        name="ret_latent" if latent else "ret_ctx",
    )(*args)


def _attn_kernel(*refs, seq_len, past_len, latent, lam_init, layer):
    if latent:
        (lam_ref, q_ref, k_ref, v_ref, gate_ref, gq_ref, gk_ref, g_ref,
         cosq_ref, sinq_ref, cosk_ref, sink_ref, kc_ref, vc_ref,
         out_ref, k_scr, v_scr) = refs
        prev_refs = ()
    else:
        (lam_ref, q_ref, k_ref, v_ref, gate_ref, gq_ref, gk_ref, g_ref) = refs[:8]
        prev_refs = refs[8:-5]
        out_ref, kn_ref, vn_ref, k_scr, v_scr = refs[-5:]
    ones = jnp.ones((ROW_BLK, DIFF_DV), BF16)

    def put_values(sl, v):
        for h in range(HEADS):
            v_scr[sl, 2 * h * DIFF_DV:(2 * h + 1) * DIFF_DV] = (
                v[:, h * DIFF_DV:(h + 1) * DIFF_DV].astype(BF16))
            v_scr[sl, (2 * h + 1) * DIFF_DV:(2 * h + 2) * DIFF_DV] = ones

    @pl.when(pl.program_id(1) == 0)
    def _():
        for r in range(seq_len // ROW_BLK):
            sl = pl.ds(r * ROW_BLK, ROW_BLK)
            kn = _group_rms(k_ref[sl, :].astype(F32), g_ref[...], DH) * gk_ref[...]
            v = v_ref[sl, :].astype(F32)
            if latent:
                kn = _rope(kn, cosk_ref[sl, :], sink_ref[sl, :])
            else:
                kn_ref[0, layer, sl, :] = kn
                vn_ref[0, layer, sl, :] = v
            k_scr[sl, :] = kn.astype(BF16)
            put_values(sl, v)
        if prev_refs:
            kn_ref[0, 0:layer] = prev_refs[0][0]
            vn_ref[0, 0:layer] = prev_refs[1][0]
        if latent:
            for r in range(past_len // ROW_BLK):
                src = pl.ds(r * ROW_BLK, ROW_BLK)
                dst = pl.ds(seq_len + r * ROW_BLK, ROW_BLK)
                k_scr[dst, :] = kc_ref[0, src, :].astype(BF16)
                put_values(dst, vc_ref[0, src, :])

    lam = lam_ref[0]
    qn = _group_rms(q_ref[...].astype(F32), g_ref[...], DH) * gq_ref[...]
    if latent:
        qn = _rope(qn, cosq_ref[...], sinq_ref[...])
    qn = qn * (DH ** -0.5 * LOG2E)
    lane = lax.broadcasted_iota(jnp.int32, (1, 2 * DH), 1)
    for h in range(HEADS):
        hs = slice(h * 2 * DH, (h + 1) * 2 * DH)
        vx = v_scr[:, 2 * h * DIFF_DV:(2 * h + 2) * DIFF_DV]
        qh = qn[:, hs]
        kh = k_scr[:, hs]
        parts = []
        for first_map in (True, False):
            qm = jnp.where((lane < DH) == first_map, qh, 0.0).astype(BF16)
            s = _dot_nt(qm, kh)
            e = jnp.exp2(s - jnp.max(s, axis=-1, keepdims=True)).astype(BF16)
            ox = _dot(e, vx)
            parts.append(ox[:, :DIFF_DV] / ox[:, DIFF_DV:])
        o = parts[0] - lam * parts[1]
        ms = jnp.mean(o * o, axis=-1, keepdims=True)
        y = o * lax.rsqrt(ms + EPS) * (1.0 - lam_init) * _silu(gate_ref[:, hs].astype(F32))
        out_ref[:, hs] = y.astype(out_ref.dtype)


def _diff_attention(z_diff, lam, gq, gk, g128, batch, seq_len, latent, lam_init, layer,
                    rope=None, ctx=None, caches=None):
    m = batch * seq_len
    tq = min(TQ, seq_len)
    nq = seq_len // tq
    past_len = ctx[0].shape[1] if latent else 0
    qmap = lambda col: (lambda b, i: (b * nq + i, col))
    kvmap = lambda col: (lambda b, i: (b, col))
    const2 = lambda b, i: (0, 0)
    in_specs = [
        pl.BlockSpec(memory_space=pltpu.SMEM),
        pl.BlockSpec((tq, DIFF_W), qmap(0)),
        pl.BlockSpec((seq_len, DIFF_W), kvmap(1)),
        pl.BlockSpec((seq_len, DIFF_W), kvmap(2)),
        pl.BlockSpec((tq, DIFF_W), qmap(3)),
        pl.BlockSpec((1, DIFF_W), const2),
        pl.BlockSpec((1, DIFF_W), const2),
        pl.BlockSpec((LANES, LANES), const2),
    ]
    args = [lam, z_diff, z_diff, z_diff, z_diff, gq, gk, g128]
    out_specs = [pl.BlockSpec((tq, DIFF_W), qmap(0))]
    out_shape = [jax.ShapeDtypeStruct((m, DIFF_W), BF16)]
    if latent:
        in_specs += [
            pl.BlockSpec((tq, DIFF_W), lambda b, i: (i, 0)),
            pl.BlockSpec((tq, DIFF_W), lambda b, i: (i, 0)),
            pl.BlockSpec((seq_len, DIFF_W), const2),
            pl.BlockSpec((seq_len, DIFF_W), const2),
            pl.BlockSpec((1, past_len, DIFF_W), lambda b, i: (b, 0, 0)),
            pl.BlockSpec((1, past_len, DIFF_W), lambda b, i: (b, 0, 0)),
        ]
        args += [rope[0], rope[1], rope[0], rope[1], ctx[0], ctx[1]]
    else:
        cache_spec = lambda n: pl.BlockSpec((1, n, seq_len, DIFF_W), lambda b, i: (b, 0, 0, 0))
        out_specs += [cache_spec(layer + 1)] * 2
        out_shape += [jax.ShapeDtypeStruct((batch, layer + 1, seq_len, DIFF_W), F32)] * 2
        if layer > 0:
            in_specs += [cache_spec(layer)] * 2
            args += list(caches)
    return pl.pallas_call(
        functools.partial(_attn_kernel, seq_len=seq_len, past_len=past_len,
                          latent=latent, lam_init=lam_init, layer=layer),
        grid=(batch, nq),
        in_specs=in_specs,
        out_specs=out_specs,
        out_shape=out_shape,
        scratch_shapes=[
            pltpu.VMEM((seq_len + past_len, DIFF_W), BF16),
            pltpu.VMEM((seq_len + past_len, 2 * DIFF_W), BF16),
        ],
        compiler_params=_params(2),
        name="attn_latent" if latent else "attn_ctx",
    )(*args)


def _hgrn_kernel(*refs, seq_len, latent, layer):
    if latent:
        (z_ref, lb_ref, tri_ref, g_ref, lm_ref, dm_ref, s0f_ref, s0b_ref,
         out_ref, qi_scr, ks_scr, dec_scr, o_scr, oi_scr) = refs
    else:
        z_ref, lb_ref, tri_ref, g_ref, lm_ref, dm_ref = refs[:6]
        prev_refs = refs[6:-8]
        out_ref, sf_ref, sb_ref, qi_scr, ks_scr, dec_scr, o_scr, oi_scr = refs[-8:]
    C, SUB = HG_C, HG_SUB
    n_chunks = seq_len // C
    chunks_per_blk = ROW_BLK // C
    lh = _lane_head()
    sub_row = lax.broadcasted_iota(jnp.int32, (ROW_BLK, 1), 0) & (SUB - 1)
    chunk_row = lax.broadcasted_iota(jnp.int32, (C, 1), 0)

    def tile_roll(x, shift):
        return jnp.concatenate(
            [pltpu.roll(x[i:i + SUBLANES], shift, 0) for i in range(0, ROW_BLK, SUBLANES)], axis=0)

    def head_rep(x):
        return jnp.concatenate([jnp.where(lh == h, x, jnp.zeros_like(x)) for h in range(HEADS)],
                               axis=0)

    def intra(blk):
        start = blk * ROW_BLK if isinstance(blk, int) else pl.multiple_of(blk * ROW_BLK, ROW_BLK)
        rows = pl.ds(start, ROW_BLK)
        lb = lb_ref[...]
        q = z_ref[rows, 0:PACK_W] * (DH ** -0.5)
        v = z_ref[rows, 3 * PACK_W:4 * PACK_W]
        v_reps = [head_rep(v[c * C:(c + 1) * C].astype(BF16)) for c in range(chunks_per_blk)]
        kks, bcs, a_nears = [], [], []
        for d in range(2):
            f = lb + (1.0 - lb) * jax.nn.sigmoid(z_ref[rows, (1 + d) * PACK_W:(2 + d) * PACK_W])
            kks.append(1.0 - f)
            l2 = jnp.log(f) * LOG2E
            hi = l2.astype(BF16)
            r1 = l2 - hi.astype(F32)
            mid = r1.astype(BF16)
            lo = (r1 - mid.astype(F32)).astype(BF16)
            cs = _dot(tri_ref[d], jnp.concatenate([hi, mid, lo], axis=1))
            bcs.append(cs[:, 0:PACK_W] + cs[:, PACK_W:2 * PACK_W] + cs[:, 2 * PACK_W:])

        for d in range(2):
            bwd = d == 1
            kk, bc = kks[d], bcs[d]
            ps = [(q * kk).astype(BF16)]
            for dist in range(1, SUB):
                shift = SUBLANES - dist if bwd else dist
                valid = (sub_row < SUB - dist) if bwd else (sub_row >= dist)
                arg = jnp.where(valid, bc - tile_roll(bc, shift), NEG_BIG)
                ps.append((q * tile_roll(kk, shift) * jnp.exp2(arg)).astype(BF16))
            r = _dot(jnp.concatenate(ps, axis=0), g_ref[...]).astype(BF16)
            a_near = r[0:ROW_BLK] * dm_ref[d, 0]
            for dist in range(1, SUB):
                a_near = a_near + r[dist * ROW_BLK:(dist + 1) * ROW_BLK] * dm_ref[d, dist]
            a_nears.append(a_near)

        for c in range(chunks_per_blk):
            for d in range(2):
                bwd = d == 1
                kk, bc, a_near = kks[d], bcs[d], a_nears[d]
                cs_ = slice(c * C, (c + 1) * C)
                qc, kc, bcc = q[cs_], kk[cs_], bc[cs_]
                last = bcc[0:1] if bwd else bcc[C - 1:C]
                a = a_near[cs_]
                for li, m in enumerate(HG_LEVELS):
                    pieces = []
                    for b2 in range(C // (2 * m)):
                        rr = 2 * m * b2 + (m if bwd else m - 1)
                        pieces.append(jnp.broadcast_to(bcc[rr:rr + 1], (2 * m, PACK_W)))
                    anchor = jnp.concatenate(pieces, axis=0) if len(pieces) > 1 else pieces[0]
                    upper = (chunk_row & (2 * m - 1)) >= m
                    later = jnp.logical_not(upper) if bwd else upper
                    e = jnp.exp2((bcc - anchor) * jnp.where(later, 1.0, -1.0))
                    a = a + _dot_nt((qc * e).astype(BF16),
                                    head_rep((kc * e).astype(BF16))).astype(BF16) * lm_ref[d, li]
                crow = pl.ds(start + c * C, C)
                o_scr[d, crow, :] = _dot(a, v_reps[c])
                qi_scr[d, crow, :] = (qc * jnp.exp2(bcc)).astype(BF16)
                ks_scr[d, crow, :] = (kc * jnp.exp2(last - bcc)).astype(BF16)
                slot = (blk * chunks_per_blk + c) * SUBLANES
                if not isinstance(slot, int):
                    slot = pl.multiple_of(slot, SUBLANES)
                dec_scr[d, pl.ds(slot, SUBLANES), :] = jnp.broadcast_to(
                    jnp.exp2(last), (SUBLANES, PACK_W))

        if not latent:
            vb = v.astype(BF16)
            for d, ref in enumerate((sf_ref, sb_ref)):
                bwd = d == 1
                sweep = range(chunks_per_blk - 1, -1, -1) if bwd else range(chunks_per_blk)
                tail = jnp.zeros((1, PACK_W), F32)
                decayed = [None] * chunks_per_blk
                for c in reversed(sweep):
                    cs_ = slice(c * C, (c + 1) * C)
                    bcc = bcs[d][cs_]
                    total = tail + (bcc[0:1] if bwd else bcc[C - 1:C])
                    decayed[c] = (kks[d][cs_] * jnp.exp2(total - bcc)).astype(BF16)
                    tail = total
                s = _dot_tn(jnp.concatenate(decayed, axis=0), vb) * g_ref[...].astype(F32)
                for h in range(HEADS):
                    ref[0, layer, h] = s[h * DH:(h + 1) * DH, h * DH:(h + 1) * DH]
                if prev_refs:
                    ref[0, 0:layer] = prev_refs[d][0]

    n_blk = seq_len // ROW_BLK
    assert latent or n_blk == 1
    if n_blk == 1:
        intra(0)
    else:
        def intra_step(i, carry):
            intra(i)
            return carry
        lax.fori_loop(0, n_blk, intra_step, 0)

    if latent:
        states = (s0f_ref[0], s0b_ref[0])
    else:
        states = (jnp.zeros((PACK_W, PACK_W), F32),) * 2
    gmask = g_ref[...].astype(F32)

    def body(i, states):
        new_states = []
        for d in range(2):
            c = (n_chunks - 1 - i) if d == 1 else i
            rows = pl.ds(pl.multiple_of(c * C, C), C)
            st = states[d]
            oi_scr[d, rows, :] = _dot_nt(qi_scr[d, rows, :], st.astype(BF16))
            dec = dec_scr[d, pl.ds(pl.multiple_of(c * SUBLANES, SUBLANES), 1), :]
            vb = z_ref[rows, 3 * PACK_W:4 * PACK_W].astype(BF16)
            new_states.append(st * dec + _dot_tn(vb, ks_scr[d, rows, :]) * gmask)
        return tuple(new_states)

    lax.fori_loop(0, n_chunks, body, states, unroll=4)

    def finish(start):
        rows = pl.ds(start, ROW_BLK)
        tot = (o_scr[0, rows, :] + oi_scr[0, rows, :]) + (o_scr[1, rows, :] + oi_scr[1, rows, :])
        gate = z_ref[rows, 4 * PACK_W:5 * PACK_W]
        y = _group_rms(tot, g_ref[0:LANES, 0:LANES], DH) * _silu(gate)
        out_ref[rows, :] = y.astype(out_ref.dtype)

    _row_blocks(seq_len, finish)


def _hgrn(z_hg, lb, consts, batch, seq_len, latent, layer, states=None):
    m = batch * seq_len
    tri, g256, lm, dm = consts
    row = lambda b: (b, 0)
    in_specs = [
        pl.BlockSpec((seq_len, HG_COLS), row),
        pl.BlockSpec((1, PACK_W), lambda b: (0, 0)),
        pl.BlockSpec(tri.shape, lambda b: (0, 0, 0)),
        pl.BlockSpec((PACK_W, PACK_W), lambda b: (0, 0)),
        pl.BlockSpec(lm.shape, lambda b: (0, 0, 0, 0)),
        pl.BlockSpec(dm.shape, lambda b: (0, 0, 0, 0)),
    ]
    args = [z_hg, lb, tri, g256, lm, dm]
    out_specs = [pl.BlockSpec((seq_len, PACK_W), row)]
    out_shape = [jax.ShapeDtypeStruct((m, PACK_W), BF16)]
    if latent:
        in_specs += [pl.BlockSpec((1, PACK_W, PACK_W), lambda b: (b, 0, 0))] * 2
        args += [states[0], states[1]]
    else:
        s_in, s_args, s_out, s_shape = _state_specs(batch, layer, states)
        in_specs += s_in
        args += s_args
        out_specs += s_out
        out_shape += s_shape
    return pl.pallas_call(
        functools.partial(_hgrn_kernel, seq_len=seq_len, latent=latent, layer=layer),
        grid=(batch,),
        in_specs=in_specs,
        out_specs=out_specs,
        out_shape=out_shape,
        scratch_shapes=[
            pltpu.VMEM((2, seq_len, PACK_W), BF16),
            pltpu.VMEM((2, seq_len, PACK_W), BF16),
            pltpu.VMEM((2, seq_len // HG_C * SUBLANES, PACK_W), F32),
            pltpu.VMEM((2, seq_len, PACK_W), F32),
            pltpu.VMEM((2, seq_len, PACK_W), F32),
        ],
        compiler_params=_params(1),
        name="hgrn_latent" if latent else "hgrn_ctx",
    )(*args)


def _block_ones(n):
    idx = np.arange(n) // DH
    return jnp.asarray((idx[:, None] == idx[None, :]).astype(np.float32), dtype=BF16)


def _hgrn_consts():
    C = HG_C
    t = np.arange(ROW_BLK)
    same_chunk = (t[:, None] // C) == (t[None, :] // C)
    tri_f = same_chunk & (t[None, :] <= t[:, None])
    tri_b = same_chunk & (t[None, :] >= t[:, None])
    tc = np.arange(C)
    lms = []
    for fwd in (True, False):
        per_level = []
        for m in HG_LEVELS:
            same = (tc[:, None] // (2 * m)) == (tc[None, :] // (2 * m))
            upper = (tc % (2 * m)) >= m
            later, earlier = (upper, ~upper) if fwd else (~upper, upper)
            per_level.append(np.tile(same & later[:, None] & earlier[None, :], (1, HEADS)))
        lms.append(np.stack(per_level))
    dms = [[np.tile(tc[None, :] == tc[:, None] + sign * dist, (ROW_BLK // C, HEADS))
            for dist in range(HG_SUB)] for sign in (-1, 1)]
    return (jnp.asarray(np.stack([tri_f, tri_b]).astype(np.float32), dtype=BF16),
            _block_ones(PACK_W),
            jnp.asarray(np.stack(lms).astype(np.float32), dtype=BF16),
            jnp.asarray(np.stack(dms).astype(np.float32), dtype=BF16))


def _rope_tables(seq_len, width):
    t = np.arange(seq_len)
    pos = np.stack([t // GRID_W, t % GRID_W], axis=1).astype(np.float32)
    j = np.arange(DH)
    axis = j // (2 * ROPE_PAIR)
    jj = j % (2 * ROPE_PAIR)
    inv = (ROPE_BASE ** (-(jj % ROPE_PAIR).astype(np.float64) / ROPE_PAIR)).astype(np.float32)
    ang = (pos[:, axis] * inv[None, :]).astype(np.float64)
    cos = np.cos(ang)
    sin = np.where(jj < ROPE_PAIR, -np.sin(ang), np.sin(ang))
    reps = width // DH
    return (jnp.asarray(np.tile(cos, (1, reps)), dtype=F32),
            jnp.asarray(np.tile(sin, (1, reps)), dtype=F32))


def _to_blockdiag_t(s):
    b = s.shape[0]
    eye = jnp.eye(HEADS, dtype=s.dtype)
    return jnp.einsum('bhde,hg->bhegd', s, eye).reshape(b, PACK_W, PACK_W)


def _layer(x2d, batch, seq_len, mods, l, p, latent, ctx, caches, states=None):
    z_ret, z_diff, z_hg = _inproj(x2d, p['norm_g'][l], mods, l, latent, p['w_in'][l], seq_len)
    lam_init = 0.8 - 0.6 * math.exp(-0.3 * l)
    if latent:
        k_ctx, v_ctx, s_rf, s_rb, s_hf, s_hb = ctx
        r = _retention(z_ret, p['log_g'][l], p['g256'], batch, seq_len, True, l,
                       rope=p['rope256'], states=(s_rf, s_rb))[0]
        a = _diff_attention(z_diff, p['lam'][l], p['gq'][l], p['gk'][l], p['g128'], batch,
                            seq_len, True, lam_init, l, rope=p['rope512'],
                            ctx=(k_ctx, v_ctx))[0]
        hgo = _hgrn(z_hg, p['lb'][l], p['hg_consts'], batch, seq_len, True, l,
                    states=(s_hf, s_hb))[0]
        extras = None
    else:
        prev = states if states is not None else (None,) * 4
        r, s_rf, s_rb = _retention(z_ret, p['log_g'][l], p['g256'], batch, seq_len, False, l,
                                   states=prev[0:2])
        a, kcache, vcache = _diff_attention(z_diff, p['lam'][l], p['gq'][l], p['gk'][l],
                                            p['g128'], batch, seq_len, False, lam_init, l,
                                            caches=caches)
        hgo, s_hf, s_hb = _hgrn(z_hg, p['lb'][l], p['hg_consts'], batch, seq_len, False, l,
                                states=prev[2:4])
        extras = ((kcache, vcache), (s_rf, s_rb, s_hf, s_hb))
    y = _outproj(x2d, r, a, hgo, mods, l, latent, p['w_out'][l], seq_len)
    return y, extras


def kernel(x_prompt, x_sample, c, c_ctx, cache_diff_k, cache_diff_v, state_ret_fwd,
           state_ret_bwd, state_hgrn_fwd, state_hgrn_bwd, norm_g, w_ada, b_ada, w_in,
           w_out, ret_decay_logit, diff_qn_g, diff_kn_g, diff_lambda, hgrn_lb_logit):
    batch, seq, _ = x_prompt.shape
    dec_batch, dec_seq, _ = x_sample.shape
    past_len = cache_diff_k.shape[2]

    assert 1 + dec_batch <= MOD_ROWS
    cv = jnp.zeros((MOD_ROWS, D_MODEL), F32).at[0].set(c_ctx).at[1:1 + dec_batch].set(c)
    mods = _ada(cv, w_ada, b_ada).reshape(DEPTH * 3 * MOD_ROWS, 1, D_MODEL)

    lb_all = jax.nn.softmax(hgrn_lb_logit.astype(F32), axis=0)
    lb_all = jnp.cumsum(lb_all, axis=0) - lb_all[0]
    lp = diff_lambda.astype(F32)
    lam_inits = jnp.asarray([0.8 - 0.6 * math.exp(-0.3 * l) for l in range(DEPTH)], F32)
    lam = (jnp.exp(jnp.sum(lp[:, 0] * lp[:, 1], axis=-1))
           - jnp.exp(jnp.sum(lp[:, 2] * lp[:, 3], axis=-1)) + lam_inits)
    p = {
        'norm_g': norm_g.reshape(DEPTH, 1, D_MODEL),
        'w_in': w_in.astype(BF16),
        'w_out': w_out.astype(BF16),
        'log_g': jax.nn.log_sigmoid(ret_decay_logit.astype(F32)),
        'lam': lam.reshape(DEPTH, 1),
        'gq': jnp.tile(diff_qn_g, (1, DIFF_W // DH)).reshape(DEPTH, 1, DIFF_W),
        'gk': jnp.tile(diff_kn_g, (1, DIFF_W // DH)).reshape(DEPTH, 1, DIFF_W),
        'lb': lb_all.reshape(DEPTH, 1, PACK_W),
        'g256': _block_ones(PACK_W),
        'g128': _block_ones(LANES),
        'hg_consts': _hgrn_consts(),
        'rope256': _rope_tables(dec_seq, PACK_W),
        'rope512': _rope_tables(dec_seq, DIFF_W),
    }

    y = x_prompt.reshape(batch * seq, D_MODEL)
    caches = None
    new_states = None
    for l in range(DEPTH):
        y, (caches, new_states) = _layer(y, batch, seq, mods, l, p, False, None, caches,
                                         new_states)
    y_prompt = y.reshape(batch, seq, D_MODEL)
    new_k = caches[0].reshape(batch, DEPTH, seq, HEADS, 2, DH)
    new_v = caches[1].reshape(batch, DEPTH, seq, HEADS, DIFF_DV)

    y = x_sample.reshape(dec_batch * dec_seq, D_MODEL)
    for l in range(DEPTH):
        ctx = (cache_diff_k[:, l].reshape(dec_batch, past_len, DIFF_W),
               cache_diff_v[:, l].reshape(dec_batch, past_len, DIFF_W),
               _to_blockdiag_t(state_ret_fwd[:, l]), _to_blockdiag_t(state_ret_bwd[:, l]),
               _to_blockdiag_t(state_hgrn_fwd[:, l]), _to_blockdiag_t(state_hgrn_bwd[:, l]))
        y, _ = _layer(y, dec_batch, dec_seq, mods, l, p, True, ctx, None)
    y_sample = y.reshape(dec_batch, dec_seq, D_MODEL)

    return (y_prompt, y_sample, new_k, new_v, *new_states)
```

```python
import functools
import math

import numpy as np
import jax
import jax.numpy as jnp
from jax import lax
from jax.experimental import pallas as pl
from jax.experimental.pallas import tpu as pltpu

F32 = jnp.float32
BF16 = jnp.bfloat16

D_MODEL = 1024
DEPTH = 2
GRID_W = 64
HEADS = 4
DH = 64
PACK_W = HEADS * DH
DIFF_W = 512
DIFF_DV = 128
RET_COLS = 4 * PACK_W
DIFF_COLS = 4 * DIFF_W
HG_COLS = 5 * PACK_W
IN_W = RET_COLS + DIFF_COLS + HG_COLS
MIX_W = PACK_W + DIFF_W + PACK_W
ROPE_BASE = 10000.0
ROPE_PAIR = 16
EPS = 1e-6
LOG2E = 1.4426950408889634
NEG_BIG = -1e30

RET_C = 256
HG_C = 64
HG_SUB = 4
HG_LEVELS = (4, 8, 16, 32)
SUBLANES = 8
LANES = 128
ROW_BLK = 256
TQ = 512
TM_IN = 1024
TM_OUT = 512

VMEM_LIMIT = 56 * 1024 * 1024


def _params(n_axes):
    return pltpu.CompilerParams(
        dimension_semantics=("arbitrary",) * n_axes, vmem_limit_bytes=VMEM_LIMIT)


def _dot(a, b):
    return jnp.dot(a, b, preferred_element_type=F32)


def _dot_nt(a, b):
    return lax.dot_general(a, b, (((1,), (1,)), ((), ())), preferred_element_type=F32)


def _dot_tn(a, b):
    return lax.dot_general(a, b, (((0,), (0,)), ((), ())), preferred_element_type=F32)


def _silu(x):
    return x * jax.nn.sigmoid(x)


def _group_rms(x, g, width):
    n, w = x.shape
    x2 = x * x
    cols = range(0, w, LANES)
    stacked = jnp.concatenate([x2[:, j:j + LANES] for j in cols], axis=0)
    hi = stacked.astype(BF16)
    lo = (stacked - hi.astype(F32)).astype(BF16)
    sums = _dot(hi, g) + _dot(lo, g)
    ms = jnp.concatenate([sums[i * n:(i + 1) * n] for i in range(len(cols))], axis=1)
    return x * lax.rsqrt(ms * (1.0 / width) + EPS)


def _rope(x, cos, sin):
    w = x.shape[-1]
    lane = lax.broadcasted_iota(jnp.int32, (1, w), 1)
    first = (lane & (2 * ROPE_PAIR - 1)) < ROPE_PAIR
    swapped = jnp.where(first, pltpu.roll(x, w - ROPE_PAIR, 1), pltpu.roll(x, ROPE_PAIR, 1))
    return x * cos + swapped * sin


def _initial_state(s_ref, gmask):
    s = s_ref[0, 0]
    return (jnp.concatenate([s] * HEADS, axis=1) * gmask).T


def _lane_head(w=PACK_W):
    return lax.broadcasted_iota(jnp.int32, (1, w), 1) // DH


def _row_blocks(n_rows, body):
    n_blk = n_rows // ROW_BLK
    if n_blk == 1:
        body(0)
    else:
        def step(i, carry):
            body(pl.multiple_of(i * ROW_BLK, ROW_BLK))
            return carry
        lax.fori_loop(0, n_blk, step, 0)


def _ada_kernel(cv_ref, w_ref, b_ref, o_ref):
    cv = cv_ref[...]
    o_ref[0, 0] = _dot(_silu(cv), w_ref[0]) + b_ref[0]


def _ada(cv, w_ada, b_ada):
    tn = 1024
    n = 3 * D_MODEL
    return pl.pallas_call(
        _ada_kernel,
        grid=(DEPTH, n // tn),
        in_specs=[
            pl.BlockSpec((8, D_MODEL), lambda l, j: (0, 0)),
            pl.BlockSpec((1, D_MODEL, tn), lambda l, j: (l, 0, j)),
            pl.BlockSpec((1, 1, tn), lambda l, j: (l, 0, j)),
        ],
        out_specs=pl.BlockSpec((1, 1, 8, tn), lambda l, j: (l, j, 0, 0)),
        out_shape=jax.ShapeDtypeStruct((DEPTH, n // tn, 8, tn), F32),
        compiler_params=_params(2),
        name="ada",
    )(cv, w_ada, b_ada.reshape(DEPTH, 1, n))


def _inproj_kernel(x_ref, g_ref, sc_ref, sh_ref, w_ref, zr_ref, zd_ref, zh_ref):
    x = x_ref[...]
    ms = jnp.mean(x * x, axis=-1, keepdims=True)
    h = x * lax.rsqrt(ms + EPS) * g_ref[...]
    h = h * (1.0 + sc_ref[0]) + sh_ref[0]
    z = _dot(h.astype(BF16), w_ref[...])
    zr_ref[...] = z[:, :RET_COLS].astype(zr_ref.dtype)
    zd_ref[...] = z[:, RET_COLS:RET_COLS + DIFF_COLS].astype(zd_ref.dtype)
    zh_ref[...] = z[:, RET_COLS + DIFF_COLS:]


MOD_SHIFT, MOD_SCALE, MOD_GATE = 0, 1, 2
MOD_ROWS = 8


def _mod_spec(layer, which, per_batch, tiles_per_seq):
    base = (layer * 3 + which) * MOD_ROWS
    if per_batch:
        return pl.BlockSpec((1, 1, D_MODEL), lambda i: (base + 1 + i // tiles_per_seq, 0, 0))
    return pl.BlockSpec((1, 1, D_MODEL), lambda i: (base, 0, 0))


def _inproj(x2d, g, mods, layer, per_batch, w_bf, seq_len):
    m = x2d.shape[0]
    tiles_per_seq = seq_len // TM_IN
    return pl.pallas_call(
        _inproj_kernel,
        grid=(m // TM_IN,),
        in_specs=[
            pl.BlockSpec((TM_IN, D_MODEL), lambda i: (i, 0)),
            pl.BlockSpec((1, D_MODEL), lambda i: (0, 0)),
            _mod_spec(layer, MOD_SCALE, per_batch, tiles_per_seq),
            _mod_spec(layer, MOD_SHIFT, per_batch, tiles_per_seq),
            pl.BlockSpec((D_MODEL, IN_W), lambda i: (0, 0)),
        ],
        out_specs=[
            pl.BlockSpec((TM_IN, RET_COLS), lambda i: (i, 0)),
            pl.BlockSpec((TM_IN, DIFF_COLS), lambda i: (i, 0)),
            pl.BlockSpec((TM_IN, HG_COLS), lambda i: (i, 0)),
        ],
        out_shape=[
            jax.ShapeDtypeStruct((m, RET_COLS), BF16),
            jax.ShapeDtypeStruct((m, DIFF_COLS), BF16),
            jax.ShapeDtypeStruct((m, HG_COLS), F32),
        ],
        compiler_params=_params(1),
        name="inproj",
    )(x2d, g, mods, mods, w_bf)


def _outproj_kernel(x_ref, r_ref, d_ref, h_ref, gate_ref, w_ref, o_ref):
    mixed = jnp.concatenate([r_ref[...], d_ref[...], h_ref[...]], axis=1)
    o_ref[...] = x_ref[...] + gate_ref[0] * _dot(mixed, w_ref[...])


def _outproj(x2d, r, d, h, mods, layer, per_batch, w_bf, seq_len):
    m = x2d.shape[0]
    tiles_per_seq = max(seq_len // TM_OUT, 1)
    tm = TM_OUT if (not per_batch or seq_len % TM_OUT == 0) else seq_len
    return pl.pallas_call(
        _outproj_kernel,
        grid=(m // tm,),
        in_specs=[
            pl.BlockSpec((tm, D_MODEL), lambda i: (i, 0)),
            pl.BlockSpec((tm, PACK_W), lambda i: (i, 0)),
            pl.BlockSpec((tm, DIFF_W), lambda i: (i, 0)),
            pl.BlockSpec((tm, PACK_W), lambda i: (i, 0)),
            _mod_spec(layer, MOD_GATE, per_batch, tiles_per_seq),
            pl.BlockSpec((MIX_W, D_MODEL), lambda i: (0, 0)),
        ],
        out_specs=pl.BlockSpec((tm, D_MODEL), lambda i: (i, 0)),
        out_shape=jax.ShapeDtypeStruct((m, D_MODEL), F32),
        compiler_params=_params(1),
        name="outproj",
    )(x2d, r, d, h, mods, w_bf)


def _ret_kernel(*refs, seq_len, latent, layer):
    if latent:
        (lg_ref, z_ref, g_ref, cos_ref, sin_ref, s0f_ref, s0b_ref,
         out_ref, mask_scr, o_scr) = refs
    else:
        lg_ref, z_ref, g_ref = refs[:3]
        prev_refs = refs[3:-5]
        out_ref, sf_ref, sb_ref, mask_scr, o_scr = refs[-5:]
    C = RET_C
    n_chunks = seq_len // C
    lh = _lane_head()

    def lane_vec(direction):
        v = jnp.zeros((1, PACK_W), F32)
        for h in range(HEADS):
            v = jnp.where(lh == h, lg_ref[direction, h], v)
        return v

    lgf, lgb = lane_vec(0), lane_vec(1)
    tau = lax.broadcasted_iota(jnp.int32, (C, 1), 0).astype(F32)
    qdec_f = jnp.exp((tau + 1.0) * lgf)
    qdec_b = jnp.exp((C - tau) * lgb)
    kdec_f = jnp.exp((C - 1.0 - tau) * lgf)
    kdec_b = jnp.exp(tau * lgb)
    sdec_f = jnp.exp(float(C) * lgf)
    sdec_b = jnp.exp(float(C) * lgb)
    gmask = g_ref[...].astype(F32)

    @pl.when(pl.program_id(0) == 0)
    def _():
        dd = (lax.broadcasted_iota(jnp.int32, (C, C), 0)
              - lax.broadcasted_iota(jnp.int32, (C, C), 1)).astype(F32)
        for h in range(HEADS):
            mf = jnp.where(dd >= 0, jnp.exp(jnp.maximum(dd, 0.0) * lg_ref[0, h]), 0.0)
            mb = jnp.where(dd <= 0, jnp.exp(jnp.maximum(-dd, 0.0) * lg_ref[1, h]), 0.0)
            mask_scr[h * C:(h + 1) * C, :] = mf + mb

    def load(r0):
        q = z_ref[pl.ds(r0, C), 0:PACK_W].astype(F32)
        k = z_ref[pl.ds(r0, C), PACK_W:2 * PACK_W].astype(F32) * (DH ** -0.5)
        v = z_ref[pl.ds(r0, C), 2 * PACK_W:3 * PACK_W].astype(F32)
        if latent:
            cos = cos_ref[pl.ds(r0, C), :]
            sin = sin_ref[pl.ds(r0, C), :]
            q = _rope(q, cos, sin)
            k = _rope(k, cos, sin)
        return q, k, v

    def intra(q, k, v):
        qb = q.astype(BF16)
        q4 = jnp.concatenate([jnp.where(lh == h, qb, jnp.zeros_like(qb)) for h in range(HEADS)],
                             axis=0)
        a4 = _dot_nt(q4, k.astype(BF16)) * mask_scr[...]
        o4 = _dot(a4.astype(BF16), v.astype(BF16))
        o = jnp.zeros((C, PACK_W), F32)
        for h in range(HEADS):
            o = o + jnp.where(lh == h, o4[h * C:(h + 1) * C], 0.0)
        return o

    def finish(o, r0):
        gate = z_ref[pl.ds(r0, C), 3 * PACK_W:4 * PACK_W].astype(F32)
        y = _group_rms(o, g_ref[0:LANES, 0:LANES], DH) * _silu(gate)
        out_ref[pl.ds(r0, C), :] = y.astype(out_ref.dtype)

    def state_step(st, k, v, kdec, sdec):
        upd = _dot_tn(v.astype(BF16), (k * kdec).astype(BF16))
        return (st * sdec + upd) * gmask

    if not latent:
        q, k, v = load(0)
        finish(intra(q, k, v), 0)
        vb = v.astype(BF16)
        for ref, kdec in ((sf_ref, kdec_f), (sb_ref, kdec_b)):
            s = _dot_tn((k * kdec).astype(BF16), vb) * gmask
            for h in range(HEADS):
                ref[0, layer, h] = s[h * DH:(h + 1) * DH, h * DH:(h + 1) * DH]
        for ref, prev in zip((sf_ref, sb_ref), prev_refs):
            ref[0, 0:layer] = prev[0]
        return

    def bwd_body(i, st):
        r0 = pl.multiple_of((n_chunks - 1 - i) * C, C)
        q, k, v = load(r0)
        o_scr[pl.ds(r0, C), :] = _dot_nt((q * qdec_b).astype(BF16), st.astype(BF16))
        return state_step(st, k, v, kdec_b, sdec_b)

    lax.fori_loop(0, n_chunks, bwd_body, _initial_state(s0b_ref, gmask), unroll=2)

    def fwd_body(i, st):
        r0 = pl.multiple_of(i * C, C)
        q, k, v = load(r0)
        o = intra(q, k, v) + o_scr[pl.ds(r0, C), :]
        o = o + _dot_nt((q * qdec_f).astype(BF16), st.astype(BF16))
        finish(o, r0)
        return state_step(st, k, v, kdec_f, sdec_f)

    lax.fori_loop(0, n_chunks, fwd_body, _initial_state(s0f_ref, gmask), unroll=2)


def _state_specs(batch, layer, prev):
    spec = lambda n: pl.BlockSpec((1, n, HEADS, DH, DH), lambda b: (b, 0, 0, 0, 0))
    out_specs = [spec(layer + 1)] * 2
    out_shape = [jax.ShapeDtypeStruct((batch, layer + 1, HEADS, DH, DH), F32)] * 2
    in_specs = [spec(layer)] * 2 if layer > 0 else []
    return in_specs, (list(prev) if layer > 0 else []), out_specs, out_shape


def _retention(z_ret, log_g, g256, batch, seq_len, latent, layer, rope=None, states=None):
    m = batch * seq_len
    row = lambda b: (b, 0)
    const2 = lambda b: (0, 0)
    in_specs = [
        pl.BlockSpec(memory_space=pltpu.SMEM),
        pl.BlockSpec((seq_len, RET_COLS), row),
        pl.BlockSpec((PACK_W, PACK_W), const2),
    ]
    args = [log_g, z_ret, g256]
    out_specs = [pl.BlockSpec((seq_len, PACK_W), row)]
    out_shape = [jax.ShapeDtypeStruct((m, PACK_W), BF16)]
    if latent:
        state_spec = pl.BlockSpec((1, 1, PACK_W, DH), lambda b: (b, layer, 0, 0))
        in_specs += [pl.BlockSpec((seq_len, PACK_W), const2)] * 2 + [state_spec] * 2
        args += [rope[0], rope[1], states[0], states[1]]
    else:
        s_in, s_args, s_out, s_shape = _state_specs(batch, layer, states)
        in_specs += s_in
        args += s_args
        out_specs += s_out
        out_shape += s_shape
    return pl.pallas_call(
        functools.partial(_ret_kernel, seq_len=seq_len, latent=latent, layer=layer),
        grid=(batch,),
        in_specs=in_specs,
        out_specs=out_specs,
        out_shape=out_shape,
        scratch_shapes=[
            pltpu.VMEM((HEADS * RET_C, RET_C), F32),
            pltpu.VMEM((seq_len, PACK_W), F32),
        ],
        compiler_params=_params(1),
        name="ret_latent" if latent else "ret_ctx",
    )(*args)


def _attn_kernel(*refs, seq_len, past_len, latent, lam_init, layer):
    if latent:
        (lam_ref, q_ref, k_ref, v_ref, gate_ref, gq_ref, gk_ref, g_ref,
         cosq_ref, sinq_ref, cosk_ref, sink_ref, kc_ref, vc_ref,
         out_ref, k_scr, v_scr) = refs
        prev_refs = ()
    else:
        (lam_ref, q_ref, k_ref, v_ref, gate_ref, gq_ref, gk_ref, g_ref) = refs[:8]
        prev_refs = refs[8:-5]
        out_ref, kn_ref, vn_ref, k_scr, v_scr = refs[-5:]
    ones = jnp.ones((ROW_BLK, DIFF_DV), BF16)

    def put_values(sl, v):
        for h in range(HEADS):
            v_scr[sl, 2 * h * DIFF_DV:(2 * h + 1) * DIFF_DV] = (
                v[:, h * DIFF_DV:(h + 1) * DIFF_DV].astype(BF16))
            v_scr[sl, (2 * h + 1) * DIFF_DV:(2 * h + 2) * DIFF_DV] = ones

    @pl.when(pl.program_id(1) == 0)
    def _():
        for r in range(seq_len // ROW_BLK):
            sl = pl.ds(r * ROW_BLK, ROW_BLK)
            kn = _group_rms(k_ref[sl, :].astype(F32), g_ref[...], DH) * gk_ref[...]
            v = v_ref[sl, :].astype(F32)
            if latent:
                kn = _rope(kn, cosk_ref[sl, :], sink_ref[sl, :])
            else:
                kn_ref[0, layer, sl, :] = kn
                vn_ref[0, layer, sl, :] = v
            k_scr[sl, :] = kn.astype(BF16)
            put_values(sl, v)
        if prev_refs:
            kn_ref[0, 0:layer] = prev_refs[0][0]
            vn_ref[0, 0:layer] = prev_refs[1][0]
        if latent:
            for r in range(past_len // ROW_BLK):
                src = pl.ds(r * ROW_BLK, ROW_BLK)
                dst = pl.ds(seq_len + r * ROW_BLK, ROW_BLK)
                k_scr[dst, :] = kc_ref[0, src, :].astype(BF16)
                put_values(dst, vc_ref[0, src, :])

    lam = lam_ref[0]
    qn = _group_rms(q_ref[...].astype(F32), g_ref[...], DH) * gq_ref[...]
    if latent:
        qn = _rope(qn, cosq_ref[...], sinq_ref[...])
    qn = qn * (DH ** -0.5 * LOG2E)
    lane = lax.broadcasted_iota(jnp.int32, (1, 2 * DH), 1)
    for h in range(HEADS):
        hs = slice(h * 2 * DH, (h + 1) * 2 * DH)
        vx = v_scr[:, 2 * h * DIFF_DV:(2 * h + 2) * DIFF_DV]
        qh = qn[:, hs]
        kh = k_scr[:, hs]
        parts = []
        for first_map in (True, False):
            qm = jnp.where((lane < DH) == first_map, qh, 0.0).astype(BF16)
            s = _dot_nt(qm, kh)
            e = jnp.exp2(s - jnp.max(s, axis=-1, keepdims=True)).astype(BF16)
            ox = _dot(e, vx)
            parts.append(ox[:, :DIFF_DV] / ox[:, DIFF_DV:])
        o = parts[0] - lam * parts[1]
        ms = jnp.mean(o * o, axis=-1, keepdims=True)
        y = o * lax.rsqrt(ms + EPS) * (1.0 - lam_init) * _silu(gate_ref[:, hs].astype(F32))
        out_ref[:, hs] = y.astype(out_ref.dtype)


def _diff_attention(z_diff, lam, gq, gk, g128, batch, seq_len, latent, lam_init, layer,
                    rope=None, ctx=None, caches=None):
    m = batch * seq_len
    tq = min(TQ, seq_len)
    nq = seq_len // tq
    past_len = ctx[0].shape[1] if latent else 0
    qmap = lambda col: (lambda b, i: (b * nq + i, col))
    kvmap = lambda col: (lambda b, i: (b, col))
    const2 = lambda b, i: (0, 0)
    in_specs = [
        pl.BlockSpec(memory_space=pltpu.SMEM),
        pl.BlockSpec((tq, DIFF_W), qmap(0)),
        pl.BlockSpec((seq_len, DIFF_W), kvmap(1)),
        pl.BlockSpec((seq_len, DIFF_W), kvmap(2)),
        pl.BlockSpec((tq, DIFF_W), qmap(3)),
        pl.BlockSpec((1, DIFF_W), const2),
        pl.BlockSpec((1, DIFF_W), const2),
        pl.BlockSpec((LANES, LANES), const2),
    ]
    args = [lam, z_diff, z_diff, z_diff, z_diff, gq, gk, g128]
    out_specs = [pl.BlockSpec((tq, DIFF_W), qmap(0))]
    out_shape = [jax.ShapeDtypeStruct((m, DIFF_W), BF16)]
    if latent:
        in_specs += [
            pl.BlockSpec((tq, DIFF_W), lambda b, i: (i, 0)),
            pl.BlockSpec((tq, DIFF_W), lambda b, i: (i, 0)),
            pl.BlockSpec((seq_len, DIFF_W), const2),
            pl.BlockSpec((seq_len, DIFF_W), const2),
            pl.BlockSpec((1, past_len, DIFF_W), lambda b, i: (b, 0, 0)),
            pl.BlockSpec((1, past_len, DIFF_W), lambda b, i: (b, 0, 0)),
        ]
        args += [rope[0], rope[1], rope[0], rope[1], ctx[0], ctx[1]]
    else:
        cache_spec = lambda n: pl.BlockSpec((1, n, seq_len, DIFF_W), lambda b, i: (b, 0, 0, 0))
        out_specs += [cache_spec(layer + 1)] * 2
        out_shape += [jax.ShapeDtypeStruct((batch, layer + 1, seq_len, DIFF_W), F32)] * 2
        if layer > 0:
            in_specs += [cache_spec(layer)] * 2
            args += list(caches)
    return pl.pallas_call(
        functools.partial(_attn_kernel, seq_len=seq_len, past_len=past_len,
                          latent=latent, lam_init=lam_init, layer=layer),
        grid=(batch, nq),
        in_specs=in_specs,
        out_specs=out_specs,
        out_shape=out_shape,
        scratch_shapes=[
            pltpu.VMEM((seq_len + past_len, DIFF_W), BF16),
            pltpu.VMEM((seq_len + past_len, 2 * DIFF_W), BF16),
        ],
        compiler_params=_params(2),
        name="attn_latent" if latent else "attn_ctx",
    )(*args)


def _hgrn_kernel(*refs, seq_len, latent, layer):
    if latent:
        (z_ref, lb_ref, tri_ref, g_ref, lm_ref, dm_ref, s0f_ref, s0b_ref,
         out_ref, qi_scr, ks_scr, dec_scr, o_scr, oi_scr) = refs
    else:
        z_ref, lb_ref, tri_ref, g_ref, lm_ref, dm_ref = refs[:6]
        prev_refs = refs[6:-8]
        out_ref, sf_ref, sb_ref, qi_scr, ks_scr, dec_scr, o_scr, oi_scr = refs[-8:]
    C, SUB = HG_C, HG_SUB
    n_chunks = seq_len // C
    chunks_per_blk = ROW_BLK // C
    lh = _lane_head()
    sub_row = lax.broadcasted_iota(jnp.int32, (ROW_BLK, 1), 0) & (SUB - 1)
    chunk_row = lax.broadcasted_iota(jnp.int32, (C, 1), 0)

    def tile_roll(x, shift):
        return jnp.concatenate(
            [pltpu.roll(x[i:i + SUBLANES], shift, 0) for i in range(0, ROW_BLK, SUBLANES)], axis=0)

    def head_rep(x):
        return jnp.concatenate([jnp.where(lh == h, x, jnp.zeros_like(x)) for h in range(HEADS)],
                               axis=0)

    def intra(blk):
        start = blk * ROW_BLK if isinstance(blk, int) else pl.multiple_of(blk * ROW_BLK, ROW_BLK)
        rows = pl.ds(start, ROW_BLK)
        lb = lb_ref[...]
        q = z_ref[rows, 0:PACK_W] * (DH ** -0.5)
        v = z_ref[rows, 3 * PACK_W:4 * PACK_W]
        v_reps = [head_rep(v[c * C:(c + 1) * C].astype(BF16)) for c in range(chunks_per_blk)]
        kks, bcs, a_nears = [], [], []
        for d in range(2):
            f = lb + (1.0 - lb) * jax.nn.sigmoid(z_ref[rows, (1 + d) * PACK_W:(2 + d) * PACK_W])
            kks.append(1.0 - f)
            l2 = jnp.log(f) * LOG2E
            hi = l2.astype(BF16)
            r1 = l2 - hi.astype(F32)
            mid = r1.astype(BF16)
            lo = (r1 - mid.astype(F32)).astype(BF16)
            cs = _dot(tri_ref[d], jnp.concatenate([hi, mid, lo], axis=1))
            bcs.append(cs[:, 0:PACK_W] + cs[:, PACK_W:2 * PACK_W] + cs[:, 2 * PACK_W:])

        for d in range(2):
            bwd = d == 1
            kk, bc = kks[d], bcs[d]
            ps = [(q * kk).astype(BF16)]
            for dist in range(1, SUB):
                shift = SUBLANES - dist if bwd else dist
                valid = (sub_row < SUB - dist) if bwd else (sub_row >= dist)
                arg = jnp.where(valid, bc - tile_roll(bc, shift), NEG_BIG)
                ps.append((q * tile_roll(kk, shift) * jnp.exp2(arg)).astype(BF16))
            r = _dot(jnp.concatenate(ps, axis=0), g_ref[...]).astype(BF16)
            a_near = r[0:ROW_BLK] * dm_ref[d, 0]
            for dist in range(1, SUB):
                a_near = a_near + r[dist * ROW_BLK:(dist + 1) * ROW_BLK] * dm_ref[d, dist]
            a_nears.append(a_near)

        for c in range(chunks_per_blk):
            for d in range(2):
                bwd = d == 1
                kk, bc, a_near = kks[d], bcs[d], a_nears[d]
                cs_ = slice(c * C, (c + 1) * C)
                qc, kc, bcc = q[cs_], kk[cs_], bc[cs_]
                last = bcc[0:1] if bwd else bcc[C - 1:C]
                a = a_near[cs_]
                for li, m in enumerate(HG_LEVELS):
                    pieces = []
                    for b2 in range(C // (2 * m)):
                        rr = 2 * m * b2 + (m if bwd else m - 1)
                        pieces.append(jnp.broadcast_to(bcc[rr:rr + 1], (2 * m, PACK_W)))
                    anchor = jnp.concatenate(pieces, axis=0) if len(pieces) > 1 else pieces[0]
                    upper = (chunk_row & (2 * m - 1)) >= m
                    later = jnp.logical_not(upper) if bwd else upper
                    e = jnp.exp2((bcc - anchor) * jnp.where(later, 1.0, -1.0))
                    a = a + _dot_nt((qc * e).astype(BF16),
                                    head_rep((kc * e).astype(BF16))).astype(BF16) * lm_ref[d, li]
                crow = pl.ds(start + c * C, C)
                o_scr[d, crow, :] = _dot(a, v_reps[c])
                qi_scr[d, crow, :] = (qc * jnp.exp2(bcc)).astype(BF16)
                ks_scr[d, crow, :] = (kc * jnp.exp2(last - bcc)).astype(BF16)
                slot = (blk * chunks_per_blk + c) * SUBLANES
                if not isinstance(slot, int):
                    slot = pl.multiple_of(slot, SUBLANES)
                dec_scr[d, pl.ds(slot, SUBLANES), :] = jnp.broadcast_to(
                    jnp.exp2(last), (SUBLANES, PACK_W))

        if not latent:
            vb = v.astype(BF16)
            for d, ref in enumerate((sf_ref, sb_ref)):
                bwd = d == 1
                sweep = range(chunks_per_blk - 1, -1, -1) if bwd else range(chunks_per_blk)
                tail = jnp.zeros((1, PACK_W), F32)
                decayed = [None] * chunks_per_blk
                for c in reversed(sweep):
                    cs_ = slice(c * C, (c + 1) * C)
                    bcc = bcs[d][cs_]
                    total = tail + (bcc[0:1] if bwd else bcc[C - 1:C])
                    decayed[c] = (kks[d][cs_] * jnp.exp2(total - bcc)).astype(BF16)
                    tail = total
                s = _dot_tn(jnp.concatenate(decayed, axis=0), vb) * g_ref[...].astype(F32)
                for h in range(HEADS):
                    ref[0, layer, h] = s[h * DH:(h + 1) * DH, h * DH:(h + 1) * DH]
                if prev_refs:
                    ref[0, 0:layer] = prev_refs[d][0]

    n_blk = seq_len // ROW_BLK
    assert latent or n_blk == 1
    if n_blk == 1:
        intra(0)
    else:
        def intra_step(i, carry):
            intra(i)
            return carry
        lax.fori_loop(0, n_blk, intra_step, 0)

    gmask = g_ref[...].astype(F32)
    if latent:
        states = (_initial_state(s0f_ref, gmask), _initial_state(s0b_ref, gmask))
    else:
        states = (jnp.zeros((PACK_W, PACK_W), F32),) * 2

    def body(i, states):
        new_states = []
        for d in range(2):
            c = (n_chunks - 1 - i) if d == 1 else i
            rows = pl.ds(pl.multiple_of(c * C, C), C)
            st = states[d]
            oi_scr[d, rows, :] = _dot_nt(qi_scr[d, rows, :], st.astype(BF16))
            dec = dec_scr[d, pl.ds(pl.multiple_of(c * SUBLANES, SUBLANES), 1), :]
            vb = z_ref[rows, 3 * PACK_W:4 * PACK_W].astype(BF16)
            new_states.append(st * dec + _dot_tn(vb, ks_scr[d, rows, :]) * gmask)
        return tuple(new_states)

    lax.fori_loop(0, n_chunks, body, states, unroll=4)

    def finish(start):
        rows = pl.ds(start, ROW_BLK)
        tot = (o_scr[0, rows, :] + oi_scr[0, rows, :]) + (o_scr[1, rows, :] + oi_scr[1, rows, :])
        gate = z_ref[rows, 4 * PACK_W:5 * PACK_W]
        y = _group_rms(tot, g_ref[0:LANES, 0:LANES], DH) * _silu(gate)
        out_ref[rows, :] = y.astype(out_ref.dtype)

    _row_blocks(seq_len, finish)


def _hgrn(z_hg, lb, consts, batch, seq_len, latent, layer, states=None):
    m = batch * seq_len
    tri, g256, lm, dm = consts
    row = lambda b: (b, 0)
    in_specs = [
        pl.BlockSpec((seq_len, HG_COLS), row),
        pl.BlockSpec((1, PACK_W), lambda b: (0, 0)),
        pl.BlockSpec(tri.shape, lambda b: (0, 0, 0)),
        pl.BlockSpec((PACK_W, PACK_W), lambda b: (0, 0)),
        pl.BlockSpec(lm.shape, lambda b: (0, 0, 0, 0)),
        pl.BlockSpec(dm.shape, lambda b: (0, 0, 0, 0)),
    ]
    args = [z_hg, lb, tri, g256, lm, dm]
    out_specs = [pl.BlockSpec((seq_len, PACK_W), row)]
    out_shape = [jax.ShapeDtypeStruct((m, PACK_W), BF16)]
    if latent:
        in_specs += [pl.BlockSpec((1, 1, PACK_W, DH), lambda b: (b, layer, 0, 0))] * 2
        args += [states[0], states[1]]
    else:
        s_in, s_args, s_out, s_shape = _state_specs(batch, layer, states)
        in_specs += s_in
        args += s_args
        out_specs += s_out
        out_shape += s_shape
    return pl.pallas_call(
        functools.partial(_hgrn_kernel, seq_len=seq_len, latent=latent, layer=layer),
        grid=(batch,),
        in_specs=in_specs,
        out_specs=out_specs,
        out_shape=out_shape,
        scratch_shapes=[
            pltpu.VMEM((2, seq_len, PACK_W), BF16),
            pltpu.VMEM((2, seq_len, PACK_W), BF16),
            pltpu.VMEM((2, seq_len // HG_C * SUBLANES, PACK_W), F32),
            pltpu.VMEM((2, seq_len, PACK_W), F32),
            pltpu.VMEM((2, seq_len, PACK_W), F32),
        ],
        compiler_params=_params(1),
        name="hgrn_latent" if latent else "hgrn_ctx",
    )(*args)


def _block_ones(n):
    idx = np.arange(n) // DH
    return jnp.asarray((idx[:, None] == idx[None, :]).astype(np.float32), dtype=BF16)


def _hgrn_consts():
    C = HG_C
    t = np.arange(ROW_BLK)
    same_chunk = (t[:, None] // C) == (t[None, :] // C)
    tri_f = same_chunk & (t[None, :] <= t[:, None])
    tri_b = same_chunk & (t[None, :] >= t[:, None])
    tc = np.arange(C)
    lms = []
    for fwd in (True, False):
        per_level = []
        for m in HG_LEVELS:
            same = (tc[:, None] // (2 * m)) == (tc[None, :] // (2 * m))
            upper = (tc % (2 * m)) >= m
            later, earlier = (upper, ~upper) if fwd else (~upper, upper)
            per_level.append(np.tile(same & later[:, None] & earlier[None, :], (1, HEADS)))
        lms.append(np.stack(per_level))
    dms = [[np.tile(tc[None, :] == tc[:, None] + sign * dist, (ROW_BLK // C, HEADS))
            for dist in range(HG_SUB)] for sign in (-1, 1)]
    return (jnp.asarray(np.stack([tri_f, tri_b]).astype(np.float32), dtype=BF16),
            _block_ones(PACK_W),
            jnp.asarray(np.stack(lms).astype(np.float32), dtype=BF16),
            jnp.asarray(np.stack(dms).astype(np.float32), dtype=BF16))


def _rope_tables(seq_len, width):
    t = np.arange(seq_len)
    pos = np.stack([t // GRID_W, t % GRID_W], axis=1).astype(np.float32)
    j = np.arange(DH)
    axis = j // (2 * ROPE_PAIR)
    jj = j % (2 * ROPE_PAIR)
    inv = (ROPE_BASE ** (-(jj % ROPE_PAIR).astype(np.float64) / ROPE_PAIR)).astype(np.float32)
    ang = (pos[:, axis] * inv[None, :]).astype(np.float64)
    cos = np.cos(ang)
    sin = np.where(jj < ROPE_PAIR, -np.sin(ang), np.sin(ang))
    reps = width // DH
    return (jnp.asarray(np.tile(cos, (1, reps)), dtype=F32),
            jnp.asarray(np.tile(sin, (1, reps)), dtype=F32))


def _layer(x2d, batch, seq_len, mods, l, p, latent, ctx, caches, states=None):
    z_ret, z_diff, z_hg = _inproj(x2d, p['norm_g'][l], mods, l, latent, p['w_in'][l], seq_len)
    lam_init = 0.8 - 0.6 * math.exp(-0.3 * l)
    if latent:
        k_ctx, v_ctx, s_rf, s_rb, s_hf, s_hb = ctx
        r = _retention(z_ret, p['log_g'][l], p['g256'], batch, seq_len, True, l,
                       rope=p['rope256'], states=(s_rf, s_rb))[0]
        a = _diff_attention(z_diff, p['lam'][l], p['gq'][l], p['gk'][l], p['g128'], batch,
                            seq_len, True, lam_init, l, rope=p['rope512'],
                            ctx=(k_ctx, v_ctx))[0]
        hgo = _hgrn(z_hg, p['lb'][l], p['hg_consts'], batch, seq_len, True, l,
                    states=(s_hf, s_hb))[0]
        extras = None
    else:
        prev = states if states is not None else (None,) * 4
        r, s_rf, s_rb = _retention(z_ret, p['log_g'][l], p['g256'], batch, seq_len, False, l,
                                   states=prev[0:2])
        a, kcache, vcache = _diff_attention(z_diff, p['lam'][l], p['gq'][l], p['gk'][l],
                                            p['g128'], batch, seq_len, False, lam_init, l,
                                            caches=caches)
        hgo, s_hf, s_hb = _hgrn(z_hg, p['lb'][l], p['hg_consts'], batch, seq_len, False, l,
                                states=prev[2:4])
        extras = ((kcache, vcache), (s_rf, s_rb, s_hf, s_hb))
    y = _outproj(x2d, r, a, hgo, mods, l, latent, p['w_out'][l], seq_len)
    return y, extras


def kernel(x_prompt, x_sample, c, c_ctx, cache_diff_k, cache_diff_v, state_ret_fwd,
           state_ret_bwd, state_hgrn_fwd, state_hgrn_bwd, norm_g, w_ada, b_ada, w_in,
           w_out, ret_decay_logit, diff_qn_g, diff_kn_g, diff_lambda, hgrn_lb_logit):
    batch, seq, _ = x_prompt.shape
    dec_batch, dec_seq, _ = x_sample.shape
    past_len = cache_diff_k.shape[2]

    assert 1 + dec_batch <= MOD_ROWS
    cv = jnp.zeros((MOD_ROWS, D_MODEL), F32).at[0].set(c_ctx).at[1:1 + dec_batch].set(c)
    mods = _ada(cv, w_ada, b_ada).reshape(DEPTH * 3 * MOD_ROWS, 1, D_MODEL)

    lb_all = jax.nn.softmax(hgrn_lb_logit.astype(F32), axis=0)
    lb_all = jnp.cumsum(lb_all, axis=0) - lb_all[0]
    lp = diff_lambda.astype(F32)
    lam_inits = jnp.asarray([0.8 - 0.6 * math.exp(-0.3 * l) for l in range(DEPTH)], F32)
    lam = (jnp.exp(jnp.sum(lp[:, 0] * lp[:, 1], axis=-1))
           - jnp.exp(jnp.sum(lp[:, 2] * lp[:, 3], axis=-1)) + lam_inits)
    p = {
        'norm_g': norm_g.reshape(DEPTH, 1, D_MODEL),
        'w_in': w_in.astype(BF16),
        'w_out': w_out.astype(BF16),
        'log_g': jax.nn.log_sigmoid(ret_decay_logit.astype(F32)),
        'lam': lam.reshape(DEPTH, 1),
        'gq': jnp.tile(diff_qn_g, (1, DIFF_W // DH)).reshape(DEPTH, 1, DIFF_W),
        'gk': jnp.tile(diff_kn_g, (1, DIFF_W // DH)).reshape(DEPTH, 1, DIFF_W),
        'lb': lb_all.reshape(DEPTH, 1, PACK_W),
        'g256': _block_ones(PACK_W),
        'g128': _block_ones(LANES),
        'hg_consts': _hgrn_consts(),
        'rope256': _rope_tables(dec_seq, PACK_W),
        'rope512': _rope_tables(dec_seq, DIFF_W),
    }

    y = x_prompt.reshape(batch * seq, D_MODEL)
    caches = None
    new_states = None
    for l in range(DEPTH):
        y, (caches, new_states) = _layer(y, batch, seq, mods, l, p, False, None, caches,
                                         new_states)
    y_prompt = y.reshape(batch, seq, D_MODEL)
    new_k = caches[0].reshape(batch, DEPTH, seq, HEADS, 2, DH)
    new_v = caches[1].reshape(batch, DEPTH, seq, HEADS, DIFF_DV)

    y = x_sample.reshape(dec_batch * dec_seq, D_MODEL)
    rows = lambda s: s.reshape(dec_batch, DEPTH, PACK_W, DH)
    for l in range(DEPTH):
        ctx = (cache_diff_k[:, l].reshape(dec_batch, past_len, DIFF_W),
               cache_diff_v[:, l].reshape(dec_batch, past_len, DIFF_W),
               rows(state_ret_fwd), rows(state_ret_bwd),
               rows(state_hgrn_fwd), rows(state_hgrn_bwd))
        y, _ = _layer(y, dec_batch, dec_seq, mods, l, p, True, ctx, None)
    y_sample = y.reshape(dec_batch, dec_seq, D_MODEL)

    return (y_prompt, y_sample, new_k, new_v, *new_states)
```

```python
import functools
import math

import numpy as np
import jax
import jax.numpy as jnp
from jax import lax
from jax.experimental import pallas as pl
from jax.experimental.pallas import tpu as pltpu

F32 = jnp.float32
BF16 = jnp.bfloat16

D_MODEL = 1024
DEPTH = 2
GRID_W = 64
HEADS = 4
DH = 64
PACK_W = HEADS * DH
DIFF_W = 512
DIFF_DV = 128
RET_COLS = 4 * PACK_W
DIFF_COLS = 4 * DIFF_W
HG_COLS = 5 * PACK_W
IN_W = RET_COLS + DIFF_COLS + HG_COLS
MIX_W = PACK_W + DIFF_W + PACK_W
ROPE_BASE = 10000.0
ROPE_PAIR = 16
EPS = 1e-6
LOG2E = 1.4426950408889634
NEG_BIG = -1e30

RET_C = 256
HG_C = 64
HG_SUB = 4
HG_LEVELS = (4, 8, 16, 32)
SUBLANES = 8
LANES = 128
ROW_BLK = 256
TQ = 512
TM_IN = 1024
TM_OUT = 512

VMEM_LIMIT = 56 * 1024 * 1024


def _params(n_axes):
    return pltpu.CompilerParams(
        dimension_semantics=("arbitrary",) * n_axes, vmem_limit_bytes=VMEM_LIMIT)


def _dot(a, b):
    return jnp.dot(a, b, preferred_element_type=F32)


def _dot_nt(a, b):
    return lax.dot_general(a, b, (((1,), (1,)), ((), ())), preferred_element_type=F32)


def _dot_tn(a, b):
    return lax.dot_general(a, b, (((0,), (0,)), ((), ())), preferred_element_type=F32)


def _silu(x):
    return x * jax.nn.sigmoid(x)


def _group_rms(x, g, width):
    n, w = x.shape
    x2 = x * x
    cols = range(0, w, LANES)
    stacked = jnp.concatenate([x2[:, j:j + LANES] for j in cols], axis=0)
    hi = stacked.astype(BF16)
    lo = (stacked - hi.astype(F32)).astype(BF16)
    sums = _dot(hi, g) + _dot(lo, g)
    ms = jnp.concatenate([sums[i * n:(i + 1) * n] for i in range(len(cols))], axis=1)
    return x * lax.rsqrt(ms * (1.0 / width) + EPS)


def _rope(x, cos, sin):
    w = x.shape[-1]
    lane = lax.broadcasted_iota(jnp.int32, (1, w), 1)
    first = (lane & (2 * ROPE_PAIR - 1)) < ROPE_PAIR
    swapped = jnp.where(first, pltpu.roll(x, w - ROPE_PAIR, 1), pltpu.roll(x, ROPE_PAIR, 1))
    return x * cos + swapped * sin


def _initial_state(s_ref, gmask):
    s = s_ref[0, 0]
    return (jnp.concatenate([s] * HEADS, axis=1) * gmask).T


def _lane_head(w=PACK_W):
    return lax.broadcasted_iota(jnp.int32, (1, w), 1) // DH


def _row_blocks(n_rows, body):
    n_blk = n_rows // ROW_BLK
    if n_blk == 1:
        body(0)
    else:
        def step(i, carry):
            body(pl.multiple_of(i * ROW_BLK, ROW_BLK))
            return carry
        lax.fori_loop(0, n_blk, step, 0)


def _ada_kernel(cv_ref, w_ref, b_ref, o_ref):
    cv = cv_ref[...]
    o_ref[0, 0] = _dot(_silu(cv), w_ref[0]) + b_ref[0]


def _ada(cv, w_ada, b_ada):
    tn = 1024
    n = 3 * D_MODEL
    return pl.pallas_call(
        _ada_kernel,
        grid=(DEPTH, n // tn),
        in_specs=[
            pl.BlockSpec((8, D_MODEL), lambda l, j: (0, 0)),
            pl.BlockSpec((1, D_MODEL, tn), lambda l, j: (l, 0, j)),
            pl.BlockSpec((1, 1, tn), lambda l, j: (l, 0, j)),
        ],
        out_specs=pl.BlockSpec((1, 1, 8, tn), lambda l, j: (l, j, 0, 0)),
        out_shape=jax.ShapeDtypeStruct((DEPTH, n // tn, 8, tn), F32),
        compiler_params=_params(2),
        name="ada",
    )(cv, w_ada, b_ada.reshape(DEPTH, 1, n))


def _inproj_kernel(x_ref, g_ref, sc_ref, sh_ref, w_ref, zr_ref, zd_ref, zh_ref):
    x = x_ref[...]
    ms = jnp.mean(x * x, axis=-1, keepdims=True)
    h = x * lax.rsqrt(ms + EPS) * g_ref[...]
    h = h * (1.0 + sc_ref[0]) + sh_ref[0]
    z = _dot(h.astype(BF16), w_ref[...])
    zr_ref[...] = z[:, :RET_COLS].astype(zr_ref.dtype)
    zd_ref[...] = z[:, RET_COLS:RET_COLS + DIFF_COLS].astype(zd_ref.dtype)
    zh_ref[...] = z[:, RET_COLS + DIFF_COLS:]


MOD_SHIFT, MOD_SCALE, MOD_GATE = 0, 1, 2
MOD_ROWS = 8


def _mod_spec(layer, which, per_batch, tiles_per_seq):
    base = (layer * 3 + which) * MOD_ROWS
    if per_batch:
        return pl.BlockSpec((1, 1, D_MODEL), lambda i: (base + 1 + i // tiles_per_seq, 0, 0))
    return pl.BlockSpec((1, 1, D_MODEL), lambda i: (base, 0, 0))


def _inproj(x2d, g, mods, layer, per_batch, w_bf, seq_len):
    m = x2d.shape[0]
    tiles_per_seq = seq_len // TM_IN
    return pl.pallas_call(
        _inproj_kernel,
        grid=(m // TM_IN,),
        in_specs=[
            pl.BlockSpec((TM_IN, D_MODEL), lambda i: (i, 0)),
            pl.BlockSpec((1, D_MODEL), lambda i: (0, 0)),
            _mod_spec(layer, MOD_SCALE, per_batch, tiles_per_seq),
            _mod_spec(layer, MOD_SHIFT, per_batch, tiles_per_seq),
            pl.BlockSpec((D_MODEL, IN_W), lambda i: (0, 0)),
        ],
        out_specs=[
            pl.BlockSpec((TM_IN, RET_COLS), lambda i: (i, 0)),
            pl.BlockSpec((TM_IN, DIFF_COLS), lambda i: (i, 0)),
            pl.BlockSpec((TM_IN, HG_COLS), lambda i: (i, 0)),
        ],
        out_shape=[
            jax.ShapeDtypeStruct((m, RET_COLS), BF16),
            jax.ShapeDtypeStruct((m, DIFF_COLS), BF16),
            jax.ShapeDtypeStruct((m, HG_COLS), F32),
        ],
        compiler_params=_params(1),
        name="inproj",
    )(x2d, g, mods, mods, w_bf)


def _outproj_kernel(x_ref, r_ref, d_ref, h_ref, gate_ref, w_ref, o_ref):
    mixed = jnp.concatenate([r_ref[...], d_ref[...], h_ref[...]], axis=1)
    o_ref[...] = x_ref[...] + gate_ref[0] * _dot(mixed, w_ref[...])


def _outproj(x2d, r, d, h, mods, layer, per_batch, w_bf, seq_len):
    m = x2d.shape[0]
    tiles_per_seq = max(seq_len // TM_OUT, 1)
    tm = TM_OUT if (not per_batch or seq_len % TM_OUT == 0) else seq_len
    return pl.pallas_call(
        _outproj_kernel,
        grid=(m // tm,),
        in_specs=[
            pl.BlockSpec((tm, D_MODEL), lambda i: (i, 0)),
            pl.BlockSpec((tm, PACK_W), lambda i: (i, 0)),
            pl.BlockSpec((tm, DIFF_W), lambda i: (i, 0)),
            pl.BlockSpec((tm, PACK_W), lambda i: (i, 0)),
            _mod_spec(layer, MOD_GATE, per_batch, tiles_per_seq),
            pl.BlockSpec((MIX_W, D_MODEL), lambda i: (0, 0)),
        ],
        out_specs=pl.BlockSpec((tm, D_MODEL), lambda i: (i, 0)),
        out_shape=jax.ShapeDtypeStruct((m, D_MODEL), F32),
        compiler_params=_params(1),
        name="outproj",
    )(x2d, r, d, h, mods, w_bf)


def _ret_kernel(*refs, seq_len, latent, layer):
    if latent:
        (lg_ref, z_ref, g_ref, cos_ref, sin_ref, s0f_ref, s0b_ref,
         out_ref, mask_scr, o_scr) = refs
    else:
        lg_ref, z_ref, g_ref = refs[:3]
        prev_refs = refs[3:-5]
        out_ref, sf_ref, sb_ref, mask_scr, o_scr = refs[-5:]
    C = RET_C
    n_chunks = seq_len // C
    lh = _lane_head()

    def lane_vec(direction):
        v = jnp.zeros((1, PACK_W), F32)
        for h in range(HEADS):
            v = jnp.where(lh == h, lg_ref[direction, h], v)
        return v

    lgf, lgb = lane_vec(0), lane_vec(1)
    tau = lax.broadcasted_iota(jnp.int32, (C, 1), 0).astype(F32)
    qdec_f = jnp.exp((tau + 1.0) * lgf)
    qdec_b = jnp.exp((C - tau) * lgb)
    kdec_f = jnp.exp((C - 1.0 - tau) * lgf)
    kdec_b = jnp.exp(tau * lgb)
    sdec_f = jnp.exp(float(C) * lgf)
    sdec_b = jnp.exp(float(C) * lgb)
    gmask = g_ref[...].astype(F32)

    @pl.when(pl.program_id(0) == 0)
    def _():
        dd = (lax.broadcasted_iota(jnp.int32, (C, C), 0)
              - lax.broadcasted_iota(jnp.int32, (C, C), 1)).astype(F32)
        for h in range(HEADS):
            mf = jnp.where(dd >= 0, jnp.exp(jnp.maximum(dd, 0.0) * lg_ref[0, h]), 0.0)
            mb = jnp.where(dd <= 0, jnp.exp(jnp.maximum(-dd, 0.0) * lg_ref[1, h]), 0.0)
            mask_scr[h * C:(h + 1) * C, :] = mf + mb

    def load(r0):
        q = z_ref[pl.ds(r0, C), 0:PACK_W].astype(F32)
        k = z_ref[pl.ds(r0, C), PACK_W:2 * PACK_W].astype(F32) * (DH ** -0.5)
        v = z_ref[pl.ds(r0, C), 2 * PACK_W:3 * PACK_W].astype(F32)
        if latent:
            cos = cos_ref[pl.ds(r0, C), :]
            sin = sin_ref[pl.ds(r0, C), :]
            q = _rope(q, cos, sin)
            k = _rope(k, cos, sin)
        return q, k, v

    def intra(q, k, v):
        qb = q.astype(BF16)
        q4 = jnp.concatenate([jnp.where(lh == h, qb, jnp.zeros_like(qb)) for h in range(HEADS)],
                             axis=0)
        a4 = _dot_nt(q4, k.astype(BF16)) * mask_scr[...]
        o4 = _dot(a4.astype(BF16), v.astype(BF16))
        o = jnp.zeros((C, PACK_W), F32)
        for h in range(HEADS):
            o = o + jnp.where(lh == h, o4[h * C:(h + 1) * C], 0.0)
        return o

    def finish(o, r0):
        gate = z_ref[pl.ds(r0, C), 3 * PACK_W:4 * PACK_W].astype(F32)
        y = _group_rms(o, g_ref[0:LANES, 0:LANES], DH) * _silu(gate)
        out_ref[pl.ds(r0, C), :] = y.astype(out_ref.dtype)

    def state_step(st, k, v, kdec, sdec):
        upd = _dot_tn(v.astype(BF16), (k * kdec).astype(BF16))
        return (st * sdec + upd) * gmask

    if not latent:
        q, k, v = load(0)
        finish(intra(q, k, v), 0)
        vb = v.astype(BF16)
        for ref, kdec in ((sf_ref, kdec_f), (sb_ref, kdec_b)):
            s = _dot_tn((k * kdec).astype(BF16), vb) * gmask
            for h in range(HEADS):
                ref[0, layer, h] = s[h * DH:(h + 1) * DH, h * DH:(h + 1) * DH]
        for ref, prev in zip((sf_ref, sb_ref), prev_refs):
            ref[0, 0:layer] = prev[0]
        return

    def bwd_body(i, st):
        r0 = pl.multiple_of((n_chunks - 1 - i) * C, C)
        q, k, v = load(r0)
        o_scr[pl.ds(r0, C), :] = _dot_nt((q * qdec_b).astype(BF16), st.astype(BF16))
        return state_step(st, k, v, kdec_b, sdec_b)

    lax.fori_loop(0, n_chunks, bwd_body, _initial_state(s0b_ref, gmask), unroll=2)

    def fwd_body(i, st):
        r0 = pl.multiple_of(i * C, C)
        q, k, v = load(r0)
        o = intra(q, k, v) + o_scr[pl.ds(r0, C), :]
        o = o + _dot_nt((q * qdec_f).astype(BF16), st.astype(BF16))
        finish(o, r0)
        return state_step(st, k, v, kdec_f, sdec_f)

    lax.fori_loop(0, n_chunks, fwd_body, _initial_state(s0f_ref, gmask), unroll=2)


def _state_specs(batch, layer, prev):
    spec = lambda n: pl.BlockSpec((1, n, HEADS, DH, DH), lambda b: (b, 0, 0, 0, 0))
    out_specs = [spec(layer + 1)] * 2
    out_shape = [jax.ShapeDtypeStruct((batch, layer + 1, HEADS, DH, DH), F32)] * 2
    in_specs = [spec(layer)] * 2 if layer > 0 else []
    return in_specs, (list(prev) if layer > 0 else []), out_specs, out_shape


def _retention(z_ret, log_g, g256, batch, seq_len, latent, layer, rope=None, states=None):
    m = batch * seq_len
    row = lambda b: (b, 0)
    const2 = lambda b: (0, 0)
    in_specs = [
        pl.BlockSpec(memory_space=pltpu.SMEM),
        pl.BlockSpec((seq_len, RET_COLS), row),
        pl.BlockSpec((PACK_W, PACK_W), const2),
    ]
    args = [log_g, z_ret, g256]
    out_specs = [pl.BlockSpec((seq_len, PACK_W), row)]
    out_shape = [jax.ShapeDtypeStruct((m, PACK_W), BF16)]
    if latent:
        state_spec = pl.BlockSpec((1, 1, PACK_W, DH), lambda b: (b, layer, 0, 0))
        in_specs += [pl.BlockSpec((seq_len, PACK_W), const2)] * 2 + [state_spec] * 2
        args += [rope[0], rope[1], states[0], states[1]]
    else:
        s_in, s_args, s_out, s_shape = _state_specs(batch, layer, states)
        in_specs += s_in
        args += s_args
        out_specs += s_out
        out_shape += s_shape
    return pl.pallas_call(
        functools.partial(_ret_kernel, seq_len=seq_len, latent=latent, layer=layer),
        grid=(batch,),
        in_specs=in_specs,
        out_specs=out_specs,
        out_shape=out_shape,
        scratch_shapes=[
            pltpu.VMEM((HEADS * RET_C, RET_C), F32),
            pltpu.VMEM((seq_len, PACK_W), F32),
        ],
        compiler_params=_params(1),
        name="ret_latent" if latent else "ret_ctx",
    )(*args)


def _attn_kernel(*refs, seq_len, past_len, latent, lam_init, layer):
    if latent:
        (lam_ref, q_ref, k_ref, v_ref, gate_ref, gq_ref, gk_ref, g_ref,
         cosq_ref, sinq_ref, cosk_ref, sink_ref, kc_ref, vc_ref,
         out_ref, k_scr, v_scr) = refs
        prev_refs = ()
    else:
        (lam_ref, q_ref, k_ref, v_ref, gate_ref, gq_ref, gk_ref, g_ref) = refs[:8]
        prev_refs = refs[8:-5]
        out_ref, kn_ref, vn_ref, k_scr, v_scr = refs[-5:]
    ones = jnp.ones((ROW_BLK, DIFF_DV), BF16)

    def put_values(sl, head_values):
        for h in range(HEADS):
            v_scr[sl, 2 * h * DIFF_DV:(2 * h + 1) * DIFF_DV] = head_values(h).astype(BF16)
            v_scr[sl, (2 * h + 1) * DIFF_DV:(2 * h + 2) * DIFF_DV] = ones

    @pl.when(pl.program_id(1) == 0)
    def _():
        for r in range(seq_len // ROW_BLK):
            sl = pl.ds(r * ROW_BLK, ROW_BLK)
            kn = _group_rms(k_ref[sl, :].astype(F32), g_ref[...], DH) * gk_ref[...]
            v = v_ref[sl, :].astype(F32)
            if latent:
                kn = _rope(kn, cosk_ref[sl, :], sink_ref[sl, :])
            else:
                kn_ref[0, layer, sl, :] = kn
                for h in range(HEADS):
                    vn_ref[0, layer, sl, h, :] = v[:, h * DIFF_DV:(h + 1) * DIFF_DV]
            k_scr[sl, :] = kn.astype(BF16)
            put_values(sl, lambda h: v[:, h * DIFF_DV:(h + 1) * DIFF_DV])
        if prev_refs:
            kn_ref[0, 0:layer] = prev_refs[0][0]
            vn_ref[0, 0:layer] = prev_refs[1][0]
        if latent:
            for r in range(past_len // ROW_BLK):
                src = pl.ds(r * ROW_BLK, ROW_BLK)
                dst = pl.ds(seq_len + r * ROW_BLK, ROW_BLK)
                k_scr[dst, :] = kc_ref[0, 0, :, src].T.astype(BF16)
                put_values(dst, lambda h: vc_ref[0, 0, src, h, :])

    lam = lam_ref[0]
    qn = _group_rms(q_ref[...].astype(F32), g_ref[...], DH) * gq_ref[...]
    if latent:
        qn = _rope(qn, cosq_ref[...], sinq_ref[...])
    qn = qn * (DH ** -0.5 * LOG2E)
    lane = lax.broadcasted_iota(jnp.int32, (1, 2 * DH), 1)
    for h in range(HEADS):
        hs = slice(h * 2 * DH, (h + 1) * 2 * DH)
        vx = v_scr[:, 2 * h * DIFF_DV:(2 * h + 2) * DIFF_DV]
        qh = qn[:, hs]
        kh = k_scr[:, hs]
        parts = []
        for first_map in (True, False):
            qm = jnp.where((lane < DH) == first_map, qh, 0.0).astype(BF16)
            s = _dot_nt(qm, kh)
            e = jnp.exp2(s - jnp.max(s, axis=-1, keepdims=True)).astype(BF16)
            ox = _dot(e, vx)
            parts.append(ox[:, :DIFF_DV] / ox[:, DIFF_DV:])
        o = parts[0] - lam * parts[1]
        ms = jnp.mean(o * o, axis=-1, keepdims=True)
        y = o * lax.rsqrt(ms + EPS) * (1.0 - lam_init) * _silu(gate_ref[:, hs].astype(F32))
        out_ref[:, hs] = y.astype(out_ref.dtype)


def _diff_attention(z_diff, lam, gq, gk, g128, batch, seq_len, latent, lam_init, layer,
                    rope=None, ctx=None, caches=None):
    m = batch * seq_len
    tq = min(TQ, seq_len)
    nq = seq_len // tq
    past_len = ctx[1].shape[2] if latent else 0
    qmap = lambda col: (lambda b, i: (b * nq + i, col))
    kvmap = lambda col: (lambda b, i: (b, col))
    const2 = lambda b, i: (0, 0)
    in_specs = [
        pl.BlockSpec(memory_space=pltpu.SMEM),
        pl.BlockSpec((tq, DIFF_W), qmap(0)),
        pl.BlockSpec((seq_len, DIFF_W), kvmap(1)),
        pl.BlockSpec((seq_len, DIFF_W), kvmap(2)),
        pl.BlockSpec((tq, DIFF_W), qmap(3)),
        pl.BlockSpec((1, DIFF_W), const2),
        pl.BlockSpec((1, DIFF_W), const2),
        pl.BlockSpec((LANES, LANES), const2),
    ]
    args = [lam, z_diff, z_diff, z_diff, z_diff, gq, gk, g128]
    out_specs = [pl.BlockSpec((tq, DIFF_W), qmap(0))]
    out_shape = [jax.ShapeDtypeStruct((m, DIFF_W), BF16)]
    if latent:
        in_specs += [
            pl.BlockSpec((tq, DIFF_W), lambda b, i: (i, 0)),
            pl.BlockSpec((tq, DIFF_W), lambda b, i: (i, 0)),
            pl.BlockSpec((seq_len, DIFF_W), const2),
            pl.BlockSpec((seq_len, DIFF_W), const2),
            pl.BlockSpec((1, 1, DIFF_W, past_len), lambda b, i: (b, layer, 0, 0)),
            pl.BlockSpec((1, 1, past_len, HEADS, DIFF_DV), lambda b, i: (b, layer, 0, 0, 0)),
        ]
        args += [rope[0], rope[1], rope[0], rope[1], ctx[0], ctx[1]]
    else:
        cache_spec = lambda n: pl.BlockSpec((1, n, seq_len, DIFF_W), lambda b, i: (b, 0, 0, 0))
        vcache_spec = lambda n: pl.BlockSpec((1, n, seq_len, HEADS, DIFF_DV),
                                             lambda b, i: (b, 0, 0, 0, 0))
        out_specs += [cache_spec(layer + 1), vcache_spec(layer + 1)]
        out_shape += [jax.ShapeDtypeStruct((batch, layer + 1, seq_len, DIFF_W), F32),
                      jax.ShapeDtypeStruct((batch, layer + 1, seq_len, HEADS, DIFF_DV), F32)]
        if layer > 0:
            in_specs += [cache_spec(layer), vcache_spec(layer)]
            args += list(caches)
    return pl.pallas_call(
        functools.partial(_attn_kernel, seq_len=seq_len, past_len=past_len,
                          latent=latent, lam_init=lam_init, layer=layer),
        grid=(batch, nq),
        in_specs=in_specs,
        out_specs=out_specs,
        out_shape=out_shape,
        scratch_shapes=[
            pltpu.VMEM((seq_len + past_len, DIFF_W), BF16),
            pltpu.VMEM((seq_len + past_len, 2 * DIFF_W), BF16),
        ],
        compiler_params=_params(2),
        name="attn_latent" if latent else "attn_ctx",
    )(*args)


def _hgrn_kernel(*refs, seq_len, latent, layer):
    if latent:
        (z_ref, lb_ref, tri_ref, g_ref, lm_ref, dm_ref, s0f_ref, s0b_ref,
         out_ref, qi_scr, ks_scr, dec_scr, o_scr, oi_scr) = refs
    else:
        z_ref, lb_ref, tri_ref, g_ref, lm_ref, dm_ref = refs[:6]
        prev_refs = refs[6:-8]
        out_ref, sf_ref, sb_ref, qi_scr, ks_scr, dec_scr, o_scr, oi_scr = refs[-8:]
    C, SUB = HG_C, HG_SUB
    n_chunks = seq_len // C
    chunks_per_blk = ROW_BLK // C
    lh = _lane_head()
    sub_row = lax.broadcasted_iota(jnp.int32, (ROW_BLK, 1), 0) & (SUB - 1)
    chunk_row = lax.broadcasted_iota(jnp.int32, (C, 1), 0)

    def tile_roll(x, shift):
        return jnp.concatenate(
            [pltpu.roll(x[i:i + SUBLANES], shift, 0) for i in range(0, ROW_BLK, SUBLANES)], axis=0)

    def head_rep(x):
        return jnp.concatenate([jnp.where(lh == h, x, jnp.zeros_like(x)) for h in range(HEADS)],
                               axis=0)

    def intra(blk):
        start = blk * ROW_BLK if isinstance(blk, int) else pl.multiple_of(blk * ROW_BLK, ROW_BLK)
        rows = pl.ds(start, ROW_BLK)
        lb = lb_ref[...]
        q = z_ref[rows, 0:PACK_W] * (DH ** -0.5)
        v = z_ref[rows, 3 * PACK_W:4 * PACK_W]
        v_reps = [head_rep(v[c * C:(c + 1) * C].astype(BF16)) for c in range(chunks_per_blk)]
        kks, bcs, a_nears = [], [], []
        for d in range(2):
            f = lb + (1.0 - lb) * jax.nn.sigmoid(z_ref[rows, (1 + d) * PACK_W:(2 + d) * PACK_W])
            kks.append(1.0 - f)
            l2 = jnp.log(f) * LOG2E
            hi = l2.astype(BF16)
            r1 = l2 - hi.astype(F32)
            mid = r1.astype(BF16)
            lo = (r1 - mid.astype(F32)).astype(BF16)
            cs = _dot(tri_ref[d], jnp.concatenate([hi, mid, lo], axis=1))
            bcs.append(cs[:, 0:PACK_W] + cs[:, PACK_W:2 * PACK_W] + cs[:, 2 * PACK_W:])

        for d in range(2):
            bwd = d == 1
            kk, bc = kks[d], bcs[d]
            ps = [(q * kk).astype(BF16)]
            for dist in range(1, SUB):
                shift = SUBLANES - dist if bwd else dist
                valid = (sub_row < SUB - dist) if bwd else (sub_row >= dist)
                arg = jnp.where(valid, bc - tile_roll(bc, shift), NEG_BIG)
                ps.append((q * tile_roll(kk, shift) * jnp.exp2(arg)).astype(BF16))
            r = _dot(jnp.concatenate(ps, axis=0), g_ref[...]).astype(BF16)
            a_near = r[0:ROW_BLK] * dm_ref[d, 0]
            for dist in range(1, SUB):
                a_near = a_near + r[dist * ROW_BLK:(dist + 1) * ROW_BLK] * dm_ref[d, dist]
            a_nears.append(a_near)

        for c in range(chunks_per_blk):
            for d in range(2):
                bwd = d == 1
                kk, bc, a_near = kks[d], bcs[d], a_nears[d]
                cs_ = slice(c * C, (c + 1) * C)
                qc, kc, bcc = q[cs_], kk[cs_], bc[cs_]
                last = bcc[0:1] if bwd else bcc[C - 1:C]
                a = a_near[cs_]
                for li, m in enumerate(HG_LEVELS):
                    pieces = []
                    for b2 in range(C // (2 * m)):
                        rr = 2 * m * b2 + (m if bwd else m - 1)
                        pieces.append(jnp.broadcast_to(bcc[rr:rr + 1], (2 * m, PACK_W)))
                    anchor = jnp.concatenate(pieces, axis=0) if len(pieces) > 1 else pieces[0]
                    upper = (chunk_row & (2 * m - 1)) >= m
                    later = jnp.logical_not(upper) if bwd else upper
                    e = jnp.exp2((bcc - anchor) * jnp.where(later, 1.0, -1.0))
                    a = a + _dot_nt((qc * e).astype(BF16),
                                    head_rep((kc * e).astype(BF16))).astype(BF16) * lm_ref[d, li]
                crow = pl.ds(start + c * C, C)
                o_scr[d, crow, :] = _dot(a, v_reps[c])
                qi_scr[d, crow, :] = (qc * jnp.exp2(bcc)).astype(BF16)
                ks_scr[d, crow, :] = (kc * jnp.exp2(last - bcc)).astype(BF16)
                slot = (blk * chunks_per_blk + c) * SUBLANES
                if not isinstance(slot, int):
                    slot = pl.multiple_of(slot, SUBLANES)
                dec_scr[d, pl.ds(slot, SUBLANES), :] = jnp.broadcast_to(
                    jnp.exp2(last), (SUBLANES, PACK_W))

        if not latent:
            vb = v.astype(BF16)
            for d, ref in enumerate((sf_ref, sb_ref)):
                bwd = d == 1
                sweep = range(chunks_per_blk - 1, -1, -1) if bwd else range(chunks_per_blk)
                tail = jnp.zeros((1, PACK_W), F32)
                decayed = [None] * chunks_per_blk
                for c in reversed(sweep):
                    cs_ = slice(c * C, (c + 1) * C)
                    bcc = bcs[d][cs_]
                    total = tail + (bcc[0:1] if bwd else bcc[C - 1:C])
                    decayed[c] = (kks[d][cs_] * jnp.exp2(total - bcc)).astype(BF16)
                    tail = total
                s = _dot_tn(jnp.concatenate(decayed, axis=0), vb) * g_ref[...].astype(F32)
                for h in range(HEADS):
                    ref[0, layer, h] = s[h * DH:(h + 1) * DH, h * DH:(h + 1) * DH]
                if prev_refs:
                    ref[0, 0:layer] = prev_refs[d][0]

    n_blk = seq_len // ROW_BLK
    assert latent or n_blk == 1
    if n_blk == 1:
        intra(0)
    else:
        def intra_step(i, carry):
            intra(i)
            return carry
        lax.fori_loop(0, n_blk, intra_step, 0)

    gmask = g_ref[...].astype(F32)
    if latent:
        states = (_initial_state(s0f_ref, gmask), _initial_state(s0b_ref, gmask))
    else:
        states = (jnp.zeros((PACK_W, PACK_W), F32),) * 2

    def body(i, states):
        new_states = []
        for d in range(2):
            c = (n_chunks - 1 - i) if d == 1 else i
            rows = pl.ds(pl.multiple_of(c * C, C), C)
            st = states[d]
            oi_scr[d, rows, :] = _dot_nt(qi_scr[d, rows, :], st.astype(BF16))
            dec = dec_scr[d, pl.ds(pl.multiple_of(c * SUBLANES, SUBLANES), 1), :]
            vb = z_ref[rows, 3 * PACK_W:4 * PACK_W].astype(BF16)
            new_states.append(st * dec + _dot_tn(vb, ks_scr[d, rows, :]) * gmask)
        return tuple(new_states)

    lax.fori_loop(0, n_chunks, body, states, unroll=4)

    def finish(start):
        rows = pl.ds(start, ROW_BLK)
        tot = (o_scr[0, rows, :] + oi_scr[0, rows, :]) + (o_scr[1, rows, :] + oi_scr[1, rows, :])
        gate = z_ref[rows, 4 * PACK_W:5 * PACK_W]
        y = _group_rms(tot, g_ref[0:LANES, 0:LANES], DH) * _silu(gate)
        out_ref[rows, :] = y.astype(out_ref.dtype)

    _row_blocks(seq_len, finish)


def _hgrn(z_hg, lb, consts, batch, seq_len, latent, layer, states=None):
    m = batch * seq_len
    tri, g256, lm, dm = consts
    row = lambda b: (b, 0)
    in_specs = [
        pl.BlockSpec((seq_len, HG_COLS), row),
        pl.BlockSpec((1, PACK_W), lambda b: (0, 0)),
        pl.BlockSpec(tri.shape, lambda b: (0, 0, 0)),
        pl.BlockSpec((PACK_W, PACK_W), lambda b: (0, 0)),
        pl.BlockSpec(lm.shape, lambda b: (0, 0, 0, 0)),
        pl.BlockSpec(dm.shape, lambda b: (0, 0, 0, 0)),
    ]
    args = [z_hg, lb, tri, g256, lm, dm]
    out_specs = [pl.BlockSpec((seq_len, PACK_W), row)]
    out_shape = [jax.ShapeDtypeStruct((m, PACK_W), BF16)]
    if latent:
        in_specs += [pl.BlockSpec((1, 1, PACK_W, DH), lambda b: (b, layer, 0, 0))] * 2
        args += [states[0], states[1]]
    else:
        s_in, s_args, s_out, s_shape = _state_specs(batch, layer, states)
        in_specs += s_in
        args += s_args
        out_specs += s_out
        out_shape += s_shape
    return pl.pallas_call(
        functools.partial(_hgrn_kernel, seq_len=seq_len, latent=latent, layer=layer),
        grid=(batch,),
        in_specs=in_specs,
        out_specs=out_specs,
        out_shape=out_shape,
        scratch_shapes=[
            pltpu.VMEM((2, seq_len, PACK_W), BF16),
            pltpu.VMEM((2, seq_len, PACK_W), BF16),
            pltpu.VMEM((2, seq_len // HG_C * SUBLANES, PACK_W), F32),
            pltpu.VMEM((2, seq_len, PACK_W), F32),
            pltpu.VMEM((2, seq_len, PACK_W), F32),
        ],
        compiler_params=_params(1),
        name="hgrn_latent" if latent else "hgrn_ctx",
    )(*args)


def _block_ones(n):
    idx = np.arange(n) // DH
    return jnp.asarray((idx[:, None] == idx[None, :]).astype(np.float32), dtype=BF16)


def _hgrn_consts():
    C = HG_C
    t = np.arange(ROW_BLK)
    same_chunk = (t[:, None] // C) == (t[None, :] // C)
    tri_f = same_chunk & (t[None, :] <= t[:, None])
    tri_b = same_chunk & (t[None, :] >= t[:, None])
    tc = np.arange(C)
    lms = []
    for fwd in (True, False):
        per_level = []
        for m in HG_LEVELS:
            same = (tc[:, None] // (2 * m)) == (tc[None, :] // (2 * m))
            upper = (tc % (2 * m)) >= m
            later, earlier = (upper, ~upper) if fwd else (~upper, upper)
            per_level.append(np.tile(same & later[:, None] & earlier[None, :], (1, HEADS)))
        lms.append(np.stack(per_level))
    dms = [[np.tile(tc[None, :] == tc[:, None] + sign * dist, (ROW_BLK // C, HEADS))
            for dist in range(HG_SUB)] for sign in (-1, 1)]
    return (jnp.asarray(np.stack([tri_f, tri_b]).astype(np.float32), dtype=BF16),
            _block_ones(PACK_W),
            jnp.asarray(np.stack(lms).astype(np.float32), dtype=BF16),
            jnp.asarray(np.stack(dms).astype(np.float32), dtype=BF16))


def _rope_tables(seq_len, width):
    t = np.arange(seq_len)
    pos = np.stack([t // GRID_W, t % GRID_W], axis=1).astype(np.float32)
    j = np.arange(DH)
    axis = j // (2 * ROPE_PAIR)
    jj = j % (2 * ROPE_PAIR)
    inv = (ROPE_BASE ** (-(jj % ROPE_PAIR).astype(np.float64) / ROPE_PAIR)).astype(np.float32)
    ang = (pos[:, axis] * inv[None, :]).astype(np.float64)
    cos = np.cos(ang)
    sin = np.where(jj < ROPE_PAIR, -np.sin(ang), np.sin(ang))
    reps = width // DH
    return (jnp.asarray(np.tile(cos, (1, reps)), dtype=F32),
            jnp.asarray(np.tile(sin, (1, reps)), dtype=F32))


def _layer(x2d, batch, seq_len, mods, l, p, latent, ctx, caches, states=None):
    z_ret, z_diff, z_hg = _inproj(x2d, p['norm_g'][l], mods, l, latent, p['w_in'][l], seq_len)
    lam_init = 0.8 - 0.6 * math.exp(-0.3 * l)
    if latent:
        k_ctx, v_ctx, s_rf, s_rb, s_hf, s_hb = ctx
        r = _retention(z_ret, p['log_g'][l], p['g256'], batch, seq_len, True, l,
                       rope=p['rope256'], states=(s_rf, s_rb))[0]
        a = _diff_attention(z_diff, p['lam'][l], p['gq'][l], p['gk'][l], p['g128'], batch,
                            seq_len, True, lam_init, l, rope=p['rope512'],
                            ctx=(k_ctx, v_ctx))[0]
        hgo = _hgrn(z_hg, p['lb'][l], p['hg_consts'], batch, seq_len, True, l,
                    states=(s_hf, s_hb))[0]
        extras = None
    else:
        prev = states if states is not None else (None,) * 4
        r, s_rf, s_rb = _retention(z_ret, p['log_g'][l], p['g256'], batch, seq_len, False, l,
                                   states=prev[0:2])
        a, kcache, vcache = _diff_attention(z_diff, p['lam'][l], p['gq'][l], p['gk'][l],
                                            p['g128'], batch, seq_len, False, lam_init, l,
                                            caches=caches)
        hgo, s_hf, s_hb = _hgrn(z_hg, p['lb'][l], p['hg_consts'], batch, seq_len, False, l,
                                states=prev[2:4])
        extras = ((kcache, vcache), (s_rf, s_rb, s_hf, s_hb))
    y = _outproj(x2d, r, a, hgo, mods, l, latent, p['w_out'][l], seq_len)
    return y, extras


def kernel(x_prompt, x_sample, c, c_ctx, cache_diff_k, cache_diff_v, state_ret_fwd,
           state_ret_bwd, state_hgrn_fwd, state_hgrn_bwd, norm_g, w_ada, b_ada, w_in,
           w_out, ret_decay_logit, diff_qn_g, diff_kn_g, diff_lambda, hgrn_lb_logit):
    batch, seq, _ = x_prompt.shape
    dec_batch, dec_seq, _ = x_sample.shape
    past_len = cache_diff_k.shape[2]

    assert 1 + dec_batch <= MOD_ROWS
    cv = jnp.zeros((MOD_ROWS, D_MODEL), F32).at[0].set(c_ctx).at[1:1 + dec_batch].set(c)
    mods = _ada(cv, w_ada, b_ada).reshape(DEPTH * 3 * MOD_ROWS, 1, D_MODEL)

    lb_all = jax.nn.softmax(hgrn_lb_logit.astype(F32), axis=0)
    lb_all = jnp.cumsum(lb_all, axis=0) - lb_all[0]
    lp = diff_lambda.astype(F32)
    lam_inits = jnp.asarray([0.8 - 0.6 * math.exp(-0.3 * l) for l in range(DEPTH)], F32)
    lam = (jnp.exp(jnp.sum(lp[:, 0] * lp[:, 1], axis=-1))
           - jnp.exp(jnp.sum(lp[:, 2] * lp[:, 3], axis=-1)) + lam_inits)
    p = {
        'norm_g': norm_g.reshape(DEPTH, 1, D_MODEL),
        'w_in': w_in.astype(BF16),
        'w_out': w_out.astype(BF16),
        'log_g': jax.nn.log_sigmoid(ret_decay_logit.astype(F32)),
        'lam': lam.reshape(DEPTH, 1),
        'gq': jnp.tile(diff_qn_g, (1, DIFF_W // DH)).reshape(DEPTH, 1, DIFF_W),
        'gk': jnp.tile(diff_kn_g, (1, DIFF_W // DH)).reshape(DEPTH, 1, DIFF_W),
        'lb': lb_all.reshape(DEPTH, 1, PACK_W),
        'g256': _block_ones(PACK_W),
        'g128': _block_ones(LANES),
        'hg_consts': _hgrn_consts(),
        'rope256': _rope_tables(dec_seq, PACK_W),
        'rope512': _rope_tables(dec_seq, DIFF_W),
    }

    y = x_prompt.reshape(batch * seq, D_MODEL)
    caches = None
    new_states = None
    for l in range(DEPTH):
        y, (caches, new_states) = _layer(y, batch, seq, mods, l, p, False, None, caches,
                                         new_states)
    y_prompt = y.reshape(batch, seq, D_MODEL)
    new_k = caches[0].reshape(batch, DEPTH, seq, HEADS, 2, DH)
    new_v = caches[1]

    y = x_sample.reshape(dec_batch * dec_seq, D_MODEL)
    rows = lambda s: s.reshape(dec_batch, DEPTH, PACK_W, DH)
    keys_t = jnp.transpose(cache_diff_k, (0, 1, 3, 4, 5, 2)).reshape(
        dec_batch, DEPTH, DIFF_W, past_len)
    ctx = (keys_t, cache_diff_v, rows(state_ret_fwd), rows(state_ret_bwd),
           rows(state_hgrn_fwd), rows(state_hgrn_bwd))
    for l in range(DEPTH):
        y, _ = _layer(y, dec_batch, dec_seq, mods, l, p, True, ctx, None)
    y_sample = y.reshape(dec_batch, dec_seq, D_MODEL)

    return (y_prompt, y_sample, new_k, new_v, *new_states)
```

```python
import functools
import math
from typing import Callable, NamedTuple

import numpy as np
import jax
import jax.numpy as jnp
from jax import lax
from jax.experimental import pallas as pl
from jax.experimental.pallas import tpu as pltpu

F32 = jnp.float32
BF16 = jnp.bfloat16

D_MODEL = 1024
DEPTH = 2
GRID_W = 64
HEADS = 4
DH = 64
PACK_W = HEADS * DH
DIFF_W = 512
DIFF_DV = 128
RET_COLS = 4 * PACK_W
DIFF_COLS = 4 * DIFF_W
HG_COLS = 5 * PACK_W
IN_W = RET_COLS + DIFF_COLS + HG_COLS
MIX_W = PACK_W + DIFF_W + PACK_W
ROPE_BASE = 10000.0
ROPE_PAIR = 16
EPS = 1e-6
LOG2E = 1.4426950408889634
NEG_BIG = -1e30

RET_C = 256
HG_C = 64
HG_SUB = 4
HG_LEVELS = (4, 8, 16, 32)
SUBLANES = 8
LANES = 128
ROW_BLK = 256
TQ = 512
TM_IN = 1024
TM_OUT = 512

VMEM_LIMIT = 56 * 1024 * 1024


def _params(n_axes):
    return pltpu.CompilerParams(
        dimension_semantics=("arbitrary",) * n_axes, vmem_limit_bytes=VMEM_LIMIT)


class _Plan(NamedTuple):
    kernel: Callable
    in_specs: list
    args: list
    out_specs: list
    out_shape: list
    scratch: list


def _launch(plans, grid, name):
    counts = [(len(p.in_specs), len(p.out_specs), len(p.scratch)) for p in plans]
    n_in = sum(c[0] for c in counts)
    n_out = sum(c[1] for c in counts)

    def body(*refs):
        i0, o0, s0 = 0, n_in, n_in + n_out
        for plan, (ni, no, ns) in zip(plans, counts):
            plan.kernel(*refs[i0:i0 + ni], *refs[o0:o0 + no], *refs[s0:s0 + ns])
            i0, o0, s0 = i0 + ni, o0 + no, s0 + ns

    outs = pl.pallas_call(
        body,
        grid=grid,
        in_specs=[s for p in plans for s in p.in_specs],
        out_specs=[s for p in plans for s in p.out_specs],
        out_shape=[s for p in plans for s in p.out_shape],
        scratch_shapes=[s for p in plans for s in p.scratch],
        compiler_params=_params(len(grid)),
        name=name,
    )(*[a for p in plans for a in p.args])
    results, o0 = [], 0
    for _, no, _ in counts:
        results.append(list(outs[o0:o0 + no]))
        o0 += no
    return results if len(plans) > 1 else results[0]


def _dot(a, b):
    return jnp.dot(a, b, preferred_element_type=F32)


def _dot_nt(a, b):
    return lax.dot_general(a, b, (((1,), (1,)), ((), ())), preferred_element_type=F32)


def _dot_tn(a, b):
    return lax.dot_general(a, b, (((0,), (0,)), ((), ())), preferred_element_type=F32)


def _silu(x):
    return x * jax.nn.sigmoid(x)


def _group_rms(x, g, width):
    n, w = x.shape
    x2 = x * x
    cols = range(0, w, LANES)
    stacked = jnp.concatenate([x2[:, j:j + LANES] for j in cols], axis=0)
    hi = stacked.astype(BF16)
    lo = (stacked - hi.astype(F32)).astype(BF16)
    sums = _dot(hi, g) + _dot(lo, g)
    ms = jnp.concatenate([sums[i * n:(i + 1) * n] for i in range(len(cols))], axis=1)
    return x * lax.rsqrt(ms * (1.0 / width) + EPS)


def _rope(x, cos, sin):
    w = x.shape[-1]
    lane = lax.broadcasted_iota(jnp.int32, (1, w), 1)
    first = (lane & (2 * ROPE_PAIR - 1)) < ROPE_PAIR
    swapped = jnp.where(first, pltpu.roll(x, w - ROPE_PAIR, 1), pltpu.roll(x, ROPE_PAIR, 1))
    return x * cos + swapped * sin


def _initial_state(s_ref, gmask):
    s = s_ref[0, 0]
    return (jnp.concatenate([s] * HEADS, axis=1) * gmask).T


def _lane_head(w=PACK_W):
    return lax.broadcasted_iota(jnp.int32, (1, w), 1) // DH


def _row_blocks(n_rows, body):
    n_blk = n_rows // ROW_BLK
    if n_blk == 1:
        body(0)
    else:
        def step(i, carry):
            body(pl.multiple_of(i * ROW_BLK, ROW_BLK))
            return carry
        lax.fori_loop(0, n_blk, step, 0)


def _ada_kernel(cv_ref, w_ref, b_ref, o_ref):
    cv = cv_ref[...]
    o_ref[0, 0] = _dot(_silu(cv), w_ref[0]) + b_ref[0]


def _ada(cv, w_ada, b_ada):
    tn = 1024
    n = 3 * D_MODEL
    return pl.pallas_call(
        _ada_kernel,
        grid=(DEPTH, n // tn),
        in_specs=[
            pl.BlockSpec((8, D_MODEL), lambda l, j: (0, 0)),
            pl.BlockSpec((1, D_MODEL, tn), lambda l, j: (l, 0, j)),
            pl.BlockSpec((1, 1, tn), lambda l, j: (l, 0, j)),
        ],
        out_specs=pl.BlockSpec((1, 1, 8, tn), lambda l, j: (l, j, 0, 0)),
        out_shape=jax.ShapeDtypeStruct((DEPTH, n // tn, 8, tn), F32),
        compiler_params=_params(2),
        name="ada",
    )(cv, w_ada, b_ada.reshape(DEPTH, 1, n))


def _inproj_kernel(x_ref, g_ref, sc_ref, sh_ref, w_ref, zr_ref, zd_ref, zh_ref):
    x = x_ref[...]
    ms = jnp.mean(x * x, axis=-1, keepdims=True)
    h = x * lax.rsqrt(ms + EPS) * g_ref[...]
    h = h * (1.0 + sc_ref[0]) + sh_ref[0]
    z = _dot(h.astype(BF16), w_ref[...])
    zr_ref[...] = z[:, :RET_COLS].astype(zr_ref.dtype)
    zd_ref[...] = z[:, RET_COLS:RET_COLS + DIFF_COLS].astype(zd_ref.dtype)
    zh_ref[...] = z[:, RET_COLS + DIFF_COLS:]


MOD_SHIFT, MOD_SCALE, MOD_GATE = 0, 1, 2
MOD_ROWS = 8


def _mod_spec(layer, which, per_batch, tiles_per_seq):
    base = (layer * 3 + which) * MOD_ROWS
    if per_batch:
        return pl.BlockSpec((1, 1, D_MODEL), lambda i: (base + 1 + i // tiles_per_seq, 0, 0))
    return pl.BlockSpec((1, 1, D_MODEL), lambda i: (base, 0, 0))


def _inproj(x2d, g, mods, layer, per_batch, w_bf, seq_len):
    m = x2d.shape[0]
    tiles_per_seq = seq_len // TM_IN
    return pl.pallas_call(
        _inproj_kernel,
        grid=(m // TM_IN,),
        in_specs=[
            pl.BlockSpec((TM_IN, D_MODEL), lambda i: (i, 0)),
            pl.BlockSpec((1, D_MODEL), lambda i: (0, 0)),
            _mod_spec(layer, MOD_SCALE, per_batch, tiles_per_seq),
            _mod_spec(layer, MOD_SHIFT, per_batch, tiles_per_seq),
            pl.BlockSpec((D_MODEL, IN_W), lambda i: (0, 0)),
        ],
        out_specs=[
            pl.BlockSpec((TM_IN, RET_COLS), lambda i: (i, 0)),
            pl.BlockSpec((TM_IN, DIFF_COLS), lambda i: (i, 0)),
            pl.BlockSpec((TM_IN, HG_COLS), lambda i: (i, 0)),
        ],
        out_shape=[
            jax.ShapeDtypeStruct((m, RET_COLS), BF16),
            jax.ShapeDtypeStruct((m, DIFF_COLS), BF16),
            jax.ShapeDtypeStruct((m, HG_COLS), F32),
        ],
        compiler_params=_params(1),
        name="inproj",
    )(x2d, g, mods, mods, w_bf)


def _outproj_kernel(x_ref, r_ref, d_ref, h_ref, gate_ref, w_ref, o_ref):
    mixed = jnp.concatenate([r_ref[...], d_ref[...], h_ref[...]], axis=1)
    o_ref[...] = x_ref[...] + gate_ref[0] * _dot(mixed, w_ref[...])


def _outproj(x2d, r, d, h, mods, layer, per_batch, w_bf, seq_len):
    m = x2d.shape[0]
    tiles_per_seq = max(seq_len // TM_OUT, 1)
    tm = TM_OUT if (not per_batch or seq_len % TM_OUT == 0) else seq_len
    return pl.pallas_call(
        _outproj_kernel,
        grid=(m // tm,),
        in_specs=[
            pl.BlockSpec((tm, D_MODEL), lambda i: (i, 0)),
            pl.BlockSpec((tm, PACK_W), lambda i: (i, 0)),
            pl.BlockSpec((tm, DIFF_W), lambda i: (i, 0)),
            pl.BlockSpec((tm, PACK_W), lambda i: (i, 0)),
            _mod_spec(layer, MOD_GATE, per_batch, tiles_per_seq),
            pl.BlockSpec((MIX_W, D_MODEL), lambda i: (0, 0)),
        ],
        out_specs=pl.BlockSpec((tm, D_MODEL), lambda i: (i, 0)),
        out_shape=jax.ShapeDtypeStruct((m, D_MODEL), F32),
        compiler_params=_params(1),
        name="outproj",
    )(x2d, r, d, h, mods, w_bf)


def _ret_kernel(*refs, seq_len, latent, layer):
    if latent:
        (lg_ref, z_ref, g_ref, cos_ref, sin_ref, s0f_ref, s0b_ref,
         out_ref, mask_scr, o_scr) = refs
    else:
        lg_ref, z_ref, g_ref = refs[:3]
        prev_refs = refs[3:-5]
        out_ref, sf_ref, sb_ref, mask_scr, o_scr = refs[-5:]
    C = RET_C
    n_chunks = seq_len // C
    lh = _lane_head()

    def lane_vec(direction):
        v = jnp.zeros((1, PACK_W), F32)
        for h in range(HEADS):
            v = jnp.where(lh == h, lg_ref[direction, h], v)
        return v

    lgf, lgb = lane_vec(0), lane_vec(1)
    tau = lax.broadcasted_iota(jnp.int32, (C, 1), 0).astype(F32)
    qdec_f = jnp.exp((tau + 1.0) * lgf)
    qdec_b = jnp.exp((C - tau) * lgb)
    kdec_f = jnp.exp((C - 1.0 - tau) * lgf)
    kdec_b = jnp.exp(tau * lgb)
    sdec_f = jnp.exp(float(C) * lgf)
    sdec_b = jnp.exp(float(C) * lgb)
    gmask = g_ref[...].astype(F32)

    @pl.when(pl.program_id(0) == 0)
    def _():
        dd = (lax.broadcasted_iota(jnp.int32, (C, C), 0)
              - lax.broadcasted_iota(jnp.int32, (C, C), 1)).astype(F32)
        for h in range(HEADS):
            mf = jnp.where(dd >= 0, jnp.exp(jnp.maximum(dd, 0.0) * lg_ref[0, h]), 0.0)
            mb = jnp.where(dd <= 0, jnp.exp(jnp.maximum(-dd, 0.0) * lg_ref[1, h]), 0.0)
            mask_scr[h * C:(h + 1) * C, :] = mf + mb

    def load(r0):
        q = z_ref[pl.ds(r0, C), 0:PACK_W].astype(F32)
        k = z_ref[pl.ds(r0, C), PACK_W:2 * PACK_W].astype(F32) * (DH ** -0.5)
        v = z_ref[pl.ds(r0, C), 2 * PACK_W:3 * PACK_W].astype(F32)
        if latent:
            cos = cos_ref[pl.ds(r0, C), :]
            sin = sin_ref[pl.ds(r0, C), :]
            q = _rope(q, cos, sin)
            k = _rope(k, cos, sin)
        return q, k, v

    def intra(q, k, v):
        qb = q.astype(BF16)
        q4 = jnp.concatenate([jnp.where(lh == h, qb, jnp.zeros_like(qb)) for h in range(HEADS)],
                             axis=0)
        a4 = _dot_nt(q4, k.astype(BF16)) * mask_scr[...]
        o4 = _dot(a4.astype(BF16), v.astype(BF16))
        o = jnp.zeros((C, PACK_W), F32)
        for h in range(HEADS):
            o = o + jnp.where(lh == h, o4[h * C:(h + 1) * C], 0.0)
        return o

    def finish(o, r0):
        gate = z_ref[pl.ds(r0, C), 3 * PACK_W:4 * PACK_W].astype(F32)
        y = _group_rms(o, g_ref[0:LANES, 0:LANES], DH) * _silu(gate)
        out_ref[pl.ds(r0, C), :] = y.astype(out_ref.dtype)

    def state_step(st, k, v, kdec, sdec):
        upd = _dot_tn(v.astype(BF16), (k * kdec).astype(BF16))
        return (st * sdec + upd) * gmask

    if not latent:
        q, k, v = load(0)
        finish(intra(q, k, v), 0)
        vb = v.astype(BF16)
        for ref, kdec in ((sf_ref, kdec_f), (sb_ref, kdec_b)):
            s = _dot_tn((k * kdec).astype(BF16), vb) * gmask
            for h in range(HEADS):
                ref[0, layer, h] = s[h * DH:(h + 1) * DH, h * DH:(h + 1) * DH]
        for ref, prev in zip((sf_ref, sb_ref), prev_refs):
            ref[0, 0:layer] = prev[0]
        return

    def bwd_body(i, st):
        r0 = pl.multiple_of((n_chunks - 1 - i) * C, C)
        q, k, v = load(r0)
        o_scr[pl.ds(r0, C), :] = _dot_nt((q * qdec_b).astype(BF16), st.astype(BF16))
        return state_step(st, k, v, kdec_b, sdec_b)

    lax.fori_loop(0, n_chunks, bwd_body, _initial_state(s0b_ref, gmask), unroll=2)

    def fwd_body(i, st):
        r0 = pl.multiple_of(i * C, C)
        q, k, v = load(r0)
        o = intra(q, k, v) + o_scr[pl.ds(r0, C), :]
        o = o + _dot_nt((q * qdec_f).astype(BF16), st.astype(BF16))
        finish(o, r0)
        return state_step(st, k, v, kdec_f, sdec_f)

    lax.fori_loop(0, n_chunks, fwd_body, _initial_state(s0f_ref, gmask), unroll=2)


def _state_specs(batch, layer, prev):
    spec = lambda n: pl.BlockSpec((1, n, HEADS, DH, DH), lambda b: (b, 0, 0, 0, 0))
    out_specs = [spec(layer + 1)] * 2
    out_shape = [jax.ShapeDtypeStruct((batch, layer + 1, HEADS, DH, DH), F32)] * 2
    in_specs = [spec(layer)] * 2 if layer > 0 else []
    return in_specs, (list(prev) if layer > 0 else []), out_specs, out_shape


def _retention(z_ret, log_g, g256, batch, seq_len, latent, layer, rope=None, states=None,
               plan_only=False):
    m = batch * seq_len
    row = lambda b: (b, 0)
    const2 = lambda b: (0, 0)
    in_specs = [
        pl.BlockSpec(memory_space=pltpu.SMEM),
        pl.BlockSpec((seq_len, RET_COLS), row),
        pl.BlockSpec((PACK_W, PACK_W), const2),
    ]
    args = [log_g, z_ret, g256]
    out_specs = [pl.BlockSpec((seq_len, PACK_W), row)]
    out_shape = [jax.ShapeDtypeStruct((m, PACK_W), BF16)]
    if latent:
        state_spec = pl.BlockSpec((1, 1, PACK_W, DH), lambda b: (b, layer, 0, 0))
        in_specs += [pl.BlockSpec((seq_len, PACK_W), const2)] * 2 + [state_spec] * 2
        args += [rope[0], rope[1], states[0], states[1]]
    else:
        s_in, s_args, s_out, s_shape = _state_specs(batch, layer, states)
        in_specs += s_in
        args += s_args
        out_specs += s_out
        out_shape += s_shape
    plan = _Plan(
        kernel=functools.partial(_ret_kernel, seq_len=seq_len, latent=latent, layer=layer),
        in_specs=in_specs, args=args, out_specs=out_specs, out_shape=out_shape,
        scratch=[
            pltpu.VMEM((HEADS * RET_C, RET_C), F32),
            pltpu.VMEM((seq_len, PACK_W), F32),
        ])
    if plan_only:
        return plan
    return _launch([plan], (batch,), "ret_latent" if latent else "ret_ctx")


def _attn_kernel(*refs, seq_len, past_len, latent, lam_init, layer, n_qblocks):
    if latent:
        (lam_ref, q_ref, k_ref, v_ref, gate_ref, gq_ref, gk_ref, g_ref,
         cosq_ref, sinq_ref, cosk_ref, sink_ref, kc_ref, vc_ref,
         out_ref, k_scr, v_scr) = refs
        prev_refs = ()
    else:
        (lam_ref, q_ref, k_ref, v_ref, gate_ref, gq_ref, gk_ref, g_ref) = refs[:8]
        prev_refs = refs[8:-5]
        out_ref, kn_ref, vn_ref, k_scr, v_scr = refs[-5:]
    ones = jnp.ones((ROW_BLK, DIFF_DV), BF16)

    def put_values(sl, head_values):
        for h in range(HEADS):
            v_scr[sl, 2 * h * DIFF_DV:(2 * h + 1) * DIFF_DV] = head_values(h).astype(BF16)
            v_scr[sl, (2 * h + 1) * DIFF_DV:(2 * h + 2) * DIFF_DV] = ones

    def prepare_keys_values():
        for r in range(seq_len // ROW_BLK):
            sl = pl.ds(r * ROW_BLK, ROW_BLK)
            kn = _group_rms(k_ref[sl, :].astype(F32), g_ref[...], DH) * gk_ref[...]
            v = v_ref[sl, :].astype(F32)
            if latent:
                kn = _rope(kn, cosk_ref[sl, :], sink_ref[sl, :])
            else:
                kn_ref[0, layer, sl, :] = kn
                for h in range(HEADS):
                    vn_ref[0, layer, sl, h, :] = v[:, h * DIFF_DV:(h + 1) * DIFF_DV]
            k_scr[sl, :] = kn.astype(BF16)
            put_values(sl, lambda h: v[:, h * DIFF_DV:(h + 1) * DIFF_DV])
        if prev_refs:
            kn_ref[0, 0:layer] = prev_refs[0][0]
            vn_ref[0, 0:layer] = prev_refs[1][0]
        if latent:
            for r in range(past_len // ROW_BLK):
                src = pl.ds(r * ROW_BLK, ROW_BLK)
                dst = pl.ds(seq_len + r * ROW_BLK, ROW_BLK)
                k_scr[dst, :] = kc_ref[0, 0, :, src].T.astype(BF16)
                put_values(dst, lambda h: vc_ref[0, 0, src, h, :])

    if n_qblocks == 1:
        prepare_keys_values()
    else:
        pl.when(pl.program_id(1) == 0)(prepare_keys_values)

    lam = lam_ref[0]
    qn = _group_rms(q_ref[...].astype(F32), g_ref[...], DH) * gq_ref[...]
    if latent:
        qn = _rope(qn, cosq_ref[...], sinq_ref[...])
    qn = qn * (DH ** -0.5 * LOG2E)
    lane = lax.broadcasted_iota(jnp.int32, (1, 2 * DH), 1)
    for h in range(HEADS):
        hs = slice(h * 2 * DH, (h + 1) * 2 * DH)
        vx = v_scr[:, 2 * h * DIFF_DV:(2 * h + 2) * DIFF_DV]
        qh = qn[:, hs]
        kh = k_scr[:, hs]
        parts = []
        for first_map in (True, False):
            qm = jnp.where((lane < DH) == first_map, qh, 0.0).astype(BF16)
            s = _dot_nt(qm, kh)
            e = jnp.exp2(s - jnp.max(s, axis=-1, keepdims=True)).astype(BF16)
            ox = _dot(e, vx)
            parts.append(ox[:, :DIFF_DV] / ox[:, DIFF_DV:])
        o = parts[0] - lam * parts[1]
        ms = jnp.mean(o * o, axis=-1, keepdims=True)
        y = o * lax.rsqrt(ms + EPS) * (1.0 - lam_init) * _silu(gate_ref[:, hs].astype(F32))
        out_ref[:, hs] = y.astype(out_ref.dtype)


def _diff_attention(z_diff, lam, gq, gk, g128, batch, seq_len, latent, lam_init, layer,
                    rope=None, ctx=None, caches=None, plan_only=False):
    m = batch * seq_len
    tq = min(TQ, seq_len)
    nq = seq_len // tq
    past_len = ctx[1].shape[2] if latent else 0
    qmap = lambda col: (lambda b, *i: (b * nq + (i[0] if i else 0), col))
    kvmap = lambda col: (lambda b, *i: (b, col))
    const2 = lambda b, *i: (0, 0)
    in_specs = [
        pl.BlockSpec(memory_space=pltpu.SMEM),
        pl.BlockSpec((tq, DIFF_W), qmap(0)),
        pl.BlockSpec((seq_len, DIFF_W), kvmap(1)),
        pl.BlockSpec((seq_len, DIFF_W), kvmap(2)),
        pl.BlockSpec((tq, DIFF_W), qmap(3)),
        pl.BlockSpec((1, DIFF_W), const2),
        pl.BlockSpec((1, DIFF_W), const2),
        pl.BlockSpec((LANES, LANES), const2),
    ]
    args = [lam, z_diff, z_diff, z_diff, z_diff, gq, gk, g128]
    out_specs = [pl.BlockSpec((tq, DIFF_W), qmap(0))]
    out_shape = [jax.ShapeDtypeStruct((m, DIFF_W), BF16)]
    if latent:
        in_specs += [
            pl.BlockSpec((tq, DIFF_W), lambda b, i: (i, 0)),
            pl.BlockSpec((tq, DIFF_W), lambda b, i: (i, 0)),
            pl.BlockSpec((seq_len, DIFF_W), const2),
            pl.BlockSpec((seq_len, DIFF_W), const2),
            pl.BlockSpec((1, 1, DIFF_W, past_len), lambda b, i: (b, layer, 0, 0)),
            pl.BlockSpec((1, 1, past_len, HEADS, DIFF_DV), lambda b, i: (b, layer, 0, 0, 0)),
        ]
        args += [rope[0], rope[1], rope[0], rope[1], ctx[0], ctx[1]]
    else:
        cache_spec = lambda n: pl.BlockSpec((1, n, seq_len, DIFF_W), lambda b, *i: (b, 0, 0, 0))
        vcache_spec = lambda n: pl.BlockSpec((1, n, seq_len, HEADS, DIFF_DV),
                                             lambda b, *i: (b, 0, 0, 0, 0))
        out_specs += [cache_spec(layer + 1), vcache_spec(layer + 1)]
        out_shape += [jax.ShapeDtypeStruct((batch, layer + 1, seq_len, DIFF_W), F32),
                      jax.ShapeDtypeStruct((batch, layer + 1, seq_len, HEADS, DIFF_DV), F32)]
        if layer > 0:
            in_specs += [cache_spec(layer), vcache_spec(layer)]
            args += list(caches)
    plan = _Plan(
        kernel=functools.partial(_attn_kernel, seq_len=seq_len, past_len=past_len,
                                 latent=latent, lam_init=lam_init, layer=layer, n_qblocks=nq),
        in_specs=in_specs, args=args, out_specs=out_specs, out_shape=out_shape,
        scratch=[
            pltpu.VMEM((seq_len + past_len, DIFF_W), BF16),
            pltpu.VMEM((seq_len + past_len, 2 * DIFF_W), BF16),
        ])
    if plan_only:
        assert nq == 1
        return plan
    return _launch([plan], (batch, nq), "attn_latent" if latent else "attn_ctx")


def _hgrn_kernel(*refs, seq_len, latent, layer):
    if latent:
        (z_ref, lb_ref, tri_ref, g_ref, lm_ref, dm_ref, s0f_ref, s0b_ref,
         out_ref, qi_scr, ks_scr, dec_scr, o_scr, oi_scr) = refs
    else:
        z_ref, lb_ref, tri_ref, g_ref, lm_ref, dm_ref = refs[:6]
        prev_refs = refs[6:-8]
        out_ref, sf_ref, sb_ref, qi_scr, ks_scr, dec_scr, o_scr, oi_scr = refs[-8:]
    C, SUB = HG_C, HG_SUB
    n_chunks = seq_len // C
    chunks_per_blk = ROW_BLK // C
    lh = _lane_head()
    sub_row = lax.broadcasted_iota(jnp.int32, (ROW_BLK, 1), 0) & (SUB - 1)
    chunk_row = lax.broadcasted_iota(jnp.int32, (C, 1), 0)

    def tile_roll(x, shift):
        return jnp.concatenate(
            [pltpu.roll(x[i:i + SUBLANES], shift, 0) for i in range(0, ROW_BLK, SUBLANES)], axis=0)

    def head_rep(x):
        return jnp.concatenate([jnp.where(lh == h, x, jnp.zeros_like(x)) for h in range(HEADS)],
                               axis=0)

    def intra(blk):
        start = blk * ROW_BLK if isinstance(blk, int) else pl.multiple_of(blk * ROW_BLK, ROW_BLK)
        rows = pl.ds(start, ROW_BLK)
        lb = lb_ref[...]
        q = z_ref[rows, 0:PACK_W] * (DH ** -0.5)
        v = z_ref[rows, 3 * PACK_W:4 * PACK_W]
        v_reps = [head_rep(v[c * C:(c + 1) * C].astype(BF16)) for c in range(chunks_per_blk)]
        kks, bcs, a_nears = [], [], []
        for d in range(2):
            f = lb + (1.0 - lb) * jax.nn.sigmoid(z_ref[rows, (1 + d) * PACK_W:(2 + d) * PACK_W])
            kks.append(1.0 - f)
            l2 = jnp.log(f) * LOG2E
            hi = l2.astype(BF16)
            r1 = l2 - hi.astype(F32)
            mid = r1.astype(BF16)
            lo = (r1 - mid.astype(F32)).astype(BF16)
            cs = _dot(tri_ref[d], jnp.concatenate([hi, mid, lo], axis=1))
            bcs.append(cs[:, 0:PACK_W] + cs[:, PACK_W:2 * PACK_W] + cs[:, 2 * PACK_W:])

        for d in range(2):
            bwd = d == 1
            kk, bc = kks[d], bcs[d]
            ps = [(q * kk).astype(BF16)]
            for dist in range(1, SUB):
                shift = SUBLANES - dist if bwd else dist
                valid = (sub_row < SUB - dist) if bwd else (sub_row >= dist)
                arg = jnp.where(valid, bc - tile_roll(bc, shift), NEG_BIG)
                ps.append((q * tile_roll(kk, shift) * jnp.exp2(arg)).astype(BF16))
            r = _dot(jnp.concatenate(ps, axis=0), g_ref[...]).astype(BF16)
            a_near = r[0:ROW_BLK] * dm_ref[d, 0]
            for dist in range(1, SUB):
                a_near = a_near + r[dist * ROW_BLK:(dist + 1) * ROW_BLK] * dm_ref[d, dist]
            a_nears.append(a_near)

        for c in range(chunks_per_blk):
            for d in range(2):
                bwd = d == 1
                kk, bc, a_near = kks[d], bcs[d], a_nears[d]
                cs_ = slice(c * C, (c + 1) * C)
                qc, kc, bcc = q[cs_], kk[cs_], bc[cs_]
                last = bcc[0:1] if bwd else bcc[C - 1:C]
                a = a_near[cs_]
                for li, m in enumerate(HG_LEVELS):
                    pieces = []
                    for b2 in range(C // (2 * m)):
                        rr = 2 * m * b2 + (m if bwd else m - 1)
                        pieces.append(jnp.broadcast_to(bcc[rr:rr + 1], (2 * m, PACK_W)))
                    anchor = jnp.concatenate(pieces, axis=0) if len(pieces) > 1 else pieces[0]
                    upper = (chunk_row & (2 * m - 1)) >= m
                    later = jnp.logical_not(upper) if bwd else upper
                    e = jnp.exp2((bcc - anchor) * jnp.where(later, 1.0, -1.0))
                    a = a + _dot_nt((qc * e).astype(BF16),
                                    head_rep((kc * e).astype(BF16))).astype(BF16) * lm_ref[d, li]
                crow = pl.ds(start + c * C, C)
                o_scr[d, crow, :] = _dot(a, v_reps[c])
                qi_scr[d, crow, :] = (qc * jnp.exp2(bcc)).astype(BF16)
                ks_scr[d, crow, :] = (kc * jnp.exp2(last - bcc)).astype(BF16)
                slot = (blk * chunks_per_blk + c) * SUBLANES
                if not isinstance(slot, int):
                    slot = pl.multiple_of(slot, SUBLANES)
                dec_scr[d, pl.ds(slot, SUBLANES), :] = jnp.broadcast_to(
                    jnp.exp2(last), (SUBLANES, PACK_W))

        if not latent:
            vb = v.astype(BF16)
            for d, ref in enumerate((sf_ref, sb_ref)):
                bwd = d == 1
                sweep = range(chunks_per_blk - 1, -1, -1) if bwd else range(chunks_per_blk)
                tail = jnp.zeros((1, PACK_W), F32)
                decayed = [None] * chunks_per_blk
                for c in reversed(sweep):
                    cs_ = slice(c * C, (c + 1) * C)
                    bcc = bcs[d][cs_]
                    total = tail + (bcc[0:1] if bwd else bcc[C - 1:C])
                    decayed[c] = (kks[d][cs_] * jnp.exp2(total - bcc)).astype(BF16)
                    tail = total
                s = _dot_tn(jnp.concatenate(decayed, axis=0), vb) * g_ref[...].astype(F32)
                for h in range(HEADS):
                    ref[0, layer, h] = s[h * DH:(h + 1) * DH, h * DH:(h + 1) * DH]
                if prev_refs:
                    ref[0, 0:layer] = prev_refs[d][0]

    n_blk = seq_len // ROW_BLK
    assert latent or n_blk == 1
    if n_blk == 1:
        intra(0)
    else:
        def intra_step(i, carry):
            intra(i)
            return carry
        lax.fori_loop(0, n_blk, intra_step, 0)

    gmask = g_ref[...].astype(F32)
    if latent:
        states = (_initial_state(s0f_ref, gmask), _initial_state(s0b_ref, gmask))
    else:
        states = (jnp.zeros((PACK_W, PACK_W), F32),) * 2

    def body(i, states):
        new_states = []
        for d in range(2):
            c = (n_chunks - 1 - i) if d == 1 else i
            rows = pl.ds(pl.multiple_of(c * C, C), C)
            st = states[d]
            oi_scr[d, rows, :] = _dot_nt(qi_scr[d, rows, :], st.astype(BF16))
            dec = dec_scr[d, pl.ds(pl.multiple_of(c * SUBLANES, SUBLANES), 1), :]
            vb = z_ref[rows, 3 * PACK_W:4 * PACK_W].astype(BF16)
            new_states.append(st * dec + _dot_tn(vb, ks_scr[d, rows, :]) * gmask)
        return tuple(new_states)

    lax.fori_loop(0, n_chunks, body, states, unroll=4)

    def finish(start):
        rows = pl.ds(start, ROW_BLK)
        tot = (o_scr[0, rows, :] + oi_scr[0, rows, :]) + (o_scr[1, rows, :] + oi_scr[1, rows, :])
        gate = z_ref[rows, 4 * PACK_W:5 * PACK_W]
        y = _group_rms(tot, g_ref[0:LANES, 0:LANES], DH) * _silu(gate)
        out_ref[rows, :] = y.astype(out_ref.dtype)

    _row_blocks(seq_len, finish)


def _hgrn(z_hg, lb, consts, batch, seq_len, latent, layer, states=None, plan_only=False):
    m = batch * seq_len
    tri, g256, lm, dm = consts
    row = lambda b: (b, 0)
    in_specs = [
        pl.BlockSpec((seq_len, HG_COLS), row),
        pl.BlockSpec((1, PACK_W), lambda b: (0, 0)),
        pl.BlockSpec(tri.shape, lambda b: (0, 0, 0)),
        pl.BlockSpec((PACK_W, PACK_W), lambda b: (0, 0)),
        pl.BlockSpec(lm.shape, lambda b: (0, 0, 0, 0)),
        pl.BlockSpec(dm.shape, lambda b: (0, 0, 0, 0)),
    ]
    args = [z_hg, lb, tri, g256, lm, dm]
    out_specs = [pl.BlockSpec((seq_len, PACK_W), row)]
    out_shape = [jax.ShapeDtypeStruct((m, PACK_W), BF16)]
    if latent:
        in_specs += [pl.BlockSpec((1, 1, PACK_W, DH), lambda b: (b, layer, 0, 0))] * 2
        args += [states[0], states[1]]
    else:
        s_in, s_args, s_out, s_shape = _state_specs(batch, layer, states)
        in_specs += s_in
        args += s_args
        out_specs += s_out
        out_shape += s_shape
    plan = _Plan(
        kernel=functools.partial(_hgrn_kernel, seq_len=seq_len, latent=latent, layer=layer),
        in_specs=in_specs, args=args, out_specs=out_specs, out_shape=out_shape,
        scratch=[
            pltpu.VMEM((2, seq_len, PACK_W), BF16),
            pltpu.VMEM((2, seq_len, PACK_W), BF16),
            pltpu.VMEM((2, seq_len // HG_C * SUBLANES, PACK_W), F32),
            pltpu.VMEM((2, seq_len, PACK_W), F32),
            pltpu.VMEM((2, seq_len, PACK_W), F32),
        ])
    if plan_only:
        return plan
    return _launch([plan], (batch,), "hgrn_latent" if latent else "hgrn_ctx")


def _block_ones(n):
    idx = np.arange(n) // DH
    return jnp.asarray((idx[:, None] == idx[None, :]).astype(np.float32), dtype=BF16)


def _hgrn_consts():
    C = HG_C
    t = np.arange(ROW_BLK)
    same_chunk = (t[:, None] // C) == (t[None, :] // C)
    tri_f = same_chunk & (t[None, :] <= t[:, None])
    tri_b = same_chunk & (t[None, :] >= t[:, None])
    tc = np.arange(C)
    lms = []
    for fwd in (True, False):
        per_level = []
        for m in HG_LEVELS:
            same = (tc[:, None] // (2 * m)) == (tc[None, :] // (2 * m))
            upper = (tc % (2 * m)) >= m
            later, earlier = (upper, ~upper) if fwd else (~upper, upper)
            per_level.append(np.tile(same & later[:, None] & earlier[None, :], (1, HEADS)))
        lms.append(np.stack(per_level))
    dms = [[np.tile(tc[None, :] == tc[:, None] + sign * dist, (ROW_BLK // C, HEADS))
            for dist in range(HG_SUB)] for sign in (-1, 1)]
    return (jnp.asarray(np.stack([tri_f, tri_b]).astype(np.float32), dtype=BF16),
            _block_ones(PACK_W),
            jnp.asarray(np.stack(lms).astype(np.float32), dtype=BF16),
            jnp.asarray(np.stack(dms).astype(np.float32), dtype=BF16))


def _rope_tables(seq_len, width):
    t = np.arange(seq_len)
    pos = np.stack([t // GRID_W, t % GRID_W], axis=1).astype(np.float32)
    j = np.arange(DH)
    axis = j // (2 * ROPE_PAIR)
    jj = j % (2 * ROPE_PAIR)
    inv = (ROPE_BASE ** (-(jj % ROPE_PAIR).astype(np.float64) / ROPE_PAIR)).astype(np.float32)
    ang = (pos[:, axis] * inv[None, :]).astype(np.float64)
    cos = np.cos(ang)
    sin = np.where(jj < ROPE_PAIR, -np.sin(ang), np.sin(ang))
    reps = width // DH
    return (jnp.asarray(np.tile(cos, (1, reps)), dtype=F32),
            jnp.asarray(np.tile(sin, (1, reps)), dtype=F32))


def _layer(x2d, batch, seq_len, mods, l, p, latent, ctx, caches, states=None):
    z_ret, z_diff, z_hg = _inproj(x2d, p['norm_g'][l], mods, l, latent, p['w_in'][l], seq_len)
    lam_init = 0.8 - 0.6 * math.exp(-0.3 * l)
    if latent:
        k_ctx, v_ctx, s_rf, s_rb, s_hf, s_hb = ctx
        r = _retention(z_ret, p['log_g'][l], p['g256'], batch, seq_len, True, l,
                       rope=p['rope256'], states=(s_rf, s_rb))[0]
        a = _diff_attention(z_diff, p['lam'][l], p['gq'][l], p['gk'][l], p['g128'], batch,
                            seq_len, True, lam_init, l, rope=p['rope512'],
                            ctx=(k_ctx, v_ctx))[0]
        hgo = _hgrn(z_hg, p['lb'][l], p['hg_consts'], batch, seq_len, True, l,
                    states=(s_hf, s_hb))[0]
        extras = None
    else:
        prev = states if states is not None else (None,) * 4
        plans = [
            _retention(z_ret, p['log_g'][l], p['g256'], batch, seq_len, False, l,
                       states=prev[0:2], plan_only=True),
            _hgrn(z_hg, p['lb'][l], p['hg_consts'], batch, seq_len, False, l,
                  states=prev[2:4], plan_only=True),
            _diff_attention(z_diff, p['lam'][l], p['gq'][l], p['gk'][l], p['g128'], batch,
                            seq_len, False, lam_init, l, caches=caches, plan_only=True),
        ]
        (r, s_rf, s_rb), (hgo, s_hf, s_hb), (a, kcache, vcache) = _launch(
            plans, (batch,), "mix_ctx")
        extras = ((kcache, vcache), (s_rf, s_rb, s_hf, s_hb))
    y = _outproj(x2d, r, a, hgo, mods, l, latent, p['w_out'][l], seq_len)
    return y, extras


def kernel(x_prompt, x_sample, c, c_ctx, cache_diff_k, cache_diff_v, state_ret_fwd,
           state_ret_bwd, state_hgrn_fwd, state_hgrn_bwd, norm_g, w_ada, b_ada, w_in,
           w_out, ret_decay_logit, diff_qn_g, diff_kn_g, diff_lambda, hgrn_lb_logit):
    batch, seq, _ = x_prompt.shape
    dec_batch, dec_seq, _ = x_sample.shape
    past_len = cache_diff_k.shape[2]

    assert 1 + dec_batch <= MOD_ROWS
    cv = jnp.zeros((MOD_ROWS, D_MODEL), F32).at[0].set(c_ctx).at[1:1 + dec_batch].set(c)
    mods = _ada(cv, w_ada, b_ada).reshape(DEPTH * 3 * MOD_ROWS, 1, D_MODEL)

    lb_all = jax.nn.softmax(hgrn_lb_logit.astype(F32), axis=0)
    lb_all = jnp.cumsum(lb_all, axis=0) - lb_all[0]
    lp = diff_lambda.astype(F32)
    lam_inits = jnp.asarray([0.8 - 0.6 * math.exp(-0.3 * l) for l in range(DEPTH)], F32)
    lam = (jnp.exp(jnp.sum(lp[:, 0] * lp[:, 1], axis=-1))
           - jnp.exp(jnp.sum(lp[:, 2] * lp[:, 3], axis=-1)) + lam_inits)
    p = {
        'norm_g': norm_g.reshape(DEPTH, 1, D_MODEL),
        'w_in': w_in.astype(BF16),
        'w_out': w_out.astype(BF16),
        'log_g': jax.nn.log_sigmoid(ret_decay_logit.astype(F32)),
        'lam': lam.reshape(DEPTH, 1),
        'gq': jnp.tile(diff_qn_g, (1, DIFF_W // DH)).reshape(DEPTH, 1, DIFF_W),
        'gk': jnp.tile(diff_kn_g, (1, DIFF_W // DH)).reshape(DEPTH, 1, DIFF_W),
        'lb': lb_all.reshape(DEPTH, 1, PACK_W),
        'g256': _block_ones(PACK_W),
        'g128': _block_ones(LANES),
        'hg_consts': _hgrn_consts(),
        'rope256': _rope_tables(dec_seq, PACK_W),
        'rope512': _rope_tables(dec_seq, DIFF_W),
    }

    y = x_prompt.reshape(batch * seq, D_MODEL)
    caches = None
    new_states = None
    for l in range(DEPTH):
        y, (caches, new_states) = _layer(y, batch, seq, mods, l, p, False, None, caches,
                                         new_states)
    y_prompt = y.reshape(batch, seq, D_MODEL)
    new_k = caches[0].reshape(batch, DEPTH, seq, HEADS, 2, DH)
    new_v = caches[1]

    y = x_sample.reshape(dec_batch * dec_seq, D_MODEL)
    rows = lambda s: s.reshape(dec_batch, DEPTH, PACK_W, DH)
    keys_t = jnp.transpose(cache_diff_k, (0, 1, 3, 4, 5, 2)).reshape(
        dec_batch, DEPTH, DIFF_W, past_len)
    ctx = (keys_t, cache_diff_v, rows(state_ret_fwd), rows(state_ret_bwd),
           rows(state_hgrn_fwd), rows(state_hgrn_bwd))
    for l in range(DEPTH):
        y, _ = _layer(y, dec_batch, dec_seq, mods, l, p, True, ctx, None)
    y_sample = y.reshape(dec_batch, dec_seq, D_MODEL)

    return (y_prompt, y_sample, new_k, new_v, *new_states)
```

```python
import functools
import math
from typing import Callable, NamedTuple

import numpy as np
import jax
import jax.numpy as jnp
from jax import lax
from jax.experimental import pallas as pl
from jax.experimental.pallas import tpu as pltpu

F32 = jnp.float32
BF16 = jnp.bfloat16

D_MODEL = 1024
DEPTH = 2
GRID_W = 64
HEADS = 4
DH = 64
PACK_W = HEADS * DH
DIFF_W = 512
DIFF_DV = 128
RET_COLS = 4 * PACK_W
DIFF_COLS = 4 * DIFF_W
HG_COLS = 5 * PACK_W
IN_W = RET_COLS + DIFF_COLS + HG_COLS
MIX_W = PACK_W + DIFF_W + PACK_W
ROPE_BASE = 10000.0
ROPE_PAIR = 16
EPS = 1e-6
LOG2E = 1.4426950408889634
NEG_BIG = -1e30

RET_C = 256
HG_C = 64
HG_SUB = 4
HG_LEVELS = (4, 8, 16, 32)
SUBLANES = 8
LANES = 128
ROW_BLK = 256
TQ = 512
TM_IN = 1024
TM_OUT = 512

VMEM_LIMIT = 56 * 1024 * 1024


def _params(n_axes):
    return pltpu.CompilerParams(
        dimension_semantics=("arbitrary",) * n_axes, vmem_limit_bytes=VMEM_LIMIT)


class _Plan(NamedTuple):
    kernel: Callable
    in_specs: list
    args: list
    out_specs: list
    out_shape: list
    scratch: list


def _launch(plans, grid, name):
    counts = [(len(p.in_specs), len(p.out_specs), len(p.scratch)) for p in plans]
    n_in = sum(c[0] for c in counts)
    n_out = sum(c[1] for c in counts)

    def body(*refs):
        i0, o0, s0 = 0, n_in, n_in + n_out
        for plan, (ni, no, ns) in zip(plans, counts):
            plan.kernel(*refs[i0:i0 + ni], *refs[o0:o0 + no], *refs[s0:s0 + ns])
            i0, o0, s0 = i0 + ni, o0 + no, s0 + ns

    outs = pl.pallas_call(
        body,
        grid=grid,
        in_specs=[s for p in plans for s in p.in_specs],
        out_specs=[s for p in plans for s in p.out_specs],
        out_shape=[s for p in plans for s in p.out_shape],
        scratch_shapes=[s for p in plans for s in p.scratch],
        compiler_params=_params(len(grid)),
        name=name,
    )(*[a for p in plans for a in p.args])
    results, o0 = [], 0
    for _, no, _ in counts:
        results.append(list(outs[o0:o0 + no]))
        o0 += no
    return results if len(plans) > 1 else results[0]


def _dot(a, b):
    return jnp.dot(a, b, preferred_element_type=F32)


def _dot_nt(a, b):
    return lax.dot_general(a, b, (((1,), (1,)), ((), ())), preferred_element_type=F32)


def _dot_tn(a, b):
    return lax.dot_general(a, b, (((0,), (0,)), ((), ())), preferred_element_type=F32)


def _silu(x):
    return x * jax.nn.sigmoid(x)


def _group_rms(x, g, width):
    n, w = x.shape
    x2 = x * x
    cols = range(0, w, LANES)
    stacked = jnp.concatenate([x2[:, j:j + LANES] for j in cols], axis=0)
    hi = stacked.astype(BF16)
    lo = (stacked - hi.astype(F32)).astype(BF16)
    sums = _dot(hi, g) + _dot(lo, g)
    ms = jnp.concatenate([sums[i * n:(i + 1) * n] for i in range(len(cols))], axis=1)
    return x * lax.rsqrt(ms * (1.0 / width) + EPS)


def _rope(x, cos, sin):
    w = x.shape[-1]
    lane = lax.broadcasted_iota(jnp.int32, (1, w), 1)
    first = (lane & (2 * ROPE_PAIR - 1)) < ROPE_PAIR
    swapped = jnp.where(first, pltpu.roll(x, w - ROPE_PAIR, 1), pltpu.roll(x, ROPE_PAIR, 1))
    return x * cos + swapped * sin


def _initial_state(s_ref, gmask):
    s = s_ref[0, 0]
    return (jnp.concatenate([s] * HEADS, axis=1) * gmask).T


def _lane_head(w=PACK_W):
    return lax.broadcasted_iota(jnp.int32, (1, w), 1) // DH


def _row_blocks(n_rows, body):
    n_blk = n_rows // ROW_BLK
    if n_blk == 1:
        body(0)
    else:
        def step(i, carry):
            body(pl.multiple_of(i * ROW_BLK, ROW_BLK))
            return carry
        lax.fori_loop(0, n_blk, step, 0)


def _ada_kernel(cv_ref, w_ref, b_ref, o_ref):
    cv = cv_ref[...]
    o_ref[0, 0] = _dot(_silu(cv), w_ref[0]) + b_ref[0]


def _ada(cv, w_ada, b_ada):
    tn = 1024
    n = 3 * D_MODEL
    return pl.pallas_call(
        _ada_kernel,
        grid=(DEPTH, n // tn),
        in_specs=[
            pl.BlockSpec((8, D_MODEL), lambda l, j: (0, 0)),
            pl.BlockSpec((1, D_MODEL, tn), lambda l, j: (l, 0, j)),
            pl.BlockSpec((1, 1, tn), lambda l, j: (l, 0, j)),
        ],
        out_specs=pl.BlockSpec((1, 1, 8, tn), lambda l, j: (l, j, 0, 0)),
        out_shape=jax.ShapeDtypeStruct((DEPTH, n // tn, 8, tn), F32),
        compiler_params=_params(2),
        name="ada",
    )(cv, w_ada, b_ada.reshape(DEPTH, 1, n))


def _inproj_kernel(x_ref, g_ref, sc_ref, sh_ref, w_ref, zr_ref, zd_ref, zh_ref):
    _project(x_ref[...], g_ref, sc_ref, sh_ref, w_ref, zr_ref, zd_ref, zh_ref)


def _project(x, g_ref, sc_ref, sh_ref, w_ref, zr_ref, zd_ref, zh_ref):
    ms = jnp.mean(x * x, axis=-1, keepdims=True)
    h = x * lax.rsqrt(ms + EPS) * g_ref[...]
    h = h * (1.0 + sc_ref[0]) + sh_ref[0]
    z = _dot(h.astype(BF16), w_ref[...])
    zr_ref[...] = z[:, :RET_COLS].astype(zr_ref.dtype)
    zd_ref[...] = z[:, RET_COLS:RET_COLS + DIFF_COLS].astype(zd_ref.dtype)
    zh_ref[...] = z[:, RET_COLS + DIFF_COLS:]


MOD_SHIFT, MOD_SCALE, MOD_GATE = 0, 1, 2
MOD_ROWS = 8


def _mod_spec(layer, which, per_batch, tiles_per_seq):
    base = (layer * 3 + which) * MOD_ROWS
    if per_batch:
        return pl.BlockSpec((1, 1, D_MODEL), lambda i: (base + 1 + i // tiles_per_seq, 0, 0))
    return pl.BlockSpec((1, 1, D_MODEL), lambda i: (base, 0, 0))


def _inproj(x2d, g, mods, layer, per_batch, w_bf, seq_len):
    m = x2d.shape[0]
    tiles_per_seq = seq_len // TM_IN
    return pl.pallas_call(
        _inproj_kernel,
        grid=(m // TM_IN,),
        in_specs=[
            pl.BlockSpec((TM_IN, D_MODEL), lambda i: (i, 0)),
            pl.BlockSpec((1, D_MODEL), lambda i: (0, 0)),
            _mod_spec(layer, MOD_SCALE, per_batch, tiles_per_seq),
            _mod_spec(layer, MOD_SHIFT, per_batch, tiles_per_seq),
            pl.BlockSpec((D_MODEL, IN_W), lambda i: (0, 0)),
        ],
        out_specs=[
            pl.BlockSpec((TM_IN, RET_COLS), lambda i: (i, 0)),
            pl.BlockSpec((TM_IN, DIFF_COLS), lambda i: (i, 0)),
            pl.BlockSpec((TM_IN, HG_COLS), lambda i: (i, 0)),
        ],
        out_shape=[
            jax.ShapeDtypeStruct((m, RET_COLS), BF16),
            jax.ShapeDtypeStruct((m, DIFF_COLS), BF16),
            jax.ShapeDtypeStruct((m, HG_COLS), F32),
        ],
        compiler_params=_params(1),
        name="inproj",
    )(x2d, g, mods, mods, w_bf)


def _outproj_kernel(x_ref, r_ref, d_ref, h_ref, gate_ref, w_ref, o_ref):
    mixed = jnp.concatenate([r_ref[...], d_ref[...], h_ref[...]], axis=1)
    o_ref[...] = x_ref[...] + gate_ref[0] * _dot(mixed, w_ref[...])


def _bridge_kernel(x_ref, r_ref, d_ref, h_ref, gate_ref, wo_ref, g_ref, sc_ref, sh_ref, wi_ref,
                   y_ref, zr_ref, zd_ref, zh_ref):
    mixed = jnp.concatenate([r_ref[...], d_ref[...], h_ref[...]], axis=1)
    y = x_ref[...] + gate_ref[0] * _dot(mixed, wo_ref[...])
    y_ref[...] = y
    _project(y, g_ref, sc_ref, sh_ref, wi_ref, zr_ref, zd_ref, zh_ref)


def _bridge(x2d, r, d, h, mods, layer, per_batch, wo_bf, g_next, wi_bf, seq_len):
    m = x2d.shape[0]
    tm = TM_OUT
    assert not per_batch or seq_len % tm == 0
    tiles_per_seq = max(seq_len // tm, 1)
    row = lambda i: (i, 0)
    const = lambda i: (0, 0)
    return pl.pallas_call(
        _bridge_kernel,
        grid=(m // tm,),
        in_specs=[
            pl.BlockSpec((tm, D_MODEL), row),
            pl.BlockSpec((tm, PACK_W), row),
            pl.BlockSpec((tm, DIFF_W), row),
            pl.BlockSpec((tm, PACK_W), row),
            _mod_spec(layer, MOD_GATE, per_batch, tiles_per_seq),
            pl.BlockSpec((MIX_W, D_MODEL), const),
            pl.BlockSpec((1, D_MODEL), const),
            _mod_spec(layer + 1, MOD_SCALE, per_batch, tiles_per_seq),
            _mod_spec(layer + 1, MOD_SHIFT, per_batch, tiles_per_seq),
            pl.BlockSpec((D_MODEL, IN_W), const),
        ],
        out_specs=[
            pl.BlockSpec((tm, D_MODEL), row),
            pl.BlockSpec((tm, RET_COLS), row),
            pl.BlockSpec((tm, DIFF_COLS), row),
            pl.BlockSpec((tm, HG_COLS), row),
        ],
        out_shape=[
            jax.ShapeDtypeStruct((m, D_MODEL), F32),
            jax.ShapeDtypeStruct((m, RET_COLS), BF16),
            jax.ShapeDtypeStruct((m, DIFF_COLS), BF16),
            jax.ShapeDtypeStruct((m, HG_COLS), F32),
        ],
        compiler_params=_params(1),
        name="bridge",
    )(x2d, r, d, h, mods, wo_bf, g_next, mods, mods, wi_bf)


def _outproj(x2d, r, d, h, mods, layer, per_batch, w_bf, seq_len):
    m = x2d.shape[0]
    tiles_per_seq = max(seq_len // TM_OUT, 1)
    tm = TM_OUT if (not per_batch or seq_len % TM_OUT == 0) else seq_len
    return pl.pallas_call(
        _outproj_kernel,
        grid=(m // tm,),
        in_specs=[
            pl.BlockSpec((tm, D_MODEL), lambda i: (i, 0)),
            pl.BlockSpec((tm, PACK_W), lambda i: (i, 0)),
            pl.BlockSpec((tm, DIFF_W), lambda i: (i, 0)),
            pl.BlockSpec((tm, PACK_W), lambda i: (i, 0)),
            _mod_spec(layer, MOD_GATE, per_batch, tiles_per_seq),
            pl.BlockSpec((MIX_W, D_MODEL), lambda i: (0, 0)),
        ],
        out_specs=pl.BlockSpec((tm, D_MODEL), lambda i: (i, 0)),
        out_shape=jax.ShapeDtypeStruct((m, D_MODEL), F32),
        compiler_params=_params(1),
        name="outproj",
    )(x2d, r, d, h, mods, w_bf)


def _ret_kernel(*refs, seq_len, latent, layer):
    if latent:
        (lg_ref, z_ref, g_ref, cos_ref, sin_ref, s0f_ref, s0b_ref,
         out_ref, mask_scr, o_scr) = refs
    else:
        lg_ref, z_ref, g_ref = refs[:3]
        prev_refs = refs[3:-5]
        out_ref, sf_ref, sb_ref, mask_scr, o_scr = refs[-5:]
    C = RET_C
    n_chunks = seq_len // C
    lh = _lane_head()

    def lane_vec(direction):
        v = jnp.zeros((1, PACK_W), F32)
        for h in range(HEADS):
            v = jnp.where(lh == h, lg_ref[direction, h], v)
        return v

    lgf, lgb = lane_vec(0), lane_vec(1)
    tau = lax.broadcasted_iota(jnp.int32, (C, 1), 0).astype(F32)
    qdec_f = jnp.exp((tau + 1.0) * lgf)
    qdec_b = jnp.exp((C - tau) * lgb)
    kdec_f = jnp.exp((C - 1.0 - tau) * lgf)
    kdec_b = jnp.exp(tau * lgb)
    sdec_f = jnp.exp(float(C) * lgf)
    sdec_b = jnp.exp(float(C) * lgb)
    gmask = g_ref[...].astype(F32)

    @pl.when(pl.program_id(0) == 0)
    def _():
        dd = (lax.broadcasted_iota(jnp.int32, (C, C), 0)
              - lax.broadcasted_iota(jnp.int32, (C, C), 1)).astype(F32)
        for h in range(HEADS):
            mf = jnp.where(dd >= 0, jnp.exp(jnp.maximum(dd, 0.0) * lg_ref[0, h]), 0.0)
            mb = jnp.where(dd <= 0, jnp.exp(jnp.maximum(-dd, 0.0) * lg_ref[1, h]), 0.0)
            mask_scr[h * C:(h + 1) * C, :] = mf + mb

    def load(r0):
        q = z_ref[pl.ds(r0, C), 0:PACK_W].astype(F32)
        k = z_ref[pl.ds(r0, C), PACK_W:2 * PACK_W].astype(F32) * (DH ** -0.5)
        v = z_ref[pl.ds(r0, C), 2 * PACK_W:3 * PACK_W].astype(F32)
        if latent:
            cos = cos_ref[pl.ds(r0, C), :]
            sin = sin_ref[pl.ds(r0, C), :]
            q = _rope(q, cos, sin)
            k = _rope(k, cos, sin)
        return q, k, v

    def intra(q, k, v):
        qb = q.astype(BF16)
        q4 = jnp.concatenate([jnp.where(lh == h, qb, jnp.zeros_like(qb)) for h in range(HEADS)],
                             axis=0)
        a4 = _dot_nt(q4, k.astype(BF16)) * mask_scr[...]
        o4 = _dot(a4.astype(BF16), v.astype(BF16))
        o = jnp.zeros((C, PACK_W), F32)
        for h in range(HEADS):
            o = o + jnp.where(lh == h, o4[h * C:(h + 1) * C], 0.0)
        return o

    def finish(o, r0):
        gate = z_ref[pl.ds(r0, C), 3 * PACK_W:4 * PACK_W].astype(F32)
        y = _group_rms(o, g_ref[0:LANES, 0:LANES], DH) * _silu(gate)
        out_ref[pl.ds(r0, C), :] = y.astype(out_ref.dtype)

    def state_step(st, k, v, kdec, sdec):
        upd = _dot_tn(v.astype(BF16), (k * kdec).astype(BF16))
        return (st * sdec + upd) * gmask

    if not latent:
        q, k, v = load(0)
        finish(intra(q, k, v), 0)
        vb = v.astype(BF16)
        for ref, kdec in ((sf_ref, kdec_f), (sb_ref, kdec_b)):
            s = _dot_tn((k * kdec).astype(BF16), vb) * gmask
            for h in range(HEADS):
                ref[0, layer, h] = s[h * DH:(h + 1) * DH, h * DH:(h + 1) * DH]
        for ref, prev in zip((sf_ref, sb_ref), prev_refs):
            ref[0, 0:layer] = prev[0]
        return

    def bwd_body(i, st):
        r0 = pl.multiple_of((n_chunks - 1 - i) * C, C)
        q, k, v = load(r0)
        o_scr[pl.ds(r0, C), :] = _dot_nt((q * qdec_b).astype(BF16), st.astype(BF16))
        return state_step(st, k, v, kdec_b, sdec_b)

    lax.fori_loop(0, n_chunks, bwd_body, _initial_state(s0b_ref, gmask), unroll=2)

    def fwd_body(i, st):
        r0 = pl.multiple_of(i * C, C)
        q, k, v = load(r0)
        o = intra(q, k, v) + o_scr[pl.ds(r0, C), :]
        o = o + _dot_nt((q * qdec_f).astype(BF16), st.astype(BF16))
        finish(o, r0)
        return state_step(st, k, v, kdec_f, sdec_f)

    lax.fori_loop(0, n_chunks, fwd_body, _initial_state(s0f_ref, gmask), unroll=2)


def _state_specs(batch, layer, prev):
    spec = lambda n: pl.BlockSpec((1, n, HEADS, DH, DH), lambda b: (b, 0, 0, 0, 0))
    out_specs = [spec(layer + 1)] * 2
    out_shape = [jax.ShapeDtypeStruct((batch, layer + 1, HEADS, DH, DH), F32)] * 2
    in_specs = [spec(layer)] * 2 if layer > 0 else []
    return in_specs, (list(prev) if layer > 0 else []), out_specs, out_shape


def _retention(z_ret, log_g, g256, batch, seq_len, latent, layer, rope=None, states=None,
               plan_only=False):
    m = batch * seq_len
    row = lambda b: (b, 0)
    const2 = lambda b: (0, 0)
    in_specs = [
        pl.BlockSpec(memory_space=pltpu.SMEM),
        pl.BlockSpec((seq_len, RET_COLS), row),
        pl.BlockSpec((PACK_W, PACK_W), const2),
    ]
    args = [log_g, z_ret, g256]
    out_specs = [pl.BlockSpec((seq_len, PACK_W), row)]
    out_shape = [jax.ShapeDtypeStruct((m, PACK_W), BF16)]
    if latent:
        state_spec = pl.BlockSpec((1, 1, PACK_W, DH), lambda b: (b, layer, 0, 0))
        in_specs += [pl.BlockSpec((seq_len, PACK_W), const2)] * 2 + [state_spec] * 2
        args += [rope[0], rope[1], states[0], states[1]]
    else:
        s_in, s_args, s_out, s_shape = _state_specs(batch, layer, states)
        in_specs += s_in
        args += s_args
        out_specs += s_out
        out_shape += s_shape
    plan = _Plan(
        kernel=functools.partial(_ret_kernel, seq_len=seq_len, latent=latent, layer=layer),
        in_specs=in_specs, args=args, out_specs=out_specs, out_shape=out_shape,
        scratch=[
            pltpu.VMEM((HEADS * RET_C, RET_C), F32),
            pltpu.VMEM((seq_len, PACK_W), F32),
        ])
    if plan_only:
        return plan
    return _launch([plan], (batch,), "ret_latent" if latent else "ret_ctx")


def _attn_kernel(*refs, seq_len, past_len, latent, lam_init, layer, n_qblocks):
    if latent:
        (lam_ref, q_ref, k_ref, v_ref, gate_ref, gq_ref, gk_ref, g_ref,
         cosq_ref, sinq_ref, cosk_ref, sink_ref, kc_ref, vc_ref,
         out_ref, k_scr, v_scr) = refs
        prev_refs = ()
    else:
        (lam_ref, q_ref, k_ref, v_ref, gate_ref, gq_ref, gk_ref, g_ref) = refs[:8]
        prev_refs = refs[8:-5]
        out_ref, kn_ref, vn_ref, k_scr, v_scr = refs[-5:]
    ones = jnp.ones((ROW_BLK, DIFF_DV), BF16)

    def put_values(sl, head_values):
        for h in range(HEADS):
            v_scr[sl, 2 * h * DIFF_DV:(2 * h + 1) * DIFF_DV] = head_values(h).astype(BF16)
            v_scr[sl, (2 * h + 1) * DIFF_DV:(2 * h + 2) * DIFF_DV] = ones

    def prepare_keys_values():
        for r in range(seq_len // ROW_BLK):
            sl = pl.ds(r * ROW_BLK, ROW_BLK)
            kn = _group_rms(k_ref[sl, :].astype(F32), g_ref[...], DH) * gk_ref[...]
            v = v_ref[sl, :].astype(F32)
            if latent:
                kn = _rope(kn, cosk_ref[sl, :], sink_ref[sl, :])
            else:
                kn_ref[0, layer, sl, :] = kn
                for h in range(HEADS):
                    vn_ref[0, layer, sl, h, :] = v[:, h * DIFF_DV:(h + 1) * DIFF_DV]
            k_scr[sl, :] = kn.astype(BF16)
            put_values(sl, lambda h: v[:, h * DIFF_DV:(h + 1) * DIFF_DV])
        if prev_refs:
            kn_ref[0, 0:layer] = prev_refs[0][0]
            vn_ref[0, 0:layer] = prev_refs[1][0]
        if latent:
            for r in range(past_len // ROW_BLK):
                src = pl.ds(r * ROW_BLK, ROW_BLK)
                dst = pl.ds(seq_len + r * ROW_BLK, ROW_BLK)
                k_scr[dst, :] = kc_ref[0, 0, :, src].T.astype(BF16)
                put_values(dst, lambda h: vc_ref[0, 0, src, h, :])

    if n_qblocks == 1:
        prepare_keys_values()
    else:
        pl.when(pl.program_id(1) == 0)(prepare_keys_values)

    lam = lam_ref[0]
    qn = _group_rms(q_ref[...].astype(F32), g_ref[...], DH) * gq_ref[...]
    if latent:
        qn = _rope(qn, cosq_ref[...], sinq_ref[...])
    qn = qn * (DH ** -0.5 * LOG2E)
    lane = lax.broadcasted_iota(jnp.int32, (1, 2 * DH), 1)
    for h in range(HEADS):
        hs = slice(h * 2 * DH, (h + 1) * 2 * DH)
        vx = v_scr[:, 2 * h * DIFF_DV:(2 * h + 2) * DIFF_DV]
        qh = qn[:, hs]
        kh = k_scr[:, hs]
        parts = []
        for first_map in (True, False):
            qm = jnp.where((lane < DH) == first_map, qh, 0.0).astype(BF16)
            s = _dot_nt(qm, kh)
            e = jnp.exp2(s - jnp.max(s, axis=-1, keepdims=True)).astype(BF16)
            ox = _dot(e, vx)
            parts.append(ox[:, :DIFF_DV] / ox[:, DIFF_DV:])
        o = parts[0] - lam * parts[1]
        ms = jnp.mean(o * o, axis=-1, keepdims=True)
        y = o * lax.rsqrt(ms + EPS) * (1.0 - lam_init) * _silu(gate_ref[:, hs].astype(F32))
        out_ref[:, hs] = y.astype(out_ref.dtype)


def _diff_attention(z_diff, lam, gq, gk, g128, batch, seq_len, latent, lam_init, layer,
                    rope=None, ctx=None, caches=None, plan_only=False):
    m = batch * seq_len
    tq = min(TQ, seq_len)
    nq = seq_len // tq
    past_len = ctx[1].shape[2] if latent else 0
    qmap = lambda col: (lambda b, *i: (b * nq + (i[0] if i else 0), col))
    kvmap = lambda col: (lambda b, *i: (b, col))
    const2 = lambda b, *i: (0, 0)
    in_specs = [
        pl.BlockSpec(memory_space=pltpu.SMEM),
        pl.BlockSpec((tq, DIFF_W), qmap(0)),
        pl.BlockSpec((seq_len, DIFF_W), kvmap(1)),
        pl.BlockSpec((seq_len, DIFF_W), kvmap(2)),
        pl.BlockSpec((tq, DIFF_W), qmap(3)),
        pl.BlockSpec((1, DIFF_W), const2),
        pl.BlockSpec((1, DIFF_W), const2),
        pl.BlockSpec((LANES, LANES), const2),
    ]
    args = [lam, z_diff, z_diff, z_diff, z_diff, gq, gk, g128]
    out_specs = [pl.BlockSpec((tq, DIFF_W), qmap(0))]
    out_shape = [jax.ShapeDtypeStruct((m, DIFF_W), BF16)]
    if latent:
        in_specs += [
            pl.BlockSpec((tq, DIFF_W), lambda b, i: (i, 0)),
            pl.BlockSpec((tq, DIFF_W), lambda b, i: (i, 0)),
            pl.BlockSpec((seq_len, DIFF_W), const2),
            pl.BlockSpec((seq_len, DIFF_W), const2),
            pl.BlockSpec((1, 1, DIFF_W, past_len), lambda b, i: (b, layer, 0, 0)),
            pl.BlockSpec((1, 1, past_len, HEADS, DIFF_DV), lambda b, i: (b, layer, 0, 0, 0)),
        ]
        args += [rope[0], rope[1], rope[0], rope[1], ctx[0], ctx[1]]
    else:
        cache_spec = lambda n: pl.BlockSpec((1, n, seq_len, DIFF_W), lambda b, *i: (b, 0, 0, 0))
        vcache_spec = lambda n: pl.BlockSpec((1, n, seq_len, HEADS, DIFF_DV),
                                             lambda b, *i: (b, 0, 0, 0, 0))
        out_specs += [cache_spec(layer + 1), vcache_spec(layer + 1)]
        out_shape += [jax.ShapeDtypeStruct((batch, layer + 1, seq_len, DIFF_W), F32),
                      jax.ShapeDtypeStruct((batch, layer + 1, seq_len, HEADS, DIFF_DV), F32)]
        if layer > 0:
            in_specs += [cache_spec(layer), vcache_spec(layer)]
            args += list(caches)
    plan = _Plan(
        kernel=functools.partial(_attn_kernel, seq_len=seq_len, past_len=past_len,
                                 latent=latent, lam_init=lam_init, layer=layer, n_qblocks=nq),
        in_specs=in_specs, args=args, out_specs=out_specs, out_shape=out_shape,
        scratch=[
            pltpu.VMEM((seq_len + past_len, DIFF_W), BF16),
            pltpu.VMEM((seq_len + past_len, 2 * DIFF_W), BF16),
        ])
    if plan_only:
        assert nq == 1
        return plan
    return _launch([plan], (batch, nq), "attn_latent" if latent else "attn_ctx")


def _hgrn_kernel(*refs, seq_len, latent, layer):
    if latent:
        (z_ref, lb_ref, tri_ref, g_ref, lm_ref, dm_ref, s0f_ref, s0b_ref,
         out_ref, qi_scr, ks_scr, dec_scr, o_scr, oi_scr) = refs
    else:
        z_ref, lb_ref, tri_ref, g_ref, lm_ref, dm_ref = refs[:6]
        prev_refs = refs[6:-8]
        out_ref, sf_ref, sb_ref, qi_scr, ks_scr, dec_scr, o_scr, oi_scr = refs[-8:]
    C, SUB = HG_C, HG_SUB
    n_chunks = seq_len // C
    chunks_per_blk = ROW_BLK // C
    lh = _lane_head()
    sub_row = lax.broadcasted_iota(jnp.int32, (ROW_BLK, 1), 0) & (SUB - 1)
    chunk_row = lax.broadcasted_iota(jnp.int32, (C, 1), 0)

    def tile_roll(x, shift):
        return jnp.concatenate(
            [pltpu.roll(x[i:i + SUBLANES], shift, 0) for i in range(0, ROW_BLK, SUBLANES)], axis=0)

    def head_rep(x):
        return jnp.concatenate([jnp.where(lh == h, x, jnp.zeros_like(x)) for h in range(HEADS)],
                               axis=0)

    def intra(blk):
        start = blk * ROW_BLK if isinstance(blk, int) else pl.multiple_of(blk * ROW_BLK, ROW_BLK)
        rows = pl.ds(start, ROW_BLK)
        lb = lb_ref[...]
        q = z_ref[rows, 0:PACK_W] * (DH ** -0.5)
        v = z_ref[rows, 3 * PACK_W:4 * PACK_W]
        v_reps = [head_rep(v[c * C:(c + 1) * C].astype(BF16)) for c in range(chunks_per_blk)]
        kks, bcs, a_nears = [], [], []
        for d in range(2):
            f = lb + (1.0 - lb) * jax.nn.sigmoid(z_ref[rows, (1 + d) * PACK_W:(2 + d) * PACK_W])
            kks.append(1.0 - f)
            l2 = jnp.log(f) * LOG2E
            hi = l2.astype(BF16)
            r1 = l2 - hi.astype(F32)
            mid = r1.astype(BF16)
            lo = (r1 - mid.astype(F32)).astype(BF16)
            cs = _dot(tri_ref[d], jnp.concatenate([hi, mid, lo], axis=1))
            bcs.append(cs[:, 0:PACK_W] + cs[:, PACK_W:2 * PACK_W] + cs[:, 2 * PACK_W:])

        for d in range(2):
            bwd = d == 1
            kk, bc = kks[d], bcs[d]
            ps = [(q * kk).astype(BF16)]
            for dist in range(1, SUB):
                shift = SUBLANES - dist if bwd else dist
                valid = (sub_row < SUB - dist) if bwd else (sub_row >= dist)
                arg = jnp.where(valid, bc - tile_roll(bc, shift), NEG_BIG)
                ps.append((q * tile_roll(kk, shift) * jnp.exp2(arg)).astype(BF16))
            r = _dot(jnp.concatenate(ps, axis=0), g_ref[...]).astype(BF16)
            a_near = r[0:ROW_BLK] * dm_ref[d, 0]
            for dist in range(1, SUB):
                a_near = a_near + r[dist * ROW_BLK:(dist + 1) * ROW_BLK] * dm_ref[d, dist]
            a_nears.append(a_near)

        for c in range(chunks_per_blk):
            for d in range(2):
                bwd = d == 1
                kk, bc, a_near = kks[d], bcs[d], a_nears[d]
                cs_ = slice(c * C, (c + 1) * C)
                qc, kc, bcc = q[cs_], kk[cs_], bc[cs_]
                last = bcc[0:1] if bwd else bcc[C - 1:C]
                a = a_near[cs_]
                for li, m in enumerate(HG_LEVELS):
                    pieces = []
                    for b2 in range(C // (2 * m)):
                        rr = 2 * m * b2 + (m if bwd else m - 1)
                        pieces.append(jnp.broadcast_to(bcc[rr:rr + 1], (2 * m, PACK_W)))
                    anchor = jnp.concatenate(pieces, axis=0) if len(pieces) > 1 else pieces[0]
                    upper = (chunk_row & (2 * m - 1)) >= m
                    later = jnp.logical_not(upper) if bwd else upper
                    e = jnp.exp2((bcc - anchor) * jnp.where(later, 1.0, -1.0))
                    a = a + _dot_nt((qc * e).astype(BF16),
                                    head_rep((kc * e).astype(BF16))).astype(BF16) * lm_ref[d, li]
                crow = pl.ds(start + c * C, C)
                o_scr[d, crow, :] = _dot(a, v_reps[c])
                qi_scr[d, crow, :] = (qc * jnp.exp2(bcc)).astype(BF16)
                ks_scr[d, crow, :] = (kc * jnp.exp2(last - bcc)).astype(BF16)
                slot = (blk * chunks_per_blk + c) * SUBLANES
                if not isinstance(slot, int):
                    slot = pl.multiple_of(slot, SUBLANES)
                dec_scr[d, pl.ds(slot, SUBLANES), :] = jnp.broadcast_to(
                    jnp.exp2(last), (SUBLANES, PACK_W))

        if not latent:
            vb = v.astype(BF16)
            for d, ref in enumerate((sf_ref, sb_ref)):
                bwd = d == 1
                sweep = range(chunks_per_blk - 1, -1, -1) if bwd else range(chunks_per_blk)
                tail = jnp.zeros((1, PACK_W), F32)
                decayed = [None] * chunks_per_blk
                for c in reversed(sweep):
                    cs_ = slice(c * C, (c + 1) * C)
                    bcc = bcs[d][cs_]
                    total = tail + (bcc[0:1] if bwd else bcc[C - 1:C])
                    decayed[c] = (kks[d][cs_] * jnp.exp2(total - bcc)).astype(BF16)
                    tail = total
                s = _dot_tn(jnp.concatenate(decayed, axis=0), vb) * g_ref[...].astype(F32)
                for h in range(HEADS):
                    ref[0, layer, h] = s[h * DH:(h + 1) * DH, h * DH:(h + 1) * DH]
                if prev_refs:
                    ref[0, 0:layer] = prev_refs[d][0]

    n_blk = seq_len // ROW_BLK
    assert latent or n_blk == 1
    if n_blk == 1:
        intra(0)
    else:
        def intra_step(i, carry):
            intra(i)
            return carry
        lax.fori_loop(0, n_blk, intra_step, 0)

    gmask = g_ref[...].astype(F32)
    if latent:
        states = (_initial_state(s0f_ref, gmask), _initial_state(s0b_ref, gmask))
    else:
        states = (jnp.zeros((PACK_W, PACK_W), F32),) * 2

    def body(i, states):
        new_states = []
        for d in range(2):
            c = (n_chunks - 1 - i) if d == 1 else i
            rows = pl.ds(pl.multiple_of(c * C, C), C)
            st = states[d]
            oi_scr[d, rows, :] = _dot_nt(qi_scr[d, rows, :], st.astype(BF16))
            dec = dec_scr[d, pl.ds(pl.multiple_of(c * SUBLANES, SUBLANES), 1), :]
            vb = z_ref[rows, 3 * PACK_W:4 * PACK_W].astype(BF16)
            new_states.append(st * dec + _dot_tn(vb, ks_scr[d, rows, :]) * gmask)
        return tuple(new_states)

    lax.fori_loop(0, n_chunks, body, states, unroll=4)

    def finish(start):
        rows = pl.ds(start, ROW_BLK)
        tot = (o_scr[0, rows, :] + oi_scr[0, rows, :]) + (o_scr[1, rows, :] + oi_scr[1, rows, :])
        gate = z_ref[rows, 4 * PACK_W:5 * PACK_W]
        y = _group_rms(tot, g_ref[0:LANES, 0:LANES], DH) * _silu(gate)
        out_ref[rows, :] = y.astype(out_ref.dtype)

    _row_blocks(seq_len, finish)


def _hgrn(z_hg, lb, consts, batch, seq_len, latent, layer, states=None, plan_only=False):
    m = batch * seq_len
    tri, g256, lm, dm = consts
    row = lambda b: (b, 0)
    in_specs = [
        pl.BlockSpec((seq_len, HG_COLS), row),
        pl.BlockSpec((1, PACK_W), lambda b: (0, 0)),
        pl.BlockSpec(tri.shape, lambda b: (0, 0, 0)),
        pl.BlockSpec((PACK_W, PACK_W), lambda b: (0, 0)),
        pl.BlockSpec(lm.shape, lambda b: (0, 0, 0, 0)),
        pl.BlockSpec(dm.shape, lambda b: (0, 0, 0, 0)),
    ]
    args = [z_hg, lb, tri, g256, lm, dm]
    out_specs = [pl.BlockSpec((seq_len, PACK_W), row)]
    out_shape = [jax.ShapeDtypeStruct((m, PACK_W), BF16)]
    if latent:
        in_specs += [pl.BlockSpec((1, 1, PACK_W, DH), lambda b: (b, layer, 0, 0))] * 2
        args += [states[0], states[1]]
    else:
        s_in, s_args, s_out, s_shape = _state_specs(batch, layer, states)
        in_specs += s_in
        args += s_args
        out_specs += s_out
        out_shape += s_shape
    plan = _Plan(
        kernel=functools.partial(_hgrn_kernel, seq_len=seq_len, latent=latent, layer=layer),
        in_specs=in_specs, args=args, out_specs=out_specs, out_shape=out_shape,
        scratch=[
            pltpu.VMEM((2, seq_len, PACK_W), BF16),
            pltpu.VMEM((2, seq_len, PACK_W), BF16),
            pltpu.VMEM((2, seq_len // HG_C * SUBLANES, PACK_W), F32),
            pltpu.VMEM((2, seq_len, PACK_W), F32),
            pltpu.VMEM((2, seq_len, PACK_W), F32),
        ])
    if plan_only:
        return plan
    return _launch([plan], (batch,), "hgrn_latent" if latent else "hgrn_ctx")


def _block_ones(n):
    idx = np.arange(n) // DH
    return jnp.asarray((idx[:, None] == idx[None, :]).astype(np.float32), dtype=BF16)


def _hgrn_consts():
    C = HG_C
    t = np.arange(ROW_BLK)
    same_chunk = (t[:, None] // C) == (t[None, :] // C)
    tri_f = same_chunk & (t[None, :] <= t[:, None])
    tri_b = same_chunk & (t[None, :] >= t[:, None])
    tc = np.arange(C)
    lms = []
    for fwd in (True, False):
        per_level = []
        for m in HG_LEVELS:
            same = (tc[:, None] // (2 * m)) == (tc[None, :] // (2 * m))
            upper = (tc % (2 * m)) >= m
            later, earlier = (upper, ~upper) if fwd else (~upper, upper)
            per_level.append(np.tile(same & later[:, None] & earlier[None, :], (1, HEADS)))
        lms.append(np.stack(per_level))
    dms = [[np.tile(tc[None, :] == tc[:, None] + sign * dist, (ROW_BLK // C, HEADS))
            for dist in range(HG_SUB)] for sign in (-1, 1)]
    return (jnp.asarray(np.stack([tri_f, tri_b]).astype(np.float32), dtype=BF16),
            _block_ones(PACK_W),
            jnp.asarray(np.stack(lms).astype(np.float32), dtype=BF16),
            jnp.asarray(np.stack(dms).astype(np.float32), dtype=BF16))


def _rope_tables(seq_len, width):
    t = np.arange(seq_len)
    pos = np.stack([t // GRID_W, t % GRID_W], axis=1).astype(np.float32)
    j = np.arange(DH)
    axis = j // (2 * ROPE_PAIR)
    jj = j % (2 * ROPE_PAIR)
    inv = (ROPE_BASE ** (-(jj % ROPE_PAIR).astype(np.float64) / ROPE_PAIR)).astype(np.float32)
    ang = (pos[:, axis] * inv[None, :]).astype(np.float64)
    cos = np.cos(ang)
    sin = np.where(jj < ROPE_PAIR, -np.sin(ang), np.sin(ang))
    reps = width // DH
    return (jnp.asarray(np.tile(cos, (1, reps)), dtype=F32),
            jnp.asarray(np.tile(sin, (1, reps)), dtype=F32))


def _layer(x2d, z, batch, seq_len, mods, l, p, latent, ctx, caches, states=None):
    z_ret, z_diff, z_hg = z
    lam_init = 0.8 - 0.6 * math.exp(-0.3 * l)
    if latent:
        k_ctx, v_ctx, s_rf, s_rb, s_hf, s_hb = ctx
        r = _retention(z_ret, p['log_g'][l], p['g256'], batch, seq_len, True, l,
                       rope=p['rope256'], states=(s_rf, s_rb))[0]
        a = _diff_attention(z_diff, p['lam'][l], p['gq'][l], p['gk'][l], p['g128'], batch,
                            seq_len, True, lam_init, l, rope=p['rope512'],
                            ctx=(k_ctx, v_ctx))[0]
        hgo = _hgrn(z_hg, p['lb'][l], p['hg_consts'], batch, seq_len, True, l,
                    states=(s_hf, s_hb))[0]
        extras = None
    else:
        prev = states if states is not None else (None,) * 4
        plans = [
            _retention(z_ret, p['log_g'][l], p['g256'], batch, seq_len, False, l,
                       states=prev[0:2], plan_only=True),
            _hgrn(z_hg, p['lb'][l], p['hg_consts'], batch, seq_len, False, l,
                  states=prev[2:4], plan_only=True),
            _diff_attention(z_diff, p['lam'][l], p['gq'][l], p['gk'][l], p['g128'], batch,
                            seq_len, False, lam_init, l, caches=caches, plan_only=True),
        ]
        (r, s_rf, s_rb), (hgo, s_hf, s_hb), (a, kcache, vcache) = _launch(
            plans, (batch,), "mix_ctx")
        extras = ((kcache, vcache), (s_rf, s_rb, s_hf, s_hb))
    if l + 1 < DEPTH:
        y, *z_next = _bridge(x2d, r, a, hgo, mods, l, latent, p['w_out'][l],
                             p['norm_g'][l + 1], p['w_in'][l + 1], seq_len)
    else:
        y = _outproj(x2d, r, a, hgo, mods, l, latent, p['w_out'][l], seq_len)
        z_next = None
    return y, z_next, extras


def kernel(x_prompt, x_sample, c, c_ctx, cache_diff_k, cache_diff_v, state_ret_fwd,
           state_ret_bwd, state_hgrn_fwd, state_hgrn_bwd, norm_g, w_ada, b_ada, w_in,
           w_out, ret_decay_logit, diff_qn_g, diff_kn_g, diff_lambda, hgrn_lb_logit):
    batch, seq, _ = x_prompt.shape
    dec_batch, dec_seq, _ = x_sample.shape
    past_len = cache_diff_k.shape[2]

    assert 1 + dec_batch <= MOD_ROWS
    cv = jnp.zeros((MOD_ROWS, D_MODEL), F32).at[0].set(c_ctx).at[1:1 + dec_batch].set(c)
    mods = _ada(cv, w_ada, b_ada).reshape(DEPTH * 3 * MOD_ROWS, 1, D_MODEL)

    lb_all = jax.nn.softmax(hgrn_lb_logit.astype(F32), axis=0)
    lb_all = jnp.cumsum(lb_all, axis=0) - lb_all[0]
    lp = diff_lambda.astype(F32)
    lam_inits = jnp.asarray([0.8 - 0.6 * math.exp(-0.3 * l) for l in range(DEPTH)], F32)
    lam = (jnp.exp(jnp.sum(lp[:, 0] * lp[:, 1], axis=-1))
           - jnp.exp(jnp.sum(lp[:, 2] * lp[:, 3], axis=-1)) + lam_inits)
    p = {
        'norm_g': norm_g.reshape(DEPTH, 1, D_MODEL),
        'w_in': w_in.astype(BF16),
        'w_out': w_out.astype(BF16),
        'log_g': jax.nn.log_sigmoid(ret_decay_logit.astype(F32)),
        'lam': lam.reshape(DEPTH, 1),
        'gq': jnp.tile(diff_qn_g, (1, DIFF_W // DH)).reshape(DEPTH, 1, DIFF_W),
        'gk': jnp.tile(diff_kn_g, (1, DIFF_W // DH)).reshape(DEPTH, 1, DIFF_W),
        'lb': lb_all.reshape(DEPTH, 1, PACK_W),
        'g256': _block_ones(PACK_W),
        'g128': _block_ones(LANES),
        'hg_consts': _hgrn_consts(),
        'rope256': _rope_tables(dec_seq, PACK_W),
        'rope512': _rope_tables(dec_seq, DIFF_W),
    }

    y = x_prompt.reshape(batch * seq, D_MODEL)
    caches = None
    new_states = None
    z = _inproj(y, p['norm_g'][0], mods, 0, False, p['w_in'][0], seq)
    for l in range(DEPTH):
        y, z, (caches, new_states) = _layer(y, z, batch, seq, mods, l, p, False, None, caches,
                                            new_states)
    y_prompt = y.reshape(batch, seq, D_MODEL)
    new_k = caches[0].reshape(batch, DEPTH, seq, HEADS, 2, DH)
    new_v = caches[1]

    y = x_sample.reshape(dec_batch * dec_seq, D_MODEL)
    rows = lambda s: s.reshape(dec_batch, DEPTH, PACK_W, DH)
    keys_t = jnp.transpose(cache_diff_k, (0, 1, 3, 4, 5, 2)).reshape(
        dec_batch, DEPTH, DIFF_W, past_len)
    ctx = (keys_t, cache_diff_v, rows(state_ret_fwd), rows(state_ret_bwd),
           rows(state_hgrn_fwd), rows(state_hgrn_bwd))
    z = _inproj(y, p['norm_g'][0], mods, 0, True, p['w_in'][0], dec_seq)
    for l in range(DEPTH):
        y, z, _ = _layer(y, z, dec_batch, dec_seq, mods, l, p, True, ctx, None)
    y_sample = y.reshape(dec_batch, dec_seq, D_MODEL)

    return (y_prompt, y_sample, new_k, new_v, *new_states)
```

```python
import functools
import math
from typing import Callable, NamedTuple

import numpy as np
import jax
import jax.numpy as jnp
from jax import lax
from jax.experimental import pallas as pl
from jax.experimental.pallas import tpu as pltpu

F32 = jnp.float32
BF16 = jnp.bfloat16

D_MODEL = 1024
DEPTH = 2
GRID_W = 64
HEADS = 4
DH = 64
PACK_W = HEADS * DH
DIFF_W = 512
DIFF_DV = 128
RET_COLS = 4 * PACK_W
DIFF_COLS = 4 * DIFF_W
HG_COLS = 5 * PACK_W
IN_W = RET_COLS + DIFF_COLS + HG_COLS
MIX_W = PACK_W + DIFF_W + PACK_W
ROPE_BASE = 10000.0
ROPE_PAIR = 16
EPS = 1e-6
LOG2E = 1.4426950408889634
NEG_BIG = -1e30

RET_C = 256
HG_C = 64
HG_SUB = 4
HG_LEVELS = (4, 8, 16, 32)
SUBLANES = 8
LANES = 128
ROW_BLK = 256
TQ = 512
TM_IN = 1024
TM_OUT = 512

VMEM_LIMIT = 56 * 1024 * 1024


def _params(n_axes):
    return pltpu.CompilerParams(
        dimension_semantics=("arbitrary",) * n_axes, vmem_limit_bytes=VMEM_LIMIT)


class _Plan(NamedTuple):
    kernel: Callable
    in_specs: list
    args: list
    out_specs: list
    out_shape: list
    scratch: list


def _launch(plans, grid, name):
    counts = [(len(p.in_specs), len(p.out_specs), len(p.scratch)) for p in plans]
    n_in = sum(c[0] for c in counts)
    n_out = sum(c[1] for c in counts)

    def body(*refs):
        i0, o0, s0 = 0, n_in, n_in + n_out
        for plan, (ni, no, ns) in zip(plans, counts):
            plan.kernel(*refs[i0:i0 + ni], *refs[o0:o0 + no], *refs[s0:s0 + ns])
            i0, o0, s0 = i0 + ni, o0 + no, s0 + ns

    outs = pl.pallas_call(
        body,
        grid=grid,
        in_specs=[s for p in plans for s in p.in_specs],
        out_specs=[s for p in plans for s in p.out_specs],
        out_shape=[s for p in plans for s in p.out_shape],
        scratch_shapes=[s for p in plans for s in p.scratch],
        compiler_params=_params(len(grid)),
        name=name,
    )(*[a for p in plans for a in p.args])
    results, o0 = [], 0
    for _, no, _ in counts:
        results.append(list(outs[o0:o0 + no]))
        o0 += no
    return results if len(plans) > 1 else results[0]


def _dot(a, b):
    return jnp.dot(a, b, preferred_element_type=F32)


def _dot_nt(a, b):
    return lax.dot_general(a, b, (((1,), (1,)), ((), ())), preferred_element_type=F32)


def _dot_tn(a, b):
    return lax.dot_general(a, b, (((0,), (0,)), ((), ())), preferred_element_type=F32)


def _silu(x):
    return x * jax.nn.sigmoid(x)


def _group_rms(x, g, width):
    n, w = x.shape
    x2 = x * x
    cols = range(0, w, LANES)
    stacked = jnp.concatenate([x2[:, j:j + LANES] for j in cols], axis=0)
    hi = stacked.astype(BF16)
    lo = (stacked - hi.astype(F32)).astype(BF16)
    sums = _dot(hi, g) + _dot(lo, g)
    ms = jnp.concatenate([sums[i * n:(i + 1) * n] for i in range(len(cols))], axis=1)
    return x * lax.rsqrt(ms * (1.0 / width) + EPS)


def _rope(x, cos, sin):
    w = x.shape[-1]
    lane = lax.broadcasted_iota(jnp.int32, (1, w), 1)
    first = (lane & (2 * ROPE_PAIR - 1)) < ROPE_PAIR
    swapped = jnp.where(first, pltpu.roll(x, w - ROPE_PAIR, 1), pltpu.roll(x, ROPE_PAIR, 1))
    return x * cos + swapped * sin


def _initial_state(s_ref, gmask):
    s = s_ref[0, 0]
    return (jnp.concatenate([s] * HEADS, axis=1) * gmask).T


def _lane_head(w=PACK_W):
    return lax.broadcasted_iota(jnp.int32, (1, w), 1) // DH


def _row_blocks(n_rows, body):
    n_blk = n_rows // ROW_BLK
    if n_blk == 1:
        body(0)
    else:
        def step(i, carry):
            body(pl.multiple_of(i * ROW_BLK, ROW_BLK))
            return carry
        lax.fori_loop(0, n_blk, step, 0)


def _ada_kernel(cv_ref, w_ref, b_ref, o_ref):
    cv = cv_ref[...]
    o_ref[0, 0] = _dot(_silu(cv), w_ref[0]) + b_ref[0]


def _ada(cv, w_ada, b_ada):
    tn = 1024
    n = 3 * D_MODEL
    return pl.pallas_call(
        _ada_kernel,
        grid=(DEPTH, n // tn),
        in_specs=[
            pl.BlockSpec((8, D_MODEL), lambda l, j: (0, 0)),
            pl.BlockSpec((1, D_MODEL, tn), lambda l, j: (l, 0, j)),
            pl.BlockSpec((1, 1, tn), lambda l, j: (l, 0, j)),
        ],
        out_specs=pl.BlockSpec((1, 1, 8, tn), lambda l, j: (l, j, 0, 0)),
        out_shape=jax.ShapeDtypeStruct((DEPTH, n // tn, 8, tn), F32),
        compiler_params=_params(2),
        name="ada",
    )(cv, w_ada, b_ada.reshape(DEPTH, 1, n))


def _inproj_kernel(x_ref, g_ref, sc_ref, sh_ref, w_ref, zr_ref, zd_ref, zh_ref):
    _project(x_ref[...], g_ref, sc_ref, sh_ref, w_ref, zr_ref, zd_ref, zh_ref)


def _project(x, g_ref, sc_ref, sh_ref, w_ref, zr_ref, zd_ref, zh_ref):
    ms = jnp.mean(x * x, axis=-1, keepdims=True)
    h = x * lax.rsqrt(ms + EPS) * g_ref[...]
    h = h * (1.0 + sc_ref[0]) + sh_ref[0]
    z = _dot(h.astype(BF16), w_ref[...])
    zr_ref[...] = z[:, :RET_COLS].astype(zr_ref.dtype)
    zd_ref[...] = z[:, RET_COLS:RET_COLS + DIFF_COLS].astype(zd_ref.dtype)
    zh_ref[...] = z[:, RET_COLS + DIFF_COLS:]


MOD_SHIFT, MOD_SCALE, MOD_GATE = 0, 1, 2
MOD_ROWS = 8


def _mod_spec(layer, which, per_batch, tiles_per_seq):
    base = (layer * 3 + which) * MOD_ROWS
    if per_batch:
        return pl.BlockSpec((1, 1, D_MODEL), lambda i: (base + 1 + i // tiles_per_seq, 0, 0))
    return pl.BlockSpec((1, 1, D_MODEL), lambda i: (base, 0, 0))


def _inproj(x2d, g, mods, layer, per_batch, w_bf, seq_len):
    m = x2d.shape[0]
    tiles_per_seq = seq_len // TM_IN
    return pl.pallas_call(
        _inproj_kernel,
        grid=(m // TM_IN,),
        in_specs=[
            pl.BlockSpec((TM_IN, D_MODEL), lambda i: (i, 0)),
            pl.BlockSpec((1, D_MODEL), lambda i: (0, 0)),
            _mod_spec(layer, MOD_SCALE, per_batch, tiles_per_seq),
            _mod_spec(layer, MOD_SHIFT, per_batch, tiles_per_seq),
            pl.BlockSpec((D_MODEL, IN_W), lambda i: (0, 0)),
        ],
        out_specs=[
            pl.BlockSpec((TM_IN, RET_COLS), lambda i: (i, 0)),
            pl.BlockSpec((TM_IN, DIFF_COLS), lambda i: (i, 0)),
            pl.BlockSpec((TM_IN, HG_COLS), lambda i: (i, 0)),
        ],
        out_shape=[
            jax.ShapeDtypeStruct((m, RET_COLS), BF16),
            jax.ShapeDtypeStruct((m, DIFF_COLS), BF16),
            jax.ShapeDtypeStruct((m, HG_COLS), F32),
        ],
        compiler_params=_params(1),
        name="inproj",
    )(x2d, g, mods, mods, w_bf)


def _outproj_kernel(x_ref, r_ref, d_ref, h_ref, gate_ref, w_ref, o_ref):
    mixed = jnp.concatenate([r_ref[...], d_ref[...], h_ref[...]], axis=1)
    o_ref[...] = x_ref[...] + gate_ref[0] * _dot(mixed, w_ref[...])


def _bridge_kernel(x_ref, r_ref, d_ref, h_ref, gate_ref, wo_ref, g_ref, sc_ref, sh_ref, wi_ref,
                   y_ref, zr_ref, zd_ref, zh_ref):
    mixed = jnp.concatenate([r_ref[...], d_ref[...], h_ref[...]], axis=1)
    y = x_ref[...] + gate_ref[0] * _dot(mixed, wo_ref[...])
    y_ref[...] = y
    _project(y, g_ref, sc_ref, sh_ref, wi_ref, zr_ref, zd_ref, zh_ref)


def _bridge(x2d, r, d, h, mods, layer, per_batch, wo_bf, g_next, wi_bf, seq_len):
    m = x2d.shape[0]
    tm = TM_OUT
    assert not per_batch or seq_len % tm == 0
    tiles_per_seq = max(seq_len // tm, 1)
    row = lambda i: (i, 0)
    const = lambda i: (0, 0)
    return pl.pallas_call(
        _bridge_kernel,
        grid=(m // tm,),
        in_specs=[
            pl.BlockSpec((tm, D_MODEL), row),
            pl.BlockSpec((tm, PACK_W), row),
            pl.BlockSpec((tm, DIFF_W), row),
            pl.BlockSpec((tm, PACK_W), row),
            _mod_spec(layer, MOD_GATE, per_batch, tiles_per_seq),
            pl.BlockSpec((MIX_W, D_MODEL), const),
            pl.BlockSpec((1, D_MODEL), const),
            _mod_spec(layer + 1, MOD_SCALE, per_batch, tiles_per_seq),
            _mod_spec(layer + 1, MOD_SHIFT, per_batch, tiles_per_seq),
            pl.BlockSpec((D_MODEL, IN_W), const),
        ],
        out_specs=[
            pl.BlockSpec((tm, D_MODEL), row),
            pl.BlockSpec((tm, RET_COLS), row),
            pl.BlockSpec((tm, DIFF_COLS), row),
            pl.BlockSpec((tm, HG_COLS), row),
        ],
        out_shape=[
            jax.ShapeDtypeStruct((m, D_MODEL), F32),
            jax.ShapeDtypeStruct((m, RET_COLS), BF16),
            jax.ShapeDtypeStruct((m, DIFF_COLS), BF16),
            jax.ShapeDtypeStruct((m, HG_COLS), F32),
        ],
        compiler_params=_params(1),
        name="bridge",
    )(x2d, r, d, h, mods, wo_bf, g_next, mods, mods, wi_bf)


def _outproj(x2d, r, d, h, mods, layer, per_batch, w_bf, seq_len):
    m = x2d.shape[0]
    tiles_per_seq = max(seq_len // TM_OUT, 1)
    tm = TM_OUT if (not per_batch or seq_len % TM_OUT == 0) else seq_len
    return pl.pallas_call(
        _outproj_kernel,
        grid=(m // tm,),
        in_specs=[
            pl.BlockSpec((tm, D_MODEL), lambda i: (i, 0)),
            pl.BlockSpec((tm, PACK_W), lambda i: (i, 0)),
            pl.BlockSpec((tm, DIFF_W), lambda i: (i, 0)),
            pl.BlockSpec((tm, PACK_W), lambda i: (i, 0)),
            _mod_spec(layer, MOD_GATE, per_batch, tiles_per_seq),
            pl.BlockSpec((MIX_W, D_MODEL), lambda i: (0, 0)),
        ],
        out_specs=pl.BlockSpec((tm, D_MODEL), lambda i: (i, 0)),
        out_shape=jax.ShapeDtypeStruct((m, D_MODEL), F32),
        compiler_params=_params(1),
        name="outproj",
    )(x2d, r, d, h, mods, w_bf)


def _ret_kernel(*refs, seq_len, latent, layer):
    if latent:
        (lg_ref, z_ref, g_ref, cos_ref, sin_ref, s0f_ref, s0b_ref,
         out_ref, mask_scr, o_scr) = refs
    else:
        lg_ref, z_ref, g_ref = refs[:3]
        prev_refs = refs[3:-5]
        out_ref, sf_ref, sb_ref, mask_scr, o_scr = refs[-5:]
    C = RET_C
    n_chunks = seq_len // C
    lh = _lane_head()

    def lane_vec(direction):
        v = jnp.zeros((1, PACK_W), F32)
        for h in range(HEADS):
            v = jnp.where(lh == h, lg_ref[direction, h], v)
        return v

    lgf, lgb = lane_vec(0), lane_vec(1)
    tau = lax.broadcasted_iota(jnp.int32, (C, 1), 0).astype(F32)
    qdec_f = jnp.exp((tau + 1.0) * lgf)
    qdec_b = jnp.exp((C - tau) * lgb)
    kdec_f = jnp.exp((C - 1.0 - tau) * lgf)
    kdec_b = jnp.exp(tau * lgb)
    sdec_f = jnp.exp(float(C) * lgf)
    sdec_b = jnp.exp(float(C) * lgb)
    gmask = g_ref[...].astype(F32)

    @pl.when(pl.program_id(0) == 0)
    def _():
        dd = (lax.broadcasted_iota(jnp.int32, (C, C), 0)
              - lax.broadcasted_iota(jnp.int32, (C, C), 1)).astype(F32)
        for h in range(HEADS):
            mf = jnp.where(dd >= 0, jnp.exp(jnp.maximum(dd, 0.0) * lg_ref[0, h]), 0.0)
            mb = jnp.where(dd <= 0, jnp.exp(jnp.maximum(-dd, 0.0) * lg_ref[1, h]), 0.0)
            mask_scr[h * C:(h + 1) * C, :] = mf + mb

    def load(r0):
        q = z_ref[pl.ds(r0, C), 0:PACK_W].astype(F32)
        k = z_ref[pl.ds(r0, C), PACK_W:2 * PACK_W].astype(F32) * (DH ** -0.5)
        v = z_ref[pl.ds(r0, C), 2 * PACK_W:3 * PACK_W].astype(F32)
        if latent:
            cos = cos_ref[pl.ds(r0, C), :]
            sin = sin_ref[pl.ds(r0, C), :]
            q = _rope(q, cos, sin)
            k = _rope(k, cos, sin)
        return q, k, v

    def intra(q, k, v):
        qb = q.astype(BF16)
        q4 = jnp.concatenate([jnp.where(lh == h, qb, jnp.zeros_like(qb)) for h in range(HEADS)],
                             axis=0)
        a4 = _dot_nt(q4, k.astype(BF16)) * mask_scr[...]
        o4 = _dot(a4.astype(BF16), v.astype(BF16))
        o = jnp.zeros((C, PACK_W), F32)
        for h in range(HEADS):
            o = o + jnp.where(lh == h, o4[h * C:(h + 1) * C], 0.0)
        return o

    def finish(o, r0):
        gate = z_ref[pl.ds(r0, C), 3 * PACK_W:4 * PACK_W].astype(F32)
        y = _group_rms(o, g_ref[0:LANES, 0:LANES], DH) * _silu(gate)
        out_ref[pl.ds(r0, C), :] = y.astype(out_ref.dtype)

    def state_step(st, k, v, kdec, sdec):
        upd = _dot_tn(v.astype(BF16), (k * kdec).astype(BF16))
        return (st * sdec + upd) * gmask

    if not latent:
        q, k, v = load(0)
        finish(intra(q, k, v), 0)
        vb = v.astype(BF16)
        for ref, kdec in ((sf_ref, kdec_f), (sb_ref, kdec_b)):
            s = _dot_tn((k * kdec).astype(BF16), vb) * gmask
            for h in range(HEADS):
                ref[0, layer, h] = s[h * DH:(h + 1) * DH, h * DH:(h + 1) * DH]
        for ref, prev in zip((sf_ref, sb_ref), prev_refs):
            ref[0, 0:layer] = prev[0]
        return

    def bwd_body(i, st):
        r0 = pl.multiple_of((n_chunks - 1 - i) * C, C)
        q, k, v = load(r0)
        o_scr[pl.ds(r0, C), :] = _dot_nt((q * qdec_b).astype(BF16), st.astype(BF16))
        return state_step(st, k, v, kdec_b, sdec_b)

    lax.fori_loop(0, n_chunks, bwd_body, _initial_state(s0b_ref, gmask), unroll=2)

    def fwd_body(i, st):
        r0 = pl.multiple_of(i * C, C)
        q, k, v = load(r0)
        o = intra(q, k, v) + o_scr[pl.ds(r0, C), :]
        o = o + _dot_nt((q * qdec_f).astype(BF16), st.astype(BF16))
        finish(o, r0)
        return state_step(st, k, v, kdec_f, sdec_f)

    lax.fori_loop(0, n_chunks, fwd_body, _initial_state(s0f_ref, gmask), unroll=2)


def _state_specs(batch, layer, prev):
    spec = lambda n: pl.BlockSpec((1, n, HEADS, DH, DH), lambda b: (b, 0, 0, 0, 0))
    out_specs = [spec(layer + 1)] * 2
    out_shape = [jax.ShapeDtypeStruct((batch, layer + 1, HEADS, DH, DH), F32)] * 2
    in_specs = [spec(layer)] * 2 if layer > 0 else []
    return in_specs, (list(prev) if layer > 0 else []), out_specs, out_shape


def _retention(z_ret, log_g, g256, batch, seq_len, latent, layer, rope=None, states=None,
               plan_only=False):
    m = batch * seq_len
    row = lambda b: (b, 0)
    const2 = lambda b: (0, 0)
    in_specs = [
        pl.BlockSpec(memory_space=pltpu.SMEM),
        pl.BlockSpec((seq_len, RET_COLS), row),
        pl.BlockSpec((PACK_W, PACK_W), const2),
    ]
    args = [log_g, z_ret, g256]
    out_specs = [pl.BlockSpec((seq_len, PACK_W), row)]
    out_shape = [jax.ShapeDtypeStruct((m, PACK_W), BF16)]
    if latent:
        state_spec = pl.BlockSpec((1, 1, PACK_W, DH), lambda b: (b, layer, 0, 0))
        in_specs += [pl.BlockSpec((seq_len, PACK_W), const2)] * 2 + [state_spec] * 2
        args += [rope[0], rope[1], states[0], states[1]]
    else:
        s_in, s_args, s_out, s_shape = _state_specs(batch, layer, states)
        in_specs += s_in
        args += s_args
        out_specs += s_out
        out_shape += s_shape
    plan = _Plan(
        kernel=functools.partial(_ret_kernel, seq_len=seq_len, latent=latent, layer=layer),
        in_specs=in_specs, args=args, out_specs=out_specs, out_shape=out_shape,
        scratch=[
            pltpu.VMEM((HEADS * RET_C, RET_C), F32),
            pltpu.VMEM((seq_len, PACK_W), F32),
        ])
    if plan_only:
        return plan
    return _launch([plan], (batch,), "ret_latent" if latent else "ret_ctx")


def _attn_kernel(*refs, seq_len, past_len, latent, lam_init, layer, n_qblocks):
    if latent:
        (lam_ref, q_ref, k_ref, v_ref, gate_ref, gq_ref, gk_ref, g_ref,
         cosq_ref, sinq_ref, cosk_ref, sink_ref, kc_ref, vc_ref,
         out_ref, k_scr, v_scr) = refs
        prev_refs = ()
    else:
        (lam_ref, q_ref, k_ref, v_ref, gate_ref, gq_ref, gk_ref, g_ref) = refs[:8]
        prev_refs = refs[8:-5]
        out_ref, kn_ref, vn_ref, k_scr, v_scr = refs[-5:]
    ones = jnp.ones((ROW_BLK, DIFF_DV), BF16)

    def put_values(sl, head_values):
        for h in range(HEADS):
            v_scr[sl, 2 * h * DIFF_DV:(2 * h + 1) * DIFF_DV] = head_values(h).astype(BF16)
            v_scr[sl, (2 * h + 1) * DIFF_DV:(2 * h + 2) * DIFF_DV] = ones

    def prepare_keys_values():
        for r in range(seq_len // ROW_BLK):
            sl = pl.ds(r * ROW_BLK, ROW_BLK)
            kn = _group_rms(k_ref[sl, :].astype(F32), g_ref[...], DH) * gk_ref[...]
            v = v_ref[sl, :].astype(F32)
            if latent:
                kn = _rope(kn, cosk_ref[sl, :], sink_ref[sl, :])
            else:
                kn_ref[0, layer, sl, :] = kn
                for h in range(HEADS):
                    vn_ref[0, layer, sl, h, :] = v[:, h * DIFF_DV:(h + 1) * DIFF_DV]
            k_scr[sl, :] = kn.astype(BF16)
            put_values(sl, lambda h: v[:, h * DIFF_DV:(h + 1) * DIFF_DV])
        if prev_refs:
            kn_ref[0, 0:layer] = prev_refs[0][0]
            vn_ref[0, 0:layer] = prev_refs[1][0]
        if latent:
            for r in range(past_len // ROW_BLK):
                src = pl.ds(r * ROW_BLK, ROW_BLK)
                dst = pl.ds(seq_len + r * ROW_BLK, ROW_BLK)
                k_scr[dst, :] = kc_ref[0, 0, :, src].T.astype(BF16)
                put_values(dst, lambda h: vc_ref[0, 0, src, h, :])

    if n_qblocks == 1:
        prepare_keys_values()
    else:
        pl.when(pl.program_id(1) == 0)(prepare_keys_values)

    lam = lam_ref[0]
    qn = _group_rms(q_ref[...].astype(F32), g_ref[...], DH) * gq_ref[...]
    if latent:
        qn = _rope(qn, cosq_ref[...], sinq_ref[...])
    qn = qn * (DH ** -0.5 * LOG2E)
    lane = lax.broadcasted_iota(jnp.int32, (1, 2 * DH), 1)
    for h in range(HEADS):
        hs = slice(h * 2 * DH, (h + 1) * 2 * DH)
        vx = v_scr[:, 2 * h * DIFF_DV:(2 * h + 2) * DIFF_DV]
        qh = qn[:, hs]
        kh = k_scr[:, hs]
        parts = []
        for first_map in (True, False):
            qm = jnp.where((lane < DH) == first_map, qh, 0.0).astype(BF16)
            s = _dot_nt(qm, kh)
            e = jnp.exp2(s - jnp.max(s, axis=-1, keepdims=True)).astype(BF16)
            ox = _dot(e, vx)
            parts.append(ox[:, :DIFF_DV] / ox[:, DIFF_DV:])
        o = parts[0] - lam * parts[1]
        ms = jnp.mean(o * o, axis=-1, keepdims=True)
        y = o * lax.rsqrt(ms + EPS) * (1.0 - lam_init) * _silu(gate_ref[:, hs].astype(F32))
        out_ref[:, hs] = y.astype(out_ref.dtype)


def _diff_attention(z_diff, lam, gq, gk, g128, batch, seq_len, latent, lam_init, layer,
                    rope=None, ctx=None, caches=None, plan_only=False):
    m = batch * seq_len
    tq = min(TQ, seq_len)
    nq = seq_len // tq
    past_len = ctx[1].shape[2] if latent else 0
    qmap = lambda col: (lambda b, *i: (b * nq + (i[0] if i else 0), col))
    kvmap = lambda col: (lambda b, *i: (b, col))
    const2 = lambda b, *i: (0, 0)
    in_specs = [
        pl.BlockSpec(memory_space=pltpu.SMEM),
        pl.BlockSpec((tq, DIFF_W), qmap(0)),
        pl.BlockSpec((seq_len, DIFF_W), kvmap(1)),
        pl.BlockSpec((seq_len, DIFF_W), kvmap(2)),
        pl.BlockSpec((tq, DIFF_W), qmap(3)),
        pl.BlockSpec((1, DIFF_W), const2),
        pl.BlockSpec((1, DIFF_W), const2),
        pl.BlockSpec((LANES, LANES), const2),
    ]
    args = [lam, z_diff, z_diff, z_diff, z_diff, gq, gk, g128]
    out_specs = [pl.BlockSpec((tq, DIFF_W), qmap(0))]
    out_shape = [jax.ShapeDtypeStruct((m, DIFF_W), BF16)]
    if latent:
        in_specs += [
            pl.BlockSpec((tq, DIFF_W), lambda b, i: (i, 0)),
            pl.BlockSpec((tq, DIFF_W), lambda b, i: (i, 0)),
            pl.BlockSpec((seq_len, DIFF_W), const2),
            pl.BlockSpec((seq_len, DIFF_W), const2),
            pl.BlockSpec((1, 1, DIFF_W, past_len), lambda b, i: (b, layer, 0, 0)),
            pl.BlockSpec((1, 1, past_len, HEADS, DIFF_DV), lambda b, i: (b, layer, 0, 0, 0)),
        ]
        args += [rope[0], rope[1], rope[0], rope[1], ctx[0], ctx[1]]
    else:
        cache_spec = lambda n: pl.BlockSpec((1, n, seq_len, DIFF_W), lambda b, *i: (b, 0, 0, 0))
        vcache_spec = lambda n: pl.BlockSpec((1, n, seq_len, HEADS, DIFF_DV),
                                             lambda b, *i: (b, 0, 0, 0, 0))
        out_specs += [cache_spec(layer + 1), vcache_spec(layer + 1)]
        out_shape += [jax.ShapeDtypeStruct((batch, layer + 1, seq_len, DIFF_W), F32),
                      jax.ShapeDtypeStruct((batch, layer + 1, seq_len, HEADS, DIFF_DV), F32)]
        if layer > 0:
            in_specs += [cache_spec(layer), vcache_spec(layer)]
            args += list(caches)
    plan = _Plan(
        kernel=functools.partial(_attn_kernel, seq_len=seq_len, past_len=past_len,
                                 latent=latent, lam_init=lam_init, layer=layer, n_qblocks=nq),
        in_specs=in_specs, args=args, out_specs=out_specs, out_shape=out_shape,
        scratch=[
            pltpu.VMEM((seq_len + past_len, DIFF_W), BF16),
            pltpu.VMEM((seq_len + past_len, 2 * DIFF_W), BF16),
        ])
    if plan_only:
        assert nq == 1
        return plan
    return _launch([plan], (batch, nq), "attn_latent" if latent else "attn_ctx")


def _hgrn_kernel(*refs, seq_len, latent, layer):
    if latent:
        (z_ref, lb_ref, tri_ref, g_ref, lm_ref, dm_ref, s0f_ref, s0b_ref,
         out_ref, qi_scr, ks_scr, dec_scr, o_scr, oi_scr) = refs
    else:
        z_ref, lb_ref, tri_ref, g_ref, lm_ref, dm_ref = refs[:6]
        prev_refs = refs[6:-8]
        out_ref, sf_ref, sb_ref, qi_scr, ks_scr, dec_scr, o_scr, oi_scr = refs[-8:]
    C, SUB = HG_C, HG_SUB
    n_chunks = seq_len // C
    chunks_per_blk = ROW_BLK // C
    lh = _lane_head()
    sub_row = lax.broadcasted_iota(jnp.int32, (ROW_BLK, 1), 0) & (SUB - 1)
    chunk_row = lax.broadcasted_iota(jnp.int32, (C, 1), 0)

    def tile_roll(x, shift):
        return jnp.concatenate(
            [pltpu.roll(x[i:i + SUBLANES], shift, 0) for i in range(0, ROW_BLK, SUBLANES)], axis=0)

    def head_rep(x):
        return jnp.concatenate([jnp.where(lh == h, x, jnp.zeros_like(x)) for h in range(HEADS)],
                               axis=0)

    def intra(blk):
        start = blk * ROW_BLK if isinstance(blk, int) else pl.multiple_of(blk * ROW_BLK, ROW_BLK)
        rows = pl.ds(start, ROW_BLK)
        lb = lb_ref[...]
        q = z_ref[rows, 0:PACK_W] * (DH ** -0.5)
        v = z_ref[rows, 3 * PACK_W:4 * PACK_W]
        v_reps = [head_rep(v[c * C:(c + 1) * C].astype(BF16)) for c in range(chunks_per_blk)]
        kks, bcs, a_nears = [], [], []
        for d in range(2):
            f = lb + (1.0 - lb) * jax.nn.sigmoid(z_ref[rows, (1 + d) * PACK_W:(2 + d) * PACK_W])
            kks.append(1.0 - f)
            l2 = jnp.log(f) * LOG2E
            hi = l2.astype(BF16)
            r1 = l2 - hi.astype(F32)
            mid = r1.astype(BF16)
            lo = (r1 - mid.astype(F32)).astype(BF16)
            cs = _dot(tri_ref[d], jnp.concatenate([hi, mid, lo], axis=1))
            bcs.append(cs[:, 0:PACK_W] + cs[:, PACK_W:2 * PACK_W] + cs[:, 2 * PACK_W:])

        for d in range(2):
            bwd = d == 1
            kk, bc = kks[d], bcs[d]
            ps = [(q * kk).astype(BF16)]
            for dist in range(1, SUB):
                shift = SUBLANES - dist if bwd else dist
                valid = (sub_row < SUB - dist) if bwd else (sub_row >= dist)
                arg = jnp.where(valid, bc - tile_roll(bc, shift), NEG_BIG)
                ps.append((q * tile_roll(kk, shift) * jnp.exp2(arg)).astype(BF16))
            r = _dot(jnp.concatenate(ps, axis=0), g_ref[...]).astype(BF16)
            a_near = r[0:ROW_BLK] * dm_ref[d, 0]
            for dist in range(1, SUB):
                a_near = a_near + r[dist * ROW_BLK:(dist + 1) * ROW_BLK] * dm_ref[d, dist]
            a_nears.append(a_near)

        for c in range(chunks_per_blk):
            for d in range(2):
                bwd = d == 1
                kk, bc, a_near = kks[d], bcs[d], a_nears[d]
                cs_ = slice(c * C, (c + 1) * C)
                qc, kc, bcc = q[cs_], kk[cs_], bc[cs_]
                last = bcc[0:1] if bwd else bcc[C - 1:C]
                a = a_near[cs_]
                for li, m in enumerate(HG_LEVELS):
                    pieces = []
                    for b2 in range(C // (2 * m)):
                        rr = 2 * m * b2 + (m if bwd else m - 1)
                        pieces.append(jnp.broadcast_to(bcc[rr:rr + 1], (2 * m, PACK_W)))
                    anchor = jnp.concatenate(pieces, axis=0) if len(pieces) > 1 else pieces[0]
                    upper = (chunk_row & (2 * m - 1)) >= m
                    later = jnp.logical_not(upper) if bwd else upper
                    e = jnp.exp2((bcc - anchor) * jnp.where(later, 1.0, -1.0))
                    a = a + _dot_nt((qc * e).astype(BF16),
                                    head_rep((kc * e).astype(BF16))).astype(BF16) * lm_ref[d, li]
                crow = pl.ds(start + c * C, C)
                o_scr[d, crow, :] = _dot(a, v_reps[c])
                qi_scr[d, crow, :] = (qc * jnp.exp2(bcc)).astype(BF16)
                ks_scr[d, crow, :] = (kc * jnp.exp2(last - bcc)).astype(BF16)
                slot = (blk * chunks_per_blk + c) * SUBLANES
                if not isinstance(slot, int):
                    slot = pl.multiple_of(slot, SUBLANES)
                dec_scr[d, pl.ds(slot, SUBLANES), :] = jnp.broadcast_to(
                    jnp.exp2(last), (SUBLANES, PACK_W))

        if not latent:
            vb = v.astype(BF16)
            for d, ref in enumerate((sf_ref, sb_ref)):
                bwd = d == 1
                sweep = range(chunks_per_blk - 1, -1, -1) if bwd else range(chunks_per_blk)
                tail = jnp.zeros((1, PACK_W), F32)
                decayed = [None] * chunks_per_blk
                for c in reversed(sweep):
                    cs_ = slice(c * C, (c + 1) * C)
                    bcc = bcs[d][cs_]
                    total = tail + (bcc[0:1] if bwd else bcc[C - 1:C])
                    decayed[c] = (kks[d][cs_] * jnp.exp2(total - bcc)).astype(BF16)
                    tail = total
                s = _dot_tn(jnp.concatenate(decayed, axis=0), vb) * g_ref[...].astype(F32)
                for h in range(HEADS):
                    ref[0, layer, h] = s[h * DH:(h + 1) * DH, h * DH:(h + 1) * DH]
                if prev_refs:
                    ref[0, 0:layer] = prev_refs[d][0]

    n_blk = seq_len // ROW_BLK
    assert latent or n_blk == 1
    if n_blk == 1:
        intra(0)
    else:
        def intra_step(i, carry):
            intra(i)
            return carry
        lax.fori_loop(0, n_blk, intra_step, 0)

    gmask = g_ref[...].astype(F32)
    if latent:
        states = (_initial_state(s0f_ref, gmask), _initial_state(s0b_ref, gmask))
    else:
        states = (jnp.zeros((PACK_W, PACK_W), F32),) * 2

    def body(i, states):
        new_states = []
        for d in range(2):
            c = (n_chunks - 1 - i) if d == 1 else i
            rows = pl.ds(pl.multiple_of(c * C, C), C)
            st = states[d]
            oi_scr[d, rows, :] = _dot_nt(qi_scr[d, rows, :], st.astype(BF16))
            dec = dec_scr[d, pl.ds(pl.multiple_of(c * SUBLANES, SUBLANES), 1), :]
            vb = z_ref[rows, 3 * PACK_W:4 * PACK_W].astype(BF16)
            new_states.append(st * dec + _dot_tn(vb, ks_scr[d, rows, :]) * gmask)
        return tuple(new_states)

    lax.fori_loop(0, n_chunks, body, states, unroll=8 if n_chunks % 8 == 0 else 4)

    def finish(start):
        rows = pl.ds(start, ROW_BLK)
        tot = (o_scr[0, rows, :] + oi_scr[0, rows, :]) + (o_scr[1, rows, :] + oi_scr[1, rows, :])
        gate = z_ref[rows, 4 * PACK_W:5 * PACK_W]
        y = _group_rms(tot, g_ref[0:LANES, 0:LANES], DH) * _silu(gate)
        out_ref[rows, :] = y.astype(out_ref.dtype)

    _row_blocks(seq_len, finish)


def _hgrn(z_hg, lb, consts, batch, seq_len, latent, layer, states=None, plan_only=False):
    m = batch * seq_len
    tri, g256, lm, dm = consts
    row = lambda b: (b, 0)
    in_specs = [
        pl.BlockSpec((seq_len, HG_COLS), row),
        pl.BlockSpec((1, PACK_W), lambda b: (0, 0)),
        pl.BlockSpec(tri.shape, lambda b: (0, 0, 0)),
        pl.BlockSpec((PACK_W, PACK_W), lambda b: (0, 0)),
        pl.BlockSpec(lm.shape, lambda b: (0, 0, 0, 0)),
        pl.BlockSpec(dm.shape, lambda b: (0, 0, 0, 0)),
    ]
    args = [z_hg, lb, tri, g256, lm, dm]
    out_specs = [pl.BlockSpec((seq_len, PACK_W), row)]
    out_shape = [jax.ShapeDtypeStruct((m, PACK_W), BF16)]
    if latent:
        in_specs += [pl.BlockSpec((1, 1, PACK_W, DH), lambda b: (b, layer, 0, 0))] * 2
        args += [states[0], states[1]]
    else:
        s_in, s_args, s_out, s_shape = _state_specs(batch, layer, states)
        in_specs += s_in
        args += s_args
        out_specs += s_out
        out_shape += s_shape
    plan = _Plan(
        kernel=functools.partial(_hgrn_kernel, seq_len=seq_len, latent=latent, layer=layer),
        in_specs=in_specs, args=args, out_specs=out_specs, out_shape=out_shape,
        scratch=[
            pltpu.VMEM((2, seq_len, PACK_W), BF16),
            pltpu.VMEM((2, seq_len, PACK_W), BF16),
            pltpu.VMEM((2, seq_len // HG_C * SUBLANES, PACK_W), F32),
            pltpu.VMEM((2, seq_len, PACK_W), F32),
            pltpu.VMEM((2, seq_len, PACK_W), F32),
        ])
    if plan_only:
        return plan
    return _launch([plan], (batch,), "hgrn_latent" if latent else "hgrn_ctx")


def _block_ones(n):
    idx = np.arange(n) // DH
    return jnp.asarray((idx[:, None] == idx[None, :]).astype(np.float32), dtype=BF16)


def _hgrn_consts():
    C = HG_C
    t = np.arange(ROW_BLK)
    same_chunk = (t[:, None] // C) == (t[None, :] // C)
    tri_f = same_chunk & (t[None, :] <= t[:, None])
    tri_b = same_chunk & (t[None, :] >= t[:, None])
    tc = np.arange(C)
    lms = []
    for fwd in (True, False):
        per_level = []
        for m in HG_LEVELS:
            same = (tc[:, None] // (2 * m)) == (tc[None, :] // (2 * m))
            upper = (tc % (2 * m)) >= m
            later, earlier = (upper, ~upper) if fwd else (~upper, upper)
            per_level.append(np.tile(same & later[:, None] & earlier[None, :], (1, HEADS)))
        lms.append(np.stack(per_level))
    dms = [[np.tile(tc[None, :] == tc[:, None] + sign * dist, (ROW_BLK // C, HEADS))
            for dist in range(HG_SUB)] for sign in (-1, 1)]
    return (jnp.asarray(np.stack([tri_f, tri_b]).astype(np.float32), dtype=BF16),
            _block_ones(PACK_W),
            jnp.asarray(np.stack(lms).astype(np.float32), dtype=BF16),
            jnp.asarray(np.stack(dms).astype(np.float32), dtype=BF16))


def _rope_tables(seq_len, width):
    t = np.arange(seq_len)
    pos = np.stack([t // GRID_W, t % GRID_W], axis=1).astype(np.float32)
    j = np.arange(DH)
    axis = j // (2 * ROPE_PAIR)
    jj = j % (2 * ROPE_PAIR)
    inv = (ROPE_BASE ** (-(jj % ROPE_PAIR).astype(np.float64) / ROPE_PAIR)).astype(np.float32)
    ang = (pos[:, axis] * inv[None, :]).astype(np.float64)
    cos = np.cos(ang)
    sin = np.where(jj < ROPE_PAIR, -np.sin(ang), np.sin(ang))
    reps = width // DH
    return (jnp.asarray(np.tile(cos, (1, reps)), dtype=F32),
            jnp.asarray(np.tile(sin, (1, reps)), dtype=F32))


def _layer(x2d, z, batch, seq_len, mods, l, p, latent, ctx, caches, states=None):
    z_ret, z_diff, z_hg = z
    lam_init = 0.8 - 0.6 * math.exp(-0.3 * l)
    if latent:
        k_ctx, v_ctx, s_rf, s_rb, s_hf, s_hb = ctx
        r = _retention(z_ret, p['log_g'][l], p['g256'], batch, seq_len, True, l,
                       rope=p['rope256'], states=(s_rf, s_rb))[0]
        a = _diff_attention(z_diff, p['lam'][l], p['gq'][l], p['gk'][l], p['g128'], batch,
                            seq_len, True, lam_init, l, rope=p['rope512'],
                            ctx=(k_ctx, v_ctx))[0]
        hgo = _hgrn(z_hg, p['lb'][l], p['hg_consts'], batch, seq_len, True, l,
                    states=(s_hf, s_hb))[0]
        extras = None
    else:
        prev = states if states is not None else (None,) * 4
        plans = [
            _retention(z_ret, p['log_g'][l], p['g256'], batch, seq_len, False, l,
                       states=prev[0:2], plan_only=True),
            _diff_attention(z_diff, p['lam'][l], p['gq'][l], p['gk'][l], p['g128'], batch,
                            seq_len, False, lam_init, l, caches=caches, plan_only=True),
            _hgrn(z_hg, p['lb'][l], p['hg_consts'], batch, seq_len, False, l,
                  states=prev[2:4], plan_only=True),
        ]
        (r, s_rf, s_rb), (a, kcache, vcache), (hgo, s_hf, s_hb) = _launch(
            plans, (batch,), "mix_ctx")
        extras = ((kcache, vcache), (s_rf, s_rb, s_hf, s_hb))
    if l + 1 < DEPTH:
        y, *z_next = _bridge(x2d, r, a, hgo, mods, l, latent, p['w_out'][l],
                             p['norm_g'][l + 1], p['w_in'][l + 1], seq_len)
    else:
        y = _outproj(x2d, r, a, hgo, mods, l, latent, p['w_out'][l], seq_len)
        z_next = None
    return y, z_next, extras


def kernel(x_prompt, x_sample, c, c_ctx, cache_diff_k, cache_diff_v, state_ret_fwd,
           state_ret_bwd, state_hgrn_fwd, state_hgrn_bwd, norm_g, w_ada, b_ada, w_in,
           w_out, ret_decay_logit, diff_qn_g, diff_kn_g, diff_lambda, hgrn_lb_logit):
    batch, seq, _ = x_prompt.shape
    dec_batch, dec_seq, _ = x_sample.shape
    past_len = cache_diff_k.shape[2]

    assert 1 + dec_batch <= MOD_ROWS
    cv = jnp.zeros((MOD_ROWS, D_MODEL), F32).at[0].set(c_ctx).at[1:1 + dec_batch].set(c)
    mods = _ada(cv, w_ada, b_ada).reshape(DEPTH * 3 * MOD_ROWS, 1, D_MODEL)

    lb_all = jax.nn.softmax(hgrn_lb_logit.astype(F32), axis=0)
    lb_all = jnp.cumsum(lb_all, axis=0) - lb_all[0]
    lp = diff_lambda.astype(F32)
    lam_inits = jnp.asarray([0.8 - 0.6 * math.exp(-0.3 * l) for l in range(DEPTH)], F32)
    lam = (jnp.exp(jnp.sum(lp[:, 0] * lp[:, 1], axis=-1))
           - jnp.exp(jnp.sum(lp[:, 2] * lp[:, 3], axis=-1)) + lam_inits)
    p = {
        'norm_g': norm_g.reshape(DEPTH, 1, D_MODEL),
        'w_in': w_in.astype(BF16),
        'w_out': w_out.astype(BF16),
        'log_g': jax.nn.log_sigmoid(ret_decay_logit.astype(F32)),
        'lam': lam.reshape(DEPTH, 1),
        'gq': jnp.tile(diff_qn_g, (1, DIFF_W // DH)).reshape(DEPTH, 1, DIFF_W),
        'gk': jnp.tile(diff_kn_g, (1, DIFF_W // DH)).reshape(DEPTH, 1, DIFF_W),
        'lb': lb_all.reshape(DEPTH, 1, PACK_W),
        'g256': _block_ones(PACK_W),
        'g128': _block_ones(LANES),
        'hg_consts': _hgrn_consts(),
        'rope256': _rope_tables(dec_seq, PACK_W),
        'rope512': _rope_tables(dec_seq, DIFF_W),
    }

    y = x_prompt.reshape(batch * seq, D_MODEL)
    caches = None
    new_states = None
    z = _inproj(y, p['norm_g'][0], mods, 0, False, p['w_in'][0], seq)
    for l in range(DEPTH):
        y, z, (caches, new_states) = _layer(y, z, batch, seq, mods, l, p, False, None, caches,
                                            new_states)
    y_prompt = y.reshape(batch, seq, D_MODEL)
    new_k = caches[0].reshape(batch, DEPTH, seq, HEADS, 2, DH)
    new_v = caches[1]

    y = x_sample.reshape(dec_batch * dec_seq, D_MODEL)
    rows = lambda s: s.reshape(dec_batch, DEPTH, PACK_W, DH)
    keys_t = jnp.transpose(cache_diff_k, (0, 1, 3, 4, 5, 2)).reshape(
        dec_batch, DEPTH, DIFF_W, past_len)
    ctx = (keys_t, cache_diff_v, rows(state_ret_fwd), rows(state_ret_bwd),
           rows(state_hgrn_fwd), rows(state_hgrn_bwd))
    z = _inproj(y, p['norm_g'][0], mods, 0, True, p['w_in'][0], dec_seq)
    for l in range(DEPTH):
        y, z, _ = _layer(y, z, dec_batch, dec_seq, mods, l, p, True, ctx, None)
    y_sample = y.reshape(dec_batch, dec_seq, D_MODEL)

    return (y_prompt, y_sample, new_k, new_v, *new_states)
```

```python
import functools
import math
from typing import Callable, NamedTuple

import numpy as np
import jax
import jax.numpy as jnp
from jax import lax
from jax.experimental import pallas as pl
from jax.experimental.pallas import tpu as pltpu

F32 = jnp.float32
BF16 = jnp.bfloat16

D_MODEL = 1024
DEPTH = 2
GRID_W = 64
HEADS = 4
DH = 64
PACK_W = HEADS * DH
DIFF_W = 512
DIFF_DV = 128
RET_COLS = 4 * PACK_W
DIFF_COLS = 4 * DIFF_W
HG_COLS = 5 * PACK_W
IN_W = RET_COLS + DIFF_COLS + HG_COLS
MIX_W = PACK_W + DIFF_W + PACK_W
ROPE_BASE = 10000.0
ROPE_PAIR = 16
EPS = 1e-6
LOG2E = 1.4426950408889634
NEG_BIG = -1e30

RET_C = 256
HG_C = 64
HG_SUB = 4
HG_LEVELS = (4, 8, 16, 32)
SUBLANES = 8
LANES = 128
ROW_BLK = 256
TQ = 512
TM_IN = 1024
TM_OUT = 512

VMEM_LIMIT = 56 * 1024 * 1024


def _params(n_axes):
    return pltpu.CompilerParams(
        dimension_semantics=("arbitrary",) * n_axes, vmem_limit_bytes=VMEM_LIMIT)


class _Plan(NamedTuple):
    kernel: Callable
    in_specs: list
    args: list
    out_specs: list
    out_shape: list
    scratch: list


def _launch(plans, grid, name):
    counts = [(len(p.in_specs), len(p.out_specs), len(p.scratch)) for p in plans]
    n_in = sum(c[0] for c in counts)
    n_out = sum(c[1] for c in counts)

    def body(*refs):
        i0, o0, s0 = 0, n_in, n_in + n_out
        for plan, (ni, no, ns) in zip(plans, counts):
            plan.kernel(*refs[i0:i0 + ni], *refs[o0:o0 + no], *refs[s0:s0 + ns])
            i0, o0, s0 = i0 + ni, o0 + no, s0 + ns

    outs = pl.pallas_call(
        body,
        grid=grid,
        in_specs=[s for p in plans for s in p.in_specs],
        out_specs=[s for p in plans for s in p.out_specs],
        out_shape=[s for p in plans for s in p.out_shape],
        scratch_shapes=[s for p in plans for s in p.scratch],
        compiler_params=_params(len(grid)),
        name=name,
    )(*[a for p in plans for a in p.args])
    results, o0 = [], 0
    for _, no, _ in counts:
        results.append(list(outs[o0:o0 + no]))
        o0 += no
    return results if len(plans) > 1 else results[0]


def _dot(a, b):
    return jnp.dot(a, b, preferred_element_type=F32)


def _dot_nt(a, b):
    return lax.dot_general(a, b, (((1,), (1,)), ((), ())), preferred_element_type=F32)


def _dot_tn(a, b):
    return lax.dot_general(a, b, (((0,), (0,)), ((), ())), preferred_element_type=F32)


def _silu(x):
    return x * jax.nn.sigmoid(x)


def _group_rms(x, g, width):
    n, w = x.shape
    x2 = x * x
    cols = range(0, w, LANES)
    stacked = jnp.concatenate([x2[:, j:j + LANES] for j in cols], axis=0)
    hi = stacked.astype(BF16)
    lo = (stacked - hi.astype(F32)).astype(BF16)
    sums = _dot(hi, g) + _dot(lo, g)
    ms = jnp.concatenate([sums[i * n:(i + 1) * n] for i in range(len(cols))], axis=1)
    return x * lax.rsqrt(ms * (1.0 / width) + EPS)


def _rope(x, cos, sin):
    w = x.shape[-1]
    lane = lax.broadcasted_iota(jnp.int32, (1, w), 1)
    first = (lane & (2 * ROPE_PAIR - 1)) < ROPE_PAIR
    swapped = jnp.where(first, pltpu.roll(x, w - ROPE_PAIR, 1), pltpu.roll(x, ROPE_PAIR, 1))
    return x * cos + swapped * sin


def _initial_state(s_ref, gmask):
    s = s_ref[0, 0]
    return (jnp.concatenate([s] * HEADS, axis=1) * gmask).T


def _lane_head(w=PACK_W):
    return lax.broadcasted_iota(jnp.int32, (1, w), 1) // DH


def _row_blocks(n_rows, body):
    n_blk = n_rows // ROW_BLK
    if n_blk == 1:
        body(0)
    else:
        def step(i, carry):
            body(pl.multiple_of(i * ROW_BLK, ROW_BLK))
            return carry
        lax.fori_loop(0, n_blk, step, 0)


def _ada_kernel(cv_ref, w_ref, b_ref, o_ref):
    cv = cv_ref[...]
    o_ref[0, 0] = _dot(_silu(cv), w_ref[0]) + b_ref[0]


def _ada(cv, w_ada, b_ada):
    tn = 1024
    n = 3 * D_MODEL
    return pl.pallas_call(
        _ada_kernel,
        grid=(DEPTH, n // tn),
        in_specs=[
            pl.BlockSpec((8, D_MODEL), lambda l, j: (0, 0)),
            pl.BlockSpec((1, D_MODEL, tn), lambda l, j: (l, 0, j)),
            pl.BlockSpec((1, 1, tn), lambda l, j: (l, 0, j)),
        ],
        out_specs=pl.BlockSpec((1, 1, 8, tn), lambda l, j: (l, j, 0, 0)),
        out_shape=jax.ShapeDtypeStruct((DEPTH, n // tn, 8, tn), F32),
        compiler_params=_params(2),
        name="ada",
    )(cv, w_ada, b_ada.reshape(DEPTH, 1, n))


def _inproj_kernel(x_ref, g_ref, sc_ref, sh_ref, w_ref, zr_ref, zd_ref, zh_ref):
    _project(x_ref[...], g_ref, sc_ref, sh_ref, w_ref, zr_ref, zd_ref, zh_ref)


def _project(x, g_ref, sc_ref, sh_ref, w_ref, zr_ref, zd_ref, zh_ref):
    ms = jnp.mean(x * x, axis=-1, keepdims=True)
    h = x * lax.rsqrt(ms + EPS) * g_ref[...]
    h = h * (1.0 + sc_ref[0]) + sh_ref[0]
    z = _dot(h.astype(BF16), w_ref[...])
    zr_ref[...] = z[:, :RET_COLS].astype(zr_ref.dtype)
    zd_ref[...] = z[:, RET_COLS:RET_COLS + DIFF_COLS].astype(zd_ref.dtype)
    zh_ref[...] = z[:, RET_COLS + DIFF_COLS:]


MOD_SHIFT, MOD_SCALE, MOD_GATE = 0, 1, 2
MOD_ROWS = 8


def _mod_spec(layer, which, per_batch, tiles_per_seq):
    base = (layer * 3 + which) * MOD_ROWS
    if per_batch:
        return pl.BlockSpec((1, 1, D_MODEL), lambda i: (base + 1 + i // tiles_per_seq, 0, 0))
    return pl.BlockSpec((1, 1, D_MODEL), lambda i: (base, 0, 0))


def _inproj(x2d, g, mods, layer, per_batch, w_bf, seq_len):
    m = x2d.shape[0]
    tiles_per_seq = seq_len // TM_IN
    return pl.pallas_call(
        _inproj_kernel,
        grid=(m // TM_IN,),
        in_specs=[
            pl.BlockSpec((TM_IN, D_MODEL), lambda i: (i, 0)),
            pl.BlockSpec((1, D_MODEL), lambda i: (0, 0)),
            _mod_spec(layer, MOD_SCALE, per_batch, tiles_per_seq),
            _mod_spec(layer, MOD_SHIFT, per_batch, tiles_per_seq),
            pl.BlockSpec((D_MODEL, IN_W), lambda i: (0, 0)),
        ],
        out_specs=[
            pl.BlockSpec((TM_IN, RET_COLS), lambda i: (i, 0)),
            pl.BlockSpec((TM_IN, DIFF_COLS), lambda i: (i, 0)),
            pl.BlockSpec((TM_IN, HG_COLS), lambda i: (i, 0)),
        ],
        out_shape=[
            jax.ShapeDtypeStruct((m, RET_COLS), BF16),
            jax.ShapeDtypeStruct((m, DIFF_COLS), BF16),
            jax.ShapeDtypeStruct((m, HG_COLS), F32),
        ],
        compiler_params=_params(1),
        name="inproj",
    )(x2d, g, mods, mods, w_bf)


def _outproj_kernel(x_ref, r_ref, d_ref, h_ref, gate_ref, w_ref, o_ref):
    mixed = jnp.concatenate([r_ref[...], d_ref[...], h_ref[...]], axis=1)
    o_ref[...] = x_ref[...] + gate_ref[0] * _dot(mixed, w_ref[...])


def _bridge_kernel(x_ref, r_ref, d_ref, h_ref, gate_ref, wo_ref, g_ref, sc_ref, sh_ref, wi_ref,
                   y_ref, zr_ref, zd_ref, zh_ref):
    mixed = jnp.concatenate([r_ref[...], d_ref[...], h_ref[...]], axis=1)
    y = x_ref[...] + gate_ref[0] * _dot(mixed, wo_ref[...])
    y_ref[...] = y
    _project(y, g_ref, sc_ref, sh_ref, wi_ref, zr_ref, zd_ref, zh_ref)


def _bridge(x2d, r, d, h, mods, layer, per_batch, wo_bf, g_next, wi_bf, seq_len):
    m = x2d.shape[0]
    tm = TM_OUT
    assert not per_batch or seq_len % tm == 0
    tiles_per_seq = max(seq_len // tm, 1)
    row = lambda i: (i, 0)
    const = lambda i: (0, 0)
    return pl.pallas_call(
        _bridge_kernel,
        grid=(m // tm,),
        in_specs=[
            pl.BlockSpec((tm, D_MODEL), row),
            pl.BlockSpec((tm, PACK_W), row),
            pl.BlockSpec((tm, DIFF_W), row),
            pl.BlockSpec((tm, PACK_W), row),
            _mod_spec(layer, MOD_GATE, per_batch, tiles_per_seq),
            pl.BlockSpec((MIX_W, D_MODEL), const),
            pl.BlockSpec((1, D_MODEL), const),
            _mod_spec(layer + 1, MOD_SCALE, per_batch, tiles_per_seq),
            _mod_spec(layer + 1, MOD_SHIFT, per_batch, tiles_per_seq),
            pl.BlockSpec((D_MODEL, IN_W), const),
        ],
        out_specs=[
            pl.BlockSpec((tm, D_MODEL), row),
            pl.BlockSpec((tm, RET_COLS), row),
            pl.BlockSpec((tm, DIFF_COLS), row),
            pl.BlockSpec((tm, HG_COLS), row),
        ],
        out_shape=[
            jax.ShapeDtypeStruct((m, D_MODEL), F32),
            jax.ShapeDtypeStruct((m, RET_COLS), BF16),
            jax.ShapeDtypeStruct((m, DIFF_COLS), BF16),
            jax.ShapeDtypeStruct((m, HG_COLS), F32),
        ],
        compiler_params=_params(1),
        name="bridge",
    )(x2d, r, d, h, mods, wo_bf, g_next, mods, mods, wi_bf)


def _outproj(x2d, r, d, h, mods, layer, per_batch, w_bf, seq_len):
    m = x2d.shape[0]
    tiles_per_seq = max(seq_len // TM_OUT, 1)
    tm = TM_OUT if (not per_batch or seq_len % TM_OUT == 0) else seq_len
    return pl.pallas_call(
        _outproj_kernel,
        grid=(m // tm,),
        in_specs=[
            pl.BlockSpec((tm, D_MODEL), lambda i: (i, 0)),
            pl.BlockSpec((tm, PACK_W), lambda i: (i, 0)),
            pl.BlockSpec((tm, DIFF_W), lambda i: (i, 0)),
            pl.BlockSpec((tm, PACK_W), lambda i: (i, 0)),
            _mod_spec(layer, MOD_GATE, per_batch, tiles_per_seq),
            pl.BlockSpec((MIX_W, D_MODEL), lambda i: (0, 0)),
        ],
        out_specs=pl.BlockSpec((tm, D_MODEL), lambda i: (i, 0)),
        out_shape=jax.ShapeDtypeStruct((m, D_MODEL), F32),
        compiler_params=_params(1),
        name="outproj",
    )(x2d, r, d, h, mods, w_bf)


def _ret_kernel(*refs, seq_len, latent, layer):
    if latent:
        (lg_ref, z_ref, g_ref, cos_ref, sin_ref, s0f_ref, s0b_ref,
         out_ref, mask_scr, o_scr) = refs
    else:
        lg_ref, z_ref, g_ref = refs[:3]
        prev_refs = refs[3:-5]
        out_ref, sf_ref, sb_ref, mask_scr, o_scr = refs[-5:]
    C = RET_C
    n_chunks = seq_len // C
    lh = _lane_head()

    def lane_vec(direction):
        v = jnp.zeros((1, PACK_W), F32)
        for h in range(HEADS):
            v = jnp.where(lh == h, lg_ref[direction, h], v)
        return v

    lgf, lgb = lane_vec(0), lane_vec(1)
    tau = lax.broadcasted_iota(jnp.int32, (C, 1), 0).astype(F32)
    qdec_f = jnp.exp((tau + 1.0) * lgf)
    qdec_b = jnp.exp((C - tau) * lgb)
    kdec_f = jnp.exp((C - 1.0 - tau) * lgf)
    kdec_b = jnp.exp(tau * lgb)
    sdec_f = jnp.exp(float(C) * lgf)
    sdec_b = jnp.exp(float(C) * lgb)
    gmask = g_ref[...].astype(F32)

    @pl.when(pl.program_id(0) == 0)
    def _():
        dd = (lax.broadcasted_iota(jnp.int32, (C, C), 0)
              - lax.broadcasted_iota(jnp.int32, (C, C), 1)).astype(F32)
        for h in range(HEADS):
            mf = jnp.where(dd >= 0, jnp.exp(jnp.maximum(dd, 0.0) * lg_ref[0, h]), 0.0)
            mb = jnp.where(dd <= 0, jnp.exp(jnp.maximum(-dd, 0.0) * lg_ref[1, h]), 0.0)
            mask_scr[h * C:(h + 1) * C, :] = mf + mb

    def load(r0):
        q = z_ref[pl.ds(r0, C), 0:PACK_W].astype(F32)
        k = z_ref[pl.ds(r0, C), PACK_W:2 * PACK_W].astype(F32) * (DH ** -0.5)
        v = z_ref[pl.ds(r0, C), 2 * PACK_W:3 * PACK_W].astype(F32)
        if latent:
            cos = cos_ref[pl.ds(r0, C), :]
            sin = sin_ref[pl.ds(r0, C), :]
            q = _rope(q, cos, sin)
            k = _rope(k, cos, sin)
        return q, k, v

    def intra(q, k, v):
        qb = q.astype(BF16)
        q4 = jnp.concatenate([jnp.where(lh == h, qb, jnp.zeros_like(qb)) for h in range(HEADS)],
                             axis=0)
        a4 = _dot_nt(q4, k.astype(BF16)) * mask_scr[...]
        o4 = _dot(a4.astype(BF16), v.astype(BF16))
        o = jnp.zeros((C, PACK_W), F32)
        for h in range(HEADS):
            o = o + jnp.where(lh == h, o4[h * C:(h + 1) * C], 0.0)
        return o

    def finish(o, r0):
        gate = z_ref[pl.ds(r0, C), 3 * PACK_W:4 * PACK_W].astype(F32)
        y = _group_rms(o, g_ref[0:LANES, 0:LANES], DH) * _silu(gate)
        out_ref[pl.ds(r0, C), :] = y.astype(out_ref.dtype)

    def state_step(st, k, v, kdec, sdec):
        upd = _dot_tn(v.astype(BF16), (k * kdec).astype(BF16))
        return (st * sdec + upd) * gmask

    if not latent:
        q, k, v = load(0)
        finish(intra(q, k, v), 0)
        vb = v.astype(BF16)
        for ref, kdec in ((sf_ref, kdec_f), (sb_ref, kdec_b)):
            s = _dot_tn((k * kdec).astype(BF16), vb) * gmask
            for h in range(HEADS):
                ref[0, layer, h] = s[h * DH:(h + 1) * DH, h * DH:(h + 1) * DH]
        for ref, prev in zip((sf_ref, sb_ref), prev_refs):
            ref[0, 0:layer] = prev[0]
        return

    def bwd_body(i, st):
        r0 = pl.multiple_of((n_chunks - 1 - i) * C, C)
        q, k, v = load(r0)
        o_scr[pl.ds(r0, C), :] = _dot_nt((q * qdec_b).astype(BF16), st.astype(BF16))
        return state_step(st, k, v, kdec_b, sdec_b)

    lax.fori_loop(0, n_chunks, bwd_body, _initial_state(s0b_ref, gmask), unroll=2)

    def fwd_body(i, st):
        r0 = pl.multiple_of(i * C, C)
        q, k, v = load(r0)
        o = intra(q, k, v) + o_scr[pl.ds(r0, C), :]
        o = o + _dot_nt((q * qdec_f).astype(BF16), st.astype(BF16))
        finish(o, r0)
        return state_step(st, k, v, kdec_f, sdec_f)

    lax.fori_loop(0, n_chunks, fwd_body, _initial_state(s0f_ref, gmask), unroll=2)


def _state_specs(batch, layer, prev):
    spec = lambda n: pl.BlockSpec((1, n, HEADS, DH, DH), lambda b: (b, 0, 0, 0, 0))
    out_specs = [spec(layer + 1)] * 2
    out_shape = [jax.ShapeDtypeStruct((batch, layer + 1, HEADS, DH, DH), F32)] * 2
    in_specs = [spec(layer)] * 2 if layer > 0 else []
    return in_specs, (list(prev) if layer > 0 else []), out_specs, out_shape


def _retention(z_ret, log_g, g256, batch, seq_len, latent, layer, rope=None, states=None,
               plan_only=False):
    m = batch * seq_len
    row = lambda b: (b, 0)
    const2 = lambda b: (0, 0)
    in_specs = [
        pl.BlockSpec(memory_space=pltpu.SMEM),
        pl.BlockSpec((seq_len, RET_COLS), row),
        pl.BlockSpec((PACK_W, PACK_W), const2),
    ]
    args = [log_g, z_ret, g256]
    out_specs = [pl.BlockSpec((seq_len, PACK_W), row)]
    out_shape = [jax.ShapeDtypeStruct((m, PACK_W), BF16)]
    if latent:
        state_spec = pl.BlockSpec((1, 1, PACK_W, DH), lambda b: (b, layer, 0, 0))
        in_specs += [pl.BlockSpec((seq_len, PACK_W), const2)] * 2 + [state_spec] * 2
        args += [rope[0], rope[1], states[0], states[1]]
    else:
        s_in, s_args, s_out, s_shape = _state_specs(batch, layer, states)
        in_specs += s_in
        args += s_args
        out_specs += s_out
        out_shape += s_shape
    plan = _Plan(
        kernel=functools.partial(_ret_kernel, seq_len=seq_len, latent=latent, layer=layer),
        in_specs=in_specs, args=args, out_specs=out_specs, out_shape=out_shape,
        scratch=[
            pltpu.VMEM((HEADS * RET_C, RET_C), F32),
            pltpu.VMEM((seq_len, PACK_W), F32),
        ])
    if plan_only:
        return plan
    return _launch([plan], (batch,), "ret_latent" if latent else "ret_ctx")


def _attn_kernel(*refs, seq_len, past_len, latent, lam_init, layer, n_qblocks):
    if latent:
        (lam_ref, q_ref, k_ref, v_ref, gate_ref, gq_ref, gk_ref, g_ref,
         cosq_ref, sinq_ref, cosk_ref, sink_ref, kc_ref, vc_ref,
         out_ref, k_scr, v_scr) = refs
        prev_refs = ()
    else:
        (lam_ref, q_ref, k_ref, v_ref, gate_ref, gq_ref, gk_ref, g_ref) = refs[:8]
        prev_refs = refs[8:-5]
        out_ref, kn_ref, vn_ref, k_scr, v_scr = refs[-5:]
    ones = jnp.ones((ROW_BLK, DIFF_DV), BF16)

    def put_values(sl, head_values):
        for h in range(HEADS):
            v_scr[sl, 2 * h * DIFF_DV:(2 * h + 1) * DIFF_DV] = head_values(h).astype(BF16)
            v_scr[sl, (2 * h + 1) * DIFF_DV:(2 * h + 2) * DIFF_DV] = ones

    def prepare_keys_values():
        for r in range(seq_len // ROW_BLK):
            sl = pl.ds(r * ROW_BLK, ROW_BLK)
            kn = _group_rms(k_ref[sl, :].astype(F32), g_ref[...], DH) * gk_ref[...]
            v = v_ref[sl, :].astype(F32)
            if latent:
                kn = _rope(kn, cosk_ref[sl, :], sink_ref[sl, :])
            else:
                kn_ref[0, layer, sl, :] = kn
                for h in range(HEADS):
                    vn_ref[0, layer, sl, h, :] = v[:, h * DIFF_DV:(h + 1) * DIFF_DV]
            k_scr[sl, :] = kn.astype(BF16)
            put_values(sl, lambda h: v[:, h * DIFF_DV:(h + 1) * DIFF_DV])
        if prev_refs:
            kn_ref[0, 0:layer] = prev_refs[0][0]
            vn_ref[0, 0:layer] = prev_refs[1][0]
        if latent:
            for r in range(past_len // ROW_BLK):
                src = pl.ds(r * ROW_BLK, ROW_BLK)
                dst = pl.ds(seq_len + r * ROW_BLK, ROW_BLK)
                k_scr[dst, :] = kc_ref[0, 0, :, src].T.astype(BF16)
                put_values(dst, lambda h: vc_ref[0, 0, src, h, :])

    if n_qblocks == 1:
        prepare_keys_values()
    else:
        pl.when(pl.program_id(1) == 0)(prepare_keys_values)

    lam = lam_ref[0]
    qn = _group_rms(q_ref[...].astype(F32), g_ref[...], DH) * gq_ref[...]
    if latent:
        qn = _rope(qn, cosq_ref[...], sinq_ref[...])
    qn = qn * (DH ** -0.5 * LOG2E)
    lane = lax.broadcasted_iota(jnp.int32, (1, 2 * DH), 1)
    for h in range(HEADS):
        hs = slice(h * 2 * DH, (h + 1) * 2 * DH)
        vx = v_scr[:, 2 * h * DIFF_DV:(2 * h + 2) * DIFF_DV]
        qh = qn[:, hs]
        kh = k_scr[:, hs]
        parts = []
        for first_map in (True, False):
            qm = jnp.where((lane < DH) == first_map, qh, 0.0).astype(BF16)
            s = _dot_nt(qm, kh)
            e = jnp.exp2(s - jnp.max(s, axis=-1, keepdims=True)).astype(BF16)
            ox = _dot(e, vx)
            parts.append(ox[:, :DIFF_DV] / ox[:, DIFF_DV:])
        o = parts[0] - lam * parts[1]
        ms = jnp.mean(o * o, axis=-1, keepdims=True)
        y = o * lax.rsqrt(ms + EPS) * (1.0 - lam_init) * _silu(gate_ref[:, hs].astype(F32))
        out_ref[:, hs] = y.astype(out_ref.dtype)


def _diff_attention(z_diff, lam, gq, gk, g128, batch, seq_len, latent, lam_init, layer,
                    rope=None, ctx=None, caches=None, plan_only=False):
    m = batch * seq_len
    tq = min(TQ, seq_len)
    nq = seq_len // tq
    past_len = ctx[1].shape[2] if latent else 0
    qmap = lambda col: (lambda b, *i: (b * nq + (i[0] if i else 0), col))
    kvmap = lambda col: (lambda b, *i: (b, col))
    const2 = lambda b, *i: (0, 0)
    in_specs = [
        pl.BlockSpec(memory_space=pltpu.SMEM),
        pl.BlockSpec((tq, DIFF_W), qmap(0)),
        pl.BlockSpec((seq_len, DIFF_W), kvmap(1)),
        pl.BlockSpec((seq_len, DIFF_W), kvmap(2)),
        pl.BlockSpec((tq, DIFF_W), qmap(3)),
        pl.BlockSpec((1, DIFF_W), const2),
        pl.BlockSpec((1, DIFF_W), const2),
        pl.BlockSpec((LANES, LANES), const2),
    ]
    args = [lam, z_diff, z_diff, z_diff, z_diff, gq, gk, g128]
    out_specs = [pl.BlockSpec((tq, DIFF_W), qmap(0))]
    out_shape = [jax.ShapeDtypeStruct((m, DIFF_W), BF16)]
    if latent:
        in_specs += [
            pl.BlockSpec((tq, DIFF_W), lambda b, i: (i, 0)),
            pl.BlockSpec((tq, DIFF_W), lambda b, i: (i, 0)),
            pl.BlockSpec((seq_len, DIFF_W), const2),
            pl.BlockSpec((seq_len, DIFF_W), const2),
            pl.BlockSpec((1, 1, DIFF_W, past_len), lambda b, i: (b, layer, 0, 0)),
            pl.BlockSpec((1, 1, past_len, HEADS, DIFF_DV), lambda b, i: (b, layer, 0, 0, 0)),
        ]
        args += [rope[0], rope[1], rope[0], rope[1], ctx[0], ctx[1]]
    else:
        cache_spec = lambda n: pl.BlockSpec((1, n, seq_len, DIFF_W), lambda b, *i: (b, 0, 0, 0))
        vcache_spec = lambda n: pl.BlockSpec((1, n, seq_len, HEADS, DIFF_DV),
                                             lambda b, *i: (b, 0, 0, 0, 0))
        out_specs += [cache_spec(layer + 1), vcache_spec(layer + 1)]
        out_shape += [jax.ShapeDtypeStruct((batch, layer + 1, seq_len, DIFF_W), F32),
                      jax.ShapeDtypeStruct((batch, layer + 1, seq_len, HEADS, DIFF_DV), F32)]
        if layer > 0:
            in_specs += [cache_spec(layer), vcache_spec(layer)]
            args += list(caches)
    plan = _Plan(
        kernel=functools.partial(_attn_kernel, seq_len=seq_len, past_len=past_len,
                                 latent=latent, lam_init=lam_init, layer=layer, n_qblocks=nq),
        in_specs=in_specs, args=args, out_specs=out_specs, out_shape=out_shape,
        scratch=[
            pltpu.VMEM((seq_len + past_len, DIFF_W), BF16),
            pltpu.VMEM((seq_len + past_len, 2 * DIFF_W), BF16),
        ])
    if plan_only:
        assert nq == 1
        return plan
    return _launch([plan], (batch, nq), "attn_latent" if latent else "attn_ctx")


def _hgrn_kernel(*refs, seq_len, latent, layer):
    if latent:
        (z_ref, lb_ref, tri_ref, g_ref, lm_ref, dm_ref, s0f_ref, s0b_ref,
         out_ref, qi_scr, ks_scr, dec_scr, o_scr, oi_scr) = refs
    else:
        z_ref, lb_ref, tri_ref, g_ref, lm_ref, dm_ref = refs[:6]
        prev_refs = refs[6:-8]
        out_ref, sf_ref, sb_ref, qi_scr, ks_scr, dec_scr, o_scr, oi_scr = refs[-8:]
    C, SUB = HG_C, HG_SUB
    n_chunks = seq_len // C
    chunks_per_blk = ROW_BLK // C
    lh = _lane_head()
    sub_row = lax.broadcasted_iota(jnp.int32, (ROW_BLK, 1), 0) & (SUB - 1)
    chunk_row = lax.broadcasted_iota(jnp.int32, (C, 1), 0)

    def tile_roll(x, shift):
        return jnp.concatenate(
            [pltpu.roll(x[i:i + SUBLANES], shift, 0) for i in range(0, ROW_BLK, SUBLANES)], axis=0)

    def head_rep(x):
        return jnp.concatenate([jnp.where(lh == h, x, jnp.zeros_like(x)) for h in range(HEADS)],
                               axis=0)

    def intra(blk):
        start = blk * ROW_BLK if isinstance(blk, int) else pl.multiple_of(blk * ROW_BLK, ROW_BLK)
        rows = pl.ds(start, ROW_BLK)
        lb = lb_ref[...]
        q = z_ref[rows, 0:PACK_W] * (DH ** -0.5)
        v = z_ref[rows, 3 * PACK_W:4 * PACK_W]
        v_reps = [head_rep(v[c * C:(c + 1) * C].astype(BF16)) for c in range(chunks_per_blk)]
        kks, bcs, a_nears = [], [], []
        for d in range(2):
            f = lb + (1.0 - lb) * jax.nn.sigmoid(z_ref[rows, (1 + d) * PACK_W:(2 + d) * PACK_W])
            kks.append(1.0 - f)
            l2 = jnp.log(f) * LOG2E
            hi = l2.astype(BF16)
            r1 = l2 - hi.astype(F32)
            mid = r1.astype(BF16)
            lo = (r1 - mid.astype(F32)).astype(BF16)
            cs = _dot(tri_ref[d], jnp.concatenate([hi, mid, lo], axis=1))
            bcs.append(cs[:, 0:PACK_W] + cs[:, PACK_W:2 * PACK_W] + cs[:, 2 * PACK_W:])

        for d in range(2):
            bwd = d == 1
            kk, bc = kks[d], bcs[d]
            ps = [(q * kk).astype(BF16)]
            for dist in range(1, SUB):
                shift = SUBLANES - dist if bwd else dist
                valid = (sub_row < SUB - dist) if bwd else (sub_row >= dist)
                arg = jnp.where(valid, bc - tile_roll(bc, shift), NEG_BIG)
                ps.append((q * tile_roll(kk, shift) * jnp.exp2(arg)).astype(BF16))
            r = _dot(jnp.concatenate(ps, axis=0), g_ref[...]).astype(BF16)
            a_near = r[0:ROW_BLK] * dm_ref[d, 0]
            for dist in range(1, SUB):
                a_near = a_near + r[dist * ROW_BLK:(dist + 1) * ROW_BLK] * dm_ref[d, dist]
            a_nears.append(a_near)

        for c in range(chunks_per_blk):
            for d in range(2):
                bwd = d == 1
                kk, bc, a_near = kks[d], bcs[d], a_nears[d]
                cs_ = slice(c * C, (c + 1) * C)
                qc, kc, bcc = q[cs_], kk[cs_], bc[cs_]
                last = bcc[0:1] if bwd else bcc[C - 1:C]
                a = a_near[cs_]
                for li, m in enumerate(HG_LEVELS):
                    pieces = []
                    for b2 in range(C // (2 * m)):
                        rr = 2 * m * b2 + (m if bwd else m - 1)
                        pieces.append(jnp.broadcast_to(bcc[rr:rr + 1], (2 * m, PACK_W)))
                    anchor = jnp.concatenate(pieces, axis=0) if len(pieces) > 1 else pieces[0]
                    upper = (chunk_row & (2 * m - 1)) >= m
                    later = jnp.logical_not(upper) if bwd else upper
                    e = jnp.exp2((bcc - anchor) * jnp.where(later, 1.0, -1.0))
                    a = a + _dot_nt((qc * e).astype(BF16),
                                    head_rep((kc * e).astype(BF16))).astype(BF16) * lm_ref[d, li]
                crow = pl.ds(start + c * C, C)
                o_scr[d, crow, :] = _dot(a, v_reps[c])
                qi_scr[d, crow, :] = (qc * jnp.exp2(bcc)).astype(BF16)
                ks_scr[d, crow, :] = (kc * jnp.exp2(last - bcc)).astype(BF16)
                slot = (blk * chunks_per_blk + c) * SUBLANES
                if not isinstance(slot, int):
                    slot = pl.multiple_of(slot, SUBLANES)
                dec_scr[d, pl.ds(slot, SUBLANES), :] = jnp.broadcast_to(
                    jnp.exp2(last), (SUBLANES, PACK_W))

        if not latent:
            vb = v.astype(BF16)
            for d, ref in enumerate((sf_ref, sb_ref)):
                bwd = d == 1
                sweep = range(chunks_per_blk - 1, -1, -1) if bwd else range(chunks_per_blk)
                tail = jnp.zeros((1, PACK_W), F32)
                decayed = [None] * chunks_per_blk
                for c in reversed(sweep):
                    cs_ = slice(c * C, (c + 1) * C)
                    bcc = bcs[d][cs_]
                    total = tail + (bcc[0:1] if bwd else bcc[C - 1:C])
                    decayed[c] = (kks[d][cs_] * jnp.exp2(total - bcc)).astype(BF16)
                    tail = total
                s = _dot_tn(jnp.concatenate(decayed, axis=0), vb) * g_ref[...].astype(F32)
                for h in range(HEADS):
                    ref[0, layer, h] = s[h * DH:(h + 1) * DH, h * DH:(h + 1) * DH]
                if prev_refs:
                    ref[0, 0:layer] = prev_refs[d][0]

    n_blk = seq_len // ROW_BLK
    assert latent or n_blk == 1
    if n_blk == 1:
        intra(0)
    else:
        def intra_step(i, carry):
            intra(i)
            return carry
        lax.fori_loop(0, n_blk, intra_step, 0, unroll=2)

    gmask = g_ref[...].astype(F32)
    if latent:
        states = (_initial_state(s0f_ref, gmask), _initial_state(s0b_ref, gmask))
    else:
        states = (jnp.zeros((PACK_W, PACK_W), F32),) * 2

    def body(i, states):
        new_states = []
        for d in range(2):
            c = (n_chunks - 1 - i) if d == 1 else i
            rows = pl.ds(pl.multiple_of(c * C, C), C)
            st = states[d]
            oi_scr[d, rows, :] = _dot_nt(qi_scr[d, rows, :], st.astype(BF16))
            dec = dec_scr[d, pl.ds(pl.multiple_of(c * SUBLANES, SUBLANES), 1), :]
            vb = z_ref[rows, 3 * PACK_W:4 * PACK_W].astype(BF16)
            new_states.append(st * dec + _dot_tn(vb, ks_scr[d, rows, :]) * gmask)
        return tuple(new_states)

    lax.fori_loop(0, n_chunks, body, states, unroll=8 if n_chunks % 8 == 0 else 4)

    def finish(start):
        rows = pl.ds(start, ROW_BLK)
        tot = (o_scr[0, rows, :] + oi_scr[0, rows, :]) + (o_scr[1, rows, :] + oi_scr[1, rows, :])
        gate = z_ref[rows, 4 * PACK_W:5 * PACK_W]
        y = _group_rms(tot, g_ref[0:LANES, 0:LANES], DH) * _silu(gate)
        out_ref[rows, :] = y.astype(out_ref.dtype)

    _row_blocks(seq_len, finish)


def _hgrn(z_hg, lb, consts, batch, seq_len, latent, layer, states=None, plan_only=False):
    m = batch * seq_len
    tri, g256, lm, dm = consts
    row = lambda b: (b, 0)
    in_specs = [
        pl.BlockSpec((seq_len, HG_COLS), row),
        pl.BlockSpec((1, PACK_W), lambda b: (0, 0)),
        pl.BlockSpec(tri.shape, lambda b: (0, 0, 0)),
        pl.BlockSpec((PACK_W, PACK_W), lambda b: (0, 0)),
        pl.BlockSpec(lm.shape, lambda b: (0, 0, 0, 0)),
        pl.BlockSpec(dm.shape, lambda b: (0, 0, 0, 0)),
    ]
    args = [z_hg, lb, tri, g256, lm, dm]
    out_specs = [pl.BlockSpec((seq_len, PACK_W), row)]
    out_shape = [jax.ShapeDtypeStruct((m, PACK_W), BF16)]
    if latent:
        in_specs += [pl.BlockSpec((1, 1, PACK_W, DH), lambda b: (b, layer, 0, 0))] * 2
        args += [states[0], states[1]]
    else:
        s_in, s_args, s_out, s_shape = _state_specs(batch, layer, states)
        in_specs += s_in
        args += s_args
        out_specs += s_out
        out_shape += s_shape
    plan = _Plan(
        kernel=functools.partial(_hgrn_kernel, seq_len=seq_len, latent=latent, layer=layer),
        in_specs=in_specs, args=args, out_specs=out_specs, out_shape=out_shape,
        scratch=[
            pltpu.VMEM((2, seq_len, PACK_W), BF16),
            pltpu.VMEM((2, seq_len, PACK_W), BF16),
            pltpu.VMEM((2, seq_len // HG_C * SUBLANES, PACK_W), F32),
            pltpu.VMEM((2, seq_len, PACK_W), F32),
            pltpu.VMEM((2, seq_len, PACK_W), F32),
        ])
    if plan_only:
        return plan
    return _launch([plan], (batch,), "hgrn_latent" if latent else "hgrn_ctx")


def _block_ones(n):
    idx = np.arange(n) // DH
    return jnp.asarray((idx[:, None] == idx[None, :]).astype(np.float32), dtype=BF16)


def _hgrn_consts():
    C = HG_C
    t = np.arange(ROW_BLK)
    same_chunk = (t[:, None] // C) == (t[None, :] // C)
    tri_f = same_chunk & (t[None, :] <= t[:, None])
    tri_b = same_chunk & (t[None, :] >= t[:, None])
    tc = np.arange(C)
    lms = []
    for fwd in (True, False):
        per_level = []
        for m in HG_LEVELS:
            same = (tc[:, None] // (2 * m)) == (tc[None, :] // (2 * m))
            upper = (tc % (2 * m)) >= m
            later, earlier = (upper, ~upper) if fwd else (~upper, upper)
            per_level.append(np.tile(same & later[:, None] & earlier[None, :], (1, HEADS)))
        lms.append(np.stack(per_level))
    dms = [[np.tile(tc[None, :] == tc[:, None] + sign * dist, (ROW_BLK // C, HEADS))
            for dist in range(HG_SUB)] for sign in (-1, 1)]
    return (jnp.asarray(np.stack([tri_f, tri_b]).astype(np.float32), dtype=BF16),
            _block_ones(PACK_W),
            jnp.asarray(np.stack(lms).astype(np.float32), dtype=BF16),
            jnp.asarray(np.stack(dms).astype(np.float32), dtype=BF16))


def _rope_tables(seq_len, width):
    t = np.arange(seq_len)
    pos = np.stack([t // GRID_W, t % GRID_W], axis=1).astype(np.float32)
    j = np.arange(DH)
    axis = j // (2 * ROPE_PAIR)
    jj = j % (2 * ROPE_PAIR)
    inv = (ROPE_BASE ** (-(jj % ROPE_PAIR).astype(np.float64) / ROPE_PAIR)).astype(np.float32)
    ang = (pos[:, axis] * inv[None, :]).astype(np.float64)
    cos = np.cos(ang)
    sin = np.where(jj < ROPE_PAIR, -np.sin(ang), np.sin(ang))
    reps = width // DH
    return (jnp.asarray(np.tile(cos, (1, reps)), dtype=F32),
            jnp.asarray(np.tile(sin, (1, reps)), dtype=F32))


def _layer(x2d, z, batch, seq_len, mods, l, p, latent, ctx, caches, states=None):
    z_ret, z_diff, z_hg = z
    lam_init = 0.8 - 0.6 * math.exp(-0.3 * l)
    if latent:
        k_ctx, v_ctx, s_rf, s_rb, s_hf, s_hb = ctx
        r = _retention(z_ret, p['log_g'][l], p['g256'], batch, seq_len, True, l,
                       rope=p['rope256'], states=(s_rf, s_rb))[0]
        a = _diff_attention(z_diff, p['lam'][l], p['gq'][l], p['gk'][l], p['g128'], batch,
                            seq_len, True, lam_init, l, rope=p['rope512'],
                            ctx=(k_ctx, v_ctx))[0]
        hgo = _hgrn(z_hg, p['lb'][l], p['hg_consts'], batch, seq_len, True, l,
                    states=(s_hf, s_hb))[0]
        extras = None
    else:
        prev = states if states is not None else (None,) * 4
        plans = [
            _retention(z_ret, p['log_g'][l], p['g256'], batch, seq_len, False, l,
                       states=prev[0:2], plan_only=True),
            _diff_attention(z_diff, p['lam'][l], p['gq'][l], p['gk'][l], p['g128'], batch,
                            seq_len, False, lam_init, l, caches=caches, plan_only=True),
            _hgrn(z_hg, p['lb'][l], p['hg_consts'], batch, seq_len, False, l,
                  states=prev[2:4], plan_only=True),
        ]
        (r, s_rf, s_rb), (a, kcache, vcache), (hgo, s_hf, s_hb) = _launch(
            plans, (batch,), "mix_ctx")
        extras = ((kcache, vcache), (s_rf, s_rb, s_hf, s_hb))
    if l + 1 < DEPTH:
        y, *z_next = _bridge(x2d, r, a, hgo, mods, l, latent, p['w_out'][l],
                             p['norm_g'][l + 1], p['w_in'][l + 1], seq_len)
    else:
        y = _outproj(x2d, r, a, hgo, mods, l, latent, p['w_out'][l], seq_len)
        z_next = None
    return y, z_next, extras


def kernel(x_prompt, x_sample, c, c_ctx, cache_diff_k, cache_diff_v, state_ret_fwd,
           state_ret_bwd, state_hgrn_fwd, state_hgrn_bwd, norm_g, w_ada, b_ada, w_in,
           w_out, ret_decay_logit, diff_qn_g, diff_kn_g, diff_lambda, hgrn_lb_logit):
    batch, seq, _ = x_prompt.shape
    dec_batch, dec_seq, _ = x_sample.shape
    past_len = cache_diff_k.shape[2]

    assert 1 + dec_batch <= MOD_ROWS
    cv = jnp.zeros((MOD_ROWS, D_MODEL), F32).at[0].set(c_ctx).at[1:1 + dec_batch].set(c)
    mods = _ada(cv, w_ada, b_ada).reshape(DEPTH * 3 * MOD_ROWS, 1, D_MODEL)

    lb_all = jax.nn.softmax(hgrn_lb_logit.astype(F32), axis=0)
    lb_all = jnp.cumsum(lb_all, axis=0) - lb_all[0]
    lp = diff_lambda.astype(F32)
    lam_inits = jnp.asarray([0.8 - 0.6 * math.exp(-0.3 * l) for l in range(DEPTH)], F32)
    lam = (jnp.exp(jnp.sum(lp[:, 0] * lp[:, 1], axis=-1))
           - jnp.exp(jnp.sum(lp[:, 2] * lp[:, 3], axis=-1)) + lam_inits)
    p = {
        'norm_g': norm_g.reshape(DEPTH, 1, D_MODEL),
        'w_in': w_in.astype(BF16),
        'w_out': w_out.astype(BF16),
        'log_g': jax.nn.log_sigmoid(ret_decay_logit.astype(F32)),
        'lam': lam.reshape(DEPTH, 1),
        'gq': jnp.tile(diff_qn_g, (1, DIFF_W // DH)).reshape(DEPTH, 1, DIFF_W),
        'gk': jnp.tile(diff_kn_g, (1, DIFF_W // DH)).reshape(DEPTH, 1, DIFF_W),
        'lb': lb_all.reshape(DEPTH, 1, PACK_W),
        'g256': _block_ones(PACK_W),
        'g128': _block_ones(LANES),
        'hg_consts': _hgrn_consts(),
        'rope256': _rope_tables(dec_seq, PACK_W),
        'rope512': _rope_tables(dec_seq, DIFF_W),
    }

    y = x_prompt.reshape(batch * seq, D_MODEL)
    caches = None
    new_states = None
    z = _inproj(y, p['norm_g'][0], mods, 0, False, p['w_in'][0], seq)
    for l in range(DEPTH):
        y, z, (caches, new_states) = _layer(y, z, batch, seq, mods, l, p, False, None, caches,
                                            new_states)
    y_prompt = y.reshape(batch, seq, D_MODEL)
    new_k = caches[0].reshape(batch, DEPTH, seq, HEADS, 2, DH)
    new_v = caches[1]

    y = x_sample.reshape(dec_batch * dec_seq, D_MODEL)
    rows = lambda s: s.reshape(dec_batch, DEPTH, PACK_W, DH)
    keys_t = jnp.transpose(cache_diff_k, (0, 1, 3, 4, 5, 2)).reshape(
        dec_batch, DEPTH, DIFF_W, past_len)
    ctx = (keys_t, cache_diff_v, rows(state_ret_fwd), rows(state_ret_bwd),
           rows(state_hgrn_fwd), rows(state_hgrn_bwd))
    z = _inproj(y, p['norm_g'][0], mods, 0, True, p['w_in'][0], dec_seq)
    for l in range(DEPTH):
        y, z, _ = _layer(y, z, dec_batch, dec_seq, mods, l, p, True, ctx, None)
    y_sample = y.reshape(dec_batch, dec_seq, D_MODEL)

    return (y_prompt, y_sample, new_k, new_v, *new_states)
```

```python
import functools
import math
from typing import Callable, NamedTuple

import numpy as np
import jax
import jax.numpy as jnp
from jax import lax
from jax.experimental import pallas as pl
from jax.experimental.pallas import tpu as pltpu

F32 = jnp.float32
BF16 = jnp.bfloat16

D_MODEL = 1024
DEPTH = 2
GRID_W = 64
HEADS = 4
DH = 64
PACK_W = HEADS * DH
DIFF_W = 512
DIFF_DV = 128
RET_COLS = 4 * PACK_W
DIFF_COLS = 4 * DIFF_W
HG_COLS = 5 * PACK_W
IN_W = RET_COLS + DIFF_COLS + HG_COLS
MIX_W = PACK_W + DIFF_W + PACK_W
ROPE_BASE = 10000.0
ROPE_PAIR = 16
EPS = 1e-6
LOG2E = 1.4426950408889634
NEG_BIG = -1e30

RET_C = 256
HG_C = 64
HG_SUB = 4
HG_LEVELS = (4, 8, 16, 32)
SUBLANES = 8
LANES = 128
ROW_BLK = 256
TQ = 512
TM_IN = 1024
TM_OUT = 512

VMEM_LIMIT = 56 * 1024 * 1024


def _params(n_axes):
    return pltpu.CompilerParams(
        dimension_semantics=("arbitrary",) * n_axes, vmem_limit_bytes=VMEM_LIMIT)


class _Plan(NamedTuple):
    kernel: Callable
    in_specs: list
    args: list
    out_specs: list
    out_shape: list
    scratch: list


def _launch(plans, grid, name):
    counts = [(len(p.in_specs), len(p.out_specs), len(p.scratch)) for p in plans]
    n_in = sum(c[0] for c in counts)
    n_out = sum(c[1] for c in counts)

    def body(*refs):
        i0, o0, s0 = 0, n_in, n_in + n_out
        for plan, (ni, no, ns) in zip(plans, counts):
            plan.kernel(*refs[i0:i0 + ni], *refs[o0:o0 + no], *refs[s0:s0 + ns])
            i0, o0, s0 = i0 + ni, o0 + no, s0 + ns

    outs = pl.pallas_call(
        body,
        grid=grid,
        in_specs=[s for p in plans for s in p.in_specs],
        out_specs=[s for p in plans for s in p.out_specs],
        out_shape=[s for p in plans for s in p.out_shape],
        scratch_shapes=[s for p in plans for s in p.scratch],
        compiler_params=_params(len(grid)),
        name=name,
    )(*[a for p in plans for a in p.args])
    results, o0 = [], 0
    for _, no, _ in counts:
        results.append(list(outs[o0:o0 + no]))
        o0 += no
    return results if len(plans) > 1 else results[0]


def _dot(a, b):
    return jnp.dot(a, b, preferred_element_type=F32)


def _dot_nt(a, b):
    return lax.dot_general(a, b, (((1,), (1,)), ((), ())), preferred_element_type=F32)


def _dot_tn(a, b):
    return lax.dot_general(a, b, (((0,), (0,)), ((), ())), preferred_element_type=F32)


def _silu(x):
    return x * jax.nn.sigmoid(x)


def _group_rms(x, g, width):
    n, w = x.shape
    x2 = x * x
    cols = range(0, w, LANES)
    stacked = jnp.concatenate([x2[:, j:j + LANES] for j in cols], axis=0)
    hi = stacked.astype(BF16)
    lo = (stacked - hi.astype(F32)).astype(BF16)
    sums = _dot(hi, g) + _dot(lo, g)
    ms = jnp.concatenate([sums[i * n:(i + 1) * n] for i in range(len(cols))], axis=1)
    return x * lax.rsqrt(ms * (1.0 / width) + EPS)


def _rope(x, cos, sin):
    w = x.shape[-1]
    lane = lax.broadcasted_iota(jnp.int32, (1, w), 1)
    first = (lane & (2 * ROPE_PAIR - 1)) < ROPE_PAIR
    swapped = jnp.where(first, pltpu.roll(x, w - ROPE_PAIR, 1), pltpu.roll(x, ROPE_PAIR, 1))
    return x * cos + swapped * sin


def _initial_state(s_ref, gmask):
    s = s_ref[0, 0]
    return (jnp.concatenate([s] * HEADS, axis=1) * gmask).T


def _lane_head(w=PACK_W):
    return lax.broadcasted_iota(jnp.int32, (1, w), 1) // DH


def _row_blocks(n_rows, body):
    n_blk = n_rows // ROW_BLK
    if n_blk == 1:
        body(0)
    else:
        def step(i, carry):
            body(pl.multiple_of(i * ROW_BLK, ROW_BLK))
            return carry
        lax.fori_loop(0, n_blk, step, 0)


def _ada_kernel(cv_ref, w_ref, b_ref, o_ref):
    cv = cv_ref[...]
    o_ref[0, 0] = _dot(_silu(cv), w_ref[0]) + b_ref[0]


def _ada(cv, w_ada, b_ada):
    tn = 1024
    n = 3 * D_MODEL
    return pl.pallas_call(
        _ada_kernel,
        grid=(DEPTH, n // tn),
        in_specs=[
            pl.BlockSpec((8, D_MODEL), lambda l, j: (0, 0)),
            pl.BlockSpec((1, D_MODEL, tn), lambda l, j: (l, 0, j)),
            pl.BlockSpec((1, 1, tn), lambda l, j: (l, 0, j)),
        ],
        out_specs=pl.BlockSpec((1, 1, 8, tn), lambda l, j: (l, j, 0, 0)),
        out_shape=jax.ShapeDtypeStruct((DEPTH, n // tn, 8, tn), F32),
        compiler_params=_params(2),
        name="ada",
    )(cv, w_ada, b_ada.reshape(DEPTH, 1, n))


def _inproj_kernel(x_ref, g_ref, sc_ref, sh_ref, w_ref, zr_ref, zd_ref, zh_ref):
    _project(x_ref[...], g_ref, sc_ref, sh_ref, w_ref, zr_ref, zd_ref, zh_ref)


def _project(x, g_ref, sc_ref, sh_ref, w_ref, zr_ref, zd_ref, zh_ref):
    ms = jnp.mean(x * x, axis=-1, keepdims=True)
    h = x * lax.rsqrt(ms + EPS) * g_ref[...]
    h = h * (1.0 + sc_ref[0]) + sh_ref[0]
    z = _dot(h.astype(BF16), w_ref[...])
    zr_ref[...] = z[:, :RET_COLS].astype(zr_ref.dtype)
    zd_ref[...] = z[:, RET_COLS:RET_COLS + DIFF_COLS].astype(zd_ref.dtype)
    zh_ref[...] = z[:, RET_COLS + DIFF_COLS:]


MOD_SHIFT, MOD_SCALE, MOD_GATE = 0, 1, 2
MOD_ROWS = 8


def _mod_spec(layer, which, per_batch, tiles_per_seq):
    base = (layer * 3 + which) * MOD_ROWS
    if per_batch:
        return pl.BlockSpec((1, 1, D_MODEL), lambda i: (base + 1 + i // tiles_per_seq, 0, 0))
    return pl.BlockSpec((1, 1, D_MODEL), lambda i: (base, 0, 0))


def _inproj(x2d, g, mods, layer, per_batch, w_bf, seq_len):
    m = x2d.shape[0]
    tiles_per_seq = seq_len // TM_IN
    return pl.pallas_call(
        _inproj_kernel,
        grid=(m // TM_IN,),
        in_specs=[
            pl.BlockSpec((TM_IN, D_MODEL), lambda i: (i, 0)),
            pl.BlockSpec((1, D_MODEL), lambda i: (0, 0)),
            _mod_spec(layer, MOD_SCALE, per_batch, tiles_per_seq),
            _mod_spec(layer, MOD_SHIFT, per_batch, tiles_per_seq),
            pl.BlockSpec((D_MODEL, IN_W), lambda i: (0, 0)),
        ],
        out_specs=[
            pl.BlockSpec((TM_IN, RET_COLS), lambda i: (i, 0)),
            pl.BlockSpec((TM_IN, DIFF_COLS), lambda i: (i, 0)),
            pl.BlockSpec((TM_IN, HG_COLS), lambda i: (i, 0)),
        ],
        out_shape=[
            jax.ShapeDtypeStruct((m, RET_COLS), BF16),
            jax.ShapeDtypeStruct((m, DIFF_COLS), BF16),
            jax.ShapeDtypeStruct((m, HG_COLS), F32),
        ],
        compiler_params=_params(1),
        name="inproj",
    )(x2d, g, mods, mods, w_bf)


def _outproj_kernel(x_ref, r_ref, d_ref, h_ref, gate_ref, w_ref, o_ref):
    mixed = jnp.concatenate([r_ref[...], d_ref[...], h_ref[...]], axis=1)
    o_ref[...] = x_ref[...] + gate_ref[0] * _dot(mixed, w_ref[...])


def _bridge_kernel(x_ref, r_ref, d_ref, h_ref, gate_ref, wo_ref, g_ref, sc_ref, sh_ref, wi_ref,
                   y_ref, zr_ref, zd_ref, zh_ref):
    mixed = jnp.concatenate([r_ref[...], d_ref[...], h_ref[...]], axis=1)
    y = x_ref[...] + gate_ref[0] * _dot(mixed, wo_ref[...])
    y_ref[...] = y
    _project(y, g_ref, sc_ref, sh_ref, wi_ref, zr_ref, zd_ref, zh_ref)


def _bridge(x2d, r, d, h, mods, layer, per_batch, wo_bf, g_next, wi_bf, seq_len):
    m = x2d.shape[0]
    tm = TM_OUT
    assert not per_batch or seq_len % tm == 0
    tiles_per_seq = max(seq_len // tm, 1)
    row = lambda i: (i, 0)
    const = lambda i: (0, 0)
    return pl.pallas_call(
        _bridge_kernel,
        grid=(m // tm,),
        in_specs=[
            pl.BlockSpec((tm, D_MODEL), row),
            pl.BlockSpec((tm, PACK_W), row),
            pl.BlockSpec((tm, DIFF_W), row),
            pl.BlockSpec((tm, PACK_W), row),
            _mod_spec(layer, MOD_GATE, per_batch, tiles_per_seq),
            pl.BlockSpec((MIX_W, D_MODEL), const),
            pl.BlockSpec((1, D_MODEL), const),
            _mod_spec(layer + 1, MOD_SCALE, per_batch, tiles_per_seq),
            _mod_spec(layer + 1, MOD_SHIFT, per_batch, tiles_per_seq),
            pl.BlockSpec((D_MODEL, IN_W), const),
        ],
        out_specs=[
            pl.BlockSpec((tm, D_MODEL), row),
            pl.BlockSpec((tm, RET_COLS), row),
            pl.BlockSpec((tm, DIFF_COLS), row),
            pl.BlockSpec((tm, HG_COLS), row),
        ],
        out_shape=[
            jax.ShapeDtypeStruct((m, D_MODEL), F32),
            jax.ShapeDtypeStruct((m, RET_COLS), BF16),
            jax.ShapeDtypeStruct((m, DIFF_COLS), BF16),
            jax.ShapeDtypeStruct((m, HG_COLS), F32),
        ],
        compiler_params=_params(1),
        name="bridge",
    )(x2d, r, d, h, mods, wo_bf, g_next, mods, mods, wi_bf)


def _outproj(x2d, r, d, h, mods, layer, per_batch, w_bf, seq_len):
    m = x2d.shape[0]
    tiles_per_seq = max(seq_len // TM_OUT, 1)
    tm = TM_OUT if (not per_batch or seq_len % TM_OUT == 0) else seq_len
    return pl.pallas_call(
        _outproj_kernel,
        grid=(m // tm,),
        in_specs=[
            pl.BlockSpec((tm, D_MODEL), lambda i: (i, 0)),
            pl.BlockSpec((tm, PACK_W), lambda i: (i, 0)),
            pl.BlockSpec((tm, DIFF_W), lambda i: (i, 0)),
            pl.BlockSpec((tm, PACK_W), lambda i: (i, 0)),
            _mod_spec(layer, MOD_GATE, per_batch, tiles_per_seq),
            pl.BlockSpec((MIX_W, D_MODEL), lambda i: (0, 0)),
        ],
        out_specs=pl.BlockSpec((tm, D_MODEL), lambda i: (i, 0)),
        out_shape=jax.ShapeDtypeStruct((m, D_MODEL), F32),
        compiler_params=_params(1),
        name="outproj",
    )(x2d, r, d, h, mods, w_bf)


def _ret_kernel(*refs, seq_len, latent, layer):
    if latent:
        (lg_ref, z_ref, g_ref, cos_ref, sin_ref, s0f_ref, s0b_ref,
         out_ref, mask_scr, o_scr) = refs
    else:
        lg_ref, z_ref, g_ref = refs[:3]
        prev_refs = refs[3:-5]
        out_ref, sf_ref, sb_ref, mask_scr, o_scr = refs[-5:]
    C = RET_C
    n_chunks = seq_len // C
    lh = _lane_head()

    def lane_vec(direction):
        v = jnp.zeros((1, PACK_W), F32)
        for h in range(HEADS):
            v = jnp.where(lh == h, lg_ref[direction, h], v)
        return v

    lgf, lgb = lane_vec(0), lane_vec(1)
    tau = lax.broadcasted_iota(jnp.int32, (C, 1), 0).astype(F32)
    qdec_f = jnp.exp((tau + 1.0) * lgf)
    qdec_b = jnp.exp((C - tau) * lgb)
    kdec_f = jnp.exp((C - 1.0 - tau) * lgf)
    kdec_b = jnp.exp(tau * lgb)
    sdec_f = jnp.exp(float(C) * lgf)
    sdec_b = jnp.exp(float(C) * lgb)
    gmask = g_ref[...].astype(F32)

    @pl.when(pl.program_id(0) == 0)
    def _():
        dd = (lax.broadcasted_iota(jnp.int32, (C, C), 0)
              - lax.broadcasted_iota(jnp.int32, (C, C), 1)).astype(F32)
        for h in range(HEADS):
            mf = jnp.where(dd >= 0, jnp.exp(jnp.maximum(dd, 0.0) * lg_ref[0, h]), 0.0)
            mb = jnp.where(dd <= 0, jnp.exp(jnp.maximum(-dd, 0.0) * lg_ref[1, h]), 0.0)
            mask_scr[h * C:(h + 1) * C, :] = mf + mb

    def load(r0):
        q = z_ref[pl.ds(r0, C), 0:PACK_W].astype(F32)
        k = z_ref[pl.ds(r0, C), PACK_W:2 * PACK_W].astype(F32) * (DH ** -0.5)
        v = z_ref[pl.ds(r0, C), 2 * PACK_W:3 * PACK_W].astype(F32)
        if latent:
            cos = cos_ref[pl.ds(r0, C), :]
            sin = sin_ref[pl.ds(r0, C), :]
            q = _rope(q, cos, sin)
            k = _rope(k, cos, sin)
        return q, k, v

    def intra(q, k, v):
        qb = q.astype(BF16)
        q4 = jnp.concatenate([jnp.where(lh == h, qb, jnp.zeros_like(qb)) for h in range(HEADS)],
                             axis=0)
        a4 = _dot_nt(q4, k.astype(BF16)) * mask_scr[...]
        o4 = _dot(a4.astype(BF16), v.astype(BF16))
        o = jnp.zeros((C, PACK_W), F32)
        for h in range(HEADS):
            o = o + jnp.where(lh == h, o4[h * C:(h + 1) * C], 0.0)
        return o

    def finish(o, r0):
        gate = z_ref[pl.ds(r0, C), 3 * PACK_W:4 * PACK_W].astype(F32)
        y = _group_rms(o, g_ref[0:LANES, 0:LANES], DH) * _silu(gate)
        out_ref[pl.ds(r0, C), :] = y.astype(out_ref.dtype)

    def state_step(st, k, v, kdec, sdec):
        upd = _dot_tn(v.astype(BF16), (k * kdec).astype(BF16))
        return (st * sdec + upd) * gmask

    if not latent:
        q, k, v = load(0)
        finish(intra(q, k, v), 0)
        vb = v.astype(BF16)
        for ref, kdec in ((sf_ref, kdec_f), (sb_ref, kdec_b)):
            s = _dot_tn((k * kdec).astype(BF16), vb) * gmask
            for h in range(HEADS):
                ref[0, layer, h] = s[h * DH:(h + 1) * DH, h * DH:(h + 1) * DH]
        for ref, prev in zip((sf_ref, sb_ref), prev_refs):
            ref[0, 0:layer] = prev[0]
        return

    def bwd_body(i, st):
        r0 = pl.multiple_of((n_chunks - 1 - i) * C, C)
        q, k, v = load(r0)
        o_scr[pl.ds(r0, C), :] = _dot_nt((q * qdec_b).astype(BF16), st.astype(BF16))
        return state_step(st, k, v, kdec_b, sdec_b)

    lax.fori_loop(0, n_chunks, bwd_body, _initial_state(s0b_ref, gmask), unroll=2)

    def fwd_body(i, st):
        r0 = pl.multiple_of(i * C, C)
        q, k, v = load(r0)
        o = intra(q, k, v) + o_scr[pl.ds(r0, C), :]
        o = o + _dot_nt((q * qdec_f).astype(BF16), st.astype(BF16))
        finish(o, r0)
        return state_step(st, k, v, kdec_f, sdec_f)

    lax.fori_loop(0, n_chunks, fwd_body, _initial_state(s0f_ref, gmask), unroll=2)


def _state_specs(batch, layer, prev):
    spec = lambda n: pl.BlockSpec((1, n, HEADS, DH, DH), lambda b: (b, 0, 0, 0, 0))
    out_specs = [spec(layer + 1)] * 2
    out_shape = [jax.ShapeDtypeStruct((batch, layer + 1, HEADS, DH, DH), F32)] * 2
    in_specs = [spec(layer)] * 2 if layer > 0 else []
    return in_specs, (list(prev) if layer > 0 else []), out_specs, out_shape


def _retention(z_ret, log_g, g256, batch, seq_len, latent, layer, rope=None, states=None,
               plan_only=False):
    m = batch * seq_len
    row = lambda b: (b, 0)
    const2 = lambda b: (0, 0)
    in_specs = [
        pl.BlockSpec(memory_space=pltpu.SMEM),
        pl.BlockSpec((seq_len, RET_COLS), row),
        pl.BlockSpec((PACK_W, PACK_W), const2),
    ]
    args = [log_g, z_ret, g256]
    out_specs = [pl.BlockSpec((seq_len, PACK_W), row)]
    out_shape = [jax.ShapeDtypeStruct((m, PACK_W), BF16)]
    if latent:
        state_spec = pl.BlockSpec((1, 1, PACK_W, DH), lambda b: (b, layer, 0, 0))
        in_specs += [pl.BlockSpec((seq_len, PACK_W), const2)] * 2 + [state_spec] * 2
        args += [rope[0], rope[1], states[0], states[1]]
    else:
        s_in, s_args, s_out, s_shape = _state_specs(batch, layer, states)
        in_specs += s_in
        args += s_args
        out_specs += s_out
        out_shape += s_shape
    plan = _Plan(
        kernel=functools.partial(_ret_kernel, seq_len=seq_len, latent=latent, layer=layer),
        in_specs=in_specs, args=args, out_specs=out_specs, out_shape=out_shape,
        scratch=[
            pltpu.VMEM((HEADS * RET_C, RET_C), F32),
            pltpu.VMEM((seq_len, PACK_W), F32),
        ])
    if plan_only:
        return plan
    return _launch([plan], (batch,), "ret_latent" if latent else "ret_ctx")


def _attn_kernel(*refs, seq_len, past_len, latent, lam_init, layer, n_qblocks):
    if latent:
        (lam_ref, q_ref, k_ref, v_ref, gate_ref, gq_ref, gk_ref, g_ref,
         cosq_ref, sinq_ref, cosk_ref, sink_ref, kc_ref, vc_ref,
         out_ref, k_scr, v_scr) = refs
        prev_refs = ()
    else:
        (lam_ref, q_ref, k_ref, v_ref, gate_ref, gq_ref, gk_ref, g_ref) = refs[:8]
        prev_refs = refs[8:-5]
        out_ref, kn_ref, vn_ref, k_scr, v_scr = refs[-5:]
    ones = jnp.ones((ROW_BLK, DIFF_DV), BF16)

    def put_values(sl, head_values):
        for h in range(HEADS):
            v_scr[sl, 2 * h * DIFF_DV:(2 * h + 1) * DIFF_DV] = head_values(h).astype(BF16)
            v_scr[sl, (2 * h + 1) * DIFF_DV:(2 * h + 2) * DIFF_DV] = ones

    def prepare_keys_values():
        for r in range(seq_len // ROW_BLK):
            sl = pl.ds(r * ROW_BLK, ROW_BLK)
            kn = _group_rms(k_ref[sl, :].astype(F32), g_ref[...], DH) * gk_ref[...]
            v = v_ref[sl, :].astype(F32)
            if latent:
                kn = _rope(kn, cosk_ref[sl, :], sink_ref[sl, :])
            else:
                kn_ref[0, layer, sl, :] = kn
                for h in range(HEADS):
                    vn_ref[0, layer, sl, h, :] = v[:, h * DIFF_DV:(h + 1) * DIFF_DV]
            k_scr[sl, :] = kn.astype(BF16)
            put_values(sl, lambda h: v[:, h * DIFF_DV:(h + 1) * DIFF_DV])
        if prev_refs:
            kn_ref[0, 0:layer] = prev_refs[0][0]
            vn_ref[0, 0:layer] = prev_refs[1][0]
        if latent:
            for r in range(past_len // ROW_BLK):
                src = pl.ds(r * ROW_BLK, ROW_BLK)
                dst = pl.ds(seq_len + r * ROW_BLK, ROW_BLK)
                k_scr[dst, :] = kc_ref[0, 0, :, src].T.astype(BF16)
                put_values(dst, lambda h: vc_ref[0, 0, src, h, :])

    if n_qblocks == 1:
        prepare_keys_values()
    else:
        pl.when(pl.program_id(1) == 0)(prepare_keys_values)

    lam = lam_ref[0]
    qn = _group_rms(q_ref[...].astype(F32), g_ref[...], DH) * gq_ref[...]
    if latent:
        qn = _rope(qn, cosq_ref[...], sinq_ref[...])
    qn = qn * (DH ** -0.5 * LOG2E)
    lane = lax.broadcasted_iota(jnp.int32, (1, 2 * DH), 1)
    for h in range(HEADS):
        hs = slice(h * 2 * DH, (h + 1) * 2 * DH)
        vx = v_scr[:, 2 * h * DIFF_DV:(2 * h + 2) * DIFF_DV]
        qh = qn[:, hs]
        kh = k_scr[:, hs]
        parts = []
        for first_map in (True, False):
            qm = jnp.where((lane < DH) == first_map, qh, 0.0).astype(BF16)
            s = _dot_nt(qm, kh)
            e = jnp.exp2(s - jnp.max(s, axis=-1, keepdims=True)).astype(BF16)
            ox = _dot(e, vx)
            parts.append(ox[:, :DIFF_DV] / ox[:, DIFF_DV:])
        o = parts[0] - lam * parts[1]
        ms = jnp.mean(o * o, axis=-1, keepdims=True)
        y = o * lax.rsqrt(ms + EPS) * (1.0 - lam_init) * _silu(gate_ref[:, hs].astype(F32))
        out_ref[:, hs] = y.astype(out_ref.dtype)


def _diff_attention(z_diff, lam, gq, gk, g128, batch, seq_len, latent, lam_init, layer,
                    rope=None, ctx=None, caches=None, plan_only=False):
    m = batch * seq_len
    tq = min(TQ, seq_len)
    nq = seq_len // tq
    past_len = ctx[1].shape[2] if latent else 0
    qmap = lambda col: (lambda b, *i: (b * nq + (i[0] if i else 0), col))
    kvmap = lambda col: (lambda b, *i: (b, col))
    const2 = lambda b, *i: (0, 0)
    in_specs = [
        pl.BlockSpec(memory_space=pltpu.SMEM),
        pl.BlockSpec((tq, DIFF_W), qmap(0)),
        pl.BlockSpec((seq_len, DIFF_W), kvmap(1)),
        pl.BlockSpec((seq_len, DIFF_W), kvmap(2)),
        pl.BlockSpec((tq, DIFF_W), qmap(3)),
        pl.BlockSpec((1, DIFF_W), const2),
        pl.BlockSpec((1, DIFF_W), const2),
        pl.BlockSpec((LANES, LANES), const2),
    ]
    args = [lam, z_diff, z_diff, z_diff, z_diff, gq, gk, g128]
    out_specs = [pl.BlockSpec((tq, DIFF_W), qmap(0))]
    out_shape = [jax.ShapeDtypeStruct((m, DIFF_W), BF16)]
    if latent:
        in_specs += [
            pl.BlockSpec((tq, DIFF_W), lambda b, i: (i, 0)),
            pl.BlockSpec((tq, DIFF_W), lambda b, i: (i, 0)),
            pl.BlockSpec((seq_len, DIFF_W), const2),
            pl.BlockSpec((seq_len, DIFF_W), const2),
            pl.BlockSpec((1, 1, DIFF_W, past_len), lambda b, i: (b, layer, 0, 0)),
            pl.BlockSpec((1, 1, past_len, HEADS, DIFF_DV), lambda b, i: (b, layer, 0, 0, 0)),
        ]
        args += [rope[0], rope[1], rope[0], rope[1], ctx[0], ctx[1]]
    else:
        cache_spec = lambda n: pl.BlockSpec((1, n, seq_len, DIFF_W), lambda b, *i: (b, 0, 0, 0))
        vcache_spec = lambda n: pl.BlockSpec((1, n, seq_len, HEADS, DIFF_DV),
                                             lambda b, *i: (b, 0, 0, 0, 0))
        out_specs += [cache_spec(layer + 1), vcache_spec(layer + 1)]
        out_shape += [jax.ShapeDtypeStruct((batch, layer + 1, seq_len, DIFF_W), F32),
                      jax.ShapeDtypeStruct((batch, layer + 1, seq_len, HEADS, DIFF_DV), F32)]
        if layer > 0:
            in_specs += [cache_spec(layer), vcache_spec(layer)]
            args += list(caches)
    plan = _Plan(
        kernel=functools.partial(_attn_kernel, seq_len=seq_len, past_len=past_len,
                                 latent=latent, lam_init=lam_init, layer=layer, n_qblocks=nq),
        in_specs=in_specs, args=args, out_specs=out_specs, out_shape=out_shape,
        scratch=[
            pltpu.VMEM((seq_len + past_len, DIFF_W), BF16),
            pltpu.VMEM((seq_len + past_len, 2 * DIFF_W), BF16),
        ])
    if plan_only:
        assert nq == 1
        return plan
    return _launch([plan], (batch, nq), "attn_latent" if latent else "attn_ctx")


def _hgrn_kernel(*refs, seq_len, latent, layer):
    if latent:
        (z_ref, lb_ref, tri_ref, g_ref, lm_ref, dm_ref, s0f_ref, s0b_ref,
         out_ref, qi_scr, ks_scr, dec_scr, o_scr, oi_scr) = refs
    else:
        z_ref, lb_ref, tri_ref, g_ref, lm_ref, dm_ref = refs[:6]
        prev_refs = refs[6:-8]
        out_ref, sf_ref, sb_ref, qi_scr, ks_scr, dec_scr, o_scr, oi_scr = refs[-8:]
    C, SUB = HG_C, HG_SUB
    n_chunks = seq_len // C
    chunks_per_blk = ROW_BLK // C
    lh = _lane_head()
    sub_row = lax.broadcasted_iota(jnp.int32, (ROW_BLK, 1), 0) & (SUB - 1)
    chunk_row = lax.broadcasted_iota(jnp.int32, (C, 1), 0)

    def tile_roll(x, shift):
        return jnp.concatenate(
            [pltpu.roll(x[i:i + SUBLANES], shift, 0) for i in range(0, ROW_BLK, SUBLANES)], axis=0)

    def head_rep(x):
        return jnp.concatenate([jnp.where(lh == h, x, jnp.zeros_like(x)) for h in range(HEADS)],
                               axis=0)

    def intra(blk):
        start = blk * ROW_BLK if isinstance(blk, int) else pl.multiple_of(blk * ROW_BLK, ROW_BLK)
        rows = pl.ds(start, ROW_BLK)
        lb = lb_ref[...]
        q = z_ref[rows, 0:PACK_W] * (DH ** -0.5)
        v = z_ref[rows, 3 * PACK_W:4 * PACK_W]
        v_reps = [head_rep(v[c * C:(c + 1) * C].astype(BF16)) for c in range(chunks_per_blk)]
        kks, bcs, a_nears = [], [], []
        for d in range(2):
            f = lb + (1.0 - lb) * jax.nn.sigmoid(z_ref[rows, (1 + d) * PACK_W:(2 + d) * PACK_W])
            kks.append(1.0 - f)
            l2 = jnp.log(f) * LOG2E
            hi = l2.astype(BF16)
            r1 = l2 - hi.astype(F32)
            mid = r1.astype(BF16)
            lo = (r1 - mid.astype(F32)).astype(BF16)
            cs = _dot(tri_ref[d], jnp.concatenate([hi, mid, lo], axis=1))
            bcs.append(cs[:, 0:PACK_W] + cs[:, PACK_W:2 * PACK_W] + cs[:, 2 * PACK_W:])

        for d in range(2):
            bwd = d == 1
            kk, bc = kks[d], bcs[d]
            ps = [(q * kk).astype(BF16)]
            for dist in range(1, SUB):
                shift = SUBLANES - dist if bwd else dist
                valid = (sub_row < SUB - dist) if bwd else (sub_row >= dist)
                arg = jnp.where(valid, bc - tile_roll(bc, shift), NEG_BIG)
                ps.append((q * tile_roll(kk, shift) * jnp.exp2(arg)).astype(BF16))
            r = _dot(jnp.concatenate(ps, axis=0), g_ref[...]).astype(BF16)
            a_near = r[0:ROW_BLK] * dm_ref[d, 0]
            for dist in range(1, SUB):
                a_near = a_near + r[dist * ROW_BLK:(dist + 1) * ROW_BLK] * dm_ref[d, dist]
            a_nears.append(a_near)

        for c in range(chunks_per_blk):
            for d in range(2):
                bwd = d == 1
                kk, bc, a_near = kks[d], bcs[d], a_nears[d]
                cs_ = slice(c * C, (c + 1) * C)
                qc, kc, bcc = q[cs_], kk[cs_], bc[cs_]
                last = bcc[0:1] if bwd else bcc[C - 1:C]
                a = a_near[cs_]
                for li, m in enumerate(HG_LEVELS):
                    pieces = []
                    for b2 in range(C // (2 * m)):
                        rr = 2 * m * b2 + (m if bwd else m - 1)
                        pieces.append(jnp.broadcast_to(bcc[rr:rr + 1], (2 * m, PACK_W)))
                    anchor = jnp.concatenate(pieces, axis=0) if len(pieces) > 1 else pieces[0]
                    upper = (chunk_row & (2 * m - 1)) >= m
                    later = jnp.logical_not(upper) if bwd else upper
                    e = jnp.exp2((bcc - anchor) * jnp.where(later, 1.0, -1.0))
                    a = a + _dot_nt((qc * e).astype(BF16),
                                    head_rep((kc * e).astype(BF16))).astype(BF16) * lm_ref[d, li]
                crow = pl.ds(start + c * C, C)
                o_scr[d, crow, :] = _dot(a, v_reps[c])
                qi_scr[d, crow, :] = (qc * jnp.exp2(bcc)).astype(BF16)
                ks_scr[d, crow, :] = (kc * jnp.exp2(last - bcc)).astype(BF16)
                slot = (blk * chunks_per_blk + c) * SUBLANES
                if not isinstance(slot, int):
                    slot = pl.multiple_of(slot, SUBLANES)
                dec_scr[d, pl.ds(slot, SUBLANES), :] = jnp.broadcast_to(
                    jnp.exp2(last), (SUBLANES, PACK_W))

        if not latent:
            vb = v.astype(BF16)
            for d, ref in enumerate((sf_ref, sb_ref)):
                bwd = d == 1
                sweep = range(chunks_per_blk - 1, -1, -1) if bwd else range(chunks_per_blk)
                tail = jnp.zeros((1, PACK_W), F32)
                decayed = [None] * chunks_per_blk
                for c in reversed(sweep):
                    cs_ = slice(c * C, (c + 1) * C)
                    bcc = bcs[d][cs_]
                    total = tail + (bcc[0:1] if bwd else bcc[C - 1:C])
                    decayed[c] = (kks[d][cs_] * jnp.exp2(total - bcc)).astype(BF16)
                    tail = total
                s = _dot_tn(jnp.concatenate(decayed, axis=0), vb) * g_ref[...].astype(F32)
                for h in range(HEADS):
                    ref[0, layer, h] = s[h * DH:(h + 1) * DH, h * DH:(h + 1) * DH]
                if prev_refs:
                    ref[0, 0:layer] = prev_refs[d][0]

    n_blk = seq_len // ROW_BLK
    assert latent or n_blk == 1
    if n_blk == 1:
        intra(0)
    else:
        def intra_step(i, carry):
            intra(i)
            return carry
        lax.fori_loop(0, n_blk, intra_step, 0, unroll=4)

    gmask = g_ref[...].astype(F32)
    if latent:
        states = (_initial_state(s0f_ref, gmask), _initial_state(s0b_ref, gmask))
    else:
        states = (jnp.zeros((PACK_W, PACK_W), F32),) * 2

    def body(i, states):
        new_states = []
        for d in range(2):
            c = (n_chunks - 1 - i) if d == 1 else i
            rows = pl.ds(pl.multiple_of(c * C, C), C)
            st = states[d]
            oi_scr[d, rows, :] = _dot_nt(qi_scr[d, rows, :], st.astype(BF16))
            dec = dec_scr[d, pl.ds(pl.multiple_of(c * SUBLANES, SUBLANES), 1), :]
            vb = z_ref[rows, 3 * PACK_W:4 * PACK_W].astype(BF16)
            new_states.append(st * dec + _dot_tn(vb, ks_scr[d, rows, :]) * gmask)
        return tuple(new_states)

    lax.fori_loop(0, n_chunks, body, states, unroll=8 if n_chunks % 8 == 0 else 4)

    def finish(start):
        rows = pl.ds(start, ROW_BLK)
        tot = (o_scr[0, rows, :] + oi_scr[0, rows, :]) + (o_scr[1, rows, :] + oi_scr[1, rows, :])
        gate = z_ref[rows, 4 * PACK_W:5 * PACK_W]
        y = _group_rms(tot, g_ref[0:LANES, 0:LANES], DH) * _silu(gate)
        out_ref[rows, :] = y.astype(out_ref.dtype)

    _row_blocks(seq_len, finish)


def _hgrn(z_hg, lb, consts, batch, seq_len, latent, layer, states=None, plan_only=False):
    m = batch * seq_len
    tri, g256, lm, dm = consts
    row = lambda b: (b, 0)
    in_specs = [
        pl.BlockSpec((seq_len, HG_COLS), row),
        pl.BlockSpec((1, PACK_W), lambda b: (0, 0)),
        pl.BlockSpec(tri.shape, lambda b: (0, 0, 0)),
        pl.BlockSpec((PACK_W, PACK_W), lambda b: (0, 0)),
        pl.BlockSpec(lm.shape, lambda b: (0, 0, 0, 0)),
        pl.BlockSpec(dm.shape, lambda b: (0, 0, 0, 0)),
    ]
    args = [z_hg, lb, tri, g256, lm, dm]
    out_specs = [pl.BlockSpec((seq_len, PACK_W), row)]
    out_shape = [jax.ShapeDtypeStruct((m, PACK_W), BF16)]
    if latent:
        in_specs += [pl.BlockSpec((1, 1, PACK_W, DH), lambda b: (b, layer, 0, 0))] * 2
        args += [states[0], states[1]]
    else:
        s_in, s_args, s_out, s_shape = _state_specs(batch, layer, states)
        in_specs += s_in
        args += s_args
        out_specs += s_out
        out_shape += s_shape
    plan = _Plan(
        kernel=functools.partial(_hgrn_kernel, seq_len=seq_len, latent=latent, layer=layer),
        in_specs=in_specs, args=args, out_specs=out_specs, out_shape=out_shape,
        scratch=[
            pltpu.VMEM((2, seq_len, PACK_W), BF16),
            pltpu.VMEM((2, seq_len, PACK_W), BF16),
            pltpu.VMEM((2, seq_len // HG_C * SUBLANES, PACK_W), F32),
            pltpu.VMEM((2, seq_len, PACK_W), F32),
            pltpu.VMEM((2, seq_len, PACK_W), F32),
        ])
    if plan_only:
        return plan
    return _launch([plan], (batch,), "hgrn_latent" if latent else "hgrn_ctx")


def _block_ones(n):
    idx = np.arange(n) // DH
    return jnp.asarray((idx[:, None] == idx[None, :]).astype(np.float32), dtype=BF16)


def _hgrn_consts():
    C = HG_C
    t = np.arange(ROW_BLK)
    same_chunk = (t[:, None] // C) == (t[None, :] // C)
    tri_f = same_chunk & (t[None, :] <= t[:, None])
    tri_b = same_chunk & (t[None, :] >= t[:, None])
    tc = np.arange(C)
    lms = []
    for fwd in (True, False):
        per_level = []
        for m in HG_LEVELS:
            same = (tc[:, None] // (2 * m)) == (tc[None, :] // (2 * m))
            upper = (tc % (2 * m)) >= m
            later, earlier = (upper, ~upper) if fwd else (~upper, upper)
            per_level.append(np.tile(same & later[:, None] & earlier[None, :], (1, HEADS)))
        lms.append(np.stack(per_level))
    dms = [[np.tile(tc[None, :] == tc[:, None] + sign * dist, (ROW_BLK // C, HEADS))
            for dist in range(HG_SUB)] for sign in (-1, 1)]
    return (jnp.asarray(np.stack([tri_f, tri_b]).astype(np.float32), dtype=BF16),
            _block_ones(PACK_W),
            jnp.asarray(np.stack(lms).astype(np.float32), dtype=BF16),
            jnp.asarray(np.stack(dms).astype(np.float32), dtype=BF16))


def _rope_tables(seq_len, width):
    t = np.arange(seq_len)
    pos = np.stack([t // GRID_W, t % GRID_W], axis=1).astype(np.float32)
    j = np.arange(DH)
    axis = j // (2 * ROPE_PAIR)
    jj = j % (2 * ROPE_PAIR)
    inv = (ROPE_BASE ** (-(jj % ROPE_PAIR).astype(np.float64) / ROPE_PAIR)).astype(np.float32)
    ang = (pos[:, axis] * inv[None, :]).astype(np.float64)
    cos = np.cos(ang)
    sin = np.where(jj < ROPE_PAIR, -np.sin(ang), np.sin(ang))
    reps = width // DH
    return (jnp.asarray(np.tile(cos, (1, reps)), dtype=F32),
            jnp.asarray(np.tile(sin, (1, reps)), dtype=F32))


def _layer(x2d, z, batch, seq_len, mods, l, p, latent, ctx, caches, states=None):
    z_ret, z_diff, z_hg = z
    lam_init = 0.8 - 0.6 * math.exp(-0.3 * l)
    if latent:
        k_ctx, v_ctx, s_rf, s_rb, s_hf, s_hb = ctx
        r = _retention(z_ret, p['log_g'][l], p['g256'], batch, seq_len, True, l,
                       rope=p['rope256'], states=(s_rf, s_rb))[0]
        a = _diff_attention(z_diff, p['lam'][l], p['gq'][l], p['gk'][l], p['g128'], batch,
                            seq_len, True, lam_init, l, rope=p['rope512'],
                            ctx=(k_ctx, v_ctx))[0]
        hgo = _hgrn(z_hg, p['lb'][l], p['hg_consts'], batch, seq_len, True, l,
                    states=(s_hf, s_hb))[0]
        extras = None
    else:
        prev = states if states is not None else (None,) * 4
        plans = [
            _retention(z_ret, p['log_g'][l], p['g256'], batch, seq_len, False, l,
                       states=prev[0:2], plan_only=True),
            _diff_attention(z_diff, p['lam'][l], p['gq'][l], p['gk'][l], p['g128'], batch,
                            seq_len, False, lam_init, l, caches=caches, plan_only=True),
            _hgrn(z_hg, p['lb'][l], p['hg_consts'], batch, seq_len, False, l,
                  states=prev[2:4], plan_only=True),
        ]
        (r, s_rf, s_rb), (a, kcache, vcache), (hgo, s_hf, s_hb) = _launch(
            plans, (batch,), "mix_ctx")
        extras = ((kcache, vcache), (s_rf, s_rb, s_hf, s_hb))
    if l + 1 < DEPTH:
        y, *z_next = _bridge(x2d, r, a, hgo, mods, l, latent, p['w_out'][l],
                             p['norm_g'][l + 1], p['w_in'][l + 1], seq_len)
    else:
        y = _outproj(x2d, r, a, hgo, mods, l, latent, p['w_out'][l], seq_len)
        z_next = None
    return y, z_next, extras


def kernel(x_prompt, x_sample, c, c_ctx, cache_diff_k, cache_diff_v, state_ret_fwd,
           state_ret_bwd, state_hgrn_fwd, state_hgrn_bwd, norm_g, w_ada, b_ada, w_in,
           w_out, ret_decay_logit, diff_qn_g, diff_kn_g, diff_lambda, hgrn_lb_logit):
    batch, seq, _ = x_prompt.shape
    dec_batch, dec_seq, _ = x_sample.shape
    past_len = cache_diff_k.shape[2]

    assert 1 + dec_batch <= MOD_ROWS
    cv = jnp.zeros((MOD_ROWS, D_MODEL), F32).at[0].set(c_ctx).at[1:1 + dec_batch].set(c)
    mods = _ada(cv, w_ada, b_ada).reshape(DEPTH * 3 * MOD_ROWS, 1, D_MODEL)

    lb_all = jax.nn.softmax(hgrn_lb_logit.astype(F32), axis=0)
    lb_all = jnp.cumsum(lb_all, axis=0) - lb_all[0]
    lp = diff_lambda.astype(F32)
    lam_inits = jnp.asarray([0.8 - 0.6 * math.exp(-0.3 * l) for l in range(DEPTH)], F32)
    lam = (jnp.exp(jnp.sum(lp[:, 0] * lp[:, 1], axis=-1))
           - jnp.exp(jnp.sum(lp[:, 2] * lp[:, 3], axis=-1)) + lam_inits)
    p = {
        'norm_g': norm_g.reshape(DEPTH, 1, D_MODEL),
        'w_in': w_in.astype(BF16),
        'w_out': w_out.astype(BF16),
        'log_g': jax.nn.log_sigmoid(ret_decay_logit.astype(F32)),
        'lam': lam.reshape(DEPTH, 1),
        'gq': jnp.tile(diff_qn_g, (1, DIFF_W // DH)).reshape(DEPTH, 1, DIFF_W),
        'gk': jnp.tile(diff_kn_g, (1, DIFF_W // DH)).reshape(DEPTH, 1, DIFF_W),
        'lb': lb_all.reshape(DEPTH, 1, PACK_W),
        'g256': _block_ones(PACK_W),
        'g128': _block_ones(LANES),
        'hg_consts': _hgrn_consts(),
        'rope256': _rope_tables(dec_seq, PACK_W),
        'rope512': _rope_tables(dec_seq, DIFF_W),
    }

    y = x_prompt.reshape(batch * seq, D_MODEL)
    caches = None
    new_states = None
    z = _inproj(y, p['norm_g'][0], mods, 0, False, p['w_in'][0], seq)
    for l in range(DEPTH):
        y, z, (caches, new_states) = _layer(y, z, batch, seq, mods, l, p, False, None, caches,
                                            new_states)
    y_prompt = y.reshape(batch, seq, D_MODEL)
    new_k = caches[0].reshape(batch, DEPTH, seq, HEADS, 2, DH)
    new_v = caches[1]

    y = x_sample.reshape(dec_batch * dec_seq, D_MODEL)
    rows = lambda s: s.reshape(dec_batch, DEPTH, PACK_W, DH)
    keys_t = jnp.transpose(cache_diff_k, (0, 1, 3, 4, 5, 2)).reshape(
        dec_batch, DEPTH, DIFF_W, past_len)
    ctx = (keys_t, cache_diff_v, rows(state_ret_fwd), rows(state_ret_bwd),
           rows(state_hgrn_fwd), rows(state_hgrn_bwd))
    z = _inproj(y, p['norm_g'][0], mods, 0, True, p['w_in'][0], dec_seq)
    for l in range(DEPTH):
        y, z, _ = _layer(y, z, dec_batch, dec_seq, mods, l, p, True, ctx, None)
    y_sample = y.reshape(dec_batch, dec_seq, D_MODEL)

    return (y_prompt, y_sample, new_k, new_v, *new_states)
```
